```python
import math
import jax, jax.numpy as jnp
from jax import lax
import numpy as np

D_MODEL = 1024
BATCH = 4
SEQ = 4096
DEPTH = 2

HEAD_DIM = 64
ROPE_THETA = 10000.0
NORM_EPS = 1e-6
BLOCK = 128
NEG_INF = -1e30
A_Q_HEADS = 8
A_KV_HEADS = 2
A_HALF_WINDOW = 128
B_GROUPS = ((128, 1), (512, 4), (2048, 16))
B_HEADS_PER_GROUP = 4
MOE_GROUPS = 4
EXPERTS_PER_GROUP = 8
N_EXPERTS = MOE_GROUPS * EXPERTS_PER_GROUP
EXPERT_TOP_K = 2
D_EXPERT = 256
MOE_CHUNK = 1024

A_Q_DIM = A_Q_HEADS * HEAD_DIM
A_KV_DIM = A_KV_HEADS * HEAD_DIM
B_DIM = B_HEADS_PER_GROUP * HEAD_DIM
IN_SPLITS = [A_Q_DIM, A_KV_DIM, A_KV_DIM] + [B_DIM] * (3 * len(B_GROUPS)) + [D_MODEL, D_MODEL]
D_IN = sum(IN_SPLITS)
IN_OFFSETS = [int(v) for v in np.cumsum(IN_SPLITS)[:-1]]

kernel_name = 'hybrid_gqa_dilated_hmoe_encoder'


def rms_norm(x, g):
    xf = x.astype(jnp.float32)
    y = xf * lax.rsqrt(jnp.mean(xf * xf, axis=-1, keepdims=True) + NORM_EPS)
    return (y * g.astype(jnp.float32)).astype(x.dtype)


def rotary_tables(seq_len):
    inv = 1.0 / (ROPE_THETA ** (jnp.arange(0, HEAD_DIM, 2, dtype=jnp.float32) / HEAD_DIM))
    ang = jnp.arange(seq_len, dtype=jnp.float32)[:, None] * inv[None, :]
    return jnp.cos(ang), jnp.sin(ang)


def apply_rope(t, cos, sin):
    half = t.shape[-1] // 2
    t1 = t[..., :half].astype(jnp.float32)
    t2 = t[..., half:].astype(jnp.float32)
    c = cos[None, :, None, :]
    s = sin[None, :, None, :]
    return jnp.concatenate([t1 * c - t2 * s, t2 * c + t1 * s], axis=-1).astype(t.dtype)


def banded_attention(q, k, v, half_window, sink):
    B, L, Hq, dh = q.shape
    Hkv = k.shape[2]
    G = Hq // Hkv
    blk = math.gcd(L, BLOCK)
    nb = L // blk
    kw = blk + 2 * half_window
    pad = ((0, 0), (half_window, half_window), (0, 0), (0, 0))
    kp = jnp.pad(k, pad)
    vp = jnp.pad(v, pad)
    key_idx = jnp.arange(nb)[:, None] * blk + jnp.arange(kw)[None, :]
    kb = kp[:, key_idx]
    vb = vp[:, key_idx]
    qb = q.reshape(B, nb, blk, Hkv, G, dh)
    s = jnp.einsum('bnqhgd,bnkhd->bnhgqk', qb, kb).astype(jnp.float32) * (dh ** -0.5)
    qpos = jnp.arange(nb)[:, None] * blk + jnp.arange(blk)[None, :]
    kpos = key_idx - half_window
    rel = kpos[:, None, :] - qpos[:, :, None]
    valid = (jnp.abs(rel) <= half_window) & (kpos >= 0)[:, None, :] & (kpos < L)[:, None, :]
    s = jnp.where(valid[None, :, None, None], s, NEG_INF)
    m = jnp.max(s, axis=-1)
    if sink is not None:
        sk = sink.astype(jnp.float32).reshape(Hkv, G)[None, None, :, :, None]
        m = jnp.maximum(m, sk)
        denom = jnp.sum(jnp.exp(s - m[..., None]), axis=-1) + jnp.exp(sk - m)
    else:
        denom = jnp.sum(jnp.exp(s - m[..., None]), axis=-1)
    lse = m + jnp.log(denom)
    p = jnp.exp(s - lse[..., None]).astype(v.dtype)
    out = jnp.einsum('bnhgqk,bnkhd->bnqhgd', p, vb).reshape(B, L, Hq, dh)
    lse = lse.transpose(0, 1, 4, 2, 3).reshape(B, L, Hq)
    return out, lse


def dilated_attention(q, k, v, window, dilation):
    B, S, H, dh = q.shape
    L = S // dilation

    def to_sub(t):
        return t.reshape(B, L, dilation, H, dh).transpose(0, 2, 1, 3, 4).reshape(B * dilation, L, H, dh)

    o, lse = banded_attention(to_sub(q), to_sub(k), to_sub(v), (window // 2) // dilation, None)
    o = o.reshape(B, dilation, L, H, dh).transpose(0, 2, 1, 3, 4).reshape(B, S, H, dh)
    lse = lse.reshape(B, dilation, L, H).transpose(0, 2, 1, 3).reshape(B, S, H)
    return o, lse


def hier_moe(h, w_rg, b_rg, w_re, b_re, w_g, w_u, w_d):
    B, S, D = h.shape
    t = h.reshape(B * S, D)
    T = t.shape[0]
    glog = (t @ w_rg).astype(jnp.float32) + b_rg.astype(jnp.float32)
    gprob = jax.nn.softmax(glog, axis=-1)
    gidx = jnp.argmax(glog, axis=-1)
    gw = jnp.take_along_axis(gprob, gidx[:, None], axis=1)
    elog = ((t @ w_re).astype(jnp.float32) + b_re.astype(jnp.float32)).reshape(T, MOE_GROUPS, EXPERTS_PER_GROUP)
    elog = jnp.take_along_axis(elog, gidx[:, None, None], axis=1)[:, 0]
    top_v, top_i = lax.top_k(elog, EXPERT_TOP_K)
    top_w = jax.nn.softmax(top_v, axis=-1) * gw
    eid = gidx[:, None] * EXPERTS_PER_GROUP + top_i
    combine = jnp.sum(jax.nn.one_hot(eid, N_EXPERTS, dtype=jnp.float32) * top_w[..., None], axis=1)
    C = math.gcd(T, MOE_CHUNK)

    def run(args):
        tc, cc = args
        a = jnp.einsum('cd,edf->cef', tc, w_g)
        u = jnp.einsum('cd,edf->cef', tc, w_u)
        z = jax.nn.silu(a) * u * cc[..., None].astype(tc.dtype)
        return jnp.einsum('cef,efd->cd', z, w_d)

    y = lax.map(run, (t.reshape(T // C, C, D), combine.reshape(T // C, C, N_EXPERTS)))
    return y.reshape(B, S, D)


def setup_inputs(seed: int = 0) -> dict:
    key = jax.random.key(seed)
    ks = jax.random.split(key, 16)
    f32 = jnp.float32
    nrm = lambda k, shape, scale: jax.random.normal(k, shape, f32) * scale
    return {
        'x': jax.random.normal(ks[0], (BATCH, SEQ, D_MODEL), f32),
        'attn_norm_g': 1.0 + nrm(ks[1], (DEPTH, D_MODEL), 0.01),
        'w_in': nrm(ks[2], (DEPTH, D_MODEL, D_IN), D_MODEL ** -0.5),
        'a_sink': nrm(ks[3], (DEPTH, A_Q_HEADS), 0.5),
        'w_branch_a': nrm(ks[4], (DEPTH, A_Q_DIM, D_MODEL), A_Q_DIM ** -0.5),
        'w_branch_b': nrm(ks[5], (DEPTH, B_DIM, D_MODEL), B_DIM ** -0.5),
        'w_out': nrm(ks[6], (DEPTH, D_MODEL, D_MODEL), D_MODEL ** -0.5),
        'ffn_norm_g': 1.0 + nrm(ks[7], (DEPTH, D_MODEL), 0.01),
        'w_router_group': nrm(ks[8], (DEPTH, D_MODEL, MOE_GROUPS), D_MODEL ** -0.5),
        'b_router_group': nrm(ks[9], (DEPTH, MOE_GROUPS), 0.01),
        'w_router_expert': nrm(ks[10], (DEPTH, D_MODEL, N_EXPERTS), D_MODEL ** -0.5),
        'b_router_expert': nrm(ks[11], (DEPTH, N_EXPERTS), 0.01),
        'w_exp_gate': nrm(ks[12], (DEPTH, N_EXPERTS, D_MODEL, D_EXPERT), D_MODEL ** -0.5),
        'w_exp_up': nrm(ks[13], (DEPTH, N_EXPERTS, D_MODEL, D_EXPERT), D_MODEL ** -0.5),
        'w_exp_down': nrm(ks[14], (DEPTH, N_EXPERTS, D_EXPERT, D_MODEL), D_EXPERT ** -0.5),
        'final_norm_g': 1.0 + nrm(ks[15], (D_MODEL,), 0.01),
    }


def reference(x, attn_norm_g, w_in, a_sink, w_branch_a, w_branch_b, w_out, ffn_norm_g,
              w_router_group, b_router_group, w_router_expert, b_router_expert,
              w_exp_gate, w_exp_up, w_exp_down, final_norm_g):
    B, S, D = x.shape
    cos, sin = rotary_tables(S)
    for l in range(DEPTH):
        h = rms_norm(x, attn_norm_g[l])
        proj = jnp.einsum('bsd,de->bse', h, w_in[l])
        parts = jnp.split(proj, IN_OFFSETS, axis=-1)
        qa = apply_rope(parts[0].reshape(B, S, A_Q_HEADS, HEAD_DIM), cos, sin)
        ka = apply_rope(parts[1].reshape(B, S, A_KV_HEADS, HEAD_DIM), cos, sin)
        va = parts[2].reshape(B, S, A_KV_HEADS, HEAD_DIM)
        ya, _ = banded_attention(qa, ka, va, A_HALF_WINDOW, a_sink[l])
        ya = jnp.einsum('bse,ed->bsd', ya.reshape(B, S, A_Q_DIM), w_branch_a[l])
        outs, lses = [], []
        for gi, (win, dil) in enumerate(B_GROUPS):
            qg = apply_rope(parts[3 + 3 * gi].reshape(B, S, B_HEADS_PER_GROUP, HEAD_DIM), cos, sin)
            kg = apply_rope(parts[4 + 3 * gi].reshape(B, S, B_HEADS_PER_GROUP, HEAD_DIM), cos, sin)
            vg = parts[5 + 3 * gi].reshape(B, S, B_HEADS_PER_GROUP, HEAD_DIM)
            o, lse = dilated_attention(qg, kg, vg, win, dil)
            outs.append(o)
            lses.append(lse)
        wg = jax.nn.softmax(jnp.stack(lses, axis=0), axis=0)
        yb = jnp.sum(wg[..., None].astype(x.dtype) * jnp.stack(outs, axis=0), axis=0)
        yb = jnp.einsum('bse,ed->bsd', yb.reshape(B, S, B_DIM), w_branch_b[l])
        merged = jax.nn.sigmoid(parts[-2]) * ya + jax.nn.sigmoid(parts[-1]) * yb
        x = x + jnp.einsum('bsd,de->bse', merged, w_out[l])
        x = x + hier_moe(rms_norm(x, ffn_norm_g[l]), w_router_group[l], b_router_group[l],
                         w_router_expert[l], b_router_expert[l],
                         w_exp_gate[l], w_exp_up[l], w_exp_down[l])
    return rms_norm(x, final_norm_g)
```

```python
import functools

import jax
import jax.numpy as jnp
from jax import lax
from jax.experimental import pallas as pl
from jax.experimental.pallas import tpu as pltpu

F32 = jnp.float32
BF16 = jnp.bfloat16
I32 = jnp.int32

D_MODEL = 1024
HEAD_DIM = 64
HALF_HEAD = HEAD_DIM // 2
ROPE_THETA = 10000.0
NORM_EPS = 1e-6
NEG_INF = -1e30
LANES = 128

A_Q_HEADS = 8
A_KV_HEADS = 2
A_GROUP = A_Q_HEADS // A_KV_HEADS
A_HALF_WINDOW = 128
A_Q_DIM = A_Q_HEADS * HEAD_DIM
A_KV_DIM = A_KV_HEADS * HEAD_DIM

B_GROUPS = ((128, 1), (512, 4), (2048, 16))
B_HEADS = 4
B_DIM = B_HEADS * HEAD_DIM
B_HALF_WINDOW = 64

MOE_GROUPS = 4
EXPERTS_PER_GROUP = 8
N_EXPERTS = MOE_GROUPS * EXPERTS_PER_GROUP
D_EXPERT = 256

NAT_COLS = A_Q_DIM + 2 * A_KV_DIM + 3 * B_DIM
GRP_COLS = 3 * B_DIM
COL_G1 = NAT_COLS
COL_G2 = COL_G1 + GRP_COLS
COL_GATE = COL_G2 + GRP_COLS
GATE_COLS = 2 * D_MODEL
D_IN = COL_GATE + GATE_COLS

TOK_TILE = 512
A_Q_TILE = 256
B_SUB = 128
EXP_TILE = 256
ROW_TILE = 256
RANK_BITS = 16
VMEM_LIMIT = 56 * 1024 * 1024


def _cparams(*sem):
    return pltpu.CompilerParams(dimension_semantics=sem, vmem_limit_bytes=VMEM_LIMIT)


def _rope_tables(seq_len):
    inv = 1.0 / (ROPE_THETA ** (jnp.arange(0, HEAD_DIM, 2, dtype=F32) / HEAD_DIM))
    ang = jnp.arange(seq_len, dtype=F32)[:, None] * inv[None, :]
    cos, sin = jnp.cos(ang), jnp.sin(ang)
    cos_t = jnp.concatenate([cos, cos, cos, cos], axis=-1)
    sin_t = jnp.concatenate([-sin, sin, -sin, sin], axis=-1)
    return cos_t, sin_t


def _residue_order(table, dilation, tile):
    s, c = table.shape
    return table.reshape(s // tile, tile // dilation, dilation, c).transpose(0, 2, 1, 3).reshape(s, c)


def _rope(t, cos, sin_signed, first_half):
    partner = jnp.where(first_half, pltpu.roll(t, LANES - HALF_HEAD, 1), pltpu.roll(t, HALF_HEAD, 1))
    return t * cos + partner * sin_signed


Q_KIND, K_KIND, V_KIND = 0, 1, 2
_NAT_KINDS = ([Q_KIND] * 4 + [K_KIND] + [V_KIND] + [Q_KIND] * 2 + [K_KIND] * 2 + [V_KIND] * 2)
_GRP_KINDS = [Q_KIND] * 2 + [K_KIND] * 2 + [V_KIND] * 2


def _in_proj_kernel(x_ref, g_ref, w_ref, c1_ref, s1_ref, c4_ref, s4_ref, c16_ref, s16_ref,
                    nat_ref, gate_ref, g1_ref, g2_ref, hf_ref, hb_ref, hd_ref):
    tm = x_ref.shape[0]
    x = x_ref[...]
    h = x * lax.rsqrt(jnp.mean(x * x, axis=-1, keepdims=True) + NORM_EPS) * g_ref[...]
    n_chunks = D_MODEL // LANES
    for c in range(n_chunks):
        hf_ref[c] = h[:, c * LANES:(c + 1) * LANES]
    hb_ref[...] = h.astype(BF16)
    lane = lax.broadcasted_iota(I32, (1, LANES), 1)
    first_half = (lane % HEAD_DIM) < HALF_HEAD

    def project(h_b, col0, kinds, cos_ref, sin_ref, store):
        width = 512
        for c0 in range(0, len(kinds) * LANES, width):
            w = min(width, len(kinds) * LANES - c0)
            res = jnp.dot(h_b, w_ref[:, col0 + c0:col0 + c0 + w], preferred_element_type=F32)
            for j in range(w // LANES):
                kind = kinds[(c0 // LANES) + j]
                t = res[:, j * LANES:(j + 1) * LANES]
                if kind != V_KIND:
                    t = _rope(t, cos_ref[...], sin_ref[...], first_half)
                if kind == Q_KIND:
                    t = t * (HEAD_DIM ** -0.5)
                store(c0 + j * LANES, t.astype(BF16))

    def store_nat(c, v):
        nat_ref[:, c:c + LANES] = v

    project(hb_ref[...], 0, _NAT_KINDS, c1_ref, s1_ref, store_nat)

    for c0 in range(0, GATE_COLS, 512):
        res = jnp.dot(hb_ref[...], w_ref[:, COL_GATE + c0:COL_GATE + c0 + 512], preferred_element_type=F32)
        gate_ref[:, c0:c0 + 512] = jax.nn.sigmoid(res).astype(BF16)

    for dil, col0, cos_ref, sin_ref, out_ref in ((4, COL_G1, c4_ref, s4_ref, g1_ref),
                                                 (16, COL_G2, c16_ref, s16_ref, g2_ref)):
        n = tm // dil
        for r in range(dil):
            for c in range(n_chunks):
                hd_ref[r * n:(r + 1) * n, c * LANES:(c + 1) * LANES] = (
                    hf_ref[c, pl.ds(r, n, stride=dil), :].astype(BF16))

        def store_grp(c, v, out_ref=out_ref, dil=dil, n=n):
            for r in range(dil):
                out_ref[r, :, c:c + LANES] = v[r * n:(r + 1) * n]

        project(hd_ref[...], col0, _GRP_KINDS, cos_ref, sin_ref, store_grp)


def _in_proj(x2d, g, w_bf16, tables, batch, seq):
    tm = TOK_TILE
    tiles_per_seq = seq // tm
    n_tok = batch * seq
    c1, s1, c4, s4, c16, s16 = tables
    tab_spec = pl.BlockSpec((tm, LANES), lambda i: (i % tiles_per_seq, 0))
    return pl.pallas_call(
        _in_proj_kernel,
        grid=(n_tok // tm,),
        in_specs=[
            pl.BlockSpec((tm, D_MODEL), lambda i: (i, 0)),
            pl.BlockSpec((1, D_MODEL), lambda i: (0, 0)),
            pl.BlockSpec((D_MODEL, D_IN), lambda i: (0, 0), pipeline_mode=pl.Buffered(1)),
            tab_spec, tab_spec, tab_spec, tab_spec, tab_spec, tab_spec,
        ],
        out_specs=[
            pl.BlockSpec((tm, NAT_COLS), lambda i: (i, 0)),
            pl.BlockSpec((tm, GATE_COLS), lambda i: (i, 0)),
            pl.BlockSpec((None, 4, tm // 4, GRP_COLS), lambda i: (i // tiles_per_seq, 0, i % tiles_per_seq, 0)),
            pl.BlockSpec((None, 16, tm // 16, GRP_COLS), lambda i: (i // tiles_per_seq, 0, i % tiles_per_seq, 0)),
        ],
        out_shape=[
            jax.ShapeDtypeStruct((n_tok, NAT_COLS), BF16),
            jax.ShapeDtypeStruct((n_tok, GATE_COLS), BF16),
            jax.ShapeDtypeStruct((batch, 4, seq // 4, GRP_COLS), BF16),
            jax.ShapeDtypeStruct((batch, 16, seq // 16, GRP_COLS), BF16),
        ],
        scratch_shapes=[
            pltpu.VMEM((D_MODEL // LANES, tm, LANES), F32),
            pltpu.VMEM((tm, D_MODEL), BF16),
            pltpu.VMEM((tm, D_MODEL), BF16),
        ],
        compiler_params=_cparams("parallel"),
        name="in_proj",
    )(x2d, g, w_bf16, c1, s1, c4, s4, c16, s16)


def _attn_a_kernel(sink_ref, q_ref, kvp_ref, kvm_ref, kvn_ref, o_ref, kv_ref, *, seq):
    tq = q_ref.shape[0]
    hw = A_HALF_WINDOW
    kw = tq + 2 * hw
    q0 = pl.program_id(1) * tq
    kv_ref[0:hw, :] = kvp_ref[...]
    kv_ref[hw:hw + tq, :] = kvm_ref[...]
    kv_ref[hw + tq:kw, :] = kvn_ref[...]
    qi = lax.broadcasted_iota(I32, (tq, kw), 0)
    kj = lax.broadcasted_iota(I32, (tq, kw), 1)
    kpos = q0 - hw + kj
    valid = (jnp.abs(kj - hw - qi) <= hw) & (kpos >= 0) & (kpos < seq)
    for h in range(A_Q_HEADS):
        hk = h // A_GROUP
        q = q_ref[:, h * HEAD_DIM:(h + 1) * HEAD_DIM]
        k = kv_ref[:, hk * HEAD_DIM:(hk + 1) * HEAD_DIM]
        v = kv_ref[:, A_KV_DIM + hk * HEAD_DIM:A_KV_DIM + (hk + 1) * HEAD_DIM]
        s = lax.dot_general(q, k, (((1,), (1,)), ((), ())), preferred_element_type=F32)
        s = jnp.where(valid, s, NEG_INF)
        sink = sink_ref[h]
        m = jnp.maximum(jnp.max(s, axis=-1, keepdims=True), sink)
        p = jnp.exp(s - m)
        denom = jnp.sum(p, axis=-1, keepdims=True) + jnp.exp(sink - m)
        acc = jnp.dot(p.astype(BF16), v, preferred_element_type=F32)
        o_ref[:, h * HEAD_DIM:(h + 1) * HEAD_DIM] = (acc / denom).astype(BF16)


def _attn_a(nat3d, sink):
    batch, seq, _ = nat3d.shape
    tq = A_Q_TILE
    hw = A_HALF_WINDOW
    per = tq // hw
    n_hw = seq // hw
    kv_blk = A_Q_DIM // (2 * A_KV_DIM)
    return pl.pallas_call(
        functools.partial(_attn_a_kernel, seq=seq),
        grid=(batch, seq // tq),
        in_specs=[
            pl.BlockSpec(memory_space=pltpu.SMEM),
            pl.BlockSpec((None, tq, A_Q_DIM), lambda b, i: (b, i, 0)),
            pl.BlockSpec((None, hw, 2 * A_KV_DIM), lambda b, i: (b, jnp.maximum(i * per - 1, 0), kv_blk)),
            pl.BlockSpec((None, tq, 2 * A_KV_DIM), lambda b, i: (b, i, kv_blk)),
            pl.BlockSpec((None, hw, 2 * A_KV_DIM), lambda b, i: (b, jnp.minimum((i + 1) * per, n_hw - 1), kv_blk)),
        ],
        out_specs=pl.BlockSpec((None, tq, A_Q_DIM), lambda b, i: (b, i, 0)),
        out_shape=jax.ShapeDtypeStruct((batch, seq, A_Q_DIM), BF16),
        scratch_shapes=[pltpu.VMEM((tq + 2 * hw, 2 * A_KV_DIM), BF16)],
        compiler_params=_cparams("parallel", "parallel"),
        name="attn_a",
    )(sink, nat3d, nat3d, nat3d, nat3d)


def _attn_b_kernel(q_ref, kp_ref, km_ref, kn_ref, vp_ref, vm_ref, vn_ref, o_ref, lse_ref,
                   k_ref, v_ref, *, sub_len):
    tq = q_ref.shape[0]
    hw = B_HALF_WINDOW
    t0 = pl.program_id(1) * tq
    for dst, prev, main, nxt in ((k_ref, kp_ref, km_ref, kn_ref), (v_ref, vp_ref, vm_ref, vn_ref)):
        dst[0:hw, :] = prev[...]
        dst[hw:hw + tq, :] = main[...]
        dst[hw + tq:tq + 2 * hw, :] = nxt[...]
    kw = B_SUB + 2 * hw
    qi = lax.broadcasted_iota(I32, (B_SUB, kw), 0)
    kj = lax.broadcasted_iota(I32, (B_SUB, kw), 1)
    in_band = jnp.abs(kj - hw - qi) <= hw
    for sb in range(tq // B_SUB):
        r0 = sb * B_SUB
        kpos = t0 + r0 - hw + kj
        valid = in_band & (kpos >= 0) & (kpos < sub_len)
        for h in range(B_HEADS):
            cs = slice(h * HEAD_DIM, (h + 1) * HEAD_DIM)
            q = q_ref[r0:r0 + B_SUB, cs]
            k = k_ref[r0:r0 + kw, cs]
            v = v_ref[r0:r0 + kw, cs]
            s = lax.dot_general(q, k, (((1,), (1,)), ((), ())), preferred_element_type=F32)
            s = jnp.where(valid, s, NEG_INF)
            m = jnp.max(s, axis=-1, keepdims=True)
            p = jnp.exp(s - m)
            denom = jnp.sum(p, axis=-1, keepdims=True)
            acc = jnp.dot(p.astype(BF16), v, preferred_element_type=F32)
            o_ref[r0:r0 + B_SUB, cs] = acc / denom
            lse_ref[r0:r0 + B_SUB, cs] = jnp.broadcast_to(m + jnp.log(denom), (B_SUB, HEAD_DIM))


def _attn_b(arr3d, q_blk, name):
    n_sub, sub_len, _ = arr3d.shape
    tq = min(512, sub_len)
    hw = B_HALF_WINDOW
    per = tq // hw
    n_hw = sub_len // hw

    def main_spec(c):
        return pl.BlockSpec((None, tq, B_DIM), lambda g, i: (g, i, c))

    def prev_spec(c):
        return pl.BlockSpec((None, hw, B_DIM), lambda g, i: (g, jnp.maximum(i * per - 1, 0), c))

    def next_spec(c):
        return pl.BlockSpec((None, hw, B_DIM), lambda g, i: (g, jnp.minimum((i + 1) * per, n_hw - 1), c))

    out_spec = pl.BlockSpec((None, tq, B_DIM), lambda g, i: (g, i, 0))
    return pl.pallas_call(
        functools.partial(_attn_b_kernel, sub_len=sub_len),
        grid=(n_sub, sub_len // tq),
        in_specs=[main_spec(q_blk),
                  prev_spec(q_blk + 1), main_spec(q_blk + 1), next_spec(q_blk + 1),
                  prev_spec(q_blk + 2), main_spec(q_blk + 2), next_spec(q_blk + 2)],
        out_specs=[out_spec, out_spec],
        out_shape=[jax.ShapeDtypeStruct((n_sub, sub_len, B_DIM), F32)] * 2,
        scratch_shapes=[pltpu.VMEM((tq + 2 * hw, B_DIM), BF16)] * 2,
        compiler_params=_cparams("parallel", "parallel"),
        name=name,
    )(arr3d, arr3d, arr3d, arr3d, arr3d, arr3d, arr3d)


def _merge_kernel(x_ref, ya_ref, o0_ref, l0_ref, o1_ref, l1_ref, o2_ref, l2_ref, gate_ref,
                  wa_ref, wb_ref, wo_ref, g2_ref, wrh_ref, wrl_ref, br_ref,
                  x1_ref, h2_ref, code_ref, wcol_ref, cnt_ref,
                  so1_ref, sl1_ref, so2_ref, sl2_ref, yb_ref, run_ref):
    tm = x_ref.shape[0]

    @pl.when(pl.program_id(0) == 0)
    def _():
        run_ref[...] = jnp.zeros_like(run_ref)

    for dil, src_o, src_l, dst_o, dst_l in ((4, o1_ref, l1_ref, so1_ref, sl1_ref),
                                            (16, o2_ref, l2_ref, so2_ref, sl2_ref)):
        n = tm // dil
        for r in range(dil):
            for c in range(B_DIM // LANES):
                cs = slice(c * LANES, (c + 1) * LANES)
                dst_o[c, pl.ds(r, n, stride=dil), :] = src_o[r, :, cs]
                dst_l[c, pl.ds(r, n, stride=dil), :] = src_l[r, :, cs]

    for c in range(B_DIM // LANES):
        cs = slice(c * LANES, (c + 1) * LANES)
        l0, l1, l2 = l0_ref[:, cs], sl1_ref[c], sl2_ref[c]
        m = jnp.maximum(jnp.maximum(l0, l1), l2)
        e0, e1, e2 = jnp.exp(l0 - m), jnp.exp(l1 - m), jnp.exp(l2 - m)
        yb = (e0 * o0_ref[:, cs] + e1 * so1_ref[c] + e2 * so2_ref[c]) / (e0 + e1 + e2)
        yb_ref[:, cs] = yb.astype(BF16)

    ya_p = jnp.dot(ya_ref[...], wa_ref[...], preferred_element_type=F32)
    yb_p = jnp.dot(yb_ref[...], wb_ref[...], preferred_element_type=F32)
    merged = gate_ref[:, 0:D_MODEL].astype(F32) * ya_p + gate_ref[:, D_MODEL:GATE_COLS].astype(F32) * yb_p
    x1 = x_ref[...] + jnp.dot(merged.astype(BF16), wo_ref[...], preferred_element_type=F32)
    x1_ref[...] = x1

    h2 = x1 * lax.rsqrt(jnp.mean(x1 * x1, axis=-1, keepdims=True) + NORM_EPS) * g2_ref[...]
    h2_ref[...] = h2

    h_hi = h2.astype(BF16)
    h_lo = (h2 - h_hi.astype(F32)).astype(BF16)
    logits = (jnp.dot(h_hi, wrh_ref[...], preferred_element_type=F32)
              + jnp.dot(h_lo, wrh_ref[...], preferred_element_type=F32)
              + jnp.dot(h_hi, wrl_ref[...], preferred_element_type=F32)) + br_ref[...]

    lane = lax.broadcasted_iota(I32, (tm, LANES), 1)
    is_grp = (lane >= N_EXPERTS) & (lane < N_EXPERTS + MOE_GROUPS)
    gl = jnp.where(is_grp, logits, -jnp.inf)
    gmax = jnp.max(gl, axis=-1, keepdims=True)
    gidx = jnp.min(jnp.where(gl == gmax, lane - N_EXPERTS, LANES), axis=-1, keepdims=True)
    gw = 1.0 / jnp.sum(jnp.exp(gl - gmax), axis=-1, keepdims=True)
    in_grp = (lane < N_EXPERTS) & ((lane // EXPERTS_PER_GROUP) == gidx)
    el = jnp.where(in_grp, logits, -jnp.inf)
    v1 = jnp.max(el, axis=-1, keepdims=True)
    i1 = jnp.min(jnp.where(el == v1, lane, LANES), axis=-1, keepdims=True)
    el2 = jnp.where(lane == i1, -jnp.inf, el)
    v2 = jnp.max(el2, axis=-1, keepdims=True)
    i2 = jnp.min(jnp.where(el2 == v2, lane, LANES), axis=-1, keepdims=True)
    t = jnp.exp(v2 - v1)
    w1 = gw / (1.0 + t)
    w2 = gw * t / (1.0 + t)

    oh1 = (lane == i1).astype(F32)
    oh2 = (lane == i2).astype(F32)
    oh = oh1 + oh2
    row = lax.broadcasted_iota(I32, (tm, tm), 0)
    col = lax.broadcasted_iota(I32, (tm, tm), 1)
    lower = (col < row).astype(BF16)
    before = jnp.dot(lower, oh.astype(BF16), preferred_element_type=F32) + run_ref[0:1, :]
    rank1 = jnp.sum(before * oh1, axis=-1, keepdims=True)
    rank2 = jnp.sum(before * oh2, axis=-1, keepdims=True)
    run_ref[...] = run_ref[...] + jnp.sum(oh, axis=0, keepdims=True)
    cnt_ref[...] = run_ref[...]

    scale = float(1 << RANK_BITS)
    code1 = i1.astype(F32) * scale + rank1
    code2 = i2.astype(F32) * scale + rank2
    meta = jnp.where(lane == 0, code1, jnp.where(lane == 1, code2, 0.0))
    code_ref[...] = meta.T[0:8, :].astype(I32)
    wcol_ref[...] = jnp.where(lane == 0, w1, jnp.where(lane == 1, w2, 0.0))


def _merge(x2d, ya, o0, l0, o1, l1, o2, l2, gates, wa, wb, wo, g2, wr_hi, wr_lo, br, batch, seq):
    tm = TOK_TILE
    tps = seq // tm
    n_tok = batch * seq

    def tok(c):
        return pl.BlockSpec((tm, c), lambda i: (i, 0))

    def full(a):
        return pl.BlockSpec(a.shape, lambda i: (0,) * a.ndim)

    def res_spec(d):
        return pl.BlockSpec((None, d, tm // d, B_DIM), lambda i: (i // tps, 0, i % tps, 0))

    return pl.pallas_call(
        _merge_kernel,
        grid=(n_tok // tm,),
        in_specs=[tok(D_MODEL), tok(A_Q_DIM), tok(B_DIM), tok(B_DIM),
                  res_spec(4), res_spec(4), res_spec(16), res_spec(16), tok(GATE_COLS),
                  full(wa), full(wb), full(wo), full(g2), full(wr_hi), full(wr_lo), full(br)],
        out_specs=[tok(D_MODEL), tok(D_MODEL),
                   pl.BlockSpec((8, tm), lambda i: (0, i)),
                   tok(LANES),
                   pl.BlockSpec((8, LANES), lambda i: (0, 0))],
        out_shape=[jax.ShapeDtypeStruct((n_tok, D_MODEL), F32),
                   jax.ShapeDtypeStruct((n_tok, D_MODEL), F32),
                   jax.ShapeDtypeStruct((8, n_tok), I32),
                   jax.ShapeDtypeStruct((n_tok, LANES), F32),
                   jax.ShapeDtypeStruct((8, LANES), F32)],
        scratch_shapes=([pltpu.VMEM((B_DIM // LANES, tm, LANES), F32)] * 4
                        + [pltpu.VMEM((tm, B_DIM), BF16), pltpu.VMEM((8, LANES), F32)]),
        compiler_params=_cparams("arbitrary"),
        name="merge_route",
    )(x2d, ya, o0, l0, o1, l1, o2, l2, gates, wa, wb, wo, g2, wr_hi, wr_lo, br)


def _slot(code_ref, offs_ref, idx):
    c = code_ref[idx]
    return offs_ref[c >> RANK_BITS] + (c & ((1 << RANK_BITS) - 1))


def _dispatch_kernel(code_ref, offs_ref, h_ref, xs_in_ref, xs_ref, sem, *, n_tok):
    del xs_in_ref
    i = pl.program_id(0)
    rows = ROW_TILE

    def wait_tile(slot):
        pltpu.make_async_copy(h_ref.at[pl.ds(0, 2 * rows)], xs_ref.at[pl.ds(0, 2 * rows)], sem.at[slot]).wait()

    def issue(j, carry):
        t = i * rows + j
        for k in range(2):
            dst = _slot(code_ref, offs_ref, k * n_tok + t)
            pltpu.make_async_copy(h_ref.at[pl.ds(t, 1)], xs_ref.at[pl.ds(dst, 1)], sem.at[i % 2]).start()
        return carry

    lax.fori_loop(0, rows, issue, 0)

    @pl.when(i > 0)
    def _():
        wait_tile((i + 1) % 2)

    @pl.when(i == pl.num_programs(0) - 1)
    def _():
        wait_tile(i % 2)


def _dispatch(code_flat, offs, h2, xs_init):
    n_tok = h2.shape[0]
    return pl.pallas_call(
        functools.partial(_dispatch_kernel, n_tok=n_tok),
        grid_spec=pltpu.PrefetchScalarGridSpec(
            num_scalar_prefetch=2,
            grid=(n_tok // ROW_TILE,),
            in_specs=[pl.BlockSpec(memory_space=pl.ANY), pl.BlockSpec(memory_space=pl.ANY)],
            out_specs=pl.BlockSpec(memory_space=pl.ANY),
            scratch_shapes=[pltpu.SemaphoreType.DMA((2,))],
        ),
        out_shape=jax.ShapeDtypeStruct(xs_init.shape, xs_init.dtype),
        input_output_aliases={3: 0},
        compiler_params=_cparams("arbitrary"),
        name="dispatch",
    )(code_flat, offs, h2, xs_init)


def _combine_kernel(code_ref, offs_ref, x_ref, w_ref, g_ref, ys_ref, o_ref, buf_ref, sem, *, n_tok, final):
    i = pl.program_id(0)
    rows = ROW_TILE

    def issue(j, carry):
        t = i * rows + j
        for k in range(2):
            src = _slot(code_ref, offs_ref, k * n_tok + t)
            pltpu.make_async_copy(ys_ref.at[pl.ds(src, 1)], buf_ref.at[k, pl.ds(j, 1)], sem).start()
        return carry

    lax.fori_loop(0, rows, issue, 0)
    pltpu.make_async_copy(ys_ref.at[pl.ds(0, 2 * rows)], buf_ref.reshape(2 * rows, D_MODEL), sem).wait()
    w = w_ref[...]
    y = x_ref[...] + w[:, 0:1] * buf_ref[0] + w[:, 1:2] * buf_ref[1]
    if final:
        y = y * lax.rsqrt(jnp.mean(y * y, axis=-1, keepdims=True) + NORM_EPS) * g_ref[...]
    o_ref[...] = y


def _combine(code_flat, offs, x1, wcol, g, ys, final):
    n_tok = x1.shape[0]
    rows = ROW_TILE
    return pl.pallas_call(
        functools.partial(_combine_kernel, n_tok=n_tok, final=final),
        grid_spec=pltpu.PrefetchScalarGridSpec(
            num_scalar_prefetch=2,
            grid=(n_tok // rows,),
            in_specs=[pl.BlockSpec((rows, D_MODEL), lambda i, c, o: (i, 0)),
                      pl.BlockSpec((rows, LANES), lambda i, c, o: (i, 0)),
                      pl.BlockSpec((1, D_MODEL), lambda i, c, o: (0, 0)),
                      pl.BlockSpec(memory_space=pl.ANY)],
            out_specs=pl.BlockSpec((rows, D_MODEL), lambda i, c, o: (i, 0)),
            scratch_shapes=[pltpu.VMEM((2, rows, D_MODEL), F32), pltpu.SemaphoreType.DMA(())],
        ),
        out_shape=jax.ShapeDtypeStruct((n_tok, D_MODEL), F32),
        compiler_params=_cparams("arbitrary"),
        name="combine",
    )(code_flat, offs, x1, wcol, g, ys)


def _experts_kernel(tile_e_ref, n_used_ref, xs_ref, wg_ref, wu_ref, wd_ref, ys_ref):
    del tile_e_ref

    @pl.when(pl.program_id(0) < n_used_ref[0])
    def _():
        x = xs_ref[...].astype(BF16)
        a = jnp.dot(x, wg_ref[...].astype(BF16), preferred_element_type=F32)
        u = jnp.dot(x, wu_ref[...].astype(BF16), preferred_element_type=F32)
        z = (a * jax.nn.sigmoid(a)) * u
        ys_ref[...] = jnp.dot(z.astype(BF16), wd_ref[...].astype(BF16), preferred_element_type=F32)

    @pl.when(pl.program_id(0) >= n_used_ref[0])
    def _():
        ys_ref[...] = jnp.zeros_like(ys_ref)


def _experts(tile_e, n_used, xs, wg, wu, wd):
    n_slots = xs.shape[0]
    te = EXP_TILE

    def row_map(i, tile_e, n_used):
        return (jnp.minimum(i, n_used[0] - 1), 0)

    def out_map(i, tile_e, n_used):
        return (i, 0)

    def w_map(i, tile_e, n_used):
        return (tile_e[i], 0, 0)

    return pl.pallas_call(
        _experts_kernel,
        grid_spec=pltpu.PrefetchScalarGridSpec(
            num_scalar_prefetch=2,
            grid=(n_slots // te,),
            in_specs=[pl.BlockSpec((te, D_MODEL), row_map),
                      pl.BlockSpec((None, D_MODEL, D_EXPERT), w_map),
                      pl.BlockSpec((None, D_MODEL, D_EXPERT), w_map),
                      pl.BlockSpec((None, D_EXPERT, D_MODEL), w_map)],
            out_specs=pl.BlockSpec((te, D_MODEL), out_map),
        ),
        out_shape=jax.ShapeDtypeStruct((n_slots, D_MODEL), F32),
        compiler_params=_cparams("arbitrary"),
        name="experts",
    )(tile_e, n_used, xs, wg, wu, wd)


def kernel(x, attn_norm_g, w_in, a_sink, w_branch_a, w_branch_b, w_out, ffn_norm_g,
           w_router_group, b_router_group, w_router_expert, b_router_expert,
           w_exp_gate, w_exp_up, w_exp_down, final_norm_g):
    batch, seq, d_model = x.shape
    depth = w_in.shape[0]
    n_tok = batch * seq
    assert d_model == D_MODEL and w_in.shape[2] == D_IN
    assert seq % (16 * B_SUB) == 0 and seq % TOK_TILE == 0 and n_tok % ROW_TILE == 0
    assert n_tok < (1 << RANK_BITS)

    cos_t, sin_t = _rope_tables(seq)
    tables = (cos_t, sin_t,
              _residue_order(cos_t, 4, TOK_TILE), _residue_order(sin_t, 4, TOK_TILE),
              _residue_order(cos_t, 16, TOK_TILE), _residue_order(sin_t, 16, TOK_TILE))

    n_slots = 2 * n_tok + N_EXPERTS * EXP_TILE
    n_tiles = n_slots // EXP_TILE
    x2d = x.reshape(n_tok, D_MODEL)

    for l in range(depth):
        nat, gates, grp1, grp2 = _in_proj(x2d, attn_norm_g[l][None, :], w_in[l].astype(BF16), tables, batch, seq)
        nat3d = nat.reshape(batch, seq, NAT_COLS)
        ya = _attn_a(nat3d, a_sink[l]).reshape(n_tok, A_Q_DIM)
        o0, l0 = _attn_b(nat3d, (A_Q_DIM + 2 * A_KV_DIM) // B_DIM, "attn_b1")
        o1, l1 = _attn_b(grp1.reshape(batch * 4, seq // 4, GRP_COLS), 0, "attn_b4")
        o2, l2 = _attn_b(grp2.reshape(batch * 16, seq // 16, GRP_COLS), 0, "attn_b16")

        wr = jnp.zeros((D_MODEL, LANES), F32)
        wr = wr.at[:, 0:N_EXPERTS].set(w_router_expert[l]).at[:, N_EXPERTS:N_EXPERTS + MOE_GROUPS].set(w_router_group[l])
        br = jnp.zeros((1, LANES), F32)
        br = br.at[0, 0:N_EXPERTS].set(b_router_expert[l]).at[0, N_EXPERTS:N_EXPERTS + MOE_GROUPS].set(b_router_group[l])
        wr_hi = wr.astype(BF16)
        wr_lo = (wr - wr_hi.astype(F32)).astype(BF16)

        x1, h2, code, wcol, cnt = _merge(
            x2d, ya, o0.reshape(n_tok, B_DIM), l0.reshape(n_tok, B_DIM),
            o1.reshape(batch, 4, seq // 4, B_DIM), l1.reshape(batch, 4, seq // 4, B_DIM),
            o2.reshape(batch, 16, seq // 16, B_DIM), l2.reshape(batch, 16, seq // 16, B_DIM),
            gates, w_branch_a[l].astype(BF16), w_branch_b[l].astype(BF16), w_out[l].astype(BF16),
            ffn_norm_g[l][None, :], wr_hi, wr_lo, br, batch, seq)

        counts = cnt[0, 0:N_EXPERTS].astype(I32)
        padded = ((counts + EXP_TILE - 1) // EXP_TILE) * EXP_TILE
        ends = jnp.cumsum(padded)
        offs = ends - padded
        n_used = (ends[-1:] // EXP_TILE).astype(I32)
        tile_e = jnp.minimum(
            jnp.searchsorted(ends, jnp.arange(n_tiles, dtype=I32) * EXP_TILE, side="right"), N_EXPERTS - 1).astype(I32)
        code_flat = code[0:2].reshape(-1)

        xs = _dispatch(code_flat, offs, h2, jnp.zeros((n_slots, D_MODEL), F32))
        ys = _experts(tile_e, n_used, xs, w_exp_gate[l], w_exp_up[l], w_exp_down[l])
        x2d = _combine(code_flat, offs, x1, wcol, final_norm_g[None, :], ys, final=(l == depth - 1))

    return x2d.reshape(batch, seq, D_MODEL)
```

```python
import functools

import jax
import jax.numpy as jnp
from jax import lax
from jax.experimental import pallas as pl
from jax.experimental.pallas import tpu as pltpu

F32 = jnp.float32
BF16 = jnp.bfloat16
I32 = jnp.int32

D_MODEL = 1024
HEAD_DIM = 64
HALF_HEAD = HEAD_DIM // 2
ROPE_THETA = 10000.0
NORM_EPS = 1e-6
NEG_INF = -1e30
LANES = 128

A_Q_HEADS = 8
A_KV_HEADS = 2
A_GROUP = A_Q_HEADS // A_KV_HEADS
A_HALF_WINDOW = 128
A_Q_DIM = A_Q_HEADS * HEAD_DIM
A_KV_DIM = A_KV_HEADS * HEAD_DIM

B_GROUPS = ((128, 1), (512, 4), (2048, 16))
B_HEADS = 4
B_DIM = B_HEADS * HEAD_DIM
B_HALF_WINDOW = 64

MOE_GROUPS = 4
EXPERTS_PER_GROUP = 8
N_EXPERTS = MOE_GROUPS * EXPERTS_PER_GROUP
D_EXPERT = 256

NAT_COLS = A_Q_DIM + 2 * A_KV_DIM + 3 * B_DIM
GRP_COLS = 3 * B_DIM
COL_G1 = NAT_COLS
COL_G2 = COL_G1 + GRP_COLS
COL_GATE = COL_G2 + GRP_COLS
GATE_COLS = 2 * D_MODEL
D_IN = COL_GATE + GATE_COLS

TOK_TILE = 512
A_Q_TILE = 256
B_SUB = 128
EXP_TILE = 256
ROW_TILE = 256
DISPATCH_TILE = 1024
RANK_BITS = 16
VMEM_LIMIT = 56 * 1024 * 1024


def _cparams(*sem):
    return pltpu.CompilerParams(dimension_semantics=sem, vmem_limit_bytes=VMEM_LIMIT)


def _rope_tables(seq_len):
    inv = 1.0 / (ROPE_THETA ** (jnp.arange(0, HEAD_DIM, 2, dtype=F32) / HEAD_DIM))
    ang = jnp.arange(seq_len, dtype=F32)[:, None] * inv[None, :]
    cos, sin = jnp.cos(ang), jnp.sin(ang)
    cos_t = jnp.concatenate([cos, cos, cos, cos], axis=-1)
    sin_t = jnp.concatenate([-sin, sin, -sin, sin], axis=-1)
    return cos_t, sin_t


def _residue_order(table, dilation, tile):
    s, c = table.shape
    return table.reshape(s // tile, tile // dilation, dilation, c).transpose(0, 2, 1, 3).reshape(s, c)


def _rope(t, cos, sin_signed, first_half):
    partner = jnp.where(first_half, pltpu.roll(t, LANES - HALF_HEAD, 1), pltpu.roll(t, HALF_HEAD, 1))
    return t * cos + partner * sin_signed


Q_KIND, K_KIND, V_KIND = 0, 1, 2
_NAT_KINDS = ([Q_KIND] * 4 + [K_KIND] + [V_KIND] + [Q_KIND] * 2 + [K_KIND] * 2 + [V_KIND] * 2)
_GRP_KINDS = [Q_KIND] * 2 + [K_KIND] * 2 + [V_KIND] * 2


def _in_proj_kernel(x_ref, g_ref, w_ref, c1_ref, s1_ref, c4_ref, s4_ref, c16_ref, s16_ref,
                    nat_ref, gate_ref, g1_ref, g2_ref, hf_ref, hb_ref, hd_ref):
    tm = x_ref.shape[0]
    x = x_ref[...]
    h = x * lax.rsqrt(jnp.mean(x * x, axis=-1, keepdims=True) + NORM_EPS) * g_ref[...]
    n_chunks = D_MODEL // LANES
    for c in range(n_chunks):
        hf_ref[c] = h[:, c * LANES:(c + 1) * LANES]
    hb_ref[...] = h.astype(BF16)
    lane = lax.broadcasted_iota(I32, (1, LANES), 1)
    first_half = (lane % HEAD_DIM) < HALF_HEAD

    def project(h_b, col0, kinds, cos_ref, sin_ref, store):
        width = 512
        for c0 in range(0, len(kinds) * LANES, width):
            w = min(width, len(kinds) * LANES - c0)
            res = jnp.dot(h_b, w_ref[:, col0 + c0:col0 + c0 + w], preferred_element_type=F32)
            for j in range(w // LANES):
                kind = kinds[(c0 // LANES) + j]
                t = res[:, j * LANES:(j + 1) * LANES]
                if kind != V_KIND:
                    t = _rope(t, cos_ref[...], sin_ref[...], first_half)
                if kind == Q_KIND:
                    t = t * (HEAD_DIM ** -0.5)
                store(c0 + j * LANES, t.astype(BF16))

    def store_nat(c, v):
        nat_ref[:, c:c + LANES] = v

    project(hb_ref[...], 0, _NAT_KINDS, c1_ref, s1_ref, store_nat)

    for c0 in range(0, GATE_COLS, 512):
        res = jnp.dot(hb_ref[...], w_ref[:, COL_GATE + c0:COL_GATE + c0 + 512], preferred_element_type=F32)
        gate_ref[:, c0:c0 + 512] = jax.nn.sigmoid(res).astype(BF16)

    for dil, col0, cos_ref, sin_ref, out_ref in ((4, COL_G1, c4_ref, s4_ref, g1_ref),
                                                 (16, COL_G2, c16_ref, s16_ref, g2_ref)):
        n = tm // dil
        for r in range(dil):
            for c in range(n_chunks):
                hd_ref[r * n:(r + 1) * n, c * LANES:(c + 1) * LANES] = (
                    hf_ref[c, pl.ds(r, n, stride=dil), :].astype(BF16))

        def store_grp(c, v, out_ref=out_ref, dil=dil, n=n):
            for r in range(dil):
                out_ref[r, :, c:c + LANES] = v[r * n:(r + 1) * n]

        project(hd_ref[...], col0, _GRP_KINDS, cos_ref, sin_ref, store_grp)


def _in_proj(x2d, g, w_bf16, tables, batch, seq):
    tm = TOK_TILE
    tiles_per_seq = seq // tm
    n_tok = batch * seq
    c1, s1, c4, s4, c16, s16 = tables
    tab_spec = pl.BlockSpec((tm, LANES), lambda i: (i % tiles_per_seq, 0))
    return pl.pallas_call(
        _in_proj_kernel,
        grid=(n_tok // tm,),
        in_specs=[
            pl.BlockSpec((tm, D_MODEL), lambda i: (i, 0)),
            pl.BlockSpec((1, D_MODEL), lambda i: (0, 0)),
            pl.BlockSpec((D_MODEL, D_IN), lambda i: (0, 0), pipeline_mode=pl.Buffered(1)),
            tab_spec, tab_spec, tab_spec, tab_spec, tab_spec, tab_spec,
        ],
        out_specs=[
            pl.BlockSpec((tm, NAT_COLS), lambda i: (i, 0)),
            pl.BlockSpec((tm, GATE_COLS), lambda i: (i, 0)),
            pl.BlockSpec((None, 4, tm // 4, GRP_COLS), lambda i: (i // tiles_per_seq, 0, i % tiles_per_seq, 0)),
            pl.BlockSpec((None, 16, tm // 16, GRP_COLS), lambda i: (i // tiles_per_seq, 0, i % tiles_per_seq, 0)),
        ],
        out_shape=[
            jax.ShapeDtypeStruct((n_tok, NAT_COLS), BF16),
            jax.ShapeDtypeStruct((n_tok, GATE_COLS), BF16),
            jax.ShapeDtypeStruct((batch, 4, seq // 4, GRP_COLS), BF16),
            jax.ShapeDtypeStruct((batch, 16, seq // 16, GRP_COLS), BF16),
        ],
        scratch_shapes=[
            pltpu.VMEM((D_MODEL // LANES, tm, LANES), F32),
            pltpu.VMEM((tm, D_MODEL), BF16),
            pltpu.VMEM((tm, D_MODEL), BF16),
        ],
        compiler_params=_cparams("parallel"),
        name="in_proj",
    )(x2d, g, w_bf16, c1, s1, c4, s4, c16, s16)


def _attn_a_kernel(sink_ref, q_ref, kvp_ref, kvm_ref, kvn_ref, o_ref, kv_ref, *, seq):
    tq = q_ref.shape[0]
    hw = A_HALF_WINDOW
    kw = tq + 2 * hw
    q0 = pl.program_id(1) * tq
    kv_ref[0:hw, :] = kvp_ref[...]
    kv_ref[hw:hw + tq, :] = kvm_ref[...]
    kv_ref[hw + tq:kw, :] = kvn_ref[...]
    qi = lax.broadcasted_iota(I32, (tq, kw), 0)
    kj = lax.broadcasted_iota(I32, (tq, kw), 1)
    kpos = q0 - hw + kj
    valid = (jnp.abs(kj - hw - qi) <= hw) & (kpos >= 0) & (kpos < seq)
    for h in range(A_Q_HEADS):
        hk = h // A_GROUP
        q = q_ref[:, h * HEAD_DIM:(h + 1) * HEAD_DIM]
        k = kv_ref[:, hk * HEAD_DIM:(hk + 1) * HEAD_DIM]
        v = kv_ref[:, A_KV_DIM + hk * HEAD_DIM:A_KV_DIM + (hk + 1) * HEAD_DIM]
        s = lax.dot_general(q, k, (((1,), (1,)), ((), ())), preferred_element_type=F32)
        s = jnp.where(valid, s, NEG_INF)
        sink = sink_ref[h]
        m = jnp.maximum(jnp.max(s, axis=-1, keepdims=True), sink)
        p = jnp.exp(s - m)
        denom = jnp.sum(p, axis=-1, keepdims=True) + jnp.exp(sink - m)
        acc = jnp.dot(p.astype(BF16), v, preferred_element_type=F32)
        o_ref[:, h * HEAD_DIM:(h + 1) * HEAD_DIM] = (acc / denom).astype(BF16)


def _attn_a(nat3d, sink):
    batch, seq, _ = nat3d.shape
    tq = A_Q_TILE
    hw = A_HALF_WINDOW
    per = tq // hw
    n_hw = seq // hw
    kv_blk = A_Q_DIM // (2 * A_KV_DIM)
    return pl.pallas_call(
        functools.partial(_attn_a_kernel, seq=seq),
        grid=(batch, seq // tq),
        in_specs=[
            pl.BlockSpec(memory_space=pltpu.SMEM),
            pl.BlockSpec((None, tq, A_Q_DIM), lambda b, i: (b, i, 0)),
            pl.BlockSpec((None, hw, 2 * A_KV_DIM), lambda b, i: (b, jnp.maximum(i * per - 1, 0), kv_blk)),
            pl.BlockSpec((None, tq, 2 * A_KV_DIM), lambda b, i: (b, i, kv_blk)),
            pl.BlockSpec((None, hw, 2 * A_KV_DIM), lambda b, i: (b, jnp.minimum((i + 1) * per, n_hw - 1), kv_blk)),
        ],
        out_specs=pl.BlockSpec((None, tq, A_Q_DIM), lambda b, i: (b, i, 0)),
        out_shape=jax.ShapeDtypeStruct((batch, seq, A_Q_DIM), BF16),
        scratch_shapes=[pltpu.VMEM((tq + 2 * hw, 2 * A_KV_DIM), BF16)],
        compiler_params=_cparams("parallel", "parallel"),
        name="attn_a",
    )(sink, nat3d, nat3d, nat3d, nat3d)


def _attn_b_kernel(q_ref, kp_ref, km_ref, kn_ref, vp_ref, vm_ref, vn_ref, o_ref, lse_ref,
                   k_ref, v_ref, *, sub_len):
    tq = q_ref.shape[0]
    hw = B_HALF_WINDOW
    t0 = pl.program_id(1) * tq
    for dst, prev, main, nxt in ((k_ref, kp_ref, km_ref, kn_ref), (v_ref, vp_ref, vm_ref, vn_ref)):
        dst[0:hw, :] = prev[...]
        dst[hw:hw + tq, :] = main[...]
        dst[hw + tq:tq + 2 * hw, :] = nxt[...]
    kw = B_SUB + 2 * hw
    qi = lax.broadcasted_iota(I32, (B_SUB, kw), 0)
    kj = lax.broadcasted_iota(I32, (B_SUB, kw), 1)
    in_band = jnp.abs(kj - hw - qi) <= hw
    for sb in range(tq // B_SUB):
        r0 = sb * B_SUB
        kpos = t0 + r0 - hw + kj
        valid = in_band & (kpos >= 0) & (kpos < sub_len)
        for h in range(B_HEADS):
            cs = slice(h * HEAD_DIM, (h + 1) * HEAD_DIM)
            q = q_ref[r0:r0 + B_SUB, cs]
            k = k_ref[r0:r0 + kw, cs]
            v = v_ref[r0:r0 + kw, cs]
            s = lax.dot_general(q, k, (((1,), (1,)), ((), ())), preferred_element_type=F32)
            s = jnp.where(valid, s, NEG_INF)
            m = jnp.max(s, axis=-1, keepdims=True)
            p = jnp.exp(s - m)
            denom = jnp.sum(p, axis=-1, keepdims=True)
            acc = jnp.dot(p.astype(BF16), v, preferred_element_type=F32)
            o_ref[r0:r0 + B_SUB, cs] = acc / denom
            lse_ref[r0:r0 + B_SUB, cs] = jnp.broadcast_to(m + jnp.log(denom), (B_SUB, HEAD_DIM))


def _attn_b(arr3d, q_blk, name):
    n_sub, sub_len, _ = arr3d.shape
    tq = min(512, sub_len)
    hw = B_HALF_WINDOW
    per = tq // hw
    n_hw = sub_len // hw

    def main_spec(c):
        return pl.BlockSpec((None, tq, B_DIM), lambda g, i: (g, i, c))

    def prev_spec(c):
        return pl.BlockSpec((None, hw, B_DIM), lambda g, i: (g, jnp.maximum(i * per - 1, 0), c))

    def next_spec(c):
        return pl.BlockSpec((None, hw, B_DIM), lambda g, i: (g, jnp.minimum((i + 1) * per, n_hw - 1), c))

    out_spec = pl.BlockSpec((None, tq, B_DIM), lambda g, i: (g, i, 0))
    return pl.pallas_call(
        functools.partial(_attn_b_kernel, sub_len=sub_len),
        grid=(n_sub, sub_len // tq),
        in_specs=[main_spec(q_blk),
                  prev_spec(q_blk + 1), main_spec(q_blk + 1), next_spec(q_blk + 1),
                  prev_spec(q_blk + 2), main_spec(q_blk + 2), next_spec(q_blk + 2)],
        out_specs=[out_spec, out_spec],
        out_shape=[jax.ShapeDtypeStruct((n_sub, sub_len, B_DIM), F32)] * 2,
        scratch_shapes=[pltpu.VMEM((tq + 2 * hw, B_DIM), BF16)] * 2,
        compiler_params=_cparams("parallel", "parallel"),
        name=name,
    )(arr3d, arr3d, arr3d, arr3d, arr3d, arr3d, arr3d)


def _merge_kernel(x_ref, ya_ref, o0_ref, l0_ref, o1_ref, l1_ref, o2_ref, l2_ref, gate_ref,
                  wa_ref, wb_ref, wo_ref, g2_ref, wrh_ref, wrl_ref, br_ref,
                  x1_ref, h2_ref, code_ref, wcol_ref, cnt_ref,
                  so1_ref, sl1_ref, so2_ref, sl2_ref, yb_ref, run_ref):
    tm = x_ref.shape[0]

    @pl.when(pl.program_id(0) == 0)
    def _():
        run_ref[...] = jnp.zeros_like(run_ref)

    for dil, src_o, src_l, dst_o, dst_l in ((4, o1_ref, l1_ref, so1_ref, sl1_ref),
                                            (16, o2_ref, l2_ref, so2_ref, sl2_ref)):
        n = tm // dil
        for r in range(dil):
            for c in range(B_DIM // LANES):
                cs = slice(c * LANES, (c + 1) * LANES)
                dst_o[c, pl.ds(r, n, stride=dil), :] = src_o[r, :, cs]
                dst_l[c, pl.ds(r, n, stride=dil), :] = src_l[r, :, cs]

    for c in range(B_DIM // LANES):
        cs = slice(c * LANES, (c + 1) * LANES)
        l0, l1, l2 = l0_ref[:, cs], sl1_ref[c], sl2_ref[c]
        m = jnp.maximum(jnp.maximum(l0, l1), l2)
        e0, e1, e2 = jnp.exp(l0 - m), jnp.exp(l1 - m), jnp.exp(l2 - m)
        yb = (e0 * o0_ref[:, cs] + e1 * so1_ref[c] + e2 * so2_ref[c]) / (e0 + e1 + e2)
        yb_ref[:, cs] = yb.astype(BF16)

    ya_p = jnp.dot(ya_ref[...], wa_ref[...], preferred_element_type=F32)
    yb_p = jnp.dot(yb_ref[...], wb_ref[...], preferred_element_type=F32)
    merged = gate_ref[:, 0:D_MODEL].astype(F32) * ya_p + gate_ref[:, D_MODEL:GATE_COLS].astype(F32) * yb_p
    x1 = x_ref[...] + jnp.dot(merged.astype(BF16), wo_ref[...], preferred_element_type=F32)
    x1_ref[...] = x1

    h2 = x1 * lax.rsqrt(jnp.mean(x1 * x1, axis=-1, keepdims=True) + NORM_EPS) * g2_ref[...]
    h2_ref[...] = h2

    h_hi = h2.astype(BF16)
    h_lo = (h2 - h_hi.astype(F32)).astype(BF16)
    logits = (jnp.dot(h_hi, wrh_ref[...], preferred_element_type=F32)
              + jnp.dot(h_lo, wrh_ref[...], preferred_element_type=F32)
              + jnp.dot(h_hi, wrl_ref[...], preferred_element_type=F32)) + br_ref[...]

    lane = lax.broadcasted_iota(I32, (tm, LANES), 1)
    is_grp = (lane >= N_EXPERTS) & (lane < N_EXPERTS + MOE_GROUPS)
    gl = jnp.where(is_grp, logits, -jnp.inf)
    gmax = jnp.max(gl, axis=-1, keepdims=True)
    gidx = jnp.min(jnp.where(gl == gmax, lane - N_EXPERTS, LANES), axis=-1, keepdims=True)
    gw = 1.0 / jnp.sum(jnp.exp(gl - gmax), axis=-1, keepdims=True)
    in_grp = (lane < N_EXPERTS) & ((lane // EXPERTS_PER_GROUP) == gidx)
    el = jnp.where(in_grp, logits, -jnp.inf)
    v1 = jnp.max(el, axis=-1, keepdims=True)
    i1 = jnp.min(jnp.where(el == v1, lane, LANES), axis=-1, keepdims=True)
    el2 = jnp.where(lane == i1, -jnp.inf, el)
    v2 = jnp.max(el2, axis=-1, keepdims=True)
    i2 = jnp.min(jnp.where(el2 == v2, lane, LANES), axis=-1, keepdims=True)
    t = jnp.exp(v2 - v1)
    w1 = gw / (1.0 + t)
    w2 = gw * t / (1.0 + t)

    oh1 = (lane == i1).astype(F32)
    oh2 = (lane == i2).astype(F32)
    oh = oh1 + oh2
    row = lax.broadcasted_iota(I32, (tm, tm), 0)
    col = lax.broadcasted_iota(I32, (tm, tm), 1)
    lower = (col < row).astype(BF16)
    before = jnp.dot(lower, oh.astype(BF16), preferred_element_type=F32) + run_ref[0:1, :]
    rank1 = jnp.sum(before * oh1, axis=-1, keepdims=True)
    rank2 = jnp.sum(before * oh2, axis=-1, keepdims=True)
    run_ref[...] = run_ref[...] + jnp.sum(oh, axis=0, keepdims=True)
    cnt_ref[...] = run_ref[...]

    scale = float(1 << RANK_BITS)
    code1 = i1.astype(F32) * scale + rank1
    code2 = i2.astype(F32) * scale + rank2
    meta = jnp.where(lane == 0, code1, jnp.where(lane == 1, code2, 0.0))
    code_ref[...] = meta.T[0:8, :].astype(I32)
    wcol_ref[...] = jnp.where(lane == 0, w1, jnp.where(lane == 1, w2, 0.0))


def _merge(x2d, ya, o0, l0, o1, l1, o2, l2, gates, wa, wb, wo, g2, wr_hi, wr_lo, br, batch, seq):
    tm = TOK_TILE
    tps = seq // tm
    n_tok = batch * seq

    def tok(c):
        return pl.BlockSpec((tm, c), lambda i: (i, 0))

    def full(a):
        return pl.BlockSpec(a.shape, lambda i: (0,) * a.ndim)

    def res_spec(d):
        return pl.BlockSpec((None, d, tm // d, B_DIM), lambda i: (i // tps, 0, i % tps, 0))

    return pl.pallas_call(
        _merge_kernel,
        grid=(n_tok // tm,),
        in_specs=[tok(D_MODEL), tok(A_Q_DIM), tok(B_DIM), tok(B_DIM),
                  res_spec(4), res_spec(4), res_spec(16), res_spec(16), tok(GATE_COLS),
                  full(wa), full(wb), full(wo), full(g2), full(wr_hi), full(wr_lo), full(br)],
        out_specs=[tok(D_MODEL), tok(D_MODEL),
                   pl.BlockSpec((8, tm), lambda i: (0, i)),
                   tok(LANES),
                   pl.BlockSpec((8, LANES), lambda i: (0, 0))],
        out_shape=[jax.ShapeDtypeStruct((n_tok, D_MODEL), F32),
                   jax.ShapeDtypeStruct((n_tok, D_MODEL), F32),
                   jax.ShapeDtypeStruct((8, n_tok), I32),
                   jax.ShapeDtypeStruct((n_tok, LANES), F32),
                   jax.ShapeDtypeStruct((8, LANES), F32)],
        scratch_shapes=([pltpu.VMEM((B_DIM // LANES, tm, LANES), F32)] * 4
                        + [pltpu.VMEM((tm, B_DIM), BF16), pltpu.VMEM((8, LANES), F32)]),
        compiler_params=_cparams("arbitrary"),
        name="merge_route",
    )(x2d, ya, o0, l0, o1, l1, o2, l2, gates, wa, wb, wo, g2, wr_hi, wr_lo, br)


def _slot(code_ref, offs_ref, idx):
    c = code_ref[idx]
    return offs_ref[c >> RANK_BITS] + (c & ((1 << RANK_BITS) - 1))


def _dispatch_kernel(code_ref, offs_ref, h_ref, xs_in_ref, xs_ref, sem, *, n_tok):
    del xs_in_ref
    i = pl.program_id(0)
    rows = h_ref.shape[0]

    def issue(j, carry):
        t = i * rows + j
        for k in range(2):
            dst = _slot(code_ref, offs_ref, k * n_tok + t)
            pltpu.make_async_copy(h_ref.at[pl.ds(j, 1)], xs_ref.at[pl.ds(dst, 1)], sem).start()
        return carry

    lax.fori_loop(0, rows, issue, 0)
    for _ in range(2):
        pltpu.make_async_copy(h_ref, xs_ref.at[pl.ds(0, rows)], sem).wait()


def _dispatch(code_flat, offs, h2, xs_init):
    n_tok = h2.shape[0]
    rows = DISPATCH_TILE
    return pl.pallas_call(
        functools.partial(_dispatch_kernel, n_tok=n_tok),
        grid_spec=pltpu.PrefetchScalarGridSpec(
            num_scalar_prefetch=2,
            grid=(n_tok // rows,),
            in_specs=[pl.BlockSpec((rows, D_MODEL), lambda i, c, o: (i, 0)), pl.BlockSpec(memory_space=pl.ANY)],
            out_specs=pl.BlockSpec(memory_space=pl.ANY),
            scratch_shapes=[pltpu.SemaphoreType.DMA(())],
        ),
        out_shape=jax.ShapeDtypeStruct(xs_init.shape, xs_init.dtype),
        input_output_aliases={3: 0},
        compiler_params=_cparams("arbitrary"),
        name="dispatch",
    )(code_flat, offs, h2, xs_init)


def _combine_kernel(code_ref, offs_ref, x_ref, w_ref, g_ref, ys_ref, o_ref, buf_ref, sem, *, n_tok, final):
    i = pl.program_id(0)
    rows = ROW_TILE

    def issue(j, carry):
        t = i * rows + j
        for k in range(2):
            src = _slot(code_ref, offs_ref, k * n_tok + t)
            pltpu.make_async_copy(ys_ref.at[pl.ds(src, 1)], buf_ref.at[k, pl.ds(j, 1)], sem).start()
        return carry

    lax.fori_loop(0, rows, issue, 0)
    pltpu.make_async_copy(ys_ref.at[pl.ds(0, 2 * rows)], buf_ref.reshape(2 * rows, D_MODEL), sem).wait()
    w = w_ref[...]
    y = x_ref[...] + w[:, 0:1] * buf_ref[0] + w[:, 1:2] * buf_ref[1]
    if final:
        y = y * lax.rsqrt(jnp.mean(y * y, axis=-1, keepdims=True) + NORM_EPS) * g_ref[...]
    o_ref[...] = y


def _combine(code_flat, offs, x1, wcol, g, ys, final):
    n_tok = x1.shape[0]
    rows = ROW_TILE
    return pl.pallas_call(
        functools.partial(_combine_kernel, n_tok=n_tok, final=final),
        grid_spec=pltpu.PrefetchScalarGridSpec(
            num_scalar_prefetch=2,
            grid=(n_tok // rows,),
            in_specs=[pl.BlockSpec((rows, D_MODEL), lambda i, c, o: (i, 0)),
                      pl.BlockSpec((rows, LANES), lambda i, c, o: (i, 0)),
                      pl.BlockSpec((1, D_MODEL), lambda i, c, o: (0, 0)),
                      pl.BlockSpec(memory_space=pl.ANY)],
            out_specs=pl.BlockSpec((rows, D_MODEL), lambda i, c, o: (i, 0)),
            scratch_shapes=[pltpu.VMEM((2, rows, D_MODEL), F32), pltpu.SemaphoreType.DMA(())],
        ),
        out_shape=jax.ShapeDtypeStruct((n_tok, D_MODEL), F32),
        compiler_params=_cparams("arbitrary"),
        name="combine",
    )(code_flat, offs, x1, wcol, g, ys)


def _experts_kernel(tile_e_ref, n_used_ref, xs_ref, wg_ref, wu_ref, wd_ref, ys_ref):
    del tile_e_ref

    @pl.when(pl.program_id(0) < n_used_ref[0])
    def _():
        x = xs_ref[...].astype(BF16)
        a = jnp.dot(x, wg_ref[...].astype(BF16), preferred_element_type=F32)
        u = jnp.dot(x, wu_ref[...].astype(BF16), preferred_element_type=F32)
        z = (a * jax.nn.sigmoid(a)) * u
        ys_ref[...] = jnp.dot(z.astype(BF16), wd_ref[...].astype(BF16), preferred_element_type=F32)

    @pl.when(pl.program_id(0) >= n_used_ref[0])
    def _():
        ys_ref[...] = jnp.zeros_like(ys_ref)


def _experts(tile_e, n_used, xs, wg, wu, wd, layer):
    n_slots = xs.shape[0]
    te = EXP_TILE

    def row_map(i, tile_e, n_used):
        return (jnp.minimum(i, n_used[0] - 1), 0)

    def out_map(i, tile_e, n_used):
        return (i, 0)

    def w_map(i, tile_e, n_used):
        return (layer, tile_e[i], 0, 0)

    return pl.pallas_call(
        _experts_kernel,
        grid_spec=pltpu.PrefetchScalarGridSpec(
            num_scalar_prefetch=2,
            grid=(n_slots // te,),
            in_specs=[pl.BlockSpec((te, D_MODEL), row_map),
                      pl.BlockSpec((None, None, D_MODEL, D_EXPERT), w_map),
                      pl.BlockSpec((None, None, D_MODEL, D_EXPERT), w_map),
                      pl.BlockSpec((None, None, D_EXPERT, D_MODEL), w_map)],
            out_specs=pl.BlockSpec((te, D_MODEL), out_map),
        ),
        out_shape=jax.ShapeDtypeStruct((n_slots, D_MODEL), F32),
        compiler_params=_cparams("arbitrary"),
        name="experts",
    )(tile_e, n_used, xs, wg, wu, wd)


def kernel(x, attn_norm_g, w_in, a_sink, w_branch_a, w_branch_b, w_out, ffn_norm_g,
           w_router_group, b_router_group, w_router_expert, b_router_expert,
           w_exp_gate, w_exp_up, w_exp_down, final_norm_g):
    batch, seq, d_model = x.shape
    depth = w_in.shape[0]
    n_tok = batch * seq
    assert d_model == D_MODEL and w_in.shape[2] == D_IN
    assert seq % (16 * B_SUB) == 0 and seq % TOK_TILE == 0 and n_tok % ROW_TILE == 0
    assert n_tok < (1 << RANK_BITS)

    cos_t, sin_t = _rope_tables(seq)
    tables = (cos_t, sin_t,
              _residue_order(cos_t, 4, TOK_TILE), _residue_order(sin_t, 4, TOK_TILE),
              _residue_order(cos_t, 16, TOK_TILE), _residue_order(sin_t, 16, TOK_TILE))

    n_slots = 2 * n_tok + N_EXPERTS * EXP_TILE
    n_tiles = n_slots // EXP_TILE
    x2d = x.reshape(n_tok, D_MODEL)

    for l in range(depth):
        nat, gates, grp1, grp2 = _in_proj(x2d, attn_norm_g[l][None, :], w_in[l].astype(BF16), tables, batch, seq)
        nat3d = nat.reshape(batch, seq, NAT_COLS)
        ya = _attn_a(nat3d, a_sink[l]).reshape(n_tok, A_Q_DIM)
        o0, l0 = _attn_b(nat3d, (A_Q_DIM + 2 * A_KV_DIM) // B_DIM, "attn_b1")
        o1, l1 = _attn_b(grp1.reshape(batch * 4, seq // 4, GRP_COLS), 0, "attn_b4")
        o2, l2 = _attn_b(grp2.reshape(batch * 16, seq // 16, GRP_COLS), 0, "attn_b16")

        wr = jnp.zeros((D_MODEL, LANES), F32)
        wr = wr.at[:, 0:N_EXPERTS].set(w_router_expert[l]).at[:, N_EXPERTS:N_EXPERTS + MOE_GROUPS].set(w_router_group[l])
        br = jnp.zeros((1, LANES), F32)
        br = br.at[0, 0:N_EXPERTS].set(b_router_expert[l]).at[0, N_EXPERTS:N_EXPERTS + MOE_GROUPS].set(b_router_group[l])
        wr_hi = wr.astype(BF16)
        wr_lo = (wr - wr_hi.astype(F32)).astype(BF16)

        x1, h2, code, wcol, cnt = _merge(
            x2d, ya, o0.reshape(n_tok, B_DIM), l0.reshape(n_tok, B_DIM),
            o1.reshape(batch, 4, seq // 4, B_DIM), l1.reshape(batch, 4, seq // 4, B_DIM),
            o2.reshape(batch, 16, seq // 16, B_DIM), l2.reshape(batch, 16, seq // 16, B_DIM),
            gates, w_branch_a[l].astype(BF16), w_branch_b[l].astype(BF16), w_out[l].astype(BF16),
            ffn_norm_g[l][None, :], wr_hi, wr_lo, br, batch, seq)

        counts = cnt[0, 0:N_EXPERTS].astype(I32)
        padded = ((counts + EXP_TILE - 1) // EXP_TILE) * EXP_TILE
        ends = jnp.cumsum(padded)
        offs = ends - padded
        n_used = (ends[-1:] // EXP_TILE).astype(I32)
        tile_start = jnp.arange(n_tiles, dtype=I32) * EXP_TILE
        tile_e = jnp.minimum(jnp.sum((ends[None, :] <= tile_start[:, None]).astype(I32), axis=1), N_EXPERTS - 1)
        code_flat = code[0:2].reshape(-1)

        xs = _dispatch(code_flat, offs, h2, jnp.zeros((n_slots, D_MODEL), F32))
        ys = _experts(tile_e, n_used, xs, w_exp_gate, w_exp_up, w_exp_down, l)
        x2d = _combine(code_flat, offs, x1, wcol, final_norm_g[None, :], ys, final=(l == depth - 1))

    return x2d.reshape(batch, seq, D_MODEL)
```

```python
import functools

import jax
import jax.numpy as jnp
from jax import lax
from jax.experimental import pallas as pl
from jax.experimental.pallas import tpu as pltpu

F32 = jnp.float32
BF16 = jnp.bfloat16
I32 = jnp.int32

D_MODEL = 1024
HEAD_DIM = 64
HALF_HEAD = HEAD_DIM // 2
ROPE_THETA = 10000.0
NORM_EPS = 1e-6
NEG_INF = -1e30
LANES = 128

A_Q_HEADS = 8
A_KV_HEADS = 2
A_GROUP = A_Q_HEADS // A_KV_HEADS
A_HALF_WINDOW = 128
A_Q_DIM = A_Q_HEADS * HEAD_DIM
A_KV_DIM = A_KV_HEADS * HEAD_DIM

B_GROUPS = ((128, 1), (512, 4), (2048, 16))
B_HEADS = 4
B_DIM = B_HEADS * HEAD_DIM
B_HALF_WINDOW = 64

MOE_GROUPS = 4
EXPERTS_PER_GROUP = 8
N_EXPERTS = MOE_GROUPS * EXPERTS_PER_GROUP
D_EXPERT = 256

NAT_COLS = A_Q_DIM + 2 * A_KV_DIM + 3 * B_DIM
GRP_COLS = 3 * B_DIM
COL_G1 = NAT_COLS
COL_G2 = COL_G1 + GRP_COLS
COL_GATE = COL_G2 + GRP_COLS
GATE_COLS = 2 * D_MODEL
D_IN = COL_GATE + GATE_COLS

TOK_TILE = 512
A_Q_TILE = 256
B_SUB = 128
EXP_TILE = 256
ROW_TILE = 256
DISPATCH_TILE = 1024
ROW_CHUNKS = D_MODEL // LANES
DMA_UNROLL = 8
RANK_BITS = 16
VMEM_LIMIT = 56 * 1024 * 1024


def _cparams(*sem):
    return pltpu.CompilerParams(dimension_semantics=sem, vmem_limit_bytes=VMEM_LIMIT)


def _rope_tables(seq_len):
    inv = 1.0 / (ROPE_THETA ** (jnp.arange(0, HEAD_DIM, 2, dtype=F32) / HEAD_DIM))
    ang = jnp.arange(seq_len, dtype=F32)[:, None] * inv[None, :]
    cos, sin = jnp.cos(ang), jnp.sin(ang)
    cos_t = jnp.concatenate([cos, cos, cos, cos], axis=-1)
    sin_t = jnp.concatenate([-sin, sin, -sin, sin], axis=-1)
    return cos_t, sin_t


def _residue_order(table, dilation, tile):
    s, c = table.shape
    return table.reshape(s // tile, tile // dilation, dilation, c).transpose(0, 2, 1, 3).reshape(s, c)


def _rope(t, cos, sin_signed, first_half):
    partner = jnp.where(first_half, pltpu.roll(t, LANES - HALF_HEAD, 1), pltpu.roll(t, HALF_HEAD, 1))
    return t * cos + partner * sin_signed


Q_KIND, K_KIND, V_KIND = 0, 1, 2
_NAT_KINDS = ([Q_KIND] * 4 + [K_KIND] + [V_KIND] + [Q_KIND] * 2 + [K_KIND] * 2 + [V_KIND] * 2)
_GRP_KINDS = [Q_KIND] * 2 + [K_KIND] * 2 + [V_KIND] * 2


def _in_proj_kernel(x_ref, g_ref, w_ref, c1_ref, s1_ref, c4_ref, s4_ref, c16_ref, s16_ref,
                    nat_ref, gate_ref, g1_ref, g2_ref, hf_ref, hb_ref, hd_ref):
    tm = x_ref.shape[0]
    x = x_ref[...]
    h = x * lax.rsqrt(jnp.mean(x * x, axis=-1, keepdims=True) + NORM_EPS) * g_ref[...]
    n_chunks = D_MODEL // LANES
    for c in range(n_chunks):
        hf_ref[c] = h[:, c * LANES:(c + 1) * LANES]
    hb_ref[...] = h.astype(BF16)
    lane = lax.broadcasted_iota(I32, (1, LANES), 1)
    first_half = (lane % HEAD_DIM) < HALF_HEAD

    def project(h_b, col0, kinds, cos_ref, sin_ref, store):
        width = 512
        for c0 in range(0, len(kinds) * LANES, width):
            w = min(width, len(kinds) * LANES - c0)
            res = jnp.dot(h_b, w_ref[:, col0 + c0:col0 + c0 + w], preferred_element_type=F32)
            for j in range(w // LANES):
                kind = kinds[(c0 // LANES) + j]
                t = res[:, j * LANES:(j + 1) * LANES]
                if kind != V_KIND:
                    t = _rope(t, cos_ref[...], sin_ref[...], first_half)
                if kind == Q_KIND:
                    t = t * (HEAD_DIM ** -0.5)
                store(c0 + j * LANES, t.astype(BF16))

    def store_nat(c, v):
        nat_ref[:, c:c + LANES] = v

    project(hb_ref[...], 0, _NAT_KINDS, c1_ref, s1_ref, store_nat)

    for c0 in range(0, GATE_COLS, 512):
        res = jnp.dot(hb_ref[...], w_ref[:, COL_GATE + c0:COL_GATE + c0 + 512], preferred_element_type=F32)
        gate_ref[:, c0:c0 + 512] = jax.nn.sigmoid(res).astype(BF16)

    for dil, col0, cos_ref, sin_ref, out_ref in ((4, COL_G1, c4_ref, s4_ref, g1_ref),
                                                 (16, COL_G2, c16_ref, s16_ref, g2_ref)):
        n = tm // dil
        for r in range(dil):
            for c in range(n_chunks):
                hd_ref[r * n:(r + 1) * n, c * LANES:(c + 1) * LANES] = (
                    hf_ref[c, pl.ds(r, n, stride=dil), :].astype(BF16))

        def store_grp(c, v, out_ref=out_ref, dil=dil, n=n):
            for r in range(dil):
                out_ref[r, :, c:c + LANES] = v[r * n:(r + 1) * n]

        project(hd_ref[...], col0, _GRP_KINDS, cos_ref, sin_ref, store_grp)


def _in_proj(x2d, g, w_bf16, tables, batch, seq):
    tm = TOK_TILE
    tiles_per_seq = seq // tm
    n_tok = batch * seq
    c1, s1, c4, s4, c16, s16 = tables
    tab_spec = pl.BlockSpec((tm, LANES), lambda i: (i % tiles_per_seq, 0))
    return pl.pallas_call(
        _in_proj_kernel,
        grid=(n_tok // tm,),
        in_specs=[
            pl.BlockSpec((tm, D_MODEL), lambda i: (i, 0)),
            pl.BlockSpec((1, D_MODEL), lambda i: (0, 0)),
            pl.BlockSpec((D_MODEL, D_IN), lambda i: (0, 0), pipeline_mode=pl.Buffered(1)),
            tab_spec, tab_spec, tab_spec, tab_spec, tab_spec, tab_spec,
        ],
        out_specs=[
            pl.BlockSpec((tm, NAT_COLS), lambda i: (i, 0)),
            pl.BlockSpec((tm, GATE_COLS), lambda i: (i, 0)),
            pl.BlockSpec((None, 4, tm // 4, GRP_COLS), lambda i: (i // tiles_per_seq, 0, i % tiles_per_seq, 0)),
            pl.BlockSpec((None, 16, tm // 16, GRP_COLS), lambda i: (i // tiles_per_seq, 0, i % tiles_per_seq, 0)),
        ],
        out_shape=[
            jax.ShapeDtypeStruct((n_tok, NAT_COLS), BF16),
            jax.ShapeDtypeStruct((n_tok, GATE_COLS), BF16),
            jax.ShapeDtypeStruct((batch, 4, seq // 4, GRP_COLS), BF16),
            jax.ShapeDtypeStruct((batch, 16, seq // 16, GRP_COLS), BF16),
        ],
        scratch_shapes=[
            pltpu.VMEM((D_MODEL // LANES, tm, LANES), F32),
            pltpu.VMEM((tm, D_MODEL), BF16),
            pltpu.VMEM((tm, D_MODEL), BF16),
        ],
        compiler_params=_cparams("parallel"),
        name="in_proj",
    )(x2d, g, w_bf16, c1, s1, c4, s4, c16, s16)


def _attn_a_kernel(sink_ref, q_ref, kvp_ref, kvm_ref, kvn_ref, o_ref, kv_ref, *, seq):
    tq = q_ref.shape[0]
    hw = A_HALF_WINDOW
    kw = tq + 2 * hw
    q0 = pl.program_id(1) * tq
    kv_ref[0:hw, :] = kvp_ref[...]
    kv_ref[hw:hw + tq, :] = kvm_ref[...]
    kv_ref[hw + tq:kw, :] = kvn_ref[...]
    qi = lax.broadcasted_iota(I32, (tq, kw), 0)
    kj = lax.broadcasted_iota(I32, (tq, kw), 1)
    kpos = q0 - hw + kj
    valid = (jnp.abs(kj - hw - qi) <= hw) & (kpos >= 0) & (kpos < seq)
    for h in range(A_Q_HEADS):
        hk = h // A_GROUP
        q = q_ref[:, h * HEAD_DIM:(h + 1) * HEAD_DIM]
        k = kv_ref[:, hk * HEAD_DIM:(hk + 1) * HEAD_DIM]
        v = kv_ref[:, A_KV_DIM + hk * HEAD_DIM:A_KV_DIM + (hk + 1) * HEAD_DIM]
        s = lax.dot_general(q, k, (((1,), (1,)), ((), ())), preferred_element_type=F32)
        s = jnp.where(valid, s, NEG_INF)
        sink = sink_ref[h]
        m = jnp.maximum(jnp.max(s, axis=-1, keepdims=True), sink)
        p = jnp.exp(s - m)
        denom = jnp.sum(p, axis=-1, keepdims=True) + jnp.exp(sink - m)
        acc = jnp.dot(p.astype(BF16), v, preferred_element_type=F32)
        o_ref[:, h * HEAD_DIM:(h + 1) * HEAD_DIM] = (acc / denom).astype(BF16)


def _attn_a(nat3d, sink):
    batch, seq, _ = nat3d.shape
    tq = A_Q_TILE
    hw = A_HALF_WINDOW
    per = tq // hw
    n_hw = seq // hw
    kv_blk = A_Q_DIM // (2 * A_KV_DIM)
    return pl.pallas_call(
        functools.partial(_attn_a_kernel, seq=seq),
        grid=(batch, seq // tq),
        in_specs=[
            pl.BlockSpec(memory_space=pltpu.SMEM),
            pl.BlockSpec((None, tq, A_Q_DIM), lambda b, i: (b, i, 0)),
            pl.BlockSpec((None, hw, 2 * A_KV_DIM), lambda b, i: (b, jnp.maximum(i * per - 1, 0), kv_blk)),
            pl.BlockSpec((None, tq, 2 * A_KV_DIM), lambda b, i: (b, i, kv_blk)),
            pl.BlockSpec((None, hw, 2 * A_KV_DIM), lambda b, i: (b, jnp.minimum((i + 1) * per, n_hw - 1), kv_blk)),
        ],
        out_specs=pl.BlockSpec((None, tq, A_Q_DIM), lambda b, i: (b, i, 0)),
        out_shape=jax.ShapeDtypeStruct((batch, seq, A_Q_DIM), BF16),
        scratch_shapes=[pltpu.VMEM((tq + 2 * hw, 2 * A_KV_DIM), BF16)],
        compiler_params=_cparams("parallel", "parallel"),
        name="attn_a",
    )(sink, nat3d, nat3d, nat3d, nat3d)


def _attn_b_kernel(q_ref, kp_ref, km_ref, kn_ref, vp_ref, vm_ref, vn_ref, o_ref, lse_ref,
                   k_ref, v_ref, *, sub_len):
    tq = q_ref.shape[0]
    hw = B_HALF_WINDOW
    t0 = pl.program_id(1) * tq
    for dst, prev, main, nxt in ((k_ref, kp_ref, km_ref, kn_ref), (v_ref, vp_ref, vm_ref, vn_ref)):
        dst[0:hw, :] = prev[...]
        dst[hw:hw + tq, :] = main[...]
        dst[hw + tq:tq + 2 * hw, :] = nxt[...]
    kw = B_SUB + 2 * hw
    qi = lax.broadcasted_iota(I32, (B_SUB, kw), 0)
    kj = lax.broadcasted_iota(I32, (B_SUB, kw), 1)
    in_band = jnp.abs(kj - hw - qi) <= hw
    for sb in range(tq // B_SUB):
        r0 = sb * B_SUB
        kpos = t0 + r0 - hw + kj
        valid = in_band & (kpos >= 0) & (kpos < sub_len)
        for h in range(B_HEADS):
            cs = slice(h * HEAD_DIM, (h + 1) * HEAD_DIM)
            q = q_ref[r0:r0 + B_SUB, cs]
            k = k_ref[r0:r0 + kw, cs]
            v = v_ref[r0:r0 + kw, cs]
            s = lax.dot_general(q, k, (((1,), (1,)), ((), ())), preferred_element_type=F32)
            s = jnp.where(valid, s, NEG_INF)
            m = jnp.max(s, axis=-1, keepdims=True)
            p = jnp.exp(s - m)
            denom = jnp.sum(p, axis=-1, keepdims=True)
            acc = jnp.dot(p.astype(BF16), v, preferred_element_type=F32)
            o_ref[r0:r0 + B_SUB, cs] = acc / denom
            lse_ref[r0:r0 + B_SUB, cs] = jnp.broadcast_to(m + jnp.log(denom), (B_SUB, HEAD_DIM))


def _attn_b(arr3d, q_blk, name):
    n_sub, sub_len, _ = arr3d.shape
    tq = min(512, sub_len)
    hw = B_HALF_WINDOW
    per = tq // hw
    n_hw = sub_len // hw

    def main_spec(c):
        return pl.BlockSpec((None, tq, B_DIM), lambda g, i: (g, i, c))

    def prev_spec(c):
        return pl.BlockSpec((None, hw, B_DIM), lambda g, i: (g, jnp.maximum(i * per - 1, 0), c))

    def next_spec(c):
        return pl.BlockSpec((None, hw, B_DIM), lambda g, i: (g, jnp.minimum((i + 1) * per, n_hw - 1), c))

    out_spec = pl.BlockSpec((None, tq, B_DIM), lambda g, i: (g, i, 0))
    return pl.pallas_call(
        functools.partial(_attn_b_kernel, sub_len=sub_len),
        grid=(n_sub, sub_len // tq),
        in_specs=[main_spec(q_blk),
                  prev_spec(q_blk + 1), main_spec(q_blk + 1), next_spec(q_blk + 1),
                  prev_spec(q_blk + 2), main_spec(q_blk + 2), next_spec(q_blk + 2)],
        out_specs=[out_spec, out_spec],
        out_shape=[jax.ShapeDtypeStruct((n_sub, sub_len, B_DIM), F32)] * 2,
        scratch_shapes=[pltpu.VMEM((tq + 2 * hw, B_DIM), BF16)] * 2,
        compiler_params=_cparams("parallel", "parallel"),
        name=name,
    )(arr3d, arr3d, arr3d, arr3d, arr3d, arr3d, arr3d)


def _merge_kernel(x_ref, ya_ref, o0_ref, l0_ref, o1_ref, l1_ref, o2_ref, l2_ref, gate_ref,
                  wa_ref, wb_ref, wo_ref, g2_ref, wrh_ref, wrl_ref, br_ref,
                  x1_ref, h2_ref, code_ref, wcol_ref, cnt_ref,
                  so1_ref, sl1_ref, so2_ref, sl2_ref, yb_ref, run_ref):
    tm = x_ref.shape[0]

    @pl.when(pl.program_id(0) == 0)
    def _():
        run_ref[...] = jnp.zeros_like(run_ref)

    for dil, src_o, src_l, dst_o, dst_l in ((4, o1_ref, l1_ref, so1_ref, sl1_ref),
                                            (16, o2_ref, l2_ref, so2_ref, sl2_ref)):
        n = tm // dil
        for r in range(dil):
            for c in range(B_DIM // LANES):
                cs = slice(c * LANES, (c + 1) * LANES)
                dst_o[c, pl.ds(r, n, stride=dil), :] = src_o[r, :, cs]
                dst_l[c, pl.ds(r, n, stride=dil), :] = src_l[r, :, cs]

    for c in range(B_DIM // LANES):
        cs = slice(c * LANES, (c + 1) * LANES)
        l0, l1, l2 = l0_ref[:, cs], sl1_ref[c], sl2_ref[c]
        m = jnp.maximum(jnp.maximum(l0, l1), l2)
        e0, e1, e2 = jnp.exp(l0 - m), jnp.exp(l1 - m), jnp.exp(l2 - m)
        yb = (e0 * o0_ref[:, cs] + e1 * so1_ref[c] + e2 * so2_ref[c]) / (e0 + e1 + e2)
        yb_ref[:, cs] = yb.astype(BF16)

    ya_p = jnp.dot(ya_ref[...], wa_ref[...], preferred_element_type=F32)
    yb_p = jnp.dot(yb_ref[...], wb_ref[...], preferred_element_type=F32)
    merged = gate_ref[:, 0:D_MODEL].astype(F32) * ya_p + gate_ref[:, D_MODEL:GATE_COLS].astype(F32) * yb_p
    x1 = x_ref[...] + jnp.dot(merged.astype(BF16), wo_ref[...], preferred_element_type=F32)
    x1_ref[...] = x1

    h2 = x1 * lax.rsqrt(jnp.mean(x1 * x1, axis=-1, keepdims=True) + NORM_EPS) * g2_ref[...]
    _store_row_tiles(h2_ref, h2)

    h_hi = h2.astype(BF16)
    h_lo = (h2 - h_hi.astype(F32)).astype(BF16)
    logits = (jnp.dot(h_hi, wrh_ref[...], preferred_element_type=F32)
              + jnp.dot(h_lo, wrh_ref[...], preferred_element_type=F32)
              + jnp.dot(h_hi, wrl_ref[...], preferred_element_type=F32)) + br_ref[...]

    lane = lax.broadcasted_iota(I32, (tm, LANES), 1)
    is_grp = (lane >= N_EXPERTS) & (lane < N_EXPERTS + MOE_GROUPS)
    gl = jnp.where(is_grp, logits, -jnp.inf)
    gmax = jnp.max(gl, axis=-1, keepdims=True)
    gidx = jnp.min(jnp.where(gl == gmax, lane - N_EXPERTS, LANES), axis=-1, keepdims=True)
    gw = 1.0 / jnp.sum(jnp.exp(gl - gmax), axis=-1, keepdims=True)
    in_grp = (lane < N_EXPERTS) & ((lane // EXPERTS_PER_GROUP) == gidx)
    el = jnp.where(in_grp, logits, -jnp.inf)
    v1 = jnp.max(el, axis=-1, keepdims=True)
    i1 = jnp.min(jnp.where(el == v1, lane, LANES), axis=-1, keepdims=True)
    el2 = jnp.where(lane == i1, -jnp.inf, el)
    v2 = jnp.max(el2, axis=-1, keepdims=True)
    i2 = jnp.min(jnp.where(el2 == v2, lane, LANES), axis=-1, keepdims=True)
    t = jnp.exp(v2 - v1)
    w1 = gw / (1.0 + t)
    w2 = gw * t / (1.0 + t)

    oh1 = (lane == i1).astype(F32)
    oh2 = (lane == i2).astype(F32)
    oh = oh1 + oh2
    row = lax.broadcasted_iota(I32, (tm, tm), 0)
    col = lax.broadcasted_iota(I32, (tm, tm), 1)
    lower = (col < row).astype(BF16)
    before = jnp.dot(lower, oh.astype(BF16), preferred_element_type=F32) + run_ref[0:1, :]
    rank1 = jnp.sum(before * oh1, axis=-1, keepdims=True)
    rank2 = jnp.sum(before * oh2, axis=-1, keepdims=True)
    run_ref[...] = run_ref[...] + jnp.sum(oh, axis=0, keepdims=True)
    cnt_ref[...] = run_ref[...]

    scale = float(1 << RANK_BITS)
    code1 = i1.astype(F32) * scale + rank1
    code2 = i2.astype(F32) * scale + rank2
    meta = jnp.where(lane == 0, code1, jnp.where(lane == 1, code2, 0.0))
    code_ref[...] = meta.T[0:8, :].astype(I32)
    wcol_ref[...] = jnp.where(lane == 0, w1, jnp.where(lane == 1, w2, 0.0))


def _merge(x2d, ya, o0, l0, o1, l1, o2, l2, gates, wa, wb, wo, g2, wr_hi, wr_lo, br, batch, seq):
    tm = TOK_TILE
    tps = seq // tm
    n_tok = batch * seq

    def tok(c):
        return pl.BlockSpec((tm, c), lambda i: (i, 0))

    def full(a):
        return pl.BlockSpec(a.shape, lambda i: (0,) * a.ndim)

    def res_spec(d):
        return pl.BlockSpec((None, d, tm // d, B_DIM), lambda i: (i // tps, 0, i % tps, 0))

    return pl.pallas_call(
        _merge_kernel,
        grid=(n_tok // tm,),
        in_specs=[tok(D_MODEL), tok(A_Q_DIM), tok(B_DIM), tok(B_DIM),
                  res_spec(4), res_spec(4), res_spec(16), res_spec(16), tok(GATE_COLS),
                  full(wa), full(wb), full(wo), full(g2), full(wr_hi), full(wr_lo), full(br)],
        out_specs=[tok(D_MODEL), pl.BlockSpec((tm * ROW_CHUNKS, LANES), lambda i: (i, 0)),
                   pl.BlockSpec((8, tm), lambda i: (0, i)),
                   tok(LANES),
                   pl.BlockSpec((8, LANES), lambda i: (0, 0))],
        out_shape=[jax.ShapeDtypeStruct((n_tok, D_MODEL), F32),
                   jax.ShapeDtypeStruct((n_tok * ROW_CHUNKS, LANES), F32),
                   jax.ShapeDtypeStruct((8, n_tok), I32),
                   jax.ShapeDtypeStruct((n_tok, LANES), F32),
                   jax.ShapeDtypeStruct((8, LANES), F32)],
        scratch_shapes=([pltpu.VMEM((B_DIM // LANES, tm, LANES), F32)] * 4
                        + [pltpu.VMEM((tm, B_DIM), BF16), pltpu.VMEM((8, LANES), F32)]),
        compiler_params=_cparams("arbitrary"),
        name="merge_route",
    )(x2d, ya, o0, l0, o1, l1, o2, l2, gates, wa, wb, wo, g2, wr_hi, wr_lo, br)


def _row_tile(ref, t):
    return ref.at[pl.ds(pl.multiple_of(t * ROW_CHUNKS, ROW_CHUNKS), ROW_CHUNKS)]


def _store_row_tiles(ref, val):
    rows = val.shape[0]
    for c in range(ROW_CHUNKS):
        ref[pl.ds(c, rows, stride=ROW_CHUNKS), :] = val[:, c * LANES:(c + 1) * LANES]


def _load_row_tiles_chunk(ref, c):
    return ref[pl.ds(c, ref.shape[0] // ROW_CHUNKS, stride=ROW_CHUNKS), :]


def _dispatch_kernel(slot_ref, h_ref, xs_in_ref, xs_ref, sem, *, n_tok):
    del xs_in_ref
    i = pl.program_id(0)
    rows = h_ref.shape[0] // ROW_CHUNKS

    def issue(j, carry):
        t = i * rows + j
        for k in range(2):
            pltpu.make_async_copy(_row_tile(h_ref, j), _row_tile(xs_ref, slot_ref[k * n_tok + t]), sem).start()
        return carry

    lax.fori_loop(0, rows, issue, 0, unroll=DMA_UNROLL)
    for _ in range(2):
        pltpu.make_async_copy(h_ref, xs_ref.at[pl.ds(0, rows * ROW_CHUNKS)], sem).wait()


def _dispatch(slots, h2, xs_init):
    n_tok = h2.shape[0] // ROW_CHUNKS
    rows = DISPATCH_TILE
    return pl.pallas_call(
        functools.partial(_dispatch_kernel, n_tok=n_tok),
        grid_spec=pltpu.PrefetchScalarGridSpec(
            num_scalar_prefetch=1,
            grid=(n_tok // rows,),
            in_specs=[pl.BlockSpec((rows * ROW_CHUNKS, LANES), lambda i, s: (i, 0)),
                      pl.BlockSpec(memory_space=pl.ANY)],
            out_specs=pl.BlockSpec(memory_space=pl.ANY),
            scratch_shapes=[pltpu.SemaphoreType.DMA(())],
        ),
        out_shape=jax.ShapeDtypeStruct(xs_init.shape, xs_init.dtype),
        input_output_aliases={2: 0},
        compiler_params=_cparams("arbitrary"),
        name="dispatch",
    )(slots, h2, xs_init)


def _combine_kernel(slot_ref, x_ref, w_ref, g_ref, ys_ref, o_ref, buf_ref, y_ref, sem, *, n_tok, final):
    i = pl.program_id(0)
    n_steps = pl.num_programs(0)
    rows = x_ref.shape[0]

    def issue_tile(tile, slot):
        def issue(j, carry):
            t = tile * rows + j
            for k in range(2):
                pltpu.make_async_copy(_row_tile(ys_ref, slot_ref[k * n_tok + t]),
                                      _row_tile(buf_ref.at[slot, k], j), sem.at[slot]).start()
            return carry

        lax.fori_loop(0, rows, issue, 0, unroll=DMA_UNROLL)

    @pl.when(i == 0)
    def _():
        issue_tile(0, 0)

    @pl.when(i + 1 < n_steps)
    def _():
        issue_tile(i + 1, (i + 1) % 2)

    cur = i % 2
    for k in range(2):
        pltpu.make_async_copy(ys_ref.at[pl.ds(0, rows * ROW_CHUNKS)], buf_ref.at[cur, k], sem.at[cur]).wait()
    w = w_ref[...]
    w1, w2 = w[:, 0:1], w[:, 1:2]
    for c in range(ROW_CHUNKS):
        cs = slice(c * LANES, (c + 1) * LANES)
        y1 = _load_row_tiles_chunk(buf_ref.at[cur, 0], c)
        y2 = _load_row_tiles_chunk(buf_ref.at[cur, 1], c)
        y_ref[:, cs] = x_ref[:, cs] + w1 * y1 + w2 * y2
    y = y_ref[...]
    if final:
        y = y * lax.rsqrt(jnp.mean(y * y, axis=-1, keepdims=True) + NORM_EPS) * g_ref[...]
    o_ref[...] = y


def _combine(slots, x1, wcol, g, ys, final):
    n_tok = x1.shape[0]
    rows = ROW_TILE
    return pl.pallas_call(
        functools.partial(_combine_kernel, n_tok=n_tok, final=final),
        grid_spec=pltpu.PrefetchScalarGridSpec(
            num_scalar_prefetch=1,
            grid=(n_tok // rows,),
            in_specs=[pl.BlockSpec((rows, D_MODEL), lambda i, s: (i, 0)),
                      pl.BlockSpec((rows, LANES), lambda i, s: (i, 0)),
                      pl.BlockSpec((1, D_MODEL), lambda i, s: (0, 0)),
                      pl.BlockSpec(memory_space=pl.ANY)],
            out_specs=pl.BlockSpec((rows, D_MODEL), lambda i, s: (i, 0)),
            scratch_shapes=[pltpu.VMEM((2, 2, rows * ROW_CHUNKS, LANES), F32),
                            pltpu.VMEM((rows, D_MODEL), F32),
                            pltpu.SemaphoreType.DMA((2,))],
        ),
        out_shape=jax.ShapeDtypeStruct((n_tok, D_MODEL), F32),
        compiler_params=_cparams("arbitrary"),
        name="combine",
    )(slots, x1, wcol, g, ys)


def _experts_kernel(tile_e_ref, n_used_ref, xs_ref, wg_ref, wu_ref, wd_ref, ys_ref, x_ref):
    del tile_e_ref

    @pl.when(pl.program_id(0) < n_used_ref[0])
    def _():
        for c in range(ROW_CHUNKS):
            x_ref[:, c * LANES:(c + 1) * LANES] = _load_row_tiles_chunk(xs_ref, c).astype(BF16)
        x = x_ref[...]
        a = jnp.dot(x, wg_ref[...].astype(BF16), preferred_element_type=F32)
        u = jnp.dot(x, wu_ref[...].astype(BF16), preferred_element_type=F32)
        z = (a * jax.nn.sigmoid(a)) * u
        _store_row_tiles(ys_ref, jnp.dot(z.astype(BF16), wd_ref[...].astype(BF16), preferred_element_type=F32))

    @pl.when(pl.program_id(0) >= n_used_ref[0])
    def _():
        ys_ref[...] = jnp.zeros_like(ys_ref)


def _experts(tile_e, n_used, xs, wg, wu, wd, layer):
    n_slots = xs.shape[0] // ROW_CHUNKS
    te = EXP_TILE

    def row_map(i, tile_e, n_used):
        return (jnp.minimum(i, n_used[0] - 1), 0)

    def out_map(i, tile_e, n_used):
        return (i, 0)

    def w_map(i, tile_e, n_used):
        return (layer, tile_e[i], 0, 0)

    return pl.pallas_call(
        _experts_kernel,
        grid_spec=pltpu.PrefetchScalarGridSpec(
            num_scalar_prefetch=2,
            grid=(n_slots // te,),
            in_specs=[pl.BlockSpec((te * ROW_CHUNKS, LANES), row_map),
                      pl.BlockSpec((None, None, D_MODEL, D_EXPERT), w_map),
                      pl.BlockSpec((None, None, D_MODEL, D_EXPERT), w_map),
                      pl.BlockSpec((None, None, D_EXPERT, D_MODEL), w_map)],
            out_specs=pl.BlockSpec((te * ROW_CHUNKS, LANES), out_map),
            scratch_shapes=[pltpu.VMEM((te, D_MODEL), BF16)],
        ),
        out_shape=jax.ShapeDtypeStruct((n_slots * ROW_CHUNKS, LANES), F32),
        compiler_params=_cparams("arbitrary"),
        name="experts",
    )(tile_e, n_used, xs, wg, wu, wd)


def kernel(x, attn_norm_g, w_in, a_sink, w_branch_a, w_branch_b, w_out, ffn_norm_g,
           w_router_group, b_router_group, w_router_expert, b_router_expert,
           w_exp_gate, w_exp_up, w_exp_down, final_norm_g):
    batch, seq, d_model = x.shape
    depth = w_in.shape[0]
    n_tok = batch * seq
    assert d_model == D_MODEL and w_in.shape[2] == D_IN
    assert seq % (16 * B_SUB) == 0 and seq % TOK_TILE == 0 and n_tok % ROW_TILE == 0
    assert n_tok < (1 << RANK_BITS)

    cos_t, sin_t = _rope_tables(seq)
    tables = (cos_t, sin_t,
              _residue_order(cos_t, 4, TOK_TILE), _residue_order(sin_t, 4, TOK_TILE),
              _residue_order(cos_t, 16, TOK_TILE), _residue_order(sin_t, 16, TOK_TILE))

    n_slots = 2 * n_tok + N_EXPERTS * EXP_TILE
    n_tiles = n_slots // EXP_TILE
    x2d = x.reshape(n_tok, D_MODEL)

    for l in range(depth):
        nat, gates, grp1, grp2 = _in_proj(x2d, attn_norm_g[l][None, :], w_in[l].astype(BF16), tables, batch, seq)
        nat3d = nat.reshape(batch, seq, NAT_COLS)
        ya = _attn_a(nat3d, a_sink[l]).reshape(n_tok, A_Q_DIM)
        o0, l0 = _attn_b(nat3d, (A_Q_DIM + 2 * A_KV_DIM) // B_DIM, "attn_b1")
        o1, l1 = _attn_b(grp1.reshape(batch * 4, seq // 4, GRP_COLS), 0, "attn_b4")
        o2, l2 = _attn_b(grp2.reshape(batch * 16, seq // 16, GRP_COLS), 0, "attn_b16")

        wr = jnp.zeros((D_MODEL, LANES), F32)
        wr = wr.at[:, 0:N_EXPERTS].set(w_router_expert[l]).at[:, N_EXPERTS:N_EXPERTS + MOE_GROUPS].set(w_router_group[l])
        br = jnp.zeros((1, LANES), F32)
        br = br.at[0, 0:N_EXPERTS].set(b_router_expert[l]).at[0, N_EXPERTS:N_EXPERTS + MOE_GROUPS].set(b_router_group[l])
        wr_hi = wr.astype(BF16)
        wr_lo = (wr - wr_hi.astype(F32)).astype(BF16)

        x1, h2, code, wcol, cnt = _merge(
            x2d, ya, o0.reshape(n_tok, B_DIM), l0.reshape(n_tok, B_DIM),
            o1.reshape(batch, 4, seq // 4, B_DIM), l1.reshape(batch, 4, seq // 4, B_DIM),
            o2.reshape(batch, 16, seq // 16, B_DIM), l2.reshape(batch, 16, seq // 16, B_DIM),
            gates, w_branch_a[l].astype(BF16), w_branch_b[l].astype(BF16), w_out[l].astype(BF16),
            ffn_norm_g[l][None, :], wr_hi, wr_lo, br, batch, seq)

        counts = cnt[0, 0:N_EXPERTS].astype(I32)
        padded = ((counts + EXP_TILE - 1) // EXP_TILE) * EXP_TILE
        ends = jnp.cumsum(padded)
        offs = ends - padded
        n_used = (ends[-1:] // EXP_TILE).astype(I32)
        tile_start = jnp.arange(n_tiles, dtype=I32) * EXP_TILE
        tile_e = jnp.minimum(jnp.sum((ends[None, :] <= tile_start[:, None]).astype(I32), axis=1), N_EXPERTS - 1)
        eid = code[0:2] >> RANK_BITS
        rank = code[0:2] & ((1 << RANK_BITS) - 1)
        expert_ids = jnp.arange(N_EXPERTS, dtype=I32)[:, None, None]
        slots = (rank + jnp.sum(jnp.where(eid[None] == expert_ids, offs[:, None, None], 0), axis=0)).reshape(-1)

        xs = _dispatch(slots, h2, jnp.zeros((n_slots * ROW_CHUNKS, LANES), F32))
        ys = _experts(tile_e, n_used, xs, w_exp_gate, w_exp_up, w_exp_down, l)
        x2d = _combine(slots, x1, wcol, final_norm_g[None, :], ys, final=(l == depth - 1))

    return x2d.reshape(batch, seq, D_MODEL)
```

```python
import functools

import jax
import jax.numpy as jnp
from jax import lax
from jax.experimental import pallas as pl
from jax.experimental.pallas import tpu as pltpu

F32 = jnp.float32
BF16 = jnp.bfloat16
I32 = jnp.int32

D_MODEL = 1024
HEAD_DIM = 64
HALF_HEAD = HEAD_DIM // 2
ROPE_THETA = 10000.0
NORM_EPS = 1e-6
NEG_INF = -1e30
LANES = 128

A_Q_HEADS = 8
A_KV_HEADS = 2
A_GROUP = A_Q_HEADS // A_KV_HEADS
A_HALF_WINDOW = 128
A_Q_DIM = A_Q_HEADS * HEAD_DIM
A_KV_DIM = A_KV_HEADS * HEAD_DIM

B_GROUPS = ((128, 1), (512, 4), (2048, 16))
B_HEADS = 4
B_DIM = B_HEADS * HEAD_DIM
B_HALF_WINDOW = 64

MOE_GROUPS = 4
EXPERTS_PER_GROUP = 8
N_EXPERTS = MOE_GROUPS * EXPERTS_PER_GROUP
D_EXPERT = 256

NAT_IN_COLS = A_Q_DIM + 2 * A_KV_DIM + 3 * B_DIM
NAT_B = A_Q_DIM
NAT_AK = NAT_B + 3 * B_DIM
NAT_AV = NAT_AK + 2 * A_KV_DIM
NAT_COLS = NAT_AV + A_KV_DIM
GRP_COLS = 3 * B_DIM
COL_G1 = NAT_IN_COLS
COL_G2 = COL_G1 + GRP_COLS
COL_GATE = COL_G2 + GRP_COLS
GATE_COLS = 2 * D_MODEL
D_IN = COL_GATE + GATE_COLS

TOK_TILE = 512
A_Q_TILE = 512
B_SUB = 128
EXP_TILE = 256
ROW_TILE = 256
DISPATCH_TILE = 1024
ROW_CHUNKS = D_MODEL // LANES
DMA_UNROLL = 8
RANK_BITS = 16
VMEM_LIMIT = 56 * 1024 * 1024


def _cparams(*sem):
    return pltpu.CompilerParams(dimension_semantics=sem, vmem_limit_bytes=VMEM_LIMIT)


def _rope_tables(seq_len):
    inv = 1.0 / (ROPE_THETA ** (jnp.arange(0, HEAD_DIM, 2, dtype=F32) / HEAD_DIM))
    ang = jnp.arange(seq_len, dtype=F32)[:, None] * inv[None, :]
    cos, sin = jnp.cos(ang), jnp.sin(ang)
    cos_t = jnp.concatenate([cos, cos, cos, cos], axis=-1)
    sin_t = jnp.concatenate([-sin, sin, -sin, sin], axis=-1)
    return cos_t, sin_t


def _residue_order(table, dilation, tile):
    s, c = table.shape
    return table.reshape(s // tile, tile // dilation, dilation, c).transpose(0, 2, 1, 3).reshape(s, c)


def _rope(t, cos, sin_signed, first_half):
    partner = jnp.where(first_half, pltpu.roll(t, LANES - HALF_HEAD, 1), pltpu.roll(t, HALF_HEAD, 1))
    return t * cos + partner * sin_signed


Q_KIND, K_KIND, V_KIND = 0, 1, 2
_NAT_KINDS = ([Q_KIND] * 4 + [K_KIND] + [V_KIND] + [Q_KIND] * 2 + [K_KIND] * 2 + [V_KIND] * 2)
_GRP_KINDS = [Q_KIND] * 2 + [K_KIND] * 2 + [V_KIND] * 2


def _in_proj_kernel(x_ref, g_ref, w_ref, c1_ref, s1_ref, c4_ref, s4_ref, c16_ref, s16_ref,
                    nat_ref, gate_ref, g1_ref, g2_ref, hf_ref, hb_ref, hd_ref):
    tm = x_ref.shape[0]
    x = x_ref[...]
    h = x * lax.rsqrt(jnp.mean(x * x, axis=-1, keepdims=True) + NORM_EPS) * g_ref[...]
    n_chunks = D_MODEL // LANES
    for c in range(n_chunks):
        hf_ref[c] = h[:, c * LANES:(c + 1) * LANES]
    hb_ref[...] = h.astype(BF16)
    lane = lax.broadcasted_iota(I32, (1, LANES), 1)
    first_half = (lane % HEAD_DIM) < HALF_HEAD

    def project(h_b, col0, kinds, cos_ref, sin_ref, store):
        width = 512
        for c0 in range(0, len(kinds) * LANES, width):
            w = min(width, len(kinds) * LANES - c0)
            res = jnp.dot(h_b, w_ref[:, col0 + c0:col0 + c0 + w], preferred_element_type=F32)
            for j in range(w // LANES):
                kind = kinds[(c0 // LANES) + j]
                t = res[:, j * LANES:(j + 1) * LANES]
                if kind != V_KIND:
                    t = _rope(t, cos_ref[...], sin_ref[...], first_half)
                if kind == Q_KIND:
                    t = t * (HEAD_DIM ** -0.5)
                store(c0 + j * LANES, t)

    low_head = lane < HEAD_DIM

    def store_nat(c, t):
        if A_Q_DIM <= c < A_Q_DIM + A_KV_DIM:
            swapped = pltpu.roll(t, HEAD_DIM, 1)
            nat_ref[:, NAT_AK:NAT_AK + LANES] = jnp.where(low_head, t, swapped).astype(BF16)
            nat_ref[:, NAT_AK + LANES:NAT_AK + 2 * LANES] = jnp.where(low_head, swapped, t).astype(BF16)
        elif c < A_Q_DIM + 2 * A_KV_DIM:
            out = c if c < A_Q_DIM else NAT_AV
            nat_ref[:, out:out + LANES] = t.astype(BF16)
        else:
            out = c - 2 * A_KV_DIM
            nat_ref[:, out:out + LANES] = t.astype(BF16)

    project(hb_ref[...], 0, _NAT_KINDS, c1_ref, s1_ref, store_nat)

    for c0 in range(0, GATE_COLS, 512):
        res = jnp.dot(hb_ref[...], w_ref[:, COL_GATE + c0:COL_GATE + c0 + 512], preferred_element_type=F32)
        gate_ref[:, c0:c0 + 512] = jax.nn.sigmoid(res).astype(BF16)

    for dil, col0, cos_ref, sin_ref, out_ref in ((4, COL_G1, c4_ref, s4_ref, g1_ref),
                                                 (16, COL_G2, c16_ref, s16_ref, g2_ref)):
        n = tm // dil
        for r in range(dil):
            for c in range(n_chunks):
                hd_ref[r * n:(r + 1) * n, c * LANES:(c + 1) * LANES] = (
                    hf_ref[c, pl.ds(r, n, stride=dil), :].astype(BF16))

        def store_grp(c, t, out_ref=out_ref, dil=dil, n=n):
            v = t.astype(BF16)
            for r in range(dil):
                out_ref[r, :, c:c + LANES] = v[r * n:(r + 1) * n]

        project(hd_ref[...], col0, _GRP_KINDS, cos_ref, sin_ref, store_grp)


def _in_proj(x2d, g, w_bf16, tables, batch, seq):
    tm = TOK_TILE
    tiles_per_seq = seq // tm
    n_tok = batch * seq
    c1, s1, c4, s4, c16, s16 = tables
    tab_spec = pl.BlockSpec((tm, LANES), lambda i: (i % tiles_per_seq, 0))
    return pl.pallas_call(
        _in_proj_kernel,
        grid=(n_tok // tm,),
        in_specs=[
            pl.BlockSpec((tm, D_MODEL), lambda i: (i, 0)),
            pl.BlockSpec((1, D_MODEL), lambda i: (0, 0)),
            pl.BlockSpec((D_MODEL, D_IN), lambda i: (0, 0), pipeline_mode=pl.Buffered(1)),
            tab_spec, tab_spec, tab_spec, tab_spec, tab_spec, tab_spec,
        ],
        out_specs=[
            pl.BlockSpec((tm, NAT_COLS), lambda i: (i, 0)),
            pl.BlockSpec((tm, GATE_COLS), lambda i: (i, 0)),
            pl.BlockSpec((None, 4, tm // 4, GRP_COLS), lambda i: (i // tiles_per_seq, 0, i % tiles_per_seq, 0)),
            pl.BlockSpec((None, 16, tm // 16, GRP_COLS), lambda i: (i // tiles_per_seq, 0, i % tiles_per_seq, 0)),
        ],
        out_shape=[
            jax.ShapeDtypeStruct((n_tok, NAT_COLS), BF16),
            jax.ShapeDtypeStruct((n_tok, GATE_COLS), BF16),
            jax.ShapeDtypeStruct((batch, 4, seq // 4, GRP_COLS), BF16),
            jax.ShapeDtypeStruct((batch, 16, seq // 16, GRP_COLS), BF16),
        ],
        scratch_shapes=[
            pltpu.VMEM((D_MODEL // LANES, tm, LANES), F32),
            pltpu.VMEM((tm, D_MODEL), BF16),
            pltpu.VMEM((tm, D_MODEL), BF16),
        ],
        compiler_params=_cparams("parallel"),
        name="in_proj",
    )(x2d, g, w_bf16, c1, s1, c4, s4, c16, s16)


def _masked_heads(q_pair, low_head, high_head):
    zero = jnp.zeros_like(q_pair)
    return [jnp.where(low_head, q_pair, zero), jnp.where(high_head, q_pair, zero)]


def _attn_a_kernel(sink_ref, q_ref, kp_ref, km_ref, kn_ref, vp_ref, vm_ref, vn_ref, o_ref, k_ref, vt_ref, *, seq):
    tq = q_ref.shape[0]
    hw = A_HALF_WINDOW
    n_sub = tq // hw
    i = pl.program_id(1)
    last_blk = seq // hw - 1
    k_ref[0:hw, :] = kp_ref[...]
    k_ref[hw:hw + tq, :] = km_ref[...]
    k_ref[hw + tq:tq + 2 * hw, :] = kn_ref[...]
    for r0, src in ((0, vp_ref), (hw, vm_ref), (hw + tq, vn_ref)):
        vt_ref[:, r0:r0 + src.shape[0]] = src[...].astype(F32).T.astype(BF16)
    grp_cols = A_GROUP * hw
    key = lax.broadcasted_iota(I32, (hw, grp_cols), 0)
    qry = lax.broadcasted_iota(I32, (hw, grp_cols), 1) % hw
    low_head = lax.broadcasted_iota(I32, (1, LANES), 1) < HEAD_DIM
    high_head = jnp.logical_not(low_head)
    for sb in range(n_sub):
        r0 = sb * hw
        blk = i * n_sub + sb
        mask_p = (key >= qry) if sb > 0 else (key >= qry + jnp.where(blk > 0, 0, hw))
        mask_n = (key <= qry) if sb < n_sub - 1 else (key <= qry - jnp.where(blk < last_blk, 0, hw))
        out_t = []
        for g in range(A_KV_HEADS):
            heads = range(g * A_GROUP, (g + 1) * A_GROUP)
            q_parts = []
            for c in range(g * A_GROUP // 2, (g + 1) * A_GROUP // 2):
                q_parts += _masked_heads(q_ref[r0:r0 + hw, c * LANES:(c + 1) * LANES], low_head, high_head)
            q = jnp.concatenate(q_parts, axis=0)
            sink = jnp.concatenate([jnp.full((1, hw), sink_ref[h], F32) for h in heads], axis=1)
            k = k_ref[r0:r0 + 3 * hw, g * LANES:(g + 1) * LANES]
            s = lax.dot_general(k, q, (((1,), (1,)), ((), ())), preferred_element_type=F32)
            sp = jnp.where(mask_p, s[0:hw], NEG_INF)
            so = s[hw:2 * hw]
            sn = jnp.where(mask_n, s[2 * hw:3 * hw], NEG_INF)
            m = jnp.max(jnp.maximum(jnp.maximum(sp, so), sn), axis=0, keepdims=True)
            m = jnp.maximum(m, sink)
            pp, po, pn = jnp.exp(sp - m), jnp.exp(so - m), jnp.exp(sn - m)
            denom = jnp.sum(pp + po + pn, axis=0, keepdims=True) + jnp.exp(sink - m)
            p = jnp.concatenate([pp, po, pn], axis=0).astype(BF16)
            vt = vt_ref[g * HEAD_DIM:(g + 1) * HEAD_DIM, r0:r0 + 3 * hw]
            o = jnp.dot(vt, p, preferred_element_type=F32) * (1.0 / denom)
            out_t += [o[:, j * hw:(j + 1) * hw] for j in range(A_GROUP)]
        o_ref[r0:r0 + hw, :] = jnp.concatenate(out_t, axis=0).T.astype(BF16)


def _attn_a(nat3d, sink):
    batch, seq, _ = nat3d.shape
    tq = A_Q_TILE
    hw = A_HALF_WINDOW
    per = tq // hw
    n_hw = seq // hw
    k_cols, v_cols = 2 * A_KV_DIM, A_KV_DIM
    k_blk, v_blk = NAT_AK // k_cols, NAT_AV // v_cols

    def prev_spec(cols, blk):
        return pl.BlockSpec((None, hw, cols), lambda b, i: (b, jnp.maximum(i * per - 1, 0), blk))

    def main_spec(cols, blk):
        return pl.BlockSpec((None, tq, cols), lambda b, i: (b, i, blk))

    def next_spec(cols, blk):
        return pl.BlockSpec((None, hw, cols), lambda b, i: (b, jnp.minimum((i + 1) * per, n_hw - 1), blk))

    return pl.pallas_call(
        functools.partial(_attn_a_kernel, seq=seq),
        grid=(batch, seq // tq),
        in_specs=[
            pl.BlockSpec(memory_space=pltpu.SMEM),
            main_spec(A_Q_DIM, 0),
            prev_spec(k_cols, k_blk), main_spec(k_cols, k_blk), next_spec(k_cols, k_blk),
            prev_spec(v_cols, v_blk), main_spec(v_cols, v_blk), next_spec(v_cols, v_blk),
        ],
        out_specs=pl.BlockSpec((None, tq, A_Q_DIM), lambda b, i: (b, i, 0)),
        out_shape=jax.ShapeDtypeStruct((batch, seq, A_Q_DIM), BF16),
        scratch_shapes=[pltpu.VMEM((tq + 2 * hw, k_cols), BF16), pltpu.VMEM((v_cols, tq + 2 * hw), BF16)],
        compiler_params=_cparams("parallel", "parallel"),
        name="attn_a",
    )(sink, nat3d, nat3d, nat3d, nat3d, nat3d, nat3d, nat3d)


def _attn_b_kernel(q_ref, kp_ref, km_ref, kn_ref, vp_ref, vm_ref, vn_ref, o_ref, lse_ref,
                   k_ref, vt_ref, *, sub_len):
    tq = q_ref.shape[0]
    hw = B_HALF_WINDOW
    t0 = pl.program_id(1) * tq
    k_ref[0:hw, :] = kp_ref[...]
    k_ref[hw:hw + tq, :] = km_ref[...]
    k_ref[hw + tq:tq + 2 * hw, :] = kn_ref[...]
    for r0, src in ((0, vp_ref), (hw, vm_ref), (hw + tq, vn_ref)):
        vt_ref[:, r0:r0 + src.shape[0]] = src[...].astype(F32).T.astype(BF16)
    kw = B_SUB + 2 * hw
    n_sub = tq // B_SUB
    all_cols = B_HEADS * B_SUB
    key = lax.broadcasted_iota(I32, (kw, all_cols), 0)
    qry = lax.broadcasted_iota(I32, (kw, all_cols), 1) % B_SUB
    in_band = jnp.abs(key - hw - qry) <= hw
    low_head = lax.broadcasted_iota(I32, (1, LANES), 1) < HEAD_DIM
    high_head = jnp.logical_not(low_head)
    for sb in range(n_sub):
        r0 = sb * B_SUB
        valid = in_band
        if sb == 0:
            valid = valid & (t0 - hw + key >= 0)
        if sb == n_sub - 1:
            valid = valid & (t0 + r0 - hw + key < sub_len)
        s_parts = []
        for c in range(B_HEADS // 2):
            cs = slice(c * LANES, (c + 1) * LANES)
            q = jnp.concatenate(_masked_heads(q_ref[r0:r0 + B_SUB, cs], low_head, high_head), axis=0)
            s_parts.append(lax.dot_general(k_ref[r0:r0 + kw, cs], q, (((1,), (1,)), ((), ())),
                                           preferred_element_type=F32))
        s = jnp.where(valid, jnp.concatenate(s_parts, axis=1), NEG_INF)
        m = jnp.max(s, axis=0, keepdims=True)
        p = jnp.exp(s - m)
        denom = jnp.sum(p, axis=0, keepdims=True)
        p = p.astype(BF16)
        inv = 1.0 / denom
        lse = m + jnp.log(denom)
        out_t, lse_t = [], []
        for h in range(B_HEADS):
            qs = slice(h * B_SUB, (h + 1) * B_SUB)
            vt = vt_ref[h * HEAD_DIM:(h + 1) * HEAD_DIM, r0:r0 + kw]
            out_t.append(jnp.dot(vt, p[:, qs], preferred_element_type=F32) * inv[:, qs])
            lse_t.append(jnp.broadcast_to(lse[:, qs], (HEAD_DIM, B_SUB)))
        o_ref[r0:r0 + B_SUB, :] = jnp.concatenate(out_t, axis=0).T
        lse_ref[r0:r0 + B_SUB, :] = jnp.concatenate(lse_t, axis=0).T


def _attn_b(arr3d, q_blk, name):
    n_sub, sub_len, _ = arr3d.shape
    tq = min(512, sub_len)
    hw = B_HALF_WINDOW
    per = tq // hw
    n_hw = sub_len // hw

    def main_spec(c):
        return pl.BlockSpec((None, tq, B_DIM), lambda g, i: (g, i, c))

    def prev_spec(c):
        return pl.BlockSpec((None, hw, B_DIM), lambda g, i: (g, jnp.maximum(i * per - 1, 0), c))

    def next_spec(c):
        return pl.BlockSpec((None, hw, B_DIM), lambda g, i: (g, jnp.minimum((i + 1) * per, n_hw - 1), c))

    out_spec = pl.BlockSpec((None, tq, B_DIM), lambda g, i: (g, i, 0))
    return pl.pallas_call(
        functools.partial(_attn_b_kernel, sub_len=sub_len),
        grid=(n_sub, sub_len // tq),
        in_specs=[main_spec(q_blk),
                  prev_spec(q_blk + 1), main_spec(q_blk + 1), next_spec(q_blk + 1),
                  prev_spec(q_blk + 2), main_spec(q_blk + 2), next_spec(q_blk + 2)],
        out_specs=[out_spec, out_spec],
        out_shape=[jax.ShapeDtypeStruct((n_sub, sub_len, B_DIM), F32)] * 2,
        scratch_shapes=[pltpu.VMEM((tq + 2 * hw, B_DIM), BF16), pltpu.VMEM((B_DIM, tq + 2 * hw), BF16)],
        compiler_params=_cparams("parallel", "parallel"),
        name=name,
    )(arr3d, arr3d, arr3d, arr3d, arr3d, arr3d, arr3d)


def _merge_kernel(x_ref, ya_ref, o0_ref, l0_ref, o1_ref, l1_ref, o2_ref, l2_ref, gate_ref,
                  wa_ref, wb_ref, wo_ref, g2_ref, wrh_ref, wrl_ref, br_ref,
                  x1_ref, h2_ref, code_ref, wcol_ref, cnt_ref,
                  so1_ref, sl1_ref, so2_ref, sl2_ref, yb_ref, run_ref):
    tm = x_ref.shape[0]

    @pl.when(pl.program_id(0) == 0)
    def _():
        run_ref[...] = jnp.zeros_like(run_ref)

    for dil, src_o, src_l, dst_o, dst_l in ((4, o1_ref, l1_ref, so1_ref, sl1_ref),
                                            (16, o2_ref, l2_ref, so2_ref, sl2_ref)):
        n = tm // dil
        for r in range(dil):
            for c in range(B_DIM // LANES):
                cs = slice(c * LANES, (c + 1) * LANES)
                dst_o[c, pl.ds(r, n, stride=dil), :] = src_o[r, :, cs]
                dst_l[c, pl.ds(r, n, stride=dil), :] = src_l[r, :, cs]

    for c in range(B_DIM // LANES):
        cs = slice(c * LANES, (c + 1) * LANES)
        l0, l1, l2 = l0_ref[:, cs], sl1_ref[c], sl2_ref[c]
        m = jnp.maximum(jnp.maximum(l0, l1), l2)
        e0, e1, e2 = jnp.exp(l0 - m), jnp.exp(l1 - m), jnp.exp(l2 - m)
        yb = (e0 * o0_ref[:, cs] + e1 * so1_ref[c] + e2 * so2_ref[c]) / (e0 + e1 + e2)
        yb_ref[:, cs] = yb.astype(BF16)

    ya_p = jnp.dot(ya_ref[...], wa_ref[...], preferred_element_type=F32)
    yb_p = jnp.dot(yb_ref[...], wb_ref[...], preferred_element_type=F32)
    merged = gate_ref[:, 0:D_MODEL].astype(F32) * ya_p + gate_ref[:, D_MODEL:GATE_COLS].astype(F32) * yb_p
    x1 = x_ref[...] + jnp.dot(merged.astype(BF16), wo_ref[...], preferred_element_type=F32)
    x1_ref[...] = x1

    h2 = x1 * lax.rsqrt(jnp.mean(x1 * x1, axis=-1, keepdims=True) + NORM_EPS) * g2_ref[...]
    _store_row_tiles(h2_ref, h2)

    h_hi = h2.astype(BF16)
    h_lo = (h2 - h_hi.astype(F32)).astype(BF16)
    logits = (jnp.dot(h_hi, wrh_ref[...], preferred_element_type=F32)
              + jnp.dot(h_lo, wrh_ref[...], preferred_element_type=F32)
              + jnp.dot(h_hi, wrl_ref[...], preferred_element_type=F32)) + br_ref[...]

    lane = lax.broadcasted_iota(I32, (tm, LANES), 1)
    is_grp = (lane >= N_EXPERTS) & (lane < N_EXPERTS + MOE_GROUPS)
    gl = jnp.where(is_grp, logits, -jnp.inf)
    gmax = jnp.max(gl, axis=-1, keepdims=True)
    gidx = jnp.min(jnp.where(gl == gmax, lane - N_EXPERTS, LANES), axis=-1, keepdims=True)
    gw = 1.0 / jnp.sum(jnp.exp(gl - gmax), axis=-1, keepdims=True)
    in_grp = (lane < N_EXPERTS) & ((lane // EXPERTS_PER_GROUP) == gidx)
    el = jnp.where(in_grp, logits, -jnp.inf)
    v1 = jnp.max(el, axis=-1, keepdims=True)
    i1 = jnp.min(jnp.where(el == v1, lane, LANES), axis=-1, keepdims=True)
    el2 = jnp.where(lane == i1, -jnp.inf, el)
    v2 = jnp.max(el2, axis=-1, keepdims=True)
    i2 = jnp.min(jnp.where(el2 == v2, lane, LANES), axis=-1, keepdims=True)
    t = jnp.exp(v2 - v1)
    w1 = gw / (1.0 + t)
    w2 = gw * t / (1.0 + t)

    oh1 = (lane == i1).astype(F32)
    oh2 = (lane == i2).astype(F32)
    oh = oh1 + oh2
    row = lax.broadcasted_iota(I32, (tm, tm), 0)
    col = lax.broadcasted_iota(I32, (tm, tm), 1)
    lower = (col < row).astype(BF16)
    before = jnp.dot(lower, oh.astype(BF16), preferred_element_type=F32) + run_ref[0:1, :]
    rank1 = jnp.sum(before * oh1, axis=-1, keepdims=True)
    rank2 = jnp.sum(before * oh2, axis=-1, keepdims=True)
    run_ref[...] = run_ref[...] + jnp.sum(oh, axis=0, keepdims=True)
    cnt_ref[...] = run_ref[...]

    scale = float(1 << RANK_BITS)
    code1 = i1.astype(F32) * scale + rank1
    code2 = i2.astype(F32) * scale + rank2
    meta = jnp.where(lane == 0, code1, jnp.where(lane == 1, code2, 0.0))
    code_ref[...] = meta.T[0:8, :].astype(I32)
    wcol_ref[...] = jnp.where(lane == 0, w1, jnp.where(lane == 1, w2, 0.0))


def _merge(x2d, ya, o0, l0, o1, l1, o2, l2, gates, wa, wb, wo, g2, wr_hi, wr_lo, br, batch, seq):
    tm = TOK_TILE
    tps = seq // tm
    n_tok = batch * seq

    def tok(c):
        return pl.BlockSpec((tm, c), lambda i: (i, 0))

    def full(a):
        return pl.BlockSpec(a.shape, lambda i: (0,) * a.ndim)

    def res_spec(d):
        return pl.BlockSpec((None, d, tm // d, B_DIM), lambda i: (i // tps, 0, i % tps, 0))

    return pl.pallas_call(
        _merge_kernel,
        grid=(n_tok // tm,),
        in_specs=[tok(D_MODEL), tok(A_Q_DIM), tok(B_DIM), tok(B_DIM),
                  res_spec(4), res_spec(4), res_spec(16), res_spec(16), tok(GATE_COLS),
                  full(wa), full(wb), full(wo), full(g2), full(wr_hi), full(wr_lo), full(br)],
        out_specs=[tok(D_MODEL), pl.BlockSpec((tm * ROW_CHUNKS, LANES), lambda i: (i, 0)),
                   pl.BlockSpec((8, tm), lambda i: (0, i)),
                   tok(LANES),
                   pl.BlockSpec((8, LANES), lambda i: (0, 0))],
        out_shape=[jax.ShapeDtypeStruct((n_tok, D_MODEL), F32),
                   jax.ShapeDtypeStruct((n_tok * ROW_CHUNKS, LANES), F32),
                   jax.ShapeDtypeStruct((8, n_tok), I32),
                   jax.ShapeDtypeStruct((n_tok, LANES), F32),
                   jax.ShapeDtypeStruct((8, LANES), F32)],
        scratch_shapes=([pltpu.VMEM((B_DIM // LANES, tm, LANES), F32)] * 4
                        + [pltpu.VMEM((tm, B_DIM), BF16), pltpu.VMEM((8, LANES), F32)]),
        compiler_params=_cparams("arbitrary"),
        name="merge_route",
    )(x2d, ya, o0, l0, o1, l1, o2, l2, gates, wa, wb, wo, g2, wr_hi, wr_lo, br)


def _row_tile(ref, t):
    return ref.at[pl.ds(pl.multiple_of(t * ROW_CHUNKS, ROW_CHUNKS), ROW_CHUNKS)]


def _store_row_tiles(ref, val):
    rows = val.shape[0]
    for c in range(ROW_CHUNKS):
        ref[pl.ds(c, rows, stride=ROW_CHUNKS), :] = val[:, c * LANES:(c + 1) * LANES]


def _load_row_tiles_chunk(ref, c):
    return ref[pl.ds(c, ref.shape[0] // ROW_CHUNKS, stride=ROW_CHUNKS), :]


def _dispatch_kernel(slot_ref, h_ref, xs_in_ref, xs_ref, sem, *, n_tok):
    del xs_in_ref
    i = pl.program_id(0)
    rows = h_ref.shape[0] // ROW_CHUNKS

    def issue(j, carry):
        t = i * rows + j
        for k in range(2):
            pltpu.make_async_copy(_row_tile(h_ref, j), _row_tile(xs_ref, slot_ref[k * n_tok + t]), sem).start()
        return carry

    lax.fori_loop(0, rows, issue, 0, unroll=DMA_UNROLL)
    for _ in range(2):
        pltpu.make_async_copy(h_ref, xs_ref.at[pl.ds(0, rows * ROW_CHUNKS)], sem).wait()


def _dispatch(slots, h2, xs_init):
    n_tok = h2.shape[0] // ROW_CHUNKS
    rows = DISPATCH_TILE
    return pl.pallas_call(
        functools.partial(_dispatch_kernel, n_tok=n_tok),
        grid_spec=pltpu.PrefetchScalarGridSpec(
            num_scalar_prefetch=1,
            grid=(n_tok // rows,),
            in_specs=[pl.BlockSpec((rows * ROW_CHUNKS, LANES), lambda i, s: (i, 0)),
                      pl.BlockSpec(memory_space=pl.ANY)],
            out_specs=pl.BlockSpec(memory_space=pl.ANY),
            scratch_shapes=[pltpu.SemaphoreType.DMA(())],
        ),
        out_shape=jax.ShapeDtypeStruct(xs_init.shape, xs_init.dtype),
        input_output_aliases={2: 0},
        compiler_params=_cparams("arbitrary"),
        name="dispatch",
    )(slots, h2, xs_init)


def _combine_kernel(slot_ref, x_ref, w_ref, g_ref, ys_ref, o_ref, buf_ref, y_ref, sem, *, n_tok, final):
    i = pl.program_id(0)
    n_steps = pl.num_programs(0)
    rows = x_ref.shape[0]

    def issue_tile(tile, slot):
        def issue(j, carry):
            t = tile * rows + j
            for k in range(2):
                pltpu.make_async_copy(_row_tile(ys_ref, slot_ref[k * n_tok + t]),
                                      _row_tile(buf_ref.at[slot, k], j), sem.at[slot]).start()
            return carry

        lax.fori_loop(0, rows, issue, 0, unroll=DMA_UNROLL)

    @pl.when(i == 0)
    def _():
        issue_tile(0, 0)

    @pl.when(i + 1 < n_steps)
    def _():
        issue_tile(i + 1, (i + 1) % 2)

    cur = i % 2
    for k in range(2):
        pltpu.make_async_copy(ys_ref.at[pl.ds(0, rows * ROW_CHUNKS)], buf_ref.at[cur, k], sem.at[cur]).wait()
    w = w_ref[...]
    w1, w2 = w[:, 0:1], w[:, 1:2]
    for c in range(ROW_CHUNKS):
        cs = slice(c * LANES, (c + 1) * LANES)
        y1 = _load_row_tiles_chunk(buf_ref.at[cur, 0], c)
        y2 = _load_row_tiles_chunk(buf_ref.at[cur, 1], c)
        y_ref[:, cs] = x_ref[:, cs] + w1 * y1 + w2 * y2
    y = y_ref[...]
    if final:
        y = y * lax.rsqrt(jnp.mean(y * y, axis=-1, keepdims=True) + NORM_EPS) * g_ref[...]
    o_ref[...] = y


def _combine(slots, x1, wcol, g, ys, final):
    n_tok = x1.shape[0]
    rows = ROW_TILE
    return pl.pallas_call(
        functools.partial(_combine_kernel, n_tok=n_tok, final=final),
        grid_spec=pltpu.PrefetchScalarGridSpec(
            num_scalar_prefetch=1,
            grid=(n_tok // rows,),
            in_specs=[pl.BlockSpec((rows, D_MODEL), lambda i, s: (i, 0)),
                      pl.BlockSpec((rows, LANES), lambda i, s: (i, 0)),
                      pl.BlockSpec((1, D_MODEL), lambda i, s: (0, 0)),
                      pl.BlockSpec(memory_space=pl.ANY)],
            out_specs=pl.BlockSpec((rows, D_MODEL), lambda i, s: (i, 0)),
            scratch_shapes=[pltpu.VMEM((2, 2, rows * ROW_CHUNKS, LANES), F32),
                            pltpu.VMEM((rows, D_MODEL), F32),
                            pltpu.SemaphoreType.DMA((2,))],
        ),
        out_shape=jax.ShapeDtypeStruct((n_tok, D_MODEL), F32),
        compiler_params=_cparams("arbitrary"),
        name="combine",
    )(slots, x1, wcol, g, ys)


def _experts_kernel(tile_e_ref, n_used_ref, xs_ref, wg_ref, wu_ref, wd_ref, ys_ref, x_ref):
    del tile_e_ref

    @pl.when(pl.program_id(0) < n_used_ref[0])
    def _():
        for c in range(ROW_CHUNKS):
            x_ref[:, c * LANES:(c + 1) * LANES] = _load_row_tiles_chunk(xs_ref, c).astype(BF16)
        x = x_ref[...]
        a = jnp.dot(x, wg_ref[...].astype(BF16), preferred_element_type=F32)
        u = jnp.dot(x, wu_ref[...].astype(BF16), preferred_element_type=F32)
        z = (a * jax.nn.sigmoid(a)) * u
        _store_row_tiles(ys_ref, jnp.dot(z.astype(BF16), wd_ref[...].astype(BF16), preferred_element_type=F32))

    @pl.when(pl.program_id(0) >= n_used_ref[0])
    def _():
        ys_ref[...] = jnp.zeros_like(ys_ref)


def _experts(tile_e, n_used, xs, wg, wu, wd, layer):
    n_slots = xs.shape[0] // ROW_CHUNKS
    te = EXP_TILE

    def row_map(i, tile_e, n_used):
        return (jnp.minimum(i, n_used[0] - 1), 0)

    def out_map(i, tile_e, n_used):
        return (i, 0)

    def w_map(i, tile_e, n_used):
        return (layer, tile_e[i], 0, 0)

    return pl.pallas_call(
        _experts_kernel,
        grid_spec=pltpu.PrefetchScalarGridSpec(
            num_scalar_prefetch=2,
            grid=(n_slots // te,),
            in_specs=[pl.BlockSpec((te * ROW_CHUNKS, LANES), row_map),
                      pl.BlockSpec((None, None, D_MODEL, D_EXPERT), w_map),
                      pl.BlockSpec((None, None, D_MODEL, D_EXPERT), w_map),
                      pl.BlockSpec((None, None, D_EXPERT, D_MODEL), w_map)],
            out_specs=pl.BlockSpec((te * ROW_CHUNKS, LANES), out_map),
            scratch_shapes=[pltpu.VMEM((te, D_MODEL), BF16)],
        ),
        out_shape=jax.ShapeDtypeStruct((n_slots * ROW_CHUNKS, LANES), F32),
        compiler_params=_cparams("arbitrary"),
        name="experts",
    )(tile_e, n_used, xs, wg, wu, wd)


def kernel(x, attn_norm_g, w_in, a_sink, w_branch_a, w_branch_b, w_out, ffn_norm_g,
           w_router_group, b_router_group, w_router_expert, b_router_expert,
           w_exp_gate, w_exp_up, w_exp_down, final_norm_g):
    batch, seq, d_model = x.shape
    depth = w_in.shape[0]
    n_tok = batch * seq
    assert d_model == D_MODEL and w_in.shape[2] == D_IN
    assert seq % (16 * B_SUB) == 0 and seq % TOK_TILE == 0 and n_tok % DISPATCH_TILE == 0
    assert n_tok < (1 << RANK_BITS)

    cos_t, sin_t = _rope_tables(seq)
    tables = (cos_t, sin_t,
              _residue_order(cos_t, 4, TOK_TILE), _residue_order(sin_t, 4, TOK_TILE),
              _residue_order(cos_t, 16, TOK_TILE), _residue_order(sin_t, 16, TOK_TILE))

    n_slots = 2 * n_tok + N_EXPERTS * EXP_TILE
    n_tiles = n_slots // EXP_TILE
    x2d = x.reshape(n_tok, D_MODEL)

    for l in range(depth):
        nat, gates, grp1, grp2 = _in_proj(x2d, attn_norm_g[l][None, :], w_in[l].astype(BF16), tables, batch, seq)
        nat3d = nat.reshape(batch, seq, NAT_COLS)
        ya = _attn_a(nat3d, a_sink[l]).reshape(n_tok, A_Q_DIM)
        o0, l0 = _attn_b(nat3d, NAT_B // B_DIM, "attn_b1")
        o1, l1 = _attn_b(grp1.reshape(batch * 4, seq // 4, GRP_COLS), 0, "attn_b4")
        o2, l2 = _attn_b(grp2.reshape(batch * 16, seq // 16, GRP_COLS), 0, "attn_b16")

        wr = jnp.zeros((D_MODEL, LANES), F32)
        wr = wr.at[:, 0:N_EXPERTS].set(w_router_expert[l]).at[:, N_EXPERTS:N_EXPERTS + MOE_GROUPS].set(w_router_group[l])
        br = jnp.zeros((1, LANES), F32)
        br = br.at[0, 0:N_EXPERTS].set(b_router_expert[l]).at[0, N_EXPERTS:N_EXPERTS + MOE_GROUPS].set(b_router_group[l])
        wr_hi = wr.astype(BF16)
        wr_lo = (wr - wr_hi.astype(F32)).astype(BF16)

        x1, h2, code, wcol, cnt = _merge(
            x2d, ya, o0.reshape(n_tok, B_DIM), l0.reshape(n_tok, B_DIM),
            o1.reshape(batch, 4, seq // 4, B_DIM), l1.reshape(batch, 4, seq // 4, B_DIM),
            o2.reshape(batch, 16, seq // 16, B_DIM), l2.reshape(batch, 16, seq // 16, B_DIM),
            gates, w_branch_a[l].astype(BF16), w_branch_b[l].astype(BF16), w_out[l].astype(BF16),
            ffn_norm_g[l][None, :], wr_hi, wr_lo, br, batch, seq)

        counts = cnt[0, 0:N_EXPERTS].astype(I32)
        padded = ((counts + EXP_TILE - 1) // EXP_TILE) * EXP_TILE
        ends = jnp.cumsum(padded)
        offs = ends - padded
        n_used = (ends[-1:] // EXP_TILE).astype(I32)
        tile_start = jnp.arange(n_tiles, dtype=I32) * EXP_TILE
        tile_e = jnp.minimum(jnp.sum((ends[None, :] <= tile_start[:, None]).astype(I32), axis=1), N_EXPERTS - 1)
        eid = code[0:2] >> RANK_BITS
        rank = code[0:2] & ((1 << RANK_BITS) - 1)
        expert_ids = jnp.arange(N_EXPERTS, dtype=I32)[:, None, None]
        slots = (rank + jnp.sum(jnp.where(eid[None] == expert_ids, offs[:, None, None], 0), axis=0)).reshape(-1)

        xs = _dispatch(slots, h2, jnp.zeros((n_slots * ROW_CHUNKS, LANES), F32))
        ys = _experts(tile_e, n_used, xs, w_exp_gate, w_exp_up, w_exp_down, l)
        x2d = _combine(slots, x1, wcol, final_norm_g[None, :], ys, final=(l == depth - 1))

    return x2d.reshape(batch, seq, D_MODEL)
```

```python
import functools

import jax
import jax.numpy as jnp
from jax import lax
from jax.experimental import pallas as pl
from jax.experimental.pallas import tpu as pltpu

F32 = jnp.float32
BF16 = jnp.bfloat16
I32 = jnp.int32

D_MODEL = 1024
HEAD_DIM = 64
HALF_HEAD = HEAD_DIM // 2
ROPE_THETA = 10000.0
NORM_EPS = 1e-6
NEG_INF = -1e30
LANES = 128

A_Q_HEADS = 8
A_KV_HEADS = 2
A_GROUP = A_Q_HEADS // A_KV_HEADS
A_HALF_WINDOW = 128
A_Q_DIM = A_Q_HEADS * HEAD_DIM
A_KV_DIM = A_KV_HEADS * HEAD_DIM

B_GROUPS = ((128, 1), (512, 4), (2048, 16))
B_HEADS = 4
B_DIM = B_HEADS * HEAD_DIM
B_HALF_WINDOW = 64

MOE_GROUPS = 4
EXPERTS_PER_GROUP = 8
N_EXPERTS = MOE_GROUPS * EXPERTS_PER_GROUP
D_EXPERT = 256

NAT_IN_COLS = A_Q_DIM + 2 * A_KV_DIM + 3 * B_DIM
NAT_B = A_Q_DIM
NAT_AK = NAT_B + 3 * B_DIM
NAT_AV = NAT_AK + 2 * A_KV_DIM
NAT_COLS = NAT_AV + A_KV_DIM
GRP_COLS = 3 * B_DIM
COL_G1 = NAT_IN_COLS
COL_G2 = COL_G1 + GRP_COLS
COL_GATE = COL_G2 + GRP_COLS
GATE_COLS = 2 * D_MODEL
D_IN = COL_GATE + GATE_COLS

TOK_TILE = 512
A_Q_TILE = 512
B_SUB = 128
EXP_TILE = 512
ROW_TILE = 256
DISPATCH_TILE = 1024
ROW_CHUNKS = D_MODEL // LANES
DMA_UNROLL = 8
RANK_BITS = 16
VMEM_LIMIT = 56 * 1024 * 1024


def _cparams(*sem):
    return pltpu.CompilerParams(dimension_semantics=sem, vmem_limit_bytes=VMEM_LIMIT)


def _rope_tables(seq_len):
    inv = 1.0 / (ROPE_THETA ** (jnp.arange(0, HEAD_DIM, 2, dtype=F32) / HEAD_DIM))
    ang = jnp.arange(seq_len, dtype=F32)[:, None] * inv[None, :]
    cos, sin = jnp.cos(ang), jnp.sin(ang)
    cos_t = jnp.concatenate([cos, cos, cos, cos], axis=-1)
    sin_t = jnp.concatenate([-sin, sin, -sin, sin], axis=-1)
    return cos_t, sin_t


def _residue_order(table, dilation, tile):
    s, c = table.shape
    return table.reshape(s // tile, tile // dilation, dilation, c).transpose(0, 2, 1, 3).reshape(s, c)


def _rope(t, cos, sin_signed, first_half):
    partner = jnp.where(first_half, pltpu.roll(t, LANES - HALF_HEAD, 1), pltpu.roll(t, HALF_HEAD, 1))
    return t * cos + partner * sin_signed


Q_KIND, K_KIND, V_KIND = 0, 1, 2
_NAT_KINDS = ([Q_KIND] * 4 + [K_KIND] + [V_KIND] + [Q_KIND] * 2 + [K_KIND] * 2 + [V_KIND] * 2)
_GRP_KINDS = [Q_KIND] * 2 + [K_KIND] * 2 + [V_KIND] * 2


def _in_proj_kernel(x_ref, g_ref, w_ref, c1_ref, s1_ref, c4_ref, s4_ref, c16_ref, s16_ref,
                    nat_ref, gate_ref, g1_ref, g2_ref, hf_ref, hb_ref, hd_ref):
    tm = x_ref.shape[0]
    x = x_ref[...]
    h = x * lax.rsqrt(jnp.mean(x * x, axis=-1, keepdims=True) + NORM_EPS) * g_ref[...]
    n_chunks = D_MODEL // LANES
    for c in range(n_chunks):
        hf_ref[c] = h[:, c * LANES:(c + 1) * LANES]
    hb_ref[...] = h.astype(BF16)
    lane = lax.broadcasted_iota(I32, (1, LANES), 1)
    first_half = (lane % HEAD_DIM) < HALF_HEAD

    def project(h_b, col0, kinds, cos_ref, sin_ref, store):
        width = 512
        for c0 in range(0, len(kinds) * LANES, width):
            w = min(width, len(kinds) * LANES - c0)
            res = jnp.dot(h_b, w_ref[:, col0 + c0:col0 + c0 + w], preferred_element_type=F32)
            for j in range(w // LANES):
                kind = kinds[(c0 // LANES) + j]
                t = res[:, j * LANES:(j + 1) * LANES]
                if kind != V_KIND:
                    t = _rope(t, cos_ref[...], sin_ref[...], first_half)
                if kind == Q_KIND:
                    t = t * (HEAD_DIM ** -0.5)
                store(c0 + j * LANES, t)

    low_head = lane < HEAD_DIM

    def store_nat(c, t):
        if A_Q_DIM <= c < A_Q_DIM + A_KV_DIM:
            swapped = pltpu.roll(t, HEAD_DIM, 1)
            nat_ref[:, NAT_AK:NAT_AK + LANES] = jnp.where(low_head, t, swapped).astype(BF16)
            nat_ref[:, NAT_AK + LANES:NAT_AK + 2 * LANES] = jnp.where(low_head, swapped, t).astype(BF16)
        elif c < A_Q_DIM + 2 * A_KV_DIM:
            out = c if c < A_Q_DIM else NAT_AV
            nat_ref[:, out:out + LANES] = t.astype(BF16)
        else:
            out = c - 2 * A_KV_DIM
            nat_ref[:, out:out + LANES] = t.astype(BF16)

    project(hb_ref[...], 0, _NAT_KINDS, c1_ref, s1_ref, store_nat)

    for c0 in range(0, GATE_COLS, 512):
        res = jnp.dot(hb_ref[...], w_ref[:, COL_GATE + c0:COL_GATE + c0 + 512], preferred_element_type=F32)
        gate_ref[:, c0:c0 + 512] = jax.nn.sigmoid(res).astype(BF16)

    for dil, col0, cos_ref, sin_ref, out_ref in ((4, COL_G1, c4_ref, s4_ref, g1_ref),
                                                 (16, COL_G2, c16_ref, s16_ref, g2_ref)):
        n = tm // dil
        for r in range(dil):
            for c in range(n_chunks):
                hd_ref[r * n:(r + 1) * n, c * LANES:(c + 1) * LANES] = (
                    hf_ref[c, pl.ds(r, n, stride=dil), :].astype(BF16))

        def store_grp(c, t, out_ref=out_ref, dil=dil, n=n):
            v = t.astype(BF16)
            for r in range(dil):
                out_ref[r, :, c:c + LANES] = v[r * n:(r + 1) * n]

        project(hd_ref[...], col0, _GRP_KINDS, cos_ref, sin_ref, store_grp)


def _in_proj(x2d, g, w_bf16, tables, batch, seq):
    tm = TOK_TILE
    tiles_per_seq = seq // tm
    n_tok = batch * seq
    c1, s1, c4, s4, c16, s16 = tables
    tab_spec = pl.BlockSpec((tm, LANES), lambda i: (i % tiles_per_seq, 0))
    return pl.pallas_call(
        _in_proj_kernel,
        grid=(n_tok // tm,),
        in_specs=[
            pl.BlockSpec((tm, D_MODEL), lambda i: (i, 0)),
            pl.BlockSpec((1, D_MODEL), lambda i: (0, 0)),
            pl.BlockSpec((D_MODEL, D_IN), lambda i: (0, 0), pipeline_mode=pl.Buffered(1)),
            tab_spec, tab_spec, tab_spec, tab_spec, tab_spec, tab_spec,
        ],
        out_specs=[
            pl.BlockSpec((tm, NAT_COLS), lambda i: (i, 0)),
            pl.BlockSpec((tm, GATE_COLS), lambda i: (i, 0)),
            pl.BlockSpec((None, 4, tm // 4, GRP_COLS), lambda i: (i // tiles_per_seq, 0, i % tiles_per_seq, 0)),
            pl.BlockSpec((None, 16, tm // 16, GRP_COLS), lambda i: (i // tiles_per_seq, 0, i % tiles_per_seq, 0)),
        ],
        out_shape=[
            jax.ShapeDtypeStruct((n_tok, NAT_COLS), BF16),
            jax.ShapeDtypeStruct((n_tok, GATE_COLS), BF16),
            jax.ShapeDtypeStruct((batch, 4, seq // 4, GRP_COLS), BF16),
            jax.ShapeDtypeStruct((batch, 16, seq // 16, GRP_COLS), BF16),
        ],
        scratch_shapes=[
            pltpu.VMEM((D_MODEL // LANES, tm, LANES), F32),
            pltpu.VMEM((tm, D_MODEL), BF16),
            pltpu.VMEM((tm, D_MODEL), BF16),
        ],
        compiler_params=_cparams("parallel"),
        name="in_proj",
    )(x2d, g, w_bf16, c1, s1, c4, s4, c16, s16)


def _masked_heads(q_pair, low_head, high_head):
    zero = jnp.zeros_like(q_pair)
    return [jnp.where(low_head, q_pair, zero), jnp.where(high_head, q_pair, zero)]


def _attn_a_kernel(sink_ref, q_ref, kp_ref, km_ref, kn_ref, vp_ref, vm_ref, vn_ref, o_ref, k_ref, vt_ref, *, seq):
    tq = q_ref.shape[0]
    hw = A_HALF_WINDOW
    n_sub = tq // hw
    i = pl.program_id(1)
    last_blk = seq // hw - 1
    k_ref[0:hw, :] = kp_ref[...]
    k_ref[hw:hw + tq, :] = km_ref[...]
    k_ref[hw + tq:tq + 2 * hw, :] = kn_ref[...]
    for r0, src in ((0, vp_ref), (hw, vm_ref), (hw + tq, vn_ref)):
        vt_ref[:, r0:r0 + src.shape[0]] = src[...].astype(F32).T.astype(BF16)
    grp_cols = A_GROUP * hw
    key = lax.broadcasted_iota(I32, (hw, grp_cols), 0)
    qry = lax.broadcasted_iota(I32, (hw, grp_cols), 1) % hw
    low_head = lax.broadcasted_iota(I32, (1, LANES), 1) < HEAD_DIM
    high_head = jnp.logical_not(low_head)
    for sb in range(n_sub):
        r0 = sb * hw
        blk = i * n_sub + sb
        mask_p = (key >= qry) if sb > 0 else (key >= qry + jnp.where(blk > 0, 0, hw))
        mask_n = (key <= qry) if sb < n_sub - 1 else (key <= qry - jnp.where(blk < last_blk, 0, hw))
        out_t = []
        for g in range(A_KV_HEADS):
            heads = range(g * A_GROUP, (g + 1) * A_GROUP)
            q_parts = []
            for c in range(g * A_GROUP // 2, (g + 1) * A_GROUP // 2):
                q_parts += _masked_heads(q_ref[r0:r0 + hw, c * LANES:(c + 1) * LANES], low_head, high_head)
            q = jnp.concatenate(q_parts, axis=0)
            sink = jnp.concatenate([jnp.full((1, hw), sink_ref[h], F32) for h in heads], axis=1)
            k = k_ref[r0:r0 + 3 * hw, g * LANES:(g + 1) * LANES]
            s = lax.dot_general(k, q, (((1,), (1,)), ((), ())), preferred_element_type=F32)
            sp = jnp.where(mask_p, s[0:hw], NEG_INF)
            so = s[hw:2 * hw]
            sn = jnp.where(mask_n, s[2 * hw:3 * hw], NEG_INF)
            m = jnp.max(jnp.maximum(jnp.maximum(sp, so), sn), axis=0, keepdims=True)
            m = jnp.maximum(m, sink)
            pp, po, pn = jnp.exp(sp - m), jnp.exp(so - m), jnp.exp(sn - m)
            denom = jnp.sum(pp + po + pn, axis=0, keepdims=True) + jnp.exp(sink - m)
            p = jnp.concatenate([pp, po, pn], axis=0).astype(BF16)
            vt = vt_ref[g * HEAD_DIM:(g + 1) * HEAD_DIM, r0:r0 + 3 * hw]
            o = jnp.dot(vt, p, preferred_element_type=F32) * (1.0 / denom)
            out_t += [o[:, j * hw:(j + 1) * hw] for j in range(A_GROUP)]
        o_ref[r0:r0 + hw, :] = jnp.concatenate(out_t, axis=0).T.astype(BF16)


def _attn_a(nat3d, sink):
    batch, seq, _ = nat3d.shape
    tq = A_Q_TILE
    hw = A_HALF_WINDOW
    per = tq // hw
    n_hw = seq // hw
    k_cols, v_cols = 2 * A_KV_DIM, A_KV_DIM
    k_blk, v_blk = NAT_AK // k_cols, NAT_AV // v_cols

    def prev_spec(cols, blk):
        return pl.BlockSpec((None, hw, cols), lambda b, i: (b, jnp.maximum(i * per - 1, 0), blk))

    def main_spec(cols, blk):
        return pl.BlockSpec((None, tq, cols), lambda b, i: (b, i, blk))

    def next_spec(cols, blk):
        return pl.BlockSpec((None, hw, cols), lambda b, i: (b, jnp.minimum((i + 1) * per, n_hw - 1), blk))

    return pl.pallas_call(
        functools.partial(_attn_a_kernel, seq=seq),
        grid=(batch, seq // tq),
        in_specs=[
            pl.BlockSpec(memory_space=pltpu.SMEM),
            main_spec(A_Q_DIM, 0),
            prev_spec(k_cols, k_blk), main_spec(k_cols, k_blk), next_spec(k_cols, k_blk),
            prev_spec(v_cols, v_blk), main_spec(v_cols, v_blk), next_spec(v_cols, v_blk),
        ],
        out_specs=pl.BlockSpec((None, tq, A_Q_DIM), lambda b, i: (b, i, 0)),
        out_shape=jax.ShapeDtypeStruct((batch, seq, A_Q_DIM), BF16),
        scratch_shapes=[pltpu.VMEM((tq + 2 * hw, k_cols), BF16), pltpu.VMEM((v_cols, tq + 2 * hw), BF16)],
        compiler_params=_cparams("parallel", "parallel"),
        name="attn_a",
    )(sink, nat3d, nat3d, nat3d, nat3d, nat3d, nat3d, nat3d)


def _attn_b_kernel(q_ref, kp_ref, km_ref, kn_ref, vp_ref, vm_ref, vn_ref, o_ref, lse_ref,
                   k_ref, vt_ref, *, sub_len):
    tq = q_ref.shape[0]
    hw = B_HALF_WINDOW
    t0 = pl.program_id(1) * tq
    k_ref[0:hw, :] = kp_ref[...]
    k_ref[hw:hw + tq, :] = km_ref[...]
    k_ref[hw + tq:tq + 2 * hw, :] = kn_ref[...]
    for r0, src in ((0, vp_ref), (hw, vm_ref), (hw + tq, vn_ref)):
        vt_ref[:, r0:r0 + src.shape[0]] = src[...].astype(F32).T.astype(BF16)
    kw = B_SUB + 2 * hw
    n_sub = tq // B_SUB
    all_cols = B_HEADS * B_SUB
    key = lax.broadcasted_iota(I32, (kw, all_cols), 0)
    qry = lax.broadcasted_iota(I32, (kw, all_cols), 1) % B_SUB
    in_band = jnp.abs(key - hw - qry) <= hw
    low_head = lax.broadcasted_iota(I32, (1, LANES), 1) < HEAD_DIM
    high_head = jnp.logical_not(low_head)
    for sb in range(n_sub):
        r0 = sb * B_SUB
        valid = in_band
        if sb == 0:
            valid = valid & (t0 - hw + key >= 0)
        if sb == n_sub - 1:
            valid = valid & (t0 + r0 - hw + key < sub_len)
        s_parts = []
        for c in range(B_HEADS // 2):
            cs = slice(c * LANES, (c + 1) * LANES)
            q = jnp.concatenate(_masked_heads(q_ref[r0:r0 + B_SUB, cs], low_head, high_head), axis=0)
            s_parts.append(lax.dot_general(k_ref[r0:r0 + kw, cs], q, (((1,), (1,)), ((), ())),
                                           preferred_element_type=F32))
        s = jnp.where(valid, jnp.concatenate(s_parts, axis=1), NEG_INF)
        m = jnp.max(s, axis=0, keepdims=True)
        p = jnp.exp(s - m)
        denom = jnp.sum(p, axis=0, keepdims=True)
        p = p.astype(BF16)
        inv = 1.0 / denom
        lse = m + jnp.log(denom)
        out_t, lse_t = [], []
        for h in range(B_HEADS):
            qs = slice(h * B_SUB, (h + 1) * B_SUB)
            vt = vt_ref[h * HEAD_DIM:(h + 1) * HEAD_DIM, r0:r0 + kw]
            out_t.append(jnp.dot(vt, p[:, qs], preferred_element_type=F32) * inv[:, qs])
            lse_t.append(jnp.broadcast_to(lse[:, qs], (HEAD_DIM, B_SUB)))
        o_ref[r0:r0 + B_SUB, :] = jnp.concatenate(out_t, axis=0).T
        lse_ref[r0:r0 + B_SUB, :] = jnp.concatenate(lse_t, axis=0).T


def _attn_b(arr3d, q_blk, name):
    n_sub, sub_len, _ = arr3d.shape
    tq = min(512, sub_len)
    hw = B_HALF_WINDOW
    per = tq // hw
    n_hw = sub_len // hw

    def main_spec(c):
        return pl.BlockSpec((None, tq, B_DIM), lambda g, i: (g, i, c))

    def prev_spec(c):
        return pl.BlockSpec((None, hw, B_DIM), lambda g, i: (g, jnp.maximum(i * per - 1, 0), c))

    def next_spec(c):
        return pl.BlockSpec((None, hw, B_DIM), lambda g, i: (g, jnp.minimum((i + 1) * per, n_hw - 1), c))

    out_spec = pl.BlockSpec((None, tq, B_DIM), lambda g, i: (g, i, 0))
    return pl.pallas_call(
        functools.partial(_attn_b_kernel, sub_len=sub_len),
        grid=(n_sub, sub_len // tq),
        in_specs=[main_spec(q_blk),
                  prev_spec(q_blk + 1), main_spec(q_blk + 1), next_spec(q_blk + 1),
                  prev_spec(q_blk + 2), main_spec(q_blk + 2), next_spec(q_blk + 2)],
        out_specs=[out_spec, out_spec],
        out_shape=[jax.ShapeDtypeStruct((n_sub, sub_len, B_DIM), F32)] * 2,
        scratch_shapes=[pltpu.VMEM((tq + 2 * hw, B_DIM), BF16), pltpu.VMEM((B_DIM, tq + 2 * hw), BF16)],
        compiler_params=_cparams("parallel", "parallel"),
        name=name,
    )(arr3d, arr3d, arr3d, arr3d, arr3d, arr3d, arr3d)


def _merge_kernel(x_ref, ya_ref, o0_ref, l0_ref, o1_ref, l1_ref, o2_ref, l2_ref, gate_ref,
                  wa_ref, wb_ref, wo_ref, g2_ref, wrh_ref, wrl_ref, br_ref,
                  x1_ref, h2_ref, code_ref, wcol_ref, cnt_ref,
                  so1_ref, sl1_ref, so2_ref, sl2_ref, yb_ref, run_ref):
    tm = x_ref.shape[0]

    @pl.when(pl.program_id(0) == 0)
    def _():
        run_ref[...] = jnp.zeros_like(run_ref)

    for dil, src_o, src_l, dst_o, dst_l in ((4, o1_ref, l1_ref, so1_ref, sl1_ref),
                                            (16, o2_ref, l2_ref, so2_ref, sl2_ref)):
        n = tm // dil
        for r in range(dil):
            for c in range(B_DIM // LANES):
                cs = slice(c * LANES, (c + 1) * LANES)
                dst_o[c, pl.ds(r, n, stride=dil), :] = src_o[r, :, cs]
                dst_l[c, pl.ds(r, n, stride=dil), :] = src_l[r, :, cs]

    for c in range(B_DIM // LANES):
        cs = slice(c * LANES, (c + 1) * LANES)
        l0, l1, l2 = l0_ref[:, cs], sl1_ref[c], sl2_ref[c]
        m = jnp.maximum(jnp.maximum(l0, l1), l2)
        e0, e1, e2 = jnp.exp(l0 - m), jnp.exp(l1 - m), jnp.exp(l2 - m)
        yb = (e0 * o0_ref[:, cs] + e1 * so1_ref[c] + e2 * so2_ref[c]) / (e0 + e1 + e2)
        yb_ref[:, cs] = yb.astype(BF16)

    ya_p = jnp.dot(ya_ref[...], wa_ref[...], preferred_element_type=F32)
    yb_p = jnp.dot(yb_ref[...], wb_ref[...], preferred_element_type=F32)
    merged = gate_ref[:, 0:D_MODEL].astype(F32) * ya_p + gate_ref[:, D_MODEL:GATE_COLS].astype(F32) * yb_p
    x1 = x_ref[...] + jnp.dot(merged.astype(BF16), wo_ref[...], preferred_element_type=F32)
    x1_ref[...] = x1

    h2 = x1 * lax.rsqrt(jnp.mean(x1 * x1, axis=-1, keepdims=True) + NORM_EPS) * g2_ref[...]
    _store_row_tiles(h2_ref, h2)

    h_hi = h2.astype(BF16)
    h_lo = (h2 - h_hi.astype(F32)).astype(BF16)
    logits = (jnp.dot(h_hi, wrh_ref[...], preferred_element_type=F32)
              + jnp.dot(h_lo, wrh_ref[...], preferred_element_type=F32)
              + jnp.dot(h_hi, wrl_ref[...], preferred_element_type=F32)) + br_ref[...]

    lane = lax.broadcasted_iota(I32, (tm, LANES), 1)
    is_grp = (lane >= N_EXPERTS) & (lane < N_EXPERTS + MOE_GROUPS)
    gl = jnp.where(is_grp, logits, -jnp.inf)
    gmax = jnp.max(gl, axis=-1, keepdims=True)
    gidx = jnp.min(jnp.where(gl == gmax, lane - N_EXPERTS, LANES), axis=-1, keepdims=True)
    gw = 1.0 / jnp.sum(jnp.exp(gl - gmax), axis=-1, keepdims=True)
    in_grp = (lane < N_EXPERTS) & ((lane // EXPERTS_PER_GROUP) == gidx)
    el = jnp.where(in_grp, logits, -jnp.inf)
    v1 = jnp.max(el, axis=-1, keepdims=True)
    i1 = jnp.min(jnp.where(el == v1, lane, LANES), axis=-1, keepdims=True)
    el2 = jnp.where(lane == i1, -jnp.inf, el)
    v2 = jnp.max(el2, axis=-1, keepdims=True)
    i2 = jnp.min(jnp.where(el2 == v2, lane, LANES), axis=-1, keepdims=True)
    t = jnp.exp(v2 - v1)
    w1 = gw / (1.0 + t)
    w2 = gw * t / (1.0 + t)

    oh1 = (lane == i1).astype(F32)
    oh2 = (lane == i2).astype(F32)
    oh = oh1 + oh2
    row = lax.broadcasted_iota(I32, (tm, tm), 0)
    col = lax.broadcasted_iota(I32, (tm, tm), 1)
    lower = (col < row).astype(BF16)
    before = jnp.dot(lower, oh.astype(BF16), preferred_element_type=F32) + run_ref[0:1, :]
    rank1 = jnp.sum(before * oh1, axis=-1, keepdims=True)
    rank2 = jnp.sum(before * oh2, axis=-1, keepdims=True)
    run_ref[...] = run_ref[...] + jnp.sum(oh, axis=0, keepdims=True)
    cnt_ref[...] = run_ref[...]

    scale = float(1 << RANK_BITS)
    code1 = i1.astype(F32) * scale + rank1
    code2 = i2.astype(F32) * scale + rank2
    meta = jnp.where(lane == 0, code1, jnp.where(lane == 1, code2, 0.0))
    code_ref[...] = meta.T[0:8, :].astype(I32)
    wcol_ref[...] = jnp.where(lane == 0, w1, jnp.where(lane == 1, w2, 0.0))


def _merge(x2d, ya, o0, l0, o1, l1, o2, l2, gates, wa, wb, wo, g2, wr_hi, wr_lo, br, batch, seq):
    tm = TOK_TILE
    tps = seq // tm
    n_tok = batch * seq

    def tok(c):
        return pl.BlockSpec((tm, c), lambda i: (i, 0))

    def full(a):
        return pl.BlockSpec(a.shape, lambda i: (0,) * a.ndim)

    def res_spec(d):
        return pl.BlockSpec((None, d, tm // d, B_DIM), lambda i: (i // tps, 0, i % tps, 0))

    return pl.pallas_call(
        _merge_kernel,
        grid=(n_tok // tm,),
        in_specs=[tok(D_MODEL), tok(A_Q_DIM), tok(B_DIM), tok(B_DIM),
                  res_spec(4), res_spec(4), res_spec(16), res_spec(16), tok(GATE_COLS),
                  full(wa), full(wb), full(wo), full(g2), full(wr_hi), full(wr_lo), full(br)],
        out_specs=[tok(D_MODEL), pl.BlockSpec((tm * ROW_CHUNKS, LANES), lambda i: (i, 0)),
                   pl.BlockSpec((8, tm), lambda i: (0, i)),
                   tok(LANES),
                   pl.BlockSpec((8, LANES), lambda i: (0, 0))],
        out_shape=[jax.ShapeDtypeStruct((n_tok, D_MODEL), F32),
                   jax.ShapeDtypeStruct((n_tok * ROW_CHUNKS, LANES), F32),
                   jax.ShapeDtypeStruct((8, n_tok), I32),
                   jax.ShapeDtypeStruct((n_tok, LANES), F32),
                   jax.ShapeDtypeStruct((8, LANES), F32)],
        scratch_shapes=([pltpu.VMEM((B_DIM // LANES, tm, LANES), F32)] * 4
                        + [pltpu.VMEM((tm, B_DIM), BF16), pltpu.VMEM((8, LANES), F32)]),
        compiler_params=_cparams("arbitrary"),
        name="merge_route",
    )(x2d, ya, o0, l0, o1, l1, o2, l2, gates, wa, wb, wo, g2, wr_hi, wr_lo, br)


def _row_tile(ref, t):
    return ref.at[pl.ds(pl.multiple_of(t * ROW_CHUNKS, ROW_CHUNKS), ROW_CHUNKS)]


def _store_row_tiles(ref, val):
    rows = val.shape[0]
    for c in range(ROW_CHUNKS):
        ref[pl.ds(c, rows, stride=ROW_CHUNKS), :] = val[:, c * LANES:(c + 1) * LANES]


def _load_row_tiles_chunk(ref, c):
    return ref[pl.ds(c, ref.shape[0] // ROW_CHUNKS, stride=ROW_CHUNKS), :]


def _dispatch_kernel(slot_ref, h_ref, xs_in_ref, xs_ref, sem, *, n_tok):
    del xs_in_ref
    i = pl.program_id(0)
    rows = h_ref.shape[0] // ROW_CHUNKS

    def issue(j, carry):
        t = i * rows + j
        for k in range(2):
            pltpu.make_async_copy(_row_tile(h_ref, j), _row_tile(xs_ref, slot_ref[k * n_tok + t]), sem).start()
        return carry

    lax.fori_loop(0, rows, issue, 0, unroll=DMA_UNROLL)
    for _ in range(2):
        pltpu.make_async_copy(h_ref, xs_ref.at[pl.ds(0, rows * ROW_CHUNKS)], sem).wait()


def _dispatch(slots, h2, xs_init):
    n_tok = h2.shape[0] // ROW_CHUNKS
    rows = DISPATCH_TILE
    return pl.pallas_call(
        functools.partial(_dispatch_kernel, n_tok=n_tok),
        grid_spec=pltpu.PrefetchScalarGridSpec(
            num_scalar_prefetch=1,
            grid=(n_tok // rows,),
            in_specs=[pl.BlockSpec((rows * ROW_CHUNKS, LANES), lambda i, s: (i, 0)),
                      pl.BlockSpec(memory_space=pl.ANY)],
            out_specs=pl.BlockSpec(memory_space=pl.ANY),
            scratch_shapes=[pltpu.SemaphoreType.DMA(())],
        ),
        out_shape=jax.ShapeDtypeStruct(xs_init.shape, xs_init.dtype),
        input_output_aliases={2: 0},
        compiler_params=_cparams("arbitrary"),
        name="dispatch",
    )(slots, h2, xs_init)


def _combine_kernel(slot_ref, x_ref, w_ref, g_ref, ys_ref, o_ref, buf_ref, y_ref, sem, *, n_tok, final):
    i = pl.program_id(0)
    n_steps = pl.num_programs(0)
    rows = x_ref.shape[0]

    def issue_tile(tile, slot):
        def issue(j, carry):
            t = tile * rows + j
            for k in range(2):
                pltpu.make_async_copy(_row_tile(ys_ref, slot_ref[k * n_tok + t]),
                                      _row_tile(buf_ref.at[slot, k], j), sem.at[slot]).start()
            return carry

        lax.fori_loop(0, rows, issue, 0, unroll=DMA_UNROLL)

    @pl.when(i == 0)
    def _():
        issue_tile(0, 0)

    @pl.when(i + 1 < n_steps)
    def _():
        issue_tile(i + 1, (i + 1) % 2)

    cur = i % 2
    for k in range(2):
        pltpu.make_async_copy(ys_ref.at[pl.ds(0, rows * ROW_CHUNKS)], buf_ref.at[cur, k], sem.at[cur]).wait()
    w = w_ref[...]
    w1, w2 = w[:, 0:1], w[:, 1:2]
    for c in range(ROW_CHUNKS):
        cs = slice(c * LANES, (c + 1) * LANES)
        y1 = _load_row_tiles_chunk(buf_ref.at[cur, 0], c)
        y2 = _load_row_tiles_chunk(buf_ref.at[cur, 1], c)
        y_ref[:, cs] = x_ref[:, cs] + w1 * y1 + w2 * y2
    y = y_ref[...]
    if final:
        y = y * lax.rsqrt(jnp.mean(y * y, axis=-1, keepdims=True) + NORM_EPS) * g_ref[...]
    o_ref[...] = y


def _combine(slots, x1, wcol, g, ys, final):
    n_tok = x1.shape[0]
    rows = ROW_TILE
    return pl.pallas_call(
        functools.partial(_combine_kernel, n_tok=n_tok, final=final),
        grid_spec=pltpu.PrefetchScalarGridSpec(
            num_scalar_prefetch=1,
            grid=(n_tok // rows,),
            in_specs=[pl.BlockSpec((rows, D_MODEL), lambda i, s: (i, 0)),
                      pl.BlockSpec((rows, LANES), lambda i, s: (i, 0)),
                      pl.BlockSpec((1, D_MODEL), lambda i, s: (0, 0)),
                      pl.BlockSpec(memory_space=pl.ANY)],
            out_specs=pl.BlockSpec((rows, D_MODEL), lambda i, s: (i, 0)),
            scratch_shapes=[pltpu.VMEM((2, 2, rows * ROW_CHUNKS, LANES), F32),
                            pltpu.VMEM((rows, D_MODEL), F32),
                            pltpu.SemaphoreType.DMA((2,))],
        ),
        out_shape=jax.ShapeDtypeStruct((n_tok, D_MODEL), F32),
        compiler_params=_cparams("arbitrary"),
        name="combine",
    )(slots, x1, wcol, g, ys)


def _experts_kernel(tile_e_ref, n_used_ref, xs_ref, wg_ref, wu_ref, wd_ref, ys_ref, x_ref):
    del tile_e_ref

    @pl.when(pl.program_id(0) < n_used_ref[0])
    def _():
        for c in range(ROW_CHUNKS):
            x_ref[:, c * LANES:(c + 1) * LANES] = _load_row_tiles_chunk(xs_ref, c).astype(BF16)
        x = x_ref[...]
        a = jnp.dot(x, wg_ref[...].astype(BF16), preferred_element_type=F32)
        u = jnp.dot(x, wu_ref[...].astype(BF16), preferred_element_type=F32)
        z = (a * jax.nn.sigmoid(a)) * u
        _store_row_tiles(ys_ref, jnp.dot(z.astype(BF16), wd_ref[...].astype(BF16), preferred_element_type=F32))

    @pl.when(pl.program_id(0) >= n_used_ref[0])
    def _():
        ys_ref[...] = jnp.zeros_like(ys_ref)


def _experts(tile_e, n_used, xs, wg, wu, wd, layer):
    n_slots = xs.shape[0] // ROW_CHUNKS
    te = EXP_TILE

    def row_map(i, tile_e, n_used):
        return (jnp.minimum(i, n_used[0] - 1), 0)

    def out_map(i, tile_e, n_used):
        return (i, 0)

    def w_map(i, tile_e, n_used):
        return (layer, tile_e[i], 0, 0)

    return pl.pallas_call(
        _experts_kernel,
        grid_spec=pltpu.PrefetchScalarGridSpec(
            num_scalar_prefetch=2,
            grid=(n_slots // te,),
            in_specs=[pl.BlockSpec((te * ROW_CHUNKS, LANES), row_map),
                      pl.BlockSpec((None, None, D_MODEL, D_EXPERT), w_map),
                      pl.BlockSpec((None, None, D_MODEL, D_EXPERT), w_map),
                      pl.BlockSpec((None, None, D_EXPERT, D_MODEL), w_map)],
            out_specs=pl.BlockSpec((te * ROW_CHUNKS, LANES), out_map),
            scratch_shapes=[pltpu.VMEM((te, D_MODEL), BF16)],
        ),
        out_shape=jax.ShapeDtypeStruct((n_slots * ROW_CHUNKS, LANES), F32),
        compiler_params=_cparams("arbitrary"),
        name="experts",
    )(tile_e, n_used, xs, wg, wu, wd)


def kernel(x, attn_norm_g, w_in, a_sink, w_branch_a, w_branch_b, w_out, ffn_norm_g,
           w_router_group, b_router_group, w_router_expert, b_router_expert,
           w_exp_gate, w_exp_up, w_exp_down, final_norm_g):
    batch, seq, d_model = x.shape
    depth = w_in.shape[0]
    n_tok = batch * seq
    assert d_model == D_MODEL and w_in.shape[2] == D_IN
    assert seq % (16 * B_SUB) == 0 and seq % TOK_TILE == 0 and n_tok % DISPATCH_TILE == 0
    assert n_tok < (1 << RANK_BITS)

    cos_t, sin_t = _rope_tables(seq)
    tables = (cos_t, sin_t,
              _residue_order(cos_t, 4, TOK_TILE), _residue_order(sin_t, 4, TOK_TILE),
              _residue_order(cos_t, 16, TOK_TILE), _residue_order(sin_t, 16, TOK_TILE))

    n_slots = 2 * n_tok + N_EXPERTS * EXP_TILE
    n_tiles = n_slots // EXP_TILE
    x2d = x.reshape(n_tok, D_MODEL)

    for l in range(depth):
        nat, gates, grp1, grp2 = _in_proj(x2d, attn_norm_g[l][None, :], w_in[l].astype(BF16), tables, batch, seq)
        nat3d = nat.reshape(batch, seq, NAT_COLS)
        ya = _attn_a(nat3d, a_sink[l]).reshape(n_tok, A_Q_DIM)
        o0, l0 = _attn_b(nat3d, NAT_B // B_DIM, "attn_b1")
        o1, l1 = _attn_b(grp1.reshape(batch * 4, seq // 4, GRP_COLS), 0, "attn_b4")
        o2, l2 = _attn_b(grp2.reshape(batch * 16, seq // 16, GRP_COLS), 0, "attn_b16")

        wr = jnp.zeros((D_MODEL, LANES), F32)
        wr = wr.at[:, 0:N_EXPERTS].set(w_router_expert[l]).at[:, N_EXPERTS:N_EXPERTS + MOE_GROUPS].set(w_router_group[l])
        br = jnp.zeros((1, LANES), F32)
        br = br.at[0, 0:N_EXPERTS].set(b_router_expert[l]).at[0, N_EXPERTS:N_EXPERTS + MOE_GROUPS].set(b_router_group[l])
        wr_hi = wr.astype(BF16)
        wr_lo = (wr - wr_hi.astype(F32)).astype(BF16)

        x1, h2, code, wcol, cnt = _merge(
            x2d, ya, o0.reshape(n_tok, B_DIM), l0.reshape(n_tok, B_DIM),
            o1.reshape(batch, 4, seq // 4, B_DIM), l1.reshape(batch, 4, seq // 4, B_DIM),
            o2.reshape(batch, 16, seq // 16, B_DIM), l2.reshape(batch, 16, seq // 16, B_DIM),
            gates, w_branch_a[l].astype(BF16), w_branch_b[l].astype(BF16), w_out[l].astype(BF16),
            ffn_norm_g[l][None, :], wr_hi, wr_lo, br, batch, seq)

        counts = cnt[0, 0:N_EXPERTS].astype(I32)
        padded = ((counts + EXP_TILE - 1) // EXP_TILE) * EXP_TILE
        ends = jnp.cumsum(padded)
        offs = ends - padded
        n_used = (ends[-1:] // EXP_TILE).astype(I32)
        tile_start = jnp.arange(n_tiles, dtype=I32) * EXP_TILE
        tile_e = jnp.minimum(jnp.sum((ends[None, :] <= tile_start[:, None]).astype(I32), axis=1), N_EXPERTS - 1)
        eid = code[0:2] >> RANK_BITS
        rank = code[0:2] & ((1 << RANK_BITS) - 1)
        expert_ids = jnp.arange(N_EXPERTS, dtype=I32)[:, None, None]
        slots = (rank + jnp.sum(jnp.where(eid[None] == expert_ids, offs[:, None, None], 0), axis=0)).reshape(-1)

        xs = _dispatch(slots, h2, jnp.zeros((n_slots * ROW_CHUNKS, LANES), F32))
        ys = _experts(tile_e, n_used, xs, w_exp_gate, w_exp_up, w_exp_down, l)
        x2d = _combine(slots, x1, wcol, final_norm_g[None, :], ys, final=(l == depth - 1))

    return x2d.reshape(batch, seq, D_MODEL)
```

```python
import functools

import jax
import jax.numpy as jnp
from jax import lax
from jax.experimental import pallas as pl
from jax.experimental.pallas import tpu as pltpu

F32 = jnp.float32
BF16 = jnp.bfloat16
I32 = jnp.int32

D_MODEL = 1024
HEAD_DIM = 64
HALF_HEAD = HEAD_DIM // 2
ROPE_THETA = 10000.0
NORM_EPS = 1e-6
NEG_INF = -1e30
LANES = 128

A_Q_HEADS = 8
A_KV_HEADS = 2
A_GROUP = A_Q_HEADS // A_KV_HEADS
A_HALF_WINDOW = 128
A_Q_DIM = A_Q_HEADS * HEAD_DIM
A_KV_DIM = A_KV_HEADS * HEAD_DIM

B_GROUPS = ((128, 1), (512, 4), (2048, 16))
B_HEADS = 4
B_DIM = B_HEADS * HEAD_DIM
B_HALF_WINDOW = 64

MOE_GROUPS = 4
EXPERTS_PER_GROUP = 8
N_EXPERTS = MOE_GROUPS * EXPERTS_PER_GROUP
D_EXPERT = 256

NAT_IN_COLS = A_Q_DIM + 2 * A_KV_DIM + 3 * B_DIM
NAT_B = A_Q_DIM
NAT_AK = NAT_B + 3 * B_DIM
NAT_AV = NAT_AK + 2 * A_KV_DIM
NAT_COLS = NAT_AV + A_KV_DIM
GRP_COLS = 3 * B_DIM
COL_G1 = NAT_IN_COLS
COL_G2 = COL_G1 + GRP_COLS
COL_GATE = COL_G2 + GRP_COLS
GATE_COLS = 2 * D_MODEL
D_IN = COL_GATE + GATE_COLS

TOK_TILE = 512
A_Q_TILE = 512
B_SUB = 128
EXP_TILE = 512
ROW_CHUNKS = D_MODEL // LANES
DMA_UNROLL = 8
RANK_BITS = 16
SRC_SHIFT = 16
VMEM_LIMIT = 56 * 1024 * 1024


def _cparams(*sem):
    return pltpu.CompilerParams(dimension_semantics=sem, vmem_limit_bytes=VMEM_LIMIT)


def _rope_tables(seq_len):
    inv = 1.0 / (ROPE_THETA ** (jnp.arange(0, HEAD_DIM, 2, dtype=F32) / HEAD_DIM))
    ang = jnp.arange(seq_len, dtype=F32)[:, None] * inv[None, :]
    cos, sin = jnp.cos(ang), jnp.sin(ang)
    cos_t = jnp.concatenate([cos, cos, cos, cos], axis=-1)
    sin_t = jnp.concatenate([-sin, sin, -sin, sin], axis=-1)
    return cos_t, sin_t


def _residue_order(table, dilation, tile):
    s, c = table.shape
    return table.reshape(s // tile, tile // dilation, dilation, c).transpose(0, 2, 1, 3).reshape(s, c)


def _rope(t, cos, sin_signed, first_half):
    partner = jnp.where(first_half, pltpu.roll(t, LANES - HALF_HEAD, 1), pltpu.roll(t, HALF_HEAD, 1))
    return t * cos + partner * sin_signed


Q_KIND, K_KIND, V_KIND = 0, 1, 2
_NAT_KINDS = ([Q_KIND] * 4 + [K_KIND] + [V_KIND] + [Q_KIND] * 2 + [K_KIND] * 2 + [V_KIND] * 2)
_GRP_KINDS = [Q_KIND] * 2 + [K_KIND] * 2 + [V_KIND] * 2


def _in_proj_kernel(x_ref, g_ref, w_ref, c1_ref, s1_ref, c4_ref, s4_ref, c16_ref, s16_ref,
                    nat_ref, gate_ref, g1_ref, g2_ref, hf_ref, hb_ref, hd_ref):
    tm = x_ref.shape[0]
    x = x_ref[...]
    h = x * lax.rsqrt(jnp.mean(x * x, axis=-1, keepdims=True) + NORM_EPS) * g_ref[...]
    n_chunks = D_MODEL // LANES
    for c in range(n_chunks):
        hf_ref[c] = h[:, c * LANES:(c + 1) * LANES]
    hb_ref[...] = h.astype(BF16)
    lane = lax.broadcasted_iota(I32, (1, LANES), 1)
    first_half = (lane % HEAD_DIM) < HALF_HEAD

    def project(h_b, col0, kinds, cos_ref, sin_ref, store):
        width = 512
        for c0 in range(0, len(kinds) * LANES, width):
            w = min(width, len(kinds) * LANES - c0)
            res = jnp.dot(h_b, w_ref[:, col0 + c0:col0 + c0 + w], preferred_element_type=F32)
            for j in range(w // LANES):
                kind = kinds[(c0 // LANES) + j]
                t = res[:, j * LANES:(j + 1) * LANES]
                if kind != V_KIND:
                    t = _rope(t, cos_ref[...], sin_ref[...], first_half)
                if kind == Q_KIND:
                    t = t * (HEAD_DIM ** -0.5)
                store(c0 + j * LANES, t)

    low_head = lane < HEAD_DIM

    def store_nat(c, t):
        if A_Q_DIM <= c < A_Q_DIM + A_KV_DIM:
            swapped = pltpu.roll(t, HEAD_DIM, 1)
            nat_ref[:, NAT_AK:NAT_AK + LANES] = jnp.where(low_head, t, swapped).astype(BF16)
            nat_ref[:, NAT_AK + LANES:NAT_AK + 2 * LANES] = jnp.where(low_head, swapped, t).astype(BF16)
        elif c < A_Q_DIM + 2 * A_KV_DIM:
            out = c if c < A_Q_DIM else NAT_AV
            nat_ref[:, out:out + LANES] = t.astype(BF16)
        else:
            out = c - 2 * A_KV_DIM
            nat_ref[:, out:out + LANES] = t.astype(BF16)

    project(hb_ref[...], 0, _NAT_KINDS, c1_ref, s1_ref, store_nat)

    for c0 in range(0, GATE_COLS, 512):
        res = jnp.dot(hb_ref[...], w_ref[:, COL_GATE + c0:COL_GATE + c0 + 512], preferred_element_type=F32)
        gate_ref[:, c0:c0 + 512] = jax.nn.sigmoid(res).astype(BF16)

    for dil, col0, cos_ref, sin_ref, out_ref in ((4, COL_G1, c4_ref, s4_ref, g1_ref),
                                                 (16, COL_G2, c16_ref, s16_ref, g2_ref)):
        n = tm // dil
        for r in range(dil):
            for c in range(n_chunks):
                hd_ref[r * n:(r + 1) * n, c * LANES:(c + 1) * LANES] = (
                    hf_ref[c, pl.ds(r, n, stride=dil), :].astype(BF16))

        def store_grp(c, t, out_ref=out_ref, dil=dil, n=n):
            v = t.astype(BF16)
            for r in range(dil):
                out_ref[r, :, c:c + LANES] = v[r * n:(r + 1) * n]

        project(hd_ref[...], col0, _GRP_KINDS, cos_ref, sin_ref, store_grp)


def _in_proj(x2d, g, w_bf16, tables, batch, seq):
    tm = TOK_TILE
    tiles_per_seq = seq // tm
    n_tok = batch * seq
    c1, s1, c4, s4, c16, s16 = tables
    tab_spec = pl.BlockSpec((tm, LANES), lambda i: (i % tiles_per_seq, 0))
    return pl.pallas_call(
        _in_proj_kernel,
        grid=(n_tok // tm,),
        in_specs=[
            pl.BlockSpec((tm, D_MODEL), lambda i: (i, 0)),
            pl.BlockSpec((1, D_MODEL), lambda i: (0, 0)),
            pl.BlockSpec((D_MODEL, D_IN), lambda i: (0, 0), pipeline_mode=pl.Buffered(1)),
            tab_spec, tab_spec, tab_spec, tab_spec, tab_spec, tab_spec,
        ],
        out_specs=[
            pl.BlockSpec((tm, NAT_COLS), lambda i: (i, 0)),
            pl.BlockSpec((tm, GATE_COLS), lambda i: (i, 0)),
            pl.BlockSpec((None, 4, tm // 4, GRP_COLS), lambda i: (i // tiles_per_seq, 0, i % tiles_per_seq, 0)),
            pl.BlockSpec((None, 16, tm // 16, GRP_COLS), lambda i: (i // tiles_per_seq, 0, i % tiles_per_seq, 0)),
        ],
        out_shape=[
            jax.ShapeDtypeStruct((n_tok, NAT_COLS), BF16),
            jax.ShapeDtypeStruct((n_tok, GATE_COLS), BF16),
            jax.ShapeDtypeStruct((batch, 4, seq // 4, GRP_COLS), BF16),
            jax.ShapeDtypeStruct((batch, 16, seq // 16, GRP_COLS), BF16),
        ],
        scratch_shapes=[
            pltpu.VMEM((D_MODEL // LANES, tm, LANES), F32),
            pltpu.VMEM((tm, D_MODEL), BF16),
            pltpu.VMEM((tm, D_MODEL), BF16),
        ],
        compiler_params=_cparams("parallel"),
        name="in_proj",
    )(x2d, g, w_bf16, c1, s1, c4, s4, c16, s16)


def _masked_heads(q_pair, low_head, high_head):
    zero = jnp.zeros_like(q_pair)
    return [jnp.where(low_head, q_pair, zero), jnp.where(high_head, q_pair, zero)]


def _attn_a_kernel(sink_ref, q_ref, kp_ref, km_ref, kn_ref, vp_ref, vm_ref, vn_ref, o_ref, k_ref, vt_ref, *, seq):
    tq = q_ref.shape[0]
    hw = A_HALF_WINDOW
    n_sub = tq // hw
    i = pl.program_id(1)
    last_blk = seq // hw - 1
    k_ref[0:hw, :] = kp_ref[...]
    k_ref[hw:hw + tq, :] = km_ref[...]
    k_ref[hw + tq:tq + 2 * hw, :] = kn_ref[...]
    for r0, src in ((0, vp_ref), (hw, vm_ref), (hw + tq, vn_ref)):
        vt_ref[:, r0:r0 + src.shape[0]] = src[...].astype(F32).T.astype(BF16)
    grp_cols = A_GROUP * hw
    key = lax.broadcasted_iota(I32, (hw, grp_cols), 0)
    qry = lax.broadcasted_iota(I32, (hw, grp_cols), 1) % hw
    low_head = lax.broadcasted_iota(I32, (1, LANES), 1) < HEAD_DIM
    high_head = jnp.logical_not(low_head)
    for sb in range(n_sub):
        r0 = sb * hw
        blk = i * n_sub + sb
        mask_p = (key >= qry) if sb > 0 else (key >= qry + jnp.where(blk > 0, 0, hw))
        mask_n = (key <= qry) if sb < n_sub - 1 else (key <= qry - jnp.where(blk < last_blk, 0, hw))
        out_t = []
        for g in range(A_KV_HEADS):
            heads = range(g * A_GROUP, (g + 1) * A_GROUP)
            q_parts = []
            for c in range(g * A_GROUP // 2, (g + 1) * A_GROUP // 2):
                q_parts += _masked_heads(q_ref[r0:r0 + hw, c * LANES:(c + 1) * LANES], low_head, high_head)
            q = jnp.concatenate(q_parts, axis=0)
            sink = jnp.concatenate([jnp.full((1, hw), sink_ref[h], F32) for h in heads], axis=1)
            k = k_ref[r0:r0 + 3 * hw, g * LANES:(g + 1) * LANES]
            s = lax.dot_general(k, q, (((1,), (1,)), ((), ())), preferred_element_type=F32)
            sp = jnp.where(mask_p, s[0:hw], NEG_INF)
            so = s[hw:2 * hw]
            sn = jnp.where(mask_n, s[2 * hw:3 * hw], NEG_INF)
            m = jnp.max(jnp.maximum(jnp.maximum(sp, so), sn), axis=0, keepdims=True)
            m = jnp.maximum(m, sink)
            pp, po, pn = jnp.exp(sp - m), jnp.exp(so - m), jnp.exp(sn - m)
            denom = jnp.sum(pp + po + pn, axis=0, keepdims=True) + jnp.exp(sink - m)
            p = jnp.concatenate([pp, po, pn], axis=0).astype(BF16)
            vt = vt_ref[g * HEAD_DIM:(g + 1) * HEAD_DIM, r0:r0 + 3 * hw]
            o = jnp.dot(vt, p, preferred_element_type=F32) * (1.0 / denom)
            out_t += [o[:, j * hw:(j + 1) * hw] for j in range(A_GROUP)]
        o_ref[r0:r0 + hw, :] = jnp.concatenate(out_t, axis=0).T.astype(BF16)


def _attn_a(nat3d, sink):
    batch, seq, _ = nat3d.shape
    tq = A_Q_TILE
    hw = A_HALF_WINDOW
    per = tq // hw
    n_hw = seq // hw
    k_cols, v_cols = 2 * A_KV_DIM, A_KV_DIM
    k_blk, v_blk = NAT_AK // k_cols, NAT_AV // v_cols

    def prev_spec(cols, blk):
        return pl.BlockSpec((None, hw, cols), lambda b, i: (b, jnp.maximum(i * per - 1, 0), blk))

    def main_spec(cols, blk):
        return pl.BlockSpec((None, tq, cols), lambda b, i: (b, i, blk))

    def next_spec(cols, blk):
        return pl.BlockSpec((None, hw, cols), lambda b, i: (b, jnp.minimum((i + 1) * per, n_hw - 1), blk))

    return pl.pallas_call(
        functools.partial(_attn_a_kernel, seq=seq),
        grid=(batch, seq // tq),
        in_specs=[
            pl.BlockSpec(memory_space=pltpu.SMEM),
            main_spec(A_Q_DIM, 0),
            prev_spec(k_cols, k_blk), main_spec(k_cols, k_blk), next_spec(k_cols, k_blk),
            prev_spec(v_cols, v_blk), main_spec(v_cols, v_blk), next_spec(v_cols, v_blk),
        ],
        out_specs=pl.BlockSpec((None, tq, A_Q_DIM), lambda b, i: (b, i, 0)),
        out_shape=jax.ShapeDtypeStruct((batch, seq, A_Q_DIM), BF16),
        scratch_shapes=[pltpu.VMEM((tq + 2 * hw, k_cols), BF16), pltpu.VMEM((v_cols, tq + 2 * hw), BF16)],
        compiler_params=_cparams("parallel", "parallel"),
        name="attn_a",
    )(sink, nat3d, nat3d, nat3d, nat3d, nat3d, nat3d, nat3d)


def _attn_b_kernel(q_ref, kp_ref, km_ref, kn_ref, vp_ref, vm_ref, vn_ref, o_ref, lse_ref,
                   k_ref, vt_ref, *, sub_len):
    tq = q_ref.shape[0]
    hw = B_HALF_WINDOW
    t0 = pl.program_id(1) * tq
    k_ref[0:hw, :] = kp_ref[...]
    k_ref[hw:hw + tq, :] = km_ref[...]
    k_ref[hw + tq:tq + 2 * hw, :] = kn_ref[...]
    for r0, src in ((0, vp_ref), (hw, vm_ref), (hw + tq, vn_ref)):
        vt_ref[:, r0:r0 + src.shape[0]] = src[...].astype(F32).T.astype(BF16)
    kw = B_SUB + 2 * hw
    n_sub = tq // B_SUB
    all_cols = B_HEADS * B_SUB
    key = lax.broadcasted_iota(I32, (kw, all_cols), 0)
    qry = lax.broadcasted_iota(I32, (kw, all_cols), 1) % B_SUB
    in_band = jnp.abs(key - hw - qry) <= hw
    low_head = lax.broadcasted_iota(I32, (1, LANES), 1) < HEAD_DIM
    high_head = jnp.logical_not(low_head)
    for sb in range(n_sub):
        r0 = sb * B_SUB
        valid = in_band
        if sb == 0:
            valid = valid & (t0 - hw + key >= 0)
        if sb == n_sub - 1:
            valid = valid & (t0 + r0 - hw + key < sub_len)
        s_parts = []
        for c in range(B_HEADS // 2):
            cs = slice(c * LANES, (c + 1) * LANES)
            q = jnp.concatenate(_masked_heads(q_ref[r0:r0 + B_SUB, cs], low_head, high_head), axis=0)
            s_parts.append(lax.dot_general(k_ref[r0:r0 + kw, cs], q, (((1,), (1,)), ((), ())),
                                           preferred_element_type=F32))
        s = jnp.where(valid, jnp.concatenate(s_parts, axis=1), NEG_INF)
        m = jnp.max(s, axis=0, keepdims=True)
        p = jnp.exp(s - m)
        denom = jnp.sum(p, axis=0, keepdims=True)
        p = p.astype(BF16)
        inv = 1.0 / denom
        lse = m + jnp.log(denom)
        out_t, lse_t = [], []
        for h in range(B_HEADS):
            qs = slice(h * B_SUB, (h + 1) * B_SUB)
            vt = vt_ref[h * HEAD_DIM:(h + 1) * HEAD_DIM, r0:r0 + kw]
            out_t.append(jnp.dot(vt, p[:, qs], preferred_element_type=F32) * inv[:, qs])
            lse_t.append(jnp.broadcast_to(lse[:, qs], (HEAD_DIM, B_SUB)))
        o_ref[r0:r0 + B_SUB, :] = jnp.concatenate(out_t, axis=0).T
        lse_ref[r0:r0 + B_SUB, :] = jnp.concatenate(lse_t, axis=0).T


def _attn_b(arr3d, q_blk, name):
    n_sub, sub_len, _ = arr3d.shape
    tq = min(512, sub_len)
    hw = B_HALF_WINDOW
    per = tq // hw
    n_hw = sub_len // hw

    def main_spec(c):
        return pl.BlockSpec((None, tq, B_DIM), lambda g, i: (g, i, c))

    def prev_spec(c):
        return pl.BlockSpec((None, hw, B_DIM), lambda g, i: (g, jnp.maximum(i * per - 1, 0), c))

    def next_spec(c):
        return pl.BlockSpec((None, hw, B_DIM), lambda g, i: (g, jnp.minimum((i + 1) * per, n_hw - 1), c))

    out_spec = pl.BlockSpec((None, tq, B_DIM), lambda g, i: (g, i, 0))
    return pl.pallas_call(
        functools.partial(_attn_b_kernel, sub_len=sub_len),
        grid=(n_sub, sub_len // tq),
        in_specs=[main_spec(q_blk),
                  prev_spec(q_blk + 1), main_spec(q_blk + 1), next_spec(q_blk + 1),
                  prev_spec(q_blk + 2), main_spec(q_blk + 2), next_spec(q_blk + 2)],
        out_specs=[out_spec, out_spec],
        out_shape=[jax.ShapeDtypeStruct((n_sub, sub_len, B_DIM), F32)] * 2,
        scratch_shapes=[pltpu.VMEM((tq + 2 * hw, B_DIM), BF16), pltpu.VMEM((B_DIM, tq + 2 * hw), BF16)],
        compiler_params=_cparams("parallel", "parallel"),
        name=name,
    )(arr3d, arr3d, arr3d, arr3d, arr3d, arr3d, arr3d)


def _merge_kernel(x_ref, ya_ref, o0_ref, l0_ref, o1_ref, l1_ref, o2_ref, l2_ref, gate_ref,
                  wa_ref, wb_ref, wo_ref, g2_ref, wrh_ref, wrl_ref, br_ref,
                  x1_ref, h2_ref, code_ref, wcol_ref, cnt_ref,
                  so1_ref, sl1_ref, so2_ref, sl2_ref, yb_ref, run_ref):
    tm = x_ref.shape[0]

    @pl.when(pl.program_id(0) == 0)
    def _():
        run_ref[...] = jnp.zeros_like(run_ref)

    for dil, src_o, src_l, dst_o, dst_l in ((4, o1_ref, l1_ref, so1_ref, sl1_ref),
                                            (16, o2_ref, l2_ref, so2_ref, sl2_ref)):
        n = tm // dil
        for r in range(dil):
            for c in range(B_DIM // LANES):
                cs = slice(c * LANES, (c + 1) * LANES)
                dst_o[c, pl.ds(r, n, stride=dil), :] = src_o[r, :, cs]
                dst_l[c, pl.ds(r, n, stride=dil), :] = src_l[r, :, cs]

    for c in range(B_DIM // LANES):
        cs = slice(c * LANES, (c + 1) * LANES)
        l0, l1, l2 = l0_ref[:, cs], sl1_ref[c], sl2_ref[c]
        m = jnp.maximum(jnp.maximum(l0, l1), l2)
        e0, e1, e2 = jnp.exp(l0 - m), jnp.exp(l1 - m), jnp.exp(l2 - m)
        yb = (e0 * o0_ref[:, cs] + e1 * so1_ref[c] + e2 * so2_ref[c]) / (e0 + e1 + e2)
        yb_ref[:, cs] = yb.astype(BF16)

    ya_p = jnp.dot(ya_ref[...], wa_ref[...], preferred_element_type=F32)
    yb_p = jnp.dot(yb_ref[...], wb_ref[...], preferred_element_type=F32)
    merged = gate_ref[:, 0:D_MODEL].astype(F32) * ya_p + gate_ref[:, D_MODEL:GATE_COLS].astype(F32) * yb_p
    x1 = x_ref[...] + jnp.dot(merged.astype(BF16), wo_ref[...], preferred_element_type=F32)
    x1_ref[...] = x1

    h2 = x1 * lax.rsqrt(jnp.mean(x1 * x1, axis=-1, keepdims=True) + NORM_EPS) * g2_ref[...]
    _store_row_tiles(h2_ref, h2)

    h_hi = h2.astype(BF16)
    h_lo = (h2 - h_hi.astype(F32)).astype(BF16)
    logits = (jnp.dot(h_hi, wrh_ref[...], preferred_element_type=F32)
              + jnp.dot(h_lo, wrh_ref[...], preferred_element_type=F32)
              + jnp.dot(h_hi, wrl_ref[...], preferred_element_type=F32)) + br_ref[...]

    lane = lax.broadcasted_iota(I32, (tm, LANES), 1)
    is_grp = (lane >= N_EXPERTS) & (lane < N_EXPERTS + MOE_GROUPS)
    gl = jnp.where(is_grp, logits, -jnp.inf)
    gmax = jnp.max(gl, axis=-1, keepdims=True)
    gidx = jnp.min(jnp.where(gl == gmax, lane - N_EXPERTS, LANES), axis=-1, keepdims=True)
    gw = 1.0 / jnp.sum(jnp.exp(gl - gmax), axis=-1, keepdims=True)
    in_grp = (lane < N_EXPERTS) & ((lane // EXPERTS_PER_GROUP) == gidx)
    el = jnp.where(in_grp, logits, -jnp.inf)
    v1 = jnp.max(el, axis=-1, keepdims=True)
    i1 = jnp.min(jnp.where(el == v1, lane, LANES), axis=-1, keepdims=True)
    el2 = jnp.where(lane == i1, -jnp.inf, el)
    v2 = jnp.max(el2, axis=-1, keepdims=True)
    i2 = jnp.min(jnp.where(el2 == v2, lane, LANES), axis=-1, keepdims=True)
    t = jnp.exp(v2 - v1)
    w1 = gw / (1.0 + t)
    w2 = gw * t / (1.0 + t)

    oh1 = (lane == i1).astype(F32)
    oh2 = (lane == i2).astype(F32)
    oh = oh1 + oh2
    row = lax.broadcasted_iota(I32, (tm, tm), 0)
    col = lax.broadcasted_iota(I32, (tm, tm), 1)
    lower = (col < row).astype(BF16)
    before = jnp.dot(lower, oh.astype(BF16), preferred_element_type=F32) + run_ref[0:1, :]
    rank1 = jnp.sum(before * oh1, axis=-1, keepdims=True)
    rank2 = jnp.sum(before * oh2, axis=-1, keepdims=True)
    run_ref[...] = run_ref[...] + jnp.sum(oh, axis=0, keepdims=True)
    cnt_ref[...] = run_ref[...]

    scale = float(1 << RANK_BITS)
    code1 = i1.astype(F32) * scale + rank1
    code2 = i2.astype(F32) * scale + rank2
    meta = jnp.where(lane == 0, code1, jnp.where(lane == 1, code2, 0.0))
    code_ref[...] = meta.T[0:8, :].astype(I32)
    wcol_ref[...] = jnp.where(lane == 0, w1, jnp.where(lane == 1, w2, 0.0))


def _merge(x2d, ya, o0, l0, o1, l1, o2, l2, gates, wa, wb, wo, g2, wr_hi, wr_lo, br, batch, seq):
    tm = TOK_TILE
    tps = seq // tm
    n_tok = batch * seq

    def tok(c):
        return pl.BlockSpec((tm, c), lambda i: (i, 0))

    def full(a):
        return pl.BlockSpec(a.shape, lambda i: (0,) * a.ndim)

    def res_spec(d):
        return pl.BlockSpec((None, d, tm // d, B_DIM), lambda i: (i // tps, 0, i % tps, 0))

    return pl.pallas_call(
        _merge_kernel,
        grid=(n_tok // tm,),
        in_specs=[tok(D_MODEL), tok(A_Q_DIM), tok(B_DIM), tok(B_DIM),
                  res_spec(4), res_spec(4), res_spec(16), res_spec(16), tok(GATE_COLS),
                  full(wa), full(wb), full(wo), full(g2), full(wr_hi), full(wr_lo), full(br)],
        out_specs=[tok(D_MODEL), pl.BlockSpec((tm * ROW_CHUNKS, LANES), lambda i: (i, 0)),
                   pl.BlockSpec((8, tm), lambda i: (0, i)),
                   tok(LANES),
                   pl.BlockSpec((8, LANES), lambda i: (0, 0))],
        out_shape=[jax.ShapeDtypeStruct((n_tok, D_MODEL), F32),
                   jax.ShapeDtypeStruct((n_tok * ROW_CHUNKS, LANES), F32),
                   jax.ShapeDtypeStruct((8, n_tok), I32),
                   jax.ShapeDtypeStruct((n_tok, LANES), F32),
                   jax.ShapeDtypeStruct((8, LANES), F32)],
        scratch_shapes=([pltpu.VMEM((B_DIM // LANES, tm, LANES), F32)] * 4
                        + [pltpu.VMEM((tm, B_DIM), BF16), pltpu.VMEM((8, LANES), F32)]),
        compiler_params=_cparams("arbitrary"),
        name="merge_route",
    )(x2d, ya, o0, l0, o1, l1, o2, l2, gates, wa, wb, wo, g2, wr_hi, wr_lo, br)


def _row_tile(ref, t):
    return ref.at[pl.ds(pl.multiple_of(t * ROW_CHUNKS, ROW_CHUNKS), ROW_CHUNKS)]


def _store_row_tiles(ref, val):
    rows = val.shape[0]
    for c in range(ROW_CHUNKS):
        ref[pl.ds(c, rows, stride=ROW_CHUNKS), :] = val[:, c * LANES:(c + 1) * LANES]


def _load_row_tiles_chunk(ref, c):
    return ref[pl.ds(c, ref.shape[0] // ROW_CHUNKS, stride=ROW_CHUNKS), :]


def _moe_kernel(tile_e_ref, n_used_ref, inv_ref, h_ref, wg_ref, wu_ref, wd_ref, y_ref,
                xbuf_ref, ybuf_ref, x_ref, sem_in, sem_out, *, n_tok):
    del tile_e_ref
    i = pl.program_id(0)
    n_used = n_used_ref[0]
    te = x_ref.shape[0]
    tile_rows = te * ROW_CHUNKS
    dump0 = 2 * n_tok

    def gather_tile(tile, slot):
        def issue(j, carry):
            tok = inv_ref[tile * te + j] >> SRC_SHIFT
            pltpu.make_async_copy(_row_tile(h_ref, tok), _row_tile(xbuf_ref.at[slot], j), sem_in.at[slot]).start()
            return carry

        lax.fori_loop(0, te, issue, 0, unroll=DMA_UNROLL)

    def scatter_tile(tile, slot):
        def issue(j, carry):
            dst = inv_ref[tile * te + j] & ((1 << SRC_SHIFT) - 1)
            pltpu.make_async_copy(_row_tile(ybuf_ref.at[slot], j), _row_tile(y_ref, dst), sem_out.at[slot]).start()
            return carry

        lax.fori_loop(0, te, issue, 0, unroll=DMA_UNROLL)

    def wait_gather(slot):
        pltpu.make_async_copy(h_ref.at[pl.ds(0, tile_rows)], xbuf_ref.at[slot], sem_in.at[slot]).wait()

    def wait_scatter(slot):
        pltpu.make_async_copy(ybuf_ref.at[slot], y_ref.at[pl.ds(0, tile_rows)], sem_out.at[slot]).wait()

    @pl.when(i == 0)
    def _():
        ybuf_ref[...] = jnp.zeros_like(ybuf_ref)
        for slot in range(2):
            dump = y_ref.at[pl.ds((dump0 + slot * te) * ROW_CHUNKS, tile_rows)]
            pltpu.make_async_copy(ybuf_ref.at[slot], dump, sem_out.at[slot]).start()
        for slot in range(2):
            wait_scatter(slot)
        gather_tile(0, 0)

    @pl.when(i < n_used)
    def _():
        cur = i % 2

        @pl.when(i + 1 < n_used)
        def _():
            gather_tile(i + 1, 1 - cur)

        wait_gather(cur)

        @pl.when(i >= 2)
        def _():
            wait_scatter(cur)

        xb = xbuf_ref.at[cur]
        for c in range(ROW_CHUNKS):
            x_ref[:, c * LANES:(c + 1) * LANES] = _load_row_tiles_chunk(xb, c).astype(BF16)
        x = x_ref[...]
        a = jnp.dot(x, wg_ref[...].astype(BF16), preferred_element_type=F32)
        u = jnp.dot(x, wu_ref[...].astype(BF16), preferred_element_type=F32)
        z = (a * jax.nn.sigmoid(a)) * u
        _store_row_tiles(ybuf_ref.at[cur], jnp.dot(z.astype(BF16), wd_ref[...].astype(BF16), preferred_element_type=F32))
        scatter_tile(i, cur)

        @pl.when(i == n_used - 1)
        def _():
            wait_scatter(cur)

            @pl.when(i >= 1)
            def _():
                wait_scatter(1 - cur)


def _moe(tile_e, n_used, inv, h2, wg, wu, wd, layer, n_tok):
    n_slots = inv.shape[0]
    te = EXP_TILE

    def w_map(i, tile_e, n_used, inv):
        return (layer, tile_e[i], 0, 0)

    return pl.pallas_call(
        functools.partial(_moe_kernel, n_tok=n_tok),
        grid_spec=pltpu.PrefetchScalarGridSpec(
            num_scalar_prefetch=3,
            grid=(n_slots // te,),
            in_specs=[pl.BlockSpec(memory_space=pl.ANY),
                      pl.BlockSpec((None, None, D_MODEL, D_EXPERT), w_map),
                      pl.BlockSpec((None, None, D_MODEL, D_EXPERT), w_map),
                      pl.BlockSpec((None, None, D_EXPERT, D_MODEL), w_map)],
            out_specs=pl.BlockSpec(memory_space=pl.ANY),
            scratch_shapes=[pltpu.VMEM((2, te * ROW_CHUNKS, LANES), F32),
                            pltpu.VMEM((2, te * ROW_CHUNKS, LANES), F32),
                            pltpu.VMEM((te, D_MODEL), BF16),
                            pltpu.SemaphoreType.DMA((2,)),
                            pltpu.SemaphoreType.DMA((2,))],
        ),
        out_shape=jax.ShapeDtypeStruct(((2 * n_tok + 2 * te) * ROW_CHUNKS, LANES), F32),
        compiler_params=_cparams("arbitrary"),
        name="moe",
    )(tile_e, n_used, inv, h2, wg, wu, wd)


def _combine_kernel(x_ref, w_ref, g_ref, y1_ref, y2_ref, o_ref, y_ref, *, final):
    w = w_ref[...]
    w1, w2 = w[:, 0:1], w[:, 1:2]
    for c in range(ROW_CHUNKS):
        cs = slice(c * LANES, (c + 1) * LANES)
        y_ref[:, cs] = x_ref[:, cs] + w1 * _load_row_tiles_chunk(y1_ref, c) + w2 * _load_row_tiles_chunk(y2_ref, c)
    y = y_ref[...]
    if final:
        y = y * lax.rsqrt(jnp.mean(y * y, axis=-1, keepdims=True) + NORM_EPS) * g_ref[...]
    o_ref[...] = y


def _combine(x1, wcol, g, y, final):
    n_tok = x1.shape[0]
    rows = TOK_TILE
    n_blocks = n_tok // rows
    return pl.pallas_call(
        functools.partial(_combine_kernel, final=final),
        grid=(n_blocks,),
        in_specs=[pl.BlockSpec((rows, D_MODEL), lambda i: (i, 0)),
                  pl.BlockSpec((rows, LANES), lambda i: (i, 0)),
                  pl.BlockSpec((1, D_MODEL), lambda i: (0, 0)),
                  pl.BlockSpec((rows * ROW_CHUNKS, LANES), lambda i: (i, 0)),
                  pl.BlockSpec((rows * ROW_CHUNKS, LANES), lambda i: (n_blocks + i, 0))],
        out_specs=pl.BlockSpec((rows, D_MODEL), lambda i: (i, 0)),
        out_shape=jax.ShapeDtypeStruct((n_tok, D_MODEL), F32),
        scratch_shapes=[pltpu.VMEM((rows, D_MODEL), F32)],
        compiler_params=_cparams("parallel"),
        name="combine",
    )(x1, wcol, g, y, y)


def kernel(x, attn_norm_g, w_in, a_sink, w_branch_a, w_branch_b, w_out, ffn_norm_g,
           w_router_group, b_router_group, w_router_expert, b_router_expert,
           w_exp_gate, w_exp_up, w_exp_down, final_norm_g):
    batch, seq, d_model = x.shape
    depth = w_in.shape[0]
    n_tok = batch * seq
    assert d_model == D_MODEL and w_in.shape[2] == D_IN
    assert seq % (16 * B_SUB) == 0 and seq % TOK_TILE == 0 and n_tok % TOK_TILE == 0
    assert n_tok < (1 << RANK_BITS) and 2 * n_tok + 2 * EXP_TILE <= (1 << SRC_SHIFT) and n_tok < (1 << 15)

    cos_t, sin_t = _rope_tables(seq)
    tables = (cos_t, sin_t,
              _residue_order(cos_t, 4, TOK_TILE), _residue_order(sin_t, 4, TOK_TILE),
              _residue_order(cos_t, 16, TOK_TILE), _residue_order(sin_t, 16, TOK_TILE))

    n_slots = 2 * n_tok + N_EXPERTS * EXP_TILE
    n_tiles = n_slots // EXP_TILE
    x2d = x.reshape(n_tok, D_MODEL)

    for l in range(depth):
        nat, gates, grp1, grp2 = _in_proj(x2d, attn_norm_g[l][None, :], w_in[l].astype(BF16), tables, batch, seq)
        nat3d = nat.reshape(batch, seq, NAT_COLS)
        ya = _attn_a(nat3d, a_sink[l]).reshape(n_tok, A_Q_DIM)
        o0, l0 = _attn_b(nat3d, NAT_B // B_DIM, "attn_b1")
        o1, l1 = _attn_b(grp1.reshape(batch * 4, seq // 4, GRP_COLS), 0, "attn_b4")
        o2, l2 = _attn_b(grp2.reshape(batch * 16, seq // 16, GRP_COLS), 0, "attn_b16")

        wr = jnp.zeros((D_MODEL, LANES), F32)
        wr = wr.at[:, 0:N_EXPERTS].set(w_router_expert[l]).at[:, N_EXPERTS:N_EXPERTS + MOE_GROUPS].set(w_router_group[l])
        br = jnp.zeros((1, LANES), F32)
        br = br.at[0, 0:N_EXPERTS].set(b_router_expert[l]).at[0, N_EXPERTS:N_EXPERTS + MOE_GROUPS].set(b_router_group[l])
        wr_hi = wr.astype(BF16)
        wr_lo = (wr - wr_hi.astype(F32)).astype(BF16)

        x1, h2, code, wcol, cnt = _merge(
            x2d, ya, o0.reshape(n_tok, B_DIM), l0.reshape(n_tok, B_DIM),
            o1.reshape(batch, 4, seq // 4, B_DIM), l1.reshape(batch, 4, seq // 4, B_DIM),
            o2.reshape(batch, 16, seq // 16, B_DIM), l2.reshape(batch, 16, seq // 16, B_DIM),
            gates, w_branch_a[l].astype(BF16), w_branch_b[l].astype(BF16), w_out[l].astype(BF16),
            ffn_norm_g[l][None, :], wr_hi, wr_lo, br, batch, seq)

        counts = cnt[0, 0:N_EXPERTS].astype(I32)
        padded = ((counts + EXP_TILE - 1) // EXP_TILE) * EXP_TILE
        ends = jnp.cumsum(padded)
        offs = ends - padded
        n_used = (ends[-1:] // EXP_TILE).astype(I32)
        tile_start = jnp.arange(n_tiles, dtype=I32) * EXP_TILE
        tile_e = jnp.minimum(jnp.sum((ends[None, :] <= tile_start[:, None]).astype(I32), axis=1), N_EXPERTS - 1)
        eid = code[0:2] >> RANK_BITS
        rank = code[0:2] & ((1 << RANK_BITS) - 1)
        expert_ids = jnp.arange(N_EXPERTS, dtype=I32)[:, None, None]
        slots = (rank + jnp.sum(jnp.where(eid[None] == expert_ids, offs[:, None, None], 0), axis=0)).reshape(-1)

        dst = jnp.full((n_slots,), -1, I32).at[slots].set(jnp.arange(2 * n_tok, dtype=I32))
        slot_id = jnp.arange(n_slots, dtype=I32)
        dump = 2 * n_tok + ((slot_id // EXP_TILE) % 2) * EXP_TILE + slot_id % EXP_TILE
        src = jnp.where(dst < 0, 0, jnp.where(dst >= n_tok, dst - n_tok, dst))
        inv = (src << SRC_SHIFT) | jnp.where(dst < 0, dump, dst)

        y = _moe(tile_e, n_used, inv, h2, w_exp_gate, w_exp_up, w_exp_down, l, n_tok)
        x2d = _combine(x1, wcol, final_norm_g[None, :], y, final=(l == depth - 1))

    return x2d.reshape(batch, seq, D_MODEL)
```

```python
import functools

import jax
import jax.numpy as jnp
from jax import lax
from jax.experimental import pallas as pl
from jax.experimental.pallas import tpu as pltpu

F32 = jnp.float32
BF16 = jnp.bfloat16
I32 = jnp.int32

D_MODEL = 1024
HEAD_DIM = 64
HALF_HEAD = HEAD_DIM // 2
ROPE_THETA = 10000.0
NORM_EPS = 1e-6
NEG_INF = -1e30
LANES = 128

A_Q_HEADS = 8
A_KV_HEADS = 2
A_GROUP = A_Q_HEADS // A_KV_HEADS
A_HALF_WINDOW = 128
A_Q_DIM = A_Q_HEADS * HEAD_DIM
A_KV_DIM = A_KV_HEADS * HEAD_DIM

B_GROUPS = ((128, 1), (512, 4), (2048, 16))
B_HEADS = 4
B_DIM = B_HEADS * HEAD_DIM
B_HALF_WINDOW = 64

MOE_GROUPS = 4
EXPERTS_PER_GROUP = 8
N_EXPERTS = MOE_GROUPS * EXPERTS_PER_GROUP
D_EXPERT = 256

NAT_IN_COLS = A_Q_DIM + 2 * A_KV_DIM + 3 * B_DIM
NAT_B = A_Q_DIM
NAT_AK = NAT_B + 3 * B_DIM
NAT_AV = NAT_AK + 2 * A_KV_DIM
NAT_COLS = NAT_AV + A_KV_DIM
GRP_COLS = 3 * B_DIM
COL_G1 = NAT_IN_COLS
COL_G2 = COL_G1 + GRP_COLS
COL_GATE = COL_G2 + GRP_COLS
GATE_COLS = 2 * D_MODEL
D_IN = COL_GATE + GATE_COLS

TOK_TILE = 512
A_Q_TILE = 512
B_SUB = 128
EXP_TILE = 512
ROW_CHUNKS = D_MODEL // LANES
ROW_TILE = 256
DISPATCH_TILE = 1024
DMA_UNROLL = 8
RANK_BITS = 16
VMEM_LIMIT = 56 * 1024 * 1024


def _cparams(*sem):
    return pltpu.CompilerParams(dimension_semantics=sem, vmem_limit_bytes=VMEM_LIMIT)


def _rope_tables(seq_len):
    inv = 1.0 / (ROPE_THETA ** (jnp.arange(0, HEAD_DIM, 2, dtype=F32) / HEAD_DIM))
    ang = jnp.arange(seq_len, dtype=F32)[:, None] * inv[None, :]
    cos, sin = jnp.cos(ang), jnp.sin(ang)
    cos_t = jnp.concatenate([cos, cos, cos, cos], axis=-1)
    sin_t = jnp.concatenate([-sin, sin, -sin, sin], axis=-1)
    return cos_t, sin_t


def _residue_order(table, dilation, tile):
    s, c = table.shape
    return table.reshape(s // tile, tile // dilation, dilation, c).transpose(0, 2, 1, 3).reshape(s, c)


def _rope(t, cos, sin_signed, first_half):
    partner = jnp.where(first_half, pltpu.roll(t, LANES - HALF_HEAD, 1), pltpu.roll(t, HALF_HEAD, 1))
    return t * cos + partner * sin_signed


Q_KIND, K_KIND, V_KIND = 0, 1, 2
_NAT_KINDS = ([Q_KIND] * 4 + [K_KIND] + [V_KIND] + [Q_KIND] * 2 + [K_KIND] * 2 + [V_KIND] * 2)
_GRP_KINDS = [Q_KIND] * 2 + [K_KIND] * 2 + [V_KIND] * 2


def _in_proj_kernel(x_ref, g_ref, w_ref, c1_ref, s1_ref, c4_ref, s4_ref, c16_ref, s16_ref,
                    nat_ref, gate_ref, g1_ref, g2_ref, hf_ref, hb_ref, hd_ref):
    tm = x_ref.shape[0]
    x = x_ref[...]
    h = x * lax.rsqrt(jnp.mean(x * x, axis=-1, keepdims=True) + NORM_EPS) * g_ref[...]
    n_chunks = D_MODEL // LANES
    for c in range(n_chunks):
        hf_ref[c] = h[:, c * LANES:(c + 1) * LANES]
    hb_ref[...] = h.astype(BF16)
    lane = lax.broadcasted_iota(I32, (1, LANES), 1)
    first_half = (lane % HEAD_DIM) < HALF_HEAD

    def project(h_b, col0, kinds, cos_ref, sin_ref, store):
        width = 512
        for c0 in range(0, len(kinds) * LANES, width):
            w = min(width, len(kinds) * LANES - c0)
            res = jnp.dot(h_b, w_ref[:, col0 + c0:col0 + c0 + w], preferred_element_type=F32)
            for j in range(w // LANES):
                kind = kinds[(c0 // LANES) + j]
                t = res[:, j * LANES:(j + 1) * LANES]
                if kind != V_KIND:
                    t = _rope(t, cos_ref[...], sin_ref[...], first_half)
                if kind == Q_KIND:
                    t = t * (HEAD_DIM ** -0.5)
                store(c0 + j * LANES, t)

    low_head = lane < HEAD_DIM

    def store_nat(c, t):
        if A_Q_DIM <= c < A_Q_DIM + A_KV_DIM:
            swapped = pltpu.roll(t, HEAD_DIM, 1)
            nat_ref[:, NAT_AK:NAT_AK + LANES] = jnp.where(low_head, t, swapped).astype(BF16)
            nat_ref[:, NAT_AK + LANES:NAT_AK + 2 * LANES] = jnp.where(low_head, swapped, t).astype(BF16)
        elif c < A_Q_DIM + 2 * A_KV_DIM:
            out = c if c < A_Q_DIM else NAT_AV
            nat_ref[:, out:out + LANES] = t.astype(BF16)
        else:
            out = c - 2 * A_KV_DIM
            nat_ref[:, out:out + LANES] = t.astype(BF16)

    project(hb_ref[...], 0, _NAT_KINDS, c1_ref, s1_ref, store_nat)

    for c0 in range(0, GATE_COLS, 512):
        res = jnp.dot(hb_ref[...], w_ref[:, COL_GATE + c0:COL_GATE + c0 + 512], preferred_element_type=F32)
        gate_ref[:, c0:c0 + 512] = jax.nn.sigmoid(res).astype(BF16)

    for dil, col0, cos_ref, sin_ref, out_ref in ((4, COL_G1, c4_ref, s4_ref, g1_ref),
                                                 (16, COL_G2, c16_ref, s16_ref, g2_ref)):
        n = tm // dil
        for r in range(dil):
            for c in range(n_chunks):
                hd_ref[r * n:(r + 1) * n, c * LANES:(c + 1) * LANES] = (
                    hf_ref[c, pl.ds(r, n, stride=dil), :].astype(BF16))

        def store_grp(c, t, out_ref=out_ref, dil=dil, n=n):
            v = t.astype(BF16)
            for r in range(dil):
                out_ref[r, :, c:c + LANES] = v[r * n:(r + 1) * n]

        project(hd_ref[...], col0, _GRP_KINDS, cos_ref, sin_ref, store_grp)


def _in_proj(x2d, g, w_bf16, tables, batch, seq):
    tm = TOK_TILE
    tiles_per_seq = seq // tm
    n_tok = batch * seq
    c1, s1, c4, s4, c16, s16 = tables
    tab_spec = pl.BlockSpec((tm, LANES), lambda i: (i % tiles_per_seq, 0))
    return pl.pallas_call(
        _in_proj_kernel,
        grid=(n_tok // tm,),
        in_specs=[
            pl.BlockSpec((tm, D_MODEL), lambda i: (i, 0)),
            pl.BlockSpec((1, D_MODEL), lambda i: (0, 0)),
            pl.BlockSpec((D_MODEL, D_IN), lambda i: (0, 0), pipeline_mode=pl.Buffered(1)),
            tab_spec, tab_spec, tab_spec, tab_spec, tab_spec, tab_spec,
        ],
        out_specs=[
            pl.BlockSpec((tm, NAT_COLS), lambda i: (i, 0)),
            pl.BlockSpec((tm, GATE_COLS), lambda i: (i, 0)),
            pl.BlockSpec((None, 4, tm // 4, GRP_COLS), lambda i: (i // tiles_per_seq, 0, i % tiles_per_seq, 0)),
            pl.BlockSpec((None, 16, tm // 16, GRP_COLS), lambda i: (i // tiles_per_seq, 0, i % tiles_per_seq, 0)),
        ],
        out_shape=[
            jax.ShapeDtypeStruct((n_tok, NAT_COLS), BF16),
            jax.ShapeDtypeStruct((n_tok, GATE_COLS), BF16),
            jax.ShapeDtypeStruct((batch, 4, seq // 4, GRP_COLS), BF16),
            jax.ShapeDtypeStruct((batch, 16, seq // 16, GRP_COLS), BF16),
        ],
        scratch_shapes=[
            pltpu.VMEM((D_MODEL // LANES, tm, LANES), F32),
            pltpu.VMEM((tm, D_MODEL), BF16),
            pltpu.VMEM((tm, D_MODEL), BF16),
        ],
        compiler_params=_cparams("parallel"),
        name="in_proj",
    )(x2d, g, w_bf16, c1, s1, c4, s4, c16, s16)


def _masked_heads(q_pair, low_head, high_head):
    zero = jnp.zeros_like(q_pair)
    return [jnp.where(low_head, q_pair, zero), jnp.where(high_head, q_pair, zero)]


def _attn_a_kernel(sink_ref, q_ref, kp_ref, km_ref, kn_ref, vp_ref, vm_ref, vn_ref, o_ref, k_ref, vt_ref, *, seq):
    tq = q_ref.shape[0]
    hw = A_HALF_WINDOW
    n_sub = tq // hw
    i = pl.program_id(1)
    last_blk = seq // hw - 1
    k_ref[0:hw, :] = kp_ref[...]
    k_ref[hw:hw + tq, :] = km_ref[...]
    k_ref[hw + tq:tq + 2 * hw, :] = kn_ref[...]
    for r0, src in ((0, vp_ref), (hw, vm_ref), (hw + tq, vn_ref)):
        vt_ref[:, r0:r0 + src.shape[0]] = src[...].astype(F32).T.astype(BF16)
    grp_cols = A_GROUP * hw
    key = lax.broadcasted_iota(I32, (hw, grp_cols), 0)
    qry = lax.broadcasted_iota(I32, (hw, grp_cols), 1) % hw
    low_head = lax.broadcasted_iota(I32, (1, LANES), 1) < HEAD_DIM
    high_head = jnp.logical_not(low_head)
    for sb in range(n_sub):
        r0 = sb * hw
        blk = i * n_sub + sb
        mask_p = (key >= qry) if sb > 0 else (key >= qry + jnp.where(blk > 0, 0, hw))
        mask_n = (key <= qry) if sb < n_sub - 1 else (key <= qry - jnp.where(blk < last_blk, 0, hw))
        out_t = []
        for g in range(A_KV_HEADS):
            heads = range(g * A_GROUP, (g + 1) * A_GROUP)
            q_parts = []
            for c in range(g * A_GROUP // 2, (g + 1) * A_GROUP // 2):
                q_parts += _masked_heads(q_ref[r0:r0 + hw, c * LANES:(c + 1) * LANES], low_head, high_head)
            q = jnp.concatenate(q_parts, axis=0)
            sink = jnp.concatenate([jnp.full((1, hw), sink_ref[h], F32) for h in heads], axis=1)
            k = k_ref[r0:r0 + 3 * hw, g * LANES:(g + 1) * LANES]
            s = lax.dot_general(k, q, (((1,), (1,)), ((), ())), preferred_element_type=F32)
            sp = jnp.where(mask_p, s[0:hw], NEG_INF)
            so = s[hw:2 * hw]
            sn = jnp.where(mask_n, s[2 * hw:3 * hw], NEG_INF)
            m = jnp.max(jnp.maximum(jnp.maximum(sp, so), sn), axis=0, keepdims=True)
            m = jnp.maximum(m, sink)
            pp, po, pn = jnp.exp(sp - m), jnp.exp(so - m), jnp.exp(sn - m)
            denom = jnp.sum(pp + po + pn, axis=0, keepdims=True) + jnp.exp(sink - m)
            p = jnp.concatenate([pp, po, pn], axis=0).astype(BF16)
            vt = vt_ref[g * HEAD_DIM:(g + 1) * HEAD_DIM, r0:r0 + 3 * hw]
            o = jnp.dot(vt, p, preferred_element_type=F32) * (1.0 / denom)
            out_t += [o[:, j * hw:(j + 1) * hw] for j in range(A_GROUP)]
        o_ref[r0:r0 + hw, :] = jnp.concatenate(out_t, axis=0).T.astype(BF16)


def _attn_a(nat3d, sink):
    batch, seq, _ = nat3d.shape
    tq = A_Q_TILE
    hw = A_HALF_WINDOW
    per = tq // hw
    n_hw = seq // hw
    k_cols, v_cols = 2 * A_KV_DIM, A_KV_DIM
    k_blk, v_blk = NAT_AK // k_cols, NAT_AV // v_cols

    def prev_spec(cols, blk):
        return pl.BlockSpec((None, hw, cols), lambda b, i: (b, jnp.maximum(i * per - 1, 0), blk))

    def main_spec(cols, blk):
        return pl.BlockSpec((None, tq, cols), lambda b, i: (b, i, blk))

    def next_spec(cols, blk):
        return pl.BlockSpec((None, hw, cols), lambda b, i: (b, jnp.minimum((i + 1) * per, n_hw - 1), blk))

    return pl.pallas_call(
        functools.partial(_attn_a_kernel, seq=seq),
        grid=(batch, seq // tq),
        in_specs=[
            pl.BlockSpec(memory_space=pltpu.SMEM),
            main_spec(A_Q_DIM, 0),
            prev_spec(k_cols, k_blk), main_spec(k_cols, k_blk), next_spec(k_cols, k_blk),
            prev_spec(v_cols, v_blk), main_spec(v_cols, v_blk), next_spec(v_cols, v_blk),
        ],
        out_specs=pl.BlockSpec((None, tq, A_Q_DIM), lambda b, i: (b, i, 0)),
        out_shape=jax.ShapeDtypeStruct((batch, seq, A_Q_DIM), BF16),
        scratch_shapes=[pltpu.VMEM((tq + 2 * hw, k_cols), BF16), pltpu.VMEM((v_cols, tq + 2 * hw), BF16)],
        compiler_params=_cparams("parallel", "parallel"),
        name="attn_a",
    )(sink, nat3d, nat3d, nat3d, nat3d, nat3d, nat3d, nat3d)


def _attn_b_kernel(q_ref, kp_ref, km_ref, kn_ref, vp_ref, vm_ref, vn_ref, o_ref, lse_ref,
                   k_ref, vt_ref, *, sub_len):
    tq = q_ref.shape[0]
    hw = B_HALF_WINDOW
    t0 = pl.program_id(1) * tq
    k_ref[0:hw, :] = kp_ref[...]
    k_ref[hw:hw + tq, :] = km_ref[...]
    k_ref[hw + tq:tq + 2 * hw, :] = kn_ref[...]
    for r0, src in ((0, vp_ref), (hw, vm_ref), (hw + tq, vn_ref)):
        vt_ref[:, r0:r0 + src.shape[0]] = src[...].astype(F32).T.astype(BF16)
    kw = B_SUB + 2 * hw
    n_sub = tq // B_SUB
    all_cols = B_HEADS * B_SUB
    key = lax.broadcasted_iota(I32, (kw, all_cols), 0)
    qry = lax.broadcasted_iota(I32, (kw, all_cols), 1) % B_SUB
    in_band = jnp.abs(key - hw - qry) <= hw
    low_head = lax.broadcasted_iota(I32, (1, LANES), 1) < HEAD_DIM
    high_head = jnp.logical_not(low_head)
    for sb in range(n_sub):
        r0 = sb * B_SUB
        valid = in_band
        if sb == 0:
            valid = valid & (t0 - hw + key >= 0)
        if sb == n_sub - 1:
            valid = valid & (t0 + r0 - hw + key < sub_len)
        s_parts = []
        for c in range(B_HEADS // 2):
            cs = slice(c * LANES, (c + 1) * LANES)
            q = jnp.concatenate(_masked_heads(q_ref[r0:r0 + B_SUB, cs], low_head, high_head), axis=0)
            s_parts.append(lax.dot_general(k_ref[r0:r0 + kw, cs], q, (((1,), (1,)), ((), ())),
                                           preferred_element_type=F32))
        s = jnp.where(valid, jnp.concatenate(s_parts, axis=1), NEG_INF)
        m = jnp.max(s, axis=0, keepdims=True)
        p = jnp.exp(s - m)
        denom = jnp.sum(p, axis=0, keepdims=True)
        p = p.astype(BF16)
        inv = 1.0 / denom
        lse = m + jnp.log(denom)
        out_t, lse_t = [], []
        for h in range(B_HEADS):
            qs = slice(h * B_SUB, (h + 1) * B_SUB)
            vt = vt_ref[h * HEAD_DIM:(h + 1) * HEAD_DIM, r0:r0 + kw]
            out_t.append(jnp.dot(vt, p[:, qs], preferred_element_type=F32) * inv[:, qs])
            lse_t.append(jnp.broadcast_to(lse[:, qs], (HEAD_DIM, B_SUB)))
        o_ref[r0:r0 + B_SUB, :] = jnp.concatenate(out_t, axis=0).T
        lse_ref[r0:r0 + B_SUB, :] = jnp.concatenate(lse_t, axis=0).T


def _attn_b(arr3d, q_blk, name):
    n_sub, sub_len, _ = arr3d.shape
    tq = min(512, sub_len)
    hw = B_HALF_WINDOW
    per = tq // hw
    n_hw = sub_len // hw

    def main_spec(c):
        return pl.BlockSpec((None, tq, B_DIM), lambda g, i: (g, i, c))

    def prev_spec(c):
        return pl.BlockSpec((None, hw, B_DIM), lambda g, i: (g, jnp.maximum(i * per - 1, 0), c))

    def next_spec(c):
        return pl.BlockSpec((None, hw, B_DIM), lambda g, i: (g, jnp.minimum((i + 1) * per, n_hw - 1), c))

    out_spec = pl.BlockSpec((None, tq, B_DIM), lambda g, i: (g, i, 0))
    return pl.pallas_call(
        functools.partial(_attn_b_kernel, sub_len=sub_len),
        grid=(n_sub, sub_len // tq),
        in_specs=[main_spec(q_blk),
                  prev_spec(q_blk + 1), main_spec(q_blk + 1), next_spec(q_blk + 1),
                  prev_spec(q_blk + 2), main_spec(q_blk + 2), next_spec(q_blk + 2)],
        out_specs=[out_spec, out_spec],
        out_shape=[jax.ShapeDtypeStruct((n_sub, sub_len, B_DIM), F32)] * 2,
        scratch_shapes=[pltpu.VMEM((tq + 2 * hw, B_DIM), BF16), pltpu.VMEM((B_DIM, tq + 2 * hw), BF16)],
        compiler_params=_cparams("parallel", "parallel"),
        name=name,
    )(arr3d, arr3d, arr3d, arr3d, arr3d, arr3d, arr3d)


def _merge_kernel(x_ref, ya_ref, o0_ref, l0_ref, o1_ref, l1_ref, o2_ref, l2_ref, gate_ref,
                  wa_ref, wb_ref, wo_ref, g2_ref, wrh_ref, wrl_ref, br_ref,
                  x1_ref, h2_ref, code_ref, wcol_ref, cnt_ref,
                  so1_ref, sl1_ref, so2_ref, sl2_ref, yb_ref, run_ref):
    tm = x_ref.shape[0]

    @pl.when(pl.program_id(0) == 0)
    def _():
        run_ref[...] = jnp.zeros_like(run_ref)

    for dil, src_o, src_l, dst_o, dst_l in ((4, o1_ref, l1_ref, so1_ref, sl1_ref),
                                            (16, o2_ref, l2_ref, so2_ref, sl2_ref)):
        n = tm // dil
        for r in range(dil):
            for c in range(B_DIM // LANES):
                cs = slice(c * LANES, (c + 1) * LANES)
                dst_o[c, pl.ds(r, n, stride=dil), :] = src_o[r, :, cs]
                dst_l[c, pl.ds(r, n, stride=dil), :] = src_l[r, :, cs]

    for c in range(B_DIM // LANES):
        cs = slice(c * LANES, (c + 1) * LANES)
        l0, l1, l2 = l0_ref[:, cs], sl1_ref[c], sl2_ref[c]
        m = jnp.maximum(jnp.maximum(l0, l1), l2)
        e0, e1, e2 = jnp.exp(l0 - m), jnp.exp(l1 - m), jnp.exp(l2 - m)
        yb = (e0 * o0_ref[:, cs] + e1 * so1_ref[c] + e2 * so2_ref[c]) / (e0 + e1 + e2)
        yb_ref[:, cs] = yb.astype(BF16)

    ya_p = jnp.dot(ya_ref[...], wa_ref[...], preferred_element_type=F32)
    yb_p = jnp.dot(yb_ref[...], wb_ref[...], preferred_element_type=F32)
    merged = gate_ref[:, 0:D_MODEL].astype(F32) * ya_p + gate_ref[:, D_MODEL:GATE_COLS].astype(F32) * yb_p
    x1 = x_ref[...] + jnp.dot(merged.astype(BF16), wo_ref[...], preferred_element_type=F32)
    x1_ref[...] = x1

    h2 = x1 * lax.rsqrt(jnp.mean(x1 * x1, axis=-1, keepdims=True) + NORM_EPS) * g2_ref[...]
    _store_row_tiles(h2_ref, h2)

    h_hi = h2.astype(BF16)
    h_lo = (h2 - h_hi.astype(F32)).astype(BF16)
    logits = (jnp.dot(h_hi, wrh_ref[...], preferred_element_type=F32)
              + jnp.dot(h_lo, wrh_ref[...], preferred_element_type=F32)
              + jnp.dot(h_hi, wrl_ref[...], preferred_element_type=F32)) + br_ref[...]

    lane = lax.broadcasted_iota(I32, (tm, LANES), 1)
    is_grp = (lane >= N_EXPERTS) & (lane < N_EXPERTS + MOE_GROUPS)
    gl = jnp.where(is_grp, logits, -jnp.inf)
    gmax = jnp.max(gl, axis=-1, keepdims=True)
    gidx = jnp.min(jnp.where(gl == gmax, lane - N_EXPERTS, LANES), axis=-1, keepdims=True)
    gw = 1.0 / jnp.sum(jnp.exp(gl - gmax), axis=-1, keepdims=True)
    in_grp = (lane < N_EXPERTS) & ((lane // EXPERTS_PER_GROUP) == gidx)
    el = jnp.where(in_grp, logits, -jnp.inf)
    v1 = jnp.max(el, axis=-1, keepdims=True)
    i1 = jnp.min(jnp.where(el == v1, lane, LANES), axis=-1, keepdims=True)
    el2 = jnp.where(lane == i1, -jnp.inf, el)
    v2 = jnp.max(el2, axis=-1, keepdims=True)
    i2 = jnp.min(jnp.where(el2 == v2, lane, LANES), axis=-1, keepdims=True)
    t = jnp.exp(v2 - v1)
    w1 = gw / (1.0 + t)
    w2 = gw * t / (1.0 + t)

    oh1 = (lane == i1).astype(F32)
    oh2 = (lane == i2).astype(F32)
    oh = oh1 + oh2
    row = lax.broadcasted_iota(I32, (tm, tm), 0)
    col = lax.broadcasted_iota(I32, (tm, tm), 1)
    lower = (col < row).astype(BF16)
    before = jnp.dot(lower, oh.astype(BF16), preferred_element_type=F32) + run_ref[0:1, :]
    rank1 = jnp.sum(before * oh1, axis=-1, keepdims=True)
    rank2 = jnp.sum(before * oh2, axis=-1, keepdims=True)
    run_ref[...] = run_ref[...] + jnp.sum(oh, axis=0, keepdims=True)
    cnt_ref[...] = run_ref[...]

    scale = float(1 << RANK_BITS)
    code1 = i1.astype(F32) * scale + rank1
    code2 = i2.astype(F32) * scale + rank2
    meta = jnp.where(lane == 0, code1, jnp.where(lane == 1, code2, 0.0))
    code_ref[...] = meta.T[0:8, :].astype(I32)
    wcol_ref[...] = jnp.where(lane == 0, w1, jnp.where(lane == 1, w2, 0.0))


def _merge(x2d, ya, o0, l0, o1, l1, o2, l2, gates, wa, wb, wo, g2, wr_hi, wr_lo, br, batch, seq):
    tm = TOK_TILE
    tps = seq // tm
    n_tok = batch * seq

    def tok(c):
        return pl.BlockSpec((tm, c), lambda i: (i, 0))

    def full(a):
        return pl.BlockSpec(a.shape, lambda i: (0,) * a.ndim)

    def res_spec(d):
        return pl.BlockSpec((None, d, tm // d, B_DIM), lambda i: (i // tps, 0, i % tps, 0))

    return pl.pallas_call(
        _merge_kernel,
        grid=(n_tok // tm,),
        in_specs=[tok(D_MODEL), tok(A_Q_DIM), tok(B_DIM), tok(B_DIM),
                  res_spec(4), res_spec(4), res_spec(16), res_spec(16), tok(GATE_COLS),
                  full(wa), full(wb), full(wo), full(g2), full(wr_hi), full(wr_lo), full(br)],
        out_specs=[tok(D_MODEL), pl.BlockSpec((tm * ROW_CHUNKS, LANES), lambda i: (i, 0)),
                   pl.BlockSpec((8, tm), lambda i: (0, i)),
                   tok(LANES),
                   pl.BlockSpec((8, LANES), lambda i: (0, 0))],
        out_shape=[jax.ShapeDtypeStruct((n_tok, D_MODEL), F32),
                   jax.ShapeDtypeStruct((n_tok * ROW_CHUNKS, LANES), F32),
                   jax.ShapeDtypeStruct((8, n_tok), I32),
                   jax.ShapeDtypeStruct((n_tok, LANES), F32),
                   jax.ShapeDtypeStruct((8, LANES), F32)],
        scratch_shapes=([pltpu.VMEM((B_DIM // LANES, tm, LANES), F32)] * 4
                        + [pltpu.VMEM((tm, B_DIM), BF16), pltpu.VMEM((8, LANES), F32)]),
        compiler_params=_cparams("arbitrary"),
        name="merge_route",
    )(x2d, ya, o0, l0, o1, l1, o2, l2, gates, wa, wb, wo, g2, wr_hi, wr_lo, br)


def _row_tile(ref, t):
    return ref.at[pl.ds(pl.multiple_of(t * ROW_CHUNKS, ROW_CHUNKS), ROW_CHUNKS)]


def _store_row_tiles(ref, val):
    rows = val.shape[0]
    for c in range(ROW_CHUNKS):
        ref[pl.ds(c, rows, stride=ROW_CHUNKS), :] = val[:, c * LANES:(c + 1) * LANES]


def _load_row_tiles_chunk(ref, c):
    return ref[pl.ds(c, ref.shape[0] // ROW_CHUNKS, stride=ROW_CHUNKS), :]


def _dispatch_kernel(slot_ref, h_ref, xs_in_ref, xs_ref, sem, *, n_tok):
    del xs_in_ref
    i = pl.program_id(0)
    rows = h_ref.shape[0] // ROW_CHUNKS

    def issue(j, carry):
        t = i * rows + j
        for k in range(2):
            pltpu.make_async_copy(_row_tile(h_ref, j), _row_tile(xs_ref, slot_ref[k * n_tok + t]), sem).start(priority=k)
        return carry

    lax.fori_loop(0, rows, issue, 0, unroll=DMA_UNROLL)
    for _ in range(2):
        pltpu.make_async_copy(h_ref, xs_ref.at[pl.ds(0, rows * ROW_CHUNKS)], sem).wait()


def _dispatch(slots, h2, xs_init):
    n_tok = h2.shape[0] // ROW_CHUNKS
    rows = DISPATCH_TILE
    return pl.pallas_call(
        functools.partial(_dispatch_kernel, n_tok=n_tok),
        grid_spec=pltpu.PrefetchScalarGridSpec(
            num_scalar_prefetch=1,
            grid=(n_tok // rows,),
            in_specs=[pl.BlockSpec((rows * ROW_CHUNKS, LANES), lambda i, s: (i, 0)),
                      pl.BlockSpec(memory_space=pl.ANY)],
            out_specs=pl.BlockSpec(memory_space=pl.ANY),
            scratch_shapes=[pltpu.SemaphoreType.DMA(())],
        ),
        out_shape=jax.ShapeDtypeStruct(xs_init.shape, xs_init.dtype),
        input_output_aliases={2: 0},
        compiler_params=_cparams("arbitrary"),
        name="dispatch",
    )(slots, h2, xs_init)


def _combine_kernel(slot_ref, x_ref, w_ref, g_ref, ys_ref, o_ref, buf_ref, y_ref, sem, *, n_tok, final):
    i = pl.program_id(0)
    n_steps = pl.num_programs(0)
    rows = x_ref.shape[0]

    def issue_tile(tile, slot):
        def issue(j, carry):
            t = tile * rows + j
            for k in range(2):
                pltpu.make_async_copy(_row_tile(ys_ref, slot_ref[k * n_tok + t]),
                                      _row_tile(buf_ref.at[slot, k], j), sem.at[slot]).start(priority=k)
            return carry

        lax.fori_loop(0, rows, issue, 0, unroll=DMA_UNROLL)

    @pl.when(i == 0)
    def _():
        issue_tile(0, 0)

    @pl.when(i + 1 < n_steps)
    def _():
        issue_tile(i + 1, (i + 1) % 2)

    cur = i % 2
    for k in range(2):
        pltpu.make_async_copy(ys_ref.at[pl.ds(0, rows * ROW_CHUNKS)], buf_ref.at[cur, k], sem.at[cur]).wait()
    w = w_ref[...]
    w1, w2 = w[:, 0:1], w[:, 1:2]
    for c in range(ROW_CHUNKS):
        cs = slice(c * LANES, (c + 1) * LANES)
        y1 = _load_row_tiles_chunk(buf_ref.at[cur, 0], c)
        y2 = _load_row_tiles_chunk(buf_ref.at[cur, 1], c)
        y_ref[:, cs] = x_ref[:, cs] + w1 * y1 + w2 * y2
    y = y_ref[...]
    if final:
        y = y * lax.rsqrt(jnp.mean(y * y, axis=-1, keepdims=True) + NORM_EPS) * g_ref[...]
    o_ref[...] = y


def _combine(slots, x1, wcol, g, ys, final):
    n_tok = x1.shape[0]
    rows = ROW_TILE
    return pl.pallas_call(
        functools.partial(_combine_kernel, n_tok=n_tok, final=final),
        grid_spec=pltpu.PrefetchScalarGridSpec(
            num_scalar_prefetch=1,
            grid=(n_tok // rows,),
            in_specs=[pl.BlockSpec((rows, D_MODEL), lambda i, s: (i, 0)),
                      pl.BlockSpec((rows, LANES), lambda i, s: (i, 0)),
                      pl.BlockSpec((1, D_MODEL), lambda i, s: (0, 0)),
                      pl.BlockSpec(memory_space=pl.ANY)],
            out_specs=pl.BlockSpec((rows, D_MODEL), lambda i, s: (i, 0)),
            scratch_shapes=[pltpu.VMEM((2, 2, rows * ROW_CHUNKS, LANES), F32),
                            pltpu.VMEM((rows, D_MODEL), F32),
                            pltpu.SemaphoreType.DMA((2,))],
        ),
        out_shape=jax.ShapeDtypeStruct((n_tok, D_MODEL), F32),
        compiler_params=_cparams("arbitrary"),
        name="combine",
    )(slots, x1, wcol, g, ys)


def _experts_kernel(tile_e_ref, n_used_ref, xs_ref, wg_ref, wu_ref, wd_ref, ys_ref, x_ref):
    del tile_e_ref

    @pl.when(pl.program_id(0) < n_used_ref[0])
    def _():
        for c in range(ROW_CHUNKS):
            x_ref[:, c * LANES:(c + 1) * LANES] = _load_row_tiles_chunk(xs_ref, c).astype(BF16)
        x = x_ref[...]
        a = jnp.dot(x, wg_ref[...].astype(BF16), preferred_element_type=F32)
        u = jnp.dot(x, wu_ref[...].astype(BF16), preferred_element_type=F32)
        z = (a * jax.nn.sigmoid(a)) * u
        _store_row_tiles(ys_ref, jnp.dot(z.astype(BF16), wd_ref[...].astype(BF16), preferred_element_type=F32))

    @pl.when(pl.program_id(0) >= n_used_ref[0])
    def _():
        ys_ref[...] = jnp.zeros_like(ys_ref)


def _experts(tile_e, n_used, xs, wg, wu, wd, layer):
    n_slots = xs.shape[0] // ROW_CHUNKS
    te = EXP_TILE

    def row_map(i, tile_e, n_used):
        return (jnp.minimum(i, n_used[0] - 1), 0)

    def out_map(i, tile_e, n_used):
        return (i, 0)

    def w_map(i, tile_e, n_used):
        return (layer, tile_e[i], 0, 0)

    return pl.pallas_call(
        _experts_kernel,
        grid_spec=pltpu.PrefetchScalarGridSpec(
            num_scalar_prefetch=2,
            grid=(n_slots // te,),
            in_specs=[pl.BlockSpec((te * ROW_CHUNKS, LANES), row_map),
                      pl.BlockSpec((None, None, D_MODEL, D_EXPERT), w_map),
                      pl.BlockSpec((None, None, D_MODEL, D_EXPERT), w_map),
                      pl.BlockSpec((None, None, D_EXPERT, D_MODEL), w_map)],
            out_specs=pl.BlockSpec((te * ROW_CHUNKS, LANES), out_map),
            scratch_shapes=[pltpu.VMEM((te, D_MODEL), BF16)],
        ),
        out_shape=jax.ShapeDtypeStruct((n_slots * ROW_CHUNKS, LANES), F32),
        compiler_params=_cparams("arbitrary"),
        name="experts",
    )(tile_e, n_used, xs, wg, wu, wd)


def kernel(x, attn_norm_g, w_in, a_sink, w_branch_a, w_branch_b, w_out, ffn_norm_g,
           w_router_group, b_router_group, w_router_expert, b_router_expert,
           w_exp_gate, w_exp_up, w_exp_down, final_norm_g):
    batch, seq, d_model = x.shape
    depth = w_in.shape[0]
    n_tok = batch * seq
    assert d_model == D_MODEL and w_in.shape[2] == D_IN
    assert seq % (16 * B_SUB) == 0 and seq % TOK_TILE == 0 and n_tok % DISPATCH_TILE == 0
    assert n_tok < (1 << RANK_BITS)

    cos_t, sin_t = _rope_tables(seq)
    tables = (cos_t, sin_t,
              _residue_order(cos_t, 4, TOK_TILE), _residue_order(sin_t, 4, TOK_TILE),
              _residue_order(cos_t, 16, TOK_TILE), _residue_order(sin_t, 16, TOK_TILE))

    n_slots = 2 * n_tok + N_EXPERTS * EXP_TILE
    n_tiles = n_slots // EXP_TILE
    x2d = x.reshape(n_tok, D_MODEL)

    for l in range(depth):
        nat, gates, grp1, grp2 = _in_proj(x2d, attn_norm_g[l][None, :], w_in[l].astype(BF16), tables, batch, seq)
        nat3d = nat.reshape(batch, seq, NAT_COLS)
        ya = _attn_a(nat3d, a_sink[l]).reshape(n_tok, A_Q_DIM)
        o0, l0 = _attn_b(nat3d, NAT_B // B_DIM, "attn_b1")
        o1, l1 = _attn_b(grp1.reshape(batch * 4, seq // 4, GRP_COLS), 0, "attn_b4")
        o2, l2 = _attn_b(grp2.reshape(batch * 16, seq // 16, GRP_COLS), 0, "attn_b16")

        wr = jnp.zeros((D_MODEL, LANES), F32)
        wr = wr.at[:, 0:N_EXPERTS].set(w_router_expert[l]).at[:, N_EXPERTS:N_EXPERTS + MOE_GROUPS].set(w_router_group[l])
        br = jnp.zeros((1, LANES), F32)
        br = br.at[0, 0:N_EXPERTS].set(b_router_expert[l]).at[0, N_EXPERTS:N_EXPERTS + MOE_GROUPS].set(b_router_group[l])
        wr_hi = wr.astype(BF16)
        wr_lo = (wr - wr_hi.astype(F32)).astype(BF16)

        x1, h2, code, wcol, cnt = _merge(
            x2d, ya, o0.reshape(n_tok, B_DIM), l0.reshape(n_tok, B_DIM),
            o1.reshape(batch, 4, seq // 4, B_DIM), l1.reshape(batch, 4, seq // 4, B_DIM),
            o2.reshape(batch, 16, seq // 16, B_DIM), l2.reshape(batch, 16, seq // 16, B_DIM),
            gates, w_branch_a[l].astype(BF16), w_branch_b[l].astype(BF16), w_out[l].astype(BF16),
            ffn_norm_g[l][None, :], wr_hi, wr_lo, br, batch, seq)

        counts = cnt[0, 0:N_EXPERTS].astype(I32)
        padded = ((counts + EXP_TILE - 1) // EXP_TILE) * EXP_TILE
        ends = jnp.cumsum(padded)
        offs = ends - padded
        n_used = (ends[-1:] // EXP_TILE).astype(I32)
        tile_start = jnp.arange(n_tiles, dtype=I32) * EXP_TILE
        tile_e = jnp.minimum(jnp.sum((ends[None, :] <= tile_start[:, None]).astype(I32), axis=1), N_EXPERTS - 1)
        eid = code[0:2] >> RANK_BITS
        rank = code[0:2] & ((1 << RANK_BITS) - 1)
        expert_ids = jnp.arange(N_EXPERTS, dtype=I32)[:, None, None]
        slots = (rank + jnp.sum(jnp.where(eid[None] == expert_ids, offs[:, None, None], 0), axis=0)).reshape(-1)

        xs = _dispatch(slots, h2, jnp.zeros((n_slots * ROW_CHUNKS, LANES), F32))
        ys = _experts(tile_e, n_used, xs, w_exp_gate, w_exp_up, w_exp_down, l)
        x2d = _combine(slots, x1, wcol, final_norm_g[None, :], ys, final=(l == depth - 1))

    return x2d.reshape(batch, seq, D_MODEL)
```

```python
import functools

import jax
import jax.numpy as jnp
from jax import lax
from jax.experimental import pallas as pl
from jax.experimental.pallas import tpu as pltpu

F32 = jnp.float32
BF16 = jnp.bfloat16
I32 = jnp.int32

D_MODEL = 1024
HEAD_DIM = 64
HALF_HEAD = HEAD_DIM // 2
ROPE_THETA = 10000.0
NORM_EPS = 1e-6
NEG_INF = -1e30
LANES = 128

A_Q_HEADS = 8
A_KV_HEADS = 2
A_GROUP = A_Q_HEADS // A_KV_HEADS
A_HALF_WINDOW = 128
A_Q_DIM = A_Q_HEADS * HEAD_DIM
A_KV_DIM = A_KV_HEADS * HEAD_DIM

B_GROUPS = ((128, 1), (512, 4), (2048, 16))
B_HEADS = 4
B_DIM = B_HEADS * HEAD_DIM
B_HALF_WINDOW = 64

MOE_GROUPS = 4
EXPERTS_PER_GROUP = 8
N_EXPERTS = MOE_GROUPS * EXPERTS_PER_GROUP
D_EXPERT = 256

NAT_IN_COLS = A_Q_DIM + 2 * A_KV_DIM + 3 * B_DIM
NAT_B = A_Q_DIM
NAT_AK = NAT_B + 3 * B_DIM
NAT_AV = NAT_AK + 2 * A_KV_DIM
NAT_COLS = NAT_AV + A_KV_DIM
GRP_COLS = 3 * B_DIM
COL_G1 = NAT_IN_COLS
COL_G2 = COL_G1 + GRP_COLS
COL_GATE = COL_G2 + GRP_COLS
GATE_COLS = 2 * D_MODEL
D_IN = COL_GATE + GATE_COLS

TOK_TILE = 512
A_Q_TILE = 512
B_SUB = 128
EXP_TILE = 512
ROW_CHUNKS = D_MODEL // LANES
ROW_TILE = 256
DISPATCH_TILE = 1024
DMA_UNROLL = 8
RANK_BITS = 16
VMEM_LIMIT = 56 * 1024 * 1024


def _cparams(*sem):
    return pltpu.CompilerParams(dimension_semantics=sem, vmem_limit_bytes=VMEM_LIMIT)


def _rope_tables(seq_len):
    inv = 1.0 / (ROPE_THETA ** (jnp.arange(0, HEAD_DIM, 2, dtype=F32) / HEAD_DIM))
    ang = jnp.arange(seq_len, dtype=F32)[:, None] * inv[None, :]
    cos, sin = jnp.cos(ang), jnp.sin(ang)
    cos_t = jnp.concatenate([cos, cos, cos, cos], axis=-1)
    sin_t = jnp.concatenate([-sin, sin, -sin, sin], axis=-1)
    return cos_t, sin_t


def _residue_order(table, dilation, tile):
    s, c = table.shape
    return table.reshape(s // tile, tile // dilation, dilation, c).transpose(0, 2, 1, 3).reshape(s, c)


def _rope(t, cos, sin_signed, first_half):
    partner = jnp.where(first_half, pltpu.roll(t, LANES - HALF_HEAD, 1), pltpu.roll(t, HALF_HEAD, 1))
    return t * cos + partner * sin_signed


Q_KIND, K_KIND, V_KIND = 0, 1, 2
_NAT_KINDS = ([Q_KIND] * 4 + [K_KIND] + [V_KIND] + [Q_KIND] * 2 + [K_KIND] * 2 + [V_KIND] * 2)
_GRP_KINDS = [Q_KIND] * 2 + [K_KIND] * 2 + [V_KIND] * 2


def _in_proj_kernel(x_ref, g_ref, w_ref, c1_ref, s1_ref, c4_ref, s4_ref, c16_ref, s16_ref,
                    nat_ref, gate_ref, g1_ref, g2_ref, hf_ref, hb_ref, hd_ref):
    tm = x_ref.shape[0]
    x = x_ref[...]
    h = x * lax.rsqrt(jnp.mean(x * x, axis=-1, keepdims=True) + NORM_EPS) * g_ref[...]
    n_chunks = D_MODEL // LANES
    for c in range(n_chunks):
        hf_ref[c] = h[:, c * LANES:(c + 1) * LANES]
    hb_ref[...] = h.astype(BF16)
    lane = lax.broadcasted_iota(I32, (1, LANES), 1)
    first_half = (lane % HEAD_DIM) < HALF_HEAD

    def project(h_b, col0, kinds, cos_ref, sin_ref, store):
        width = 512
        for c0 in range(0, len(kinds) * LANES, width):
            w = min(width, len(kinds) * LANES - c0)
            res = jnp.dot(h_b, w_ref[:, col0 + c0:col0 + c0 + w], preferred_element_type=F32)
            for j in range(w // LANES):
                kind = kinds[(c0 // LANES) + j]
                t = res[:, j * LANES:(j + 1) * LANES]
                if kind != V_KIND:
                    t = _rope(t, cos_ref[...], sin_ref[...], first_half)
                if kind == Q_KIND:
                    t = t * (HEAD_DIM ** -0.5)
                store(c0 + j * LANES, t)

    low_head = lane < HEAD_DIM

    def store_nat(c, t):
        if A_Q_DIM <= c < A_Q_DIM + A_KV_DIM:
            swapped = pltpu.roll(t, HEAD_DIM, 1)
            nat_ref[:, NAT_AK:NAT_AK + LANES] = jnp.where(low_head, t, swapped).astype(BF16)
            nat_ref[:, NAT_AK + LANES:NAT_AK + 2 * LANES] = jnp.where(low_head, swapped, t).astype(BF16)
        elif c < A_Q_DIM + 2 * A_KV_DIM:
            out = c if c < A_Q_DIM else NAT_AV
            nat_ref[:, out:out + LANES] = t.astype(BF16)
        else:
            out = c - 2 * A_KV_DIM
            nat_ref[:, out:out + LANES] = t.astype(BF16)

    project(hb_ref[...], 0, _NAT_KINDS, c1_ref, s1_ref, store_nat)

    for c0 in range(0, GATE_COLS, 512):
        res = jnp.dot(hb_ref[...], w_ref[:, COL_GATE + c0:COL_GATE + c0 + 512], preferred_element_type=F32)
        gate_ref[:, c0:c0 + 512] = jax.nn.sigmoid(res).astype(BF16)

    for dil, col0, cos_ref, sin_ref, out_ref in ((4, COL_G1, c4_ref, s4_ref, g1_ref),
                                                 (16, COL_G2, c16_ref, s16_ref, g2_ref)):
        n = tm // dil
        for r in range(dil):
            for c in range(n_chunks):
                hd_ref[r * n:(r + 1) * n, c * LANES:(c + 1) * LANES] = (
                    hf_ref[c, pl.ds(r, n, stride=dil), :].astype(BF16))

        def store_grp(c, t, out_ref=out_ref, dil=dil, n=n):
            v = t.astype(BF16)
            for r in range(dil):
                out_ref[r, :, c:c + LANES] = v[r * n:(r + 1) * n]

        project(hd_ref[...], col0, _GRP_KINDS, cos_ref, sin_ref, store_grp)


def _in_proj(x2d, g, w_bf16, tables, batch, seq):
    tm = TOK_TILE
    tiles_per_seq = seq // tm
    n_tok = batch * seq
    c1, s1, c4, s4, c16, s16 = tables
    tab_spec = pl.BlockSpec((tm, LANES), lambda i: (i % tiles_per_seq, 0))
    return pl.pallas_call(
        _in_proj_kernel,
        grid=(n_tok // tm,),
        in_specs=[
            pl.BlockSpec((tm, D_MODEL), lambda i: (i, 0)),
            pl.BlockSpec((1, D_MODEL), lambda i: (0, 0)),
            pl.BlockSpec((D_MODEL, D_IN), lambda i: (0, 0), pipeline_mode=pl.Buffered(1)),
            tab_spec, tab_spec, tab_spec, tab_spec, tab_spec, tab_spec,
        ],
        out_specs=[
            pl.BlockSpec((tm, NAT_COLS), lambda i: (i, 0)),
            pl.BlockSpec((tm, GATE_COLS), lambda i: (i, 0)),
            pl.BlockSpec((None, 4, tm // 4, GRP_COLS), lambda i: (i // tiles_per_seq, 0, i % tiles_per_seq, 0)),
            pl.BlockSpec((None, 16, tm // 16, GRP_COLS), lambda i: (i // tiles_per_seq, 0, i % tiles_per_seq, 0)),
        ],
        out_shape=[
            jax.ShapeDtypeStruct((n_tok, NAT_COLS), BF16),
            jax.ShapeDtypeStruct((n_tok, GATE_COLS), BF16),
            jax.ShapeDtypeStruct((batch, 4, seq // 4, GRP_COLS), BF16),
            jax.ShapeDtypeStruct((batch, 16, seq // 16, GRP_COLS), BF16),
        ],
        scratch_shapes=[
            pltpu.VMEM((D_MODEL // LANES, tm, LANES), F32),
            pltpu.VMEM((tm, D_MODEL), BF16),
            pltpu.VMEM((tm, D_MODEL), BF16),
        ],
        compiler_params=_cparams("parallel"),
        name="in_proj",
    )(x2d, g, w_bf16, c1, s1, c4, s4, c16, s16)


def _masked_heads(q_pair, low_head, high_head):
    zero = jnp.zeros_like(q_pair)
    return [jnp.where(low_head, q_pair, zero), jnp.where(high_head, q_pair, zero)]


def _attn_a_kernel(sink_ref, q_ref, kp_ref, km_ref, kn_ref, vp_ref, vm_ref, vn_ref, o_ref, k_ref, vt_ref, *, seq):
    tq = q_ref.shape[0]
    hw = A_HALF_WINDOW
    n_sub = tq // hw
    i = pl.program_id(1)
    last_blk = seq // hw - 1
    k_ref[0:hw, :] = kp_ref[...]
    k_ref[hw:hw + tq, :] = km_ref[...]
    k_ref[hw + tq:tq + 2 * hw, :] = kn_ref[...]
    for r0, src in ((0, vp_ref), (hw, vm_ref), (hw + tq, vn_ref)):
        vt_ref[:, r0:r0 + src.shape[0]] = src[...].astype(F32).T.astype(BF16)
    grp_cols = A_GROUP * hw
    key = lax.broadcasted_iota(I32, (hw, grp_cols), 0)
    qry = lax.broadcasted_iota(I32, (hw, grp_cols), 1) % hw
    low_head = lax.broadcasted_iota(I32, (1, LANES), 1) < HEAD_DIM
    high_head = jnp.logical_not(low_head)
    for sb in range(n_sub):
        r0 = sb * hw
        blk = i * n_sub + sb
        mask_p = (key >= qry) if sb > 0 else (key >= qry + jnp.where(blk > 0, 0, hw))
        mask_n = (key <= qry) if sb < n_sub - 1 else (key <= qry - jnp.where(blk < last_blk, 0, hw))
        out_t = []
        for g in range(A_KV_HEADS):
            heads = range(g * A_GROUP, (g + 1) * A_GROUP)
            q_parts = []
            for c in range(g * A_GROUP // 2, (g + 1) * A_GROUP // 2):
                q_parts += _masked_heads(q_ref[r0:r0 + hw, c * LANES:(c + 1) * LANES], low_head, high_head)
            q = jnp.concatenate(q_parts, axis=0)
            sink = jnp.concatenate([jnp.full((1, hw), sink_ref[h], F32) for h in heads], axis=1)
            k = k_ref[r0:r0 + 3 * hw, g * LANES:(g + 1) * LANES]
            s = lax.dot_general(k, q, (((1,), (1,)), ((), ())), preferred_element_type=F32)
            sp = jnp.where(mask_p, s[0:hw], NEG_INF)
            so = s[hw:2 * hw]
            sn = jnp.where(mask_n, s[2 * hw:3 * hw], NEG_INF)
            m = jnp.max(jnp.maximum(jnp.maximum(sp, so), sn), axis=0, keepdims=True)
            m = jnp.maximum(m, sink)
            pp, po, pn = jnp.exp(sp - m), jnp.exp(so - m), jnp.exp(sn - m)
            denom = jnp.sum(pp + po + pn, axis=0, keepdims=True) + jnp.exp(sink - m)
            p = jnp.concatenate([pp, po, pn], axis=0).astype(BF16)
            vt = vt_ref[g * HEAD_DIM:(g + 1) * HEAD_DIM, r0:r0 + 3 * hw]
            o = jnp.dot(vt, p, preferred_element_type=F32) * (1.0 / denom)
            out_t += [o[:, j * hw:(j + 1) * hw] for j in range(A_GROUP)]
        o_ref[r0:r0 + hw, :] = jnp.concatenate(out_t, axis=0).T.astype(BF16)


def _attn_a(nat3d, sink):
    batch, seq, _ = nat3d.shape
    tq = A_Q_TILE
    hw = A_HALF_WINDOW
    per = tq // hw
    n_hw = seq // hw
    k_cols, v_cols = 2 * A_KV_DIM, A_KV_DIM
    k_blk, v_blk = NAT_AK // k_cols, NAT_AV // v_cols

    def prev_spec(cols, blk):
        return pl.BlockSpec((None, hw, cols), lambda b, i: (b, jnp.maximum(i * per - 1, 0), blk))

    def main_spec(cols, blk):
        return pl.BlockSpec((None, tq, cols), lambda b, i: (b, i, blk))

    def next_spec(cols, blk):
        return pl.BlockSpec((None, hw, cols), lambda b, i: (b, jnp.minimum((i + 1) * per, n_hw - 1), blk))

    return pl.pallas_call(
        functools.partial(_attn_a_kernel, seq=seq),
        grid=(batch, seq // tq),
        in_specs=[
            pl.BlockSpec(memory_space=pltpu.SMEM),
            main_spec(A_Q_DIM, 0),
            prev_spec(k_cols, k_blk), main_spec(k_cols, k_blk), next_spec(k_cols, k_blk),
            prev_spec(v_cols, v_blk), main_spec(v_cols, v_blk), next_spec(v_cols, v_blk),
        ],
        out_specs=pl.BlockSpec((None, tq, A_Q_DIM), lambda b, i: (b, i, 0)),
        out_shape=jax.ShapeDtypeStruct((batch, seq, A_Q_DIM), BF16),
        scratch_shapes=[pltpu.VMEM((tq + 2 * hw, k_cols), BF16), pltpu.VMEM((v_cols, tq + 2 * hw), BF16)],
        compiler_params=_cparams("parallel", "parallel"),
        name="attn_a",
    )(sink, nat3d, nat3d, nat3d, nat3d, nat3d, nat3d, nat3d)


def _attn_b_kernel(q_ref, kp_ref, km_ref, kn_ref, vp_ref, vm_ref, vn_ref, o_ref, lse_ref,
                   k_ref, vt_ref, *, sub_len):
    tq = q_ref.shape[0]
    hw = B_HALF_WINDOW
    t0 = pl.program_id(1) * tq
    k_ref[0:hw, :] = kp_ref[...]
    k_ref[hw:hw + tq, :] = km_ref[...]
    k_ref[hw + tq:tq + 2 * hw, :] = kn_ref[...]
    for r0, src in ((0, vp_ref), (hw, vm_ref), (hw + tq, vn_ref)):
        vt_ref[:, r0:r0 + src.shape[0]] = src[...].astype(F32).T.astype(BF16)
    kw = B_SUB + 2 * hw
    n_sub = tq // B_SUB
    all_cols = B_HEADS * B_SUB
    key = lax.broadcasted_iota(I32, (kw, all_cols), 0)
    qry = lax.broadcasted_iota(I32, (kw, all_cols), 1) % B_SUB
    in_band = jnp.abs(key - hw - qry) <= hw
    low_head = lax.broadcasted_iota(I32, (1, LANES), 1) < HEAD_DIM
    high_head = jnp.logical_not(low_head)
    for sb in range(n_sub):
        r0 = sb * B_SUB
        valid = in_band
        if sb == 0:
            valid = valid & (t0 - hw + key >= 0)
        if sb == n_sub - 1:
            valid = valid & (t0 + r0 - hw + key < sub_len)
        s_parts = []
        for c in range(B_HEADS // 2):
            cs = slice(c * LANES, (c + 1) * LANES)
            q = jnp.concatenate(_masked_heads(q_ref[r0:r0 + B_SUB, cs], low_head, high_head), axis=0)
            s_parts.append(lax.dot_general(k_ref[r0:r0 + kw, cs], q, (((1,), (1,)), ((), ())),
                                           preferred_element_type=F32))
        s = jnp.where(valid, jnp.concatenate(s_parts, axis=1), NEG_INF)
        m = jnp.max(s, axis=0, keepdims=True)
        p = jnp.exp(s - m)
        denom = jnp.sum(p, axis=0, keepdims=True)
        p = p.astype(BF16)
        inv = 1.0 / denom
        lse = m + jnp.log(denom)
        out_t, lse_t = [], []
        for h in range(B_HEADS):
            qs = slice(h * B_SUB, (h + 1) * B_SUB)
            vt = vt_ref[h * HEAD_DIM:(h + 1) * HEAD_DIM, r0:r0 + kw]
            out_t.append(jnp.dot(vt, p[:, qs], preferred_element_type=F32) * inv[:, qs])
            lse_t.append(jnp.broadcast_to(lse[:, qs], (HEAD_DIM, B_SUB)))
        o_ref[r0:r0 + B_SUB, :] = jnp.concatenate(out_t, axis=0).T
        lse_ref[r0:r0 + B_SUB, :] = jnp.concatenate(lse_t, axis=0).T


def _attn_b(arr3d, q_blk, name):
    n_sub, sub_len, _ = arr3d.shape
    tq = min(512, sub_len)
    hw = B_HALF_WINDOW
    per = tq // hw
    n_hw = sub_len // hw

    def main_spec(c):
        return pl.BlockSpec((None, tq, B_DIM), lambda g, i: (g, i, c))

    def prev_spec(c):
        return pl.BlockSpec((None, hw, B_DIM), lambda g, i: (g, jnp.maximum(i * per - 1, 0), c))

    def next_spec(c):
        return pl.BlockSpec((None, hw, B_DIM), lambda g, i: (g, jnp.minimum((i + 1) * per, n_hw - 1), c))

    out_spec = pl.BlockSpec((None, tq, B_DIM), lambda g, i: (g, i, 0))
    return pl.pallas_call(
        functools.partial(_attn_b_kernel, sub_len=sub_len),
        grid=(n_sub, sub_len // tq),
        in_specs=[main_spec(q_blk),
                  prev_spec(q_blk + 1), main_spec(q_blk + 1), next_spec(q_blk + 1),
                  prev_spec(q_blk + 2), main_spec(q_blk + 2), next_spec(q_blk + 2)],
        out_specs=[out_spec, out_spec],
        out_shape=[jax.ShapeDtypeStruct((n_sub, sub_len, B_DIM), F32)] * 2,
        scratch_shapes=[pltpu.VMEM((tq + 2 * hw, B_DIM), BF16), pltpu.VMEM((B_DIM, tq + 2 * hw), BF16)],
        compiler_params=_cparams("parallel", "parallel"),
        name=name,
    )(arr3d, arr3d, arr3d, arr3d, arr3d, arr3d, arr3d)


def _merge_kernel(x_ref, ya_ref, o0_ref, l0_ref, o1_ref, l1_ref, o2_ref, l2_ref, gate_ref,
                  wa_ref, wb_ref, wo_ref, g2_ref, wr_ref, br_ref,
                  x1_ref, h2_ref, code_ref, wcol_ref, cnt_ref,
                  so1_ref, sl1_ref, so2_ref, sl2_ref, yb_ref, run_ref, earlier_ref):
    tm = x_ref.shape[0]

    @pl.when(pl.program_id(0) == 0)
    def _():
        run_ref[...] = jnp.zeros_like(run_ref)
        row = lax.broadcasted_iota(I32, (tm, tm), 0)
        col = lax.broadcasted_iota(I32, (tm, tm), 1)
        earlier_ref[...] = (row < col).astype(BF16)

    for dil, src_o, src_l, dst_o, dst_l in ((4, o1_ref, l1_ref, so1_ref, sl1_ref),
                                            (16, o2_ref, l2_ref, so2_ref, sl2_ref)):
        n = tm // dil
        for r in range(dil):
            for c in range(B_DIM // LANES):
                cs = slice(c * LANES, (c + 1) * LANES)
                dst_o[c, pl.ds(r, n, stride=dil), :] = src_o[r, :, cs]
                dst_l[c, pl.ds(r, n, stride=dil), :] = src_l[r, :, cs]

    for c in range(B_DIM // LANES):
        cs = slice(c * LANES, (c + 1) * LANES)
        l0, l1, l2 = l0_ref[:, cs], sl1_ref[c], sl2_ref[c]
        m = jnp.maximum(jnp.maximum(l0, l1), l2)
        e0, e1, e2 = jnp.exp(l0 - m), jnp.exp(l1 - m), jnp.exp(l2 - m)
        yb = (e0 * o0_ref[:, cs] + e1 * so1_ref[c] + e2 * so2_ref[c]) / (e0 + e1 + e2)
        yb_ref[:, cs] = yb.astype(BF16)

    ya_p = jnp.dot(ya_ref[...], wa_ref[...], preferred_element_type=F32)
    yb_p = jnp.dot(yb_ref[...], wb_ref[...], preferred_element_type=F32)
    merged = gate_ref[:, 0:D_MODEL].astype(F32) * ya_p + gate_ref[:, D_MODEL:GATE_COLS].astype(F32) * yb_p
    x1 = x_ref[...] + jnp.dot(merged.astype(BF16), wo_ref[...], preferred_element_type=F32)
    x1_ref[...] = x1

    h2 = x1 * lax.rsqrt(jnp.mean(x1 * x1, axis=-1, keepdims=True) + NORM_EPS) * g2_ref[...]
    _store_row_tiles(h2_ref, h2)

    h_hi = h2.astype(BF16)
    h_lo = (h2 - h_hi.astype(F32)).astype(BF16)
    w_hi, w_lo = wr_ref[0:D_MODEL], wr_ref[D_MODEL:2 * D_MODEL]
    logits = (jnp.dot(h_hi, w_hi, preferred_element_type=F32)
              + jnp.dot(h_lo, w_hi, preferred_element_type=F32)
              + jnp.dot(h_hi, w_lo, preferred_element_type=F32)) + br_ref[...]

    logits_t = logits.T
    sub = lax.broadcasted_iota(I32, (EXPERTS_PER_GROUP, tm), 0).astype(F32)
    none = float(EXPERTS_PER_GROUP)
    gl = jnp.where(sub < MOE_GROUPS, logits_t[N_EXPERTS:N_EXPERTS + EXPERTS_PER_GROUP], -jnp.inf)
    gmax = jnp.max(gl, axis=0, keepdims=True)
    gidx = jnp.min(jnp.where(gl == gmax, sub, none), axis=0, keepdims=True)
    gw = 1.0 / jnp.sum(jnp.exp(gl - gmax), axis=0, keepdims=True)
    el = logits_t[0:EXPERTS_PER_GROUP]
    for g in range(1, MOE_GROUPS):
        el = jnp.where(gidx == g, logits_t[g * EXPERTS_PER_GROUP:(g + 1) * EXPERTS_PER_GROUP], el)
    v1 = jnp.max(el, axis=0, keepdims=True)
    i1 = jnp.min(jnp.where(el == v1, sub, none), axis=0, keepdims=True)
    el2 = jnp.where(sub == i1, -jnp.inf, el)
    v2 = jnp.max(el2, axis=0, keepdims=True)
    i2 = jnp.min(jnp.where(el2 == v2, sub, none), axis=0, keepdims=True)
    t = jnp.exp(v2 - v1)
    w1 = gw / (1.0 + t)
    w2 = gw * t / (1.0 + t)
    e1 = gidx * EXPERTS_PER_GROUP + i1
    e2 = gidx * EXPERTS_PER_GROUP + i2

    expert = lax.broadcasted_iota(I32, (N_EXPERTS, tm), 0).astype(F32)
    oh1 = (expert == e1).astype(F32)
    oh2 = (expert == e2).astype(F32)
    oh = oh1 + oh2
    run = run_ref[...]
    before = (jnp.dot(oh.astype(BF16), earlier_ref[...], preferred_element_type=F32)
              + jnp.concatenate([run] * (tm // LANES), axis=1))
    rank1 = jnp.sum(before * oh1, axis=0, keepdims=True)
    rank2 = jnp.sum(before * oh2, axis=0, keepdims=True)
    run_ref[...] = run + jnp.sum(oh, axis=1, keepdims=True)
    cnt_ref[...] = run_ref[...]

    scale = float(1 << RANK_BITS)
    zeros = jnp.zeros((6, tm), F32)
    code_ref[...] = jnp.concatenate([e1 * scale + rank1, e2 * scale + rank2, zeros], axis=0).astype(I32)
    w_rows = jnp.concatenate([w1, w2, jnp.zeros((LANES - 2, tm), F32)], axis=0)
    wcol_ref[...] = w_rows.T


def _merge(x2d, ya, o0, l0, o1, l1, o2, l2, gates, wa, wb, wo, g2, wr, br, batch, seq):
    tm = TOK_TILE
    tps = seq // tm
    n_tok = batch * seq

    def tok(c):
        return pl.BlockSpec((tm, c), lambda i: (i, 0))

    def full(a):
        return pl.BlockSpec(a.shape, lambda i: (0,) * a.ndim)

    def res_spec(d):
        return pl.BlockSpec((None, d, tm // d, B_DIM), lambda i: (i // tps, 0, i % tps, 0))

    return pl.pallas_call(
        _merge_kernel,
        grid=(n_tok // tm,),
        in_specs=[tok(D_MODEL), tok(A_Q_DIM), tok(B_DIM), tok(B_DIM),
                  res_spec(4), res_spec(4), res_spec(16), res_spec(16), tok(GATE_COLS),
                  full(wa), full(wb), full(wo), full(g2), full(wr), full(br)],
        out_specs=[tok(D_MODEL), pl.BlockSpec((tm * ROW_CHUNKS, LANES), lambda i: (i, 0)),
                   pl.BlockSpec((8, tm), lambda i: (0, i)),
                   tok(LANES),
                   pl.BlockSpec((N_EXPERTS, LANES), lambda i: (0, 0))],
        out_shape=[jax.ShapeDtypeStruct((n_tok, D_MODEL), F32),
                   jax.ShapeDtypeStruct((n_tok * ROW_CHUNKS, LANES), F32),
                   jax.ShapeDtypeStruct((8, n_tok), I32),
                   jax.ShapeDtypeStruct((n_tok, LANES), F32),
                   jax.ShapeDtypeStruct((N_EXPERTS, LANES), F32)],
        scratch_shapes=([pltpu.VMEM((B_DIM // LANES, tm, LANES), F32)] * 4
                        + [pltpu.VMEM((tm, B_DIM), BF16), pltpu.VMEM((N_EXPERTS, LANES), F32),
                           pltpu.VMEM((tm, tm), BF16)]),
        compiler_params=_cparams("arbitrary"),
        name="merge_route",
    )(x2d, ya, o0, l0, o1, l1, o2, l2, gates, wa, wb, wo, g2, wr, br)


def _row_tile(ref, t):
    return ref.at[pl.ds(pl.multiple_of(t * ROW_CHUNKS, ROW_CHUNKS), ROW_CHUNKS)]


def _store_row_tiles(ref, val):
    rows = val.shape[0]
    for c in range(ROW_CHUNKS):
        ref[pl.ds(c, rows, stride=ROW_CHUNKS), :] = val[:, c * LANES:(c + 1) * LANES]


def _load_row_tiles_chunk(ref, c):
    return ref[pl.ds(c, ref.shape[0] // ROW_CHUNKS, stride=ROW_CHUNKS), :]


def _dispatch_kernel(slot_ref, h_ref, xs_in_ref, xs_ref, sem, *, n_tok):
    del xs_in_ref
    i = pl.program_id(0)
    rows = h_ref.shape[0] // ROW_CHUNKS

    def issue(j, carry):
        t = i * rows + j
        for k in range(2):
            pltpu.make_async_copy(_row_tile(h_ref, j), _row_tile(xs_ref, slot_ref[k * n_tok + t]), sem).start(priority=k)
        return carry

    lax.fori_loop(0, rows, issue, 0, unroll=DMA_UNROLL)
    for _ in range(2):
        pltpu.make_async_copy(h_ref, xs_ref.at[pl.ds(0, rows * ROW_CHUNKS)], sem).wait()


def _dispatch(slots, h2, xs_init):
    n_tok = h2.shape[0] // ROW_CHUNKS
    rows = DISPATCH_TILE
    return pl.pallas_call(
        functools.partial(_dispatch_kernel, n_tok=n_tok),
        grid_spec=pltpu.PrefetchScalarGridSpec(
            num_scalar_prefetch=1,
            grid=(n_tok // rows,),
            in_specs=[pl.BlockSpec((rows * ROW_CHUNKS, LANES), lambda i, s: (i, 0)),
                      pl.BlockSpec(memory_space=pl.ANY)],
            out_specs=pl.BlockSpec(memory_space=pl.ANY),
            scratch_shapes=[pltpu.SemaphoreType.DMA(())],
        ),
        out_shape=jax.ShapeDtypeStruct(xs_init.shape, xs_init.dtype),
        input_output_aliases={2: 0},
        compiler_params=_cparams("arbitrary"),
        name="dispatch",
    )(slots, h2, xs_init)


def _combine_kernel(slot_ref, x_ref, w_ref, g_ref, ys_ref, o_ref, buf_ref, y_ref, sem, *, n_tok, final):
    i = pl.program_id(0)
    n_steps = pl.num_programs(0)
    rows = x_ref.shape[0]

    def issue_tile(tile, slot):
        def issue(j, carry):
            t = tile * rows + j
            for k in range(2):
                pltpu.make_async_copy(_row_tile(ys_ref, slot_ref[k * n_tok + t]),
                                      _row_tile(buf_ref.at[slot, k], j), sem.at[slot]).start(priority=k)
            return carry

        lax.fori_loop(0, rows, issue, 0, unroll=DMA_UNROLL)

    @pl.when(i == 0)
    def _():
        issue_tile(0, 0)

    @pl.when(i + 1 < n_steps)
    def _():
        issue_tile(i + 1, (i + 1) % 2)

    cur = i % 2
    for k in range(2):
        pltpu.make_async_copy(ys_ref.at[pl.ds(0, rows * ROW_CHUNKS)], buf_ref.at[cur, k], sem.at[cur]).wait()
    w = w_ref[...]
    w1, w2 = w[:, 0:1], w[:, 1:2]
    for c in range(ROW_CHUNKS):
        cs = slice(c * LANES, (c + 1) * LANES)
        y1 = _load_row_tiles_chunk(buf_ref.at[cur, 0], c)
        y2 = _load_row_tiles_chunk(buf_ref.at[cur, 1], c)
        y_ref[:, cs] = x_ref[:, cs] + w1 * y1 + w2 * y2
    y = y_ref[...]
    if final:
        y = y * lax.rsqrt(jnp.mean(y * y, axis=-1, keepdims=True) + NORM_EPS) * g_ref[...]
    o_ref[...] = y


def _combine(slots, x1, wcol, g, ys, final):
    n_tok = x1.shape[0]
    rows = ROW_TILE
    return pl.pallas_call(
        functools.partial(_combine_kernel, n_tok=n_tok, final=final),
        grid_spec=pltpu.PrefetchScalarGridSpec(
            num_scalar_prefetch=1,
            grid=(n_tok // rows,),
            in_specs=[pl.BlockSpec((rows, D_MODEL), lambda i, s: (i, 0)),
                      pl.BlockSpec((rows, LANES), lambda i, s: (i, 0)),
                      pl.BlockSpec((1, D_MODEL), lambda i, s: (0, 0)),
                      pl.BlockSpec(memory_space=pl.ANY)],
            out_specs=pl.BlockSpec((rows, D_MODEL), lambda i, s: (i, 0)),
            scratch_shapes=[pltpu.VMEM((2, 2, rows * ROW_CHUNKS, LANES), F32),
                            pltpu.VMEM((rows, D_MODEL), F32),
                            pltpu.SemaphoreType.DMA((2,))],
        ),
        out_shape=jax.ShapeDtypeStruct((n_tok, D_MODEL), F32),
        compiler_params=_cparams("arbitrary"),
        name="combine",
    )(slots, x1, wcol, g, ys)


def _experts_kernel(tile_e_ref, n_used_ref, xs_ref, wg_ref, wu_ref, wd_ref, ys_ref, x_ref):
    del tile_e_ref

    @pl.when(pl.program_id(0) < n_used_ref[0])
    def _():
        for c in range(ROW_CHUNKS):
            x_ref[:, c * LANES:(c + 1) * LANES] = _load_row_tiles_chunk(xs_ref, c).astype(BF16)
        x = x_ref[...]
        a = jnp.dot(x, wg_ref[...].astype(BF16), preferred_element_type=F32)
        u = jnp.dot(x, wu_ref[...].astype(BF16), preferred_element_type=F32)
        z = (a * jax.nn.sigmoid(a)) * u
        _store_row_tiles(ys_ref, jnp.dot(z.astype(BF16), wd_ref[...].astype(BF16), preferred_element_type=F32))

    @pl.when(pl.program_id(0) >= n_used_ref[0])
    def _():
        ys_ref[...] = jnp.zeros_like(ys_ref)


def _experts(tile_e, n_used, xs, wg, wu, wd, layer):
    n_slots = xs.shape[0] // ROW_CHUNKS
    te = EXP_TILE

    def row_map(i, tile_e, n_used):
        return (jnp.minimum(i, n_used[0] - 1), 0)

    def out_map(i, tile_e, n_used):
        return (i, 0)

    def w_map(i, tile_e, n_used):
        return (layer, tile_e[i], 0, 0)

    return pl.pallas_call(
        _experts_kernel,
        grid_spec=pltpu.PrefetchScalarGridSpec(
            num_scalar_prefetch=2,
            grid=(n_slots // te,),
            in_specs=[pl.BlockSpec((te * ROW_CHUNKS, LANES), row_map),
                      pl.BlockSpec((None, None, D_MODEL, D_EXPERT), w_map),
                      pl.BlockSpec((None, None, D_MODEL, D_EXPERT), w_map),
                      pl.BlockSpec((None, None, D_EXPERT, D_MODEL), w_map)],
            out_specs=pl.BlockSpec((te * ROW_CHUNKS, LANES), out_map),
            scratch_shapes=[pltpu.VMEM((te, D_MODEL), BF16)],
        ),
        out_shape=jax.ShapeDtypeStruct((n_slots * ROW_CHUNKS, LANES), F32),
        compiler_params=_cparams("arbitrary"),
        name="experts",
    )(tile_e, n_used, xs, wg, wu, wd)


def kernel(x, attn_norm_g, w_in, a_sink, w_branch_a, w_branch_b, w_out, ffn_norm_g,
           w_router_group, b_router_group, w_router_expert, b_router_expert,
           w_exp_gate, w_exp_up, w_exp_down, final_norm_g):
    batch, seq, d_model = x.shape
    depth = w_in.shape[0]
    n_tok = batch * seq
    assert d_model == D_MODEL and w_in.shape[2] == D_IN
    assert seq % (16 * B_SUB) == 0 and seq % TOK_TILE == 0 and n_tok % DISPATCH_TILE == 0
    assert n_tok < (1 << RANK_BITS)

    cos_t, sin_t = _rope_tables(seq)
    tables = (cos_t, sin_t,
              _residue_order(cos_t, 4, TOK_TILE), _residue_order(sin_t, 4, TOK_TILE),
              _residue_order(cos_t, 16, TOK_TILE), _residue_order(sin_t, 16, TOK_TILE))

    n_slots = 2 * n_tok + N_EXPERTS * EXP_TILE
    n_tiles = n_slots // EXP_TILE
    x2d = x.reshape(n_tok, D_MODEL)

    for l in range(depth):
        nat, gates, grp1, grp2 = _in_proj(x2d, attn_norm_g[l][None, :], w_in[l].astype(BF16), tables, batch, seq)
        nat3d = nat.reshape(batch, seq, NAT_COLS)
        ya = _attn_a(nat3d, a_sink[l]).reshape(n_tok, A_Q_DIM)
        o0, l0 = _attn_b(nat3d, NAT_B // B_DIM, "attn_b1")
        o1, l1 = _attn_b(grp1.reshape(batch * 4, seq // 4, GRP_COLS), 0, "attn_b4")
        o2, l2 = _attn_b(grp2.reshape(batch * 16, seq // 16, GRP_COLS), 0, "attn_b16")

        wr = jnp.zeros((D_MODEL, LANES), F32)
        wr = wr.at[:, 0:N_EXPERTS].set(w_router_expert[l]).at[:, N_EXPERTS:N_EXPERTS + MOE_GROUPS].set(w_router_group[l])
        br = jnp.zeros((1, LANES), F32)
        br = br.at[0, 0:N_EXPERTS].set(b_router_expert[l]).at[0, N_EXPERTS:N_EXPERTS + MOE_GROUPS].set(b_router_group[l])
        wr_hi = wr.astype(BF16)
        wr_lo = (wr - wr_hi.astype(F32)).astype(BF16)
        wr_stack = jnp.concatenate([wr_hi, wr_lo], axis=0)

        x1, h2, code, wcol, cnt = _merge(
            x2d, ya, o0.reshape(n_tok, B_DIM), l0.reshape(n_tok, B_DIM),
            o1.reshape(batch, 4, seq // 4, B_DIM), l1.reshape(batch, 4, seq // 4, B_DIM),
            o2.reshape(batch, 16, seq // 16, B_DIM), l2.reshape(batch, 16, seq // 16, B_DIM),
            gates, w_branch_a[l].astype(BF16), w_branch_b[l].astype(BF16), w_out[l].astype(BF16),
            ffn_norm_g[l][None, :], wr_stack, br, batch, seq)

        counts = cnt[:, 0].astype(I32)
        padded = ((counts + EXP_TILE - 1) // EXP_TILE) * EXP_TILE
        ends = jnp.cumsum(padded)
        offs = ends - padded
        n_used = (ends[-1:] // EXP_TILE).astype(I32)
        tile_start = jnp.arange(n_tiles, dtype=I32) * EXP_TILE
        tile_e = jnp.minimum(jnp.sum((ends[None, :] <= tile_start[:, None]).astype(I32), axis=1), N_EXPERTS - 1)
        eid = code[0:2] >> RANK_BITS
        rank = code[0:2] & ((1 << RANK_BITS) - 1)
        expert_ids = jnp.arange(N_EXPERTS, dtype=I32)[:, None, None]
        slots = (rank + jnp.sum(jnp.where(eid[None] == expert_ids, offs[:, None, None], 0), axis=0)).reshape(-1)

        xs = _dispatch(slots, h2, jnp.zeros((n_slots * ROW_CHUNKS, LANES), F32) if l == 0 else xs)
        ys = _experts(tile_e, n_used, xs, w_exp_gate, w_exp_up, w_exp_down, l)
        x2d = _combine(slots, x1, wcol, final_norm_g[None, :], ys, final=(l == depth - 1))

    return x2d.reshape(batch, seq, D_MODEL)
```

```python
import functools

import jax
import jax.numpy as jnp
from jax import lax
from jax.experimental import pallas as pl
from jax.experimental.pallas import tpu as pltpu

F32 = jnp.float32
BF16 = jnp.bfloat16
I32 = jnp.int32

D_MODEL = 1024
HEAD_DIM = 64
HALF_HEAD = HEAD_DIM // 2
ROPE_THETA = 10000.0
NORM_EPS = 1e-6
NEG_INF = -1e30
LANES = 128

A_Q_HEADS = 8
A_KV_HEADS = 2
A_GROUP = A_Q_HEADS // A_KV_HEADS
A_HALF_WINDOW = 128
A_Q_DIM = A_Q_HEADS * HEAD_DIM
A_KV_DIM = A_KV_HEADS * HEAD_DIM

B_GROUPS = ((128, 1), (512, 4), (2048, 16))
B_HEADS = 4
B_DIM = B_HEADS * HEAD_DIM
B_HALF_WINDOW = 64

MOE_GROUPS = 4
EXPERTS_PER_GROUP = 8
N_EXPERTS = MOE_GROUPS * EXPERTS_PER_GROUP
D_EXPERT = 256

NAT_IN_COLS = A_Q_DIM + 2 * A_KV_DIM + 3 * B_DIM
NAT_B = A_Q_DIM
NAT_AK = NAT_B + 3 * B_DIM
NAT_AV = NAT_AK + 2 * A_KV_DIM
NAT_COLS = NAT_AV + A_KV_DIM
GRP_COLS = 3 * B_DIM
COL_G1 = NAT_IN_COLS
COL_G2 = COL_G1 + GRP_COLS
COL_GATE = COL_G2 + GRP_COLS
GATE_COLS = 2 * D_MODEL
D_IN = COL_GATE + GATE_COLS

TOK_TILE = 512
IN_TILE = 512
A_Q_TILE = 512
B_SUB = 128
EXP_TILE = 512
ROW_CHUNKS = D_MODEL // LANES
ROW_TILE = 256
DISPATCH_TILE = 1024
DMA_UNROLL = 8
RANK_BITS = 16
VMEM_LIMIT = 56 * 1024 * 1024


def _cparams(*sem):
    return pltpu.CompilerParams(dimension_semantics=sem, vmem_limit_bytes=VMEM_LIMIT)


def _rope_tables(seq_len):
    inv = 1.0 / (ROPE_THETA ** (jnp.arange(0, HEAD_DIM, 2, dtype=F32) / HEAD_DIM))
    ang = jnp.arange(seq_len, dtype=F32)[:, None] * inv[None, :]
    cos, sin = jnp.cos(ang), jnp.sin(ang)
    cos_t = jnp.concatenate([cos, cos, cos, cos], axis=-1)
    sin_t = jnp.concatenate([-sin, sin, -sin, sin], axis=-1)
    return cos_t, sin_t


def _rope(t, cos, sin_signed, first_half):
    partner = jnp.where(first_half, pltpu.roll(t, LANES - HALF_HEAD, 1), pltpu.roll(t, HALF_HEAD, 1))
    return t * cos + partner * sin_signed


Q_KIND, K_KIND, V_KIND = 0, 1, 2
_NAT_KINDS = ([Q_KIND] * 4 + [K_KIND] + [V_KIND] + [Q_KIND] * 2 + [K_KIND] * 2 + [V_KIND] * 2)
_GRP_KINDS = [Q_KIND] * 2 + [K_KIND] * 2 + [V_KIND] * 2


def _in_proj_kernel(x_ref, g_ref, w_ref, c1_ref, s1_ref,
                    nat_ref, gate_ref, g1_ref, g2_ref, hf_ref, hb_ref, tab_ref):
    tm = x_ref.shape[0]
    x = x_ref[...]
    h = x * lax.rsqrt(jnp.mean(x * x, axis=-1, keepdims=True) + NORM_EPS) * g_ref[...]
    n_chunks = D_MODEL // LANES
    for c in range(n_chunks):
        hf_ref[c] = h[:, c * LANES:(c + 1) * LANES]
    hb_ref[...] = h.astype(BF16)
    lane = lax.broadcasted_iota(I32, (1, LANES), 1)
    first_half = (lane % HEAD_DIM) < HALF_HEAD

    def project(col0, kinds, cos_ref, sin_ref, store):
        width = 512
        for c0 in range(0, len(kinds) * LANES, width):
            w = min(width, len(kinds) * LANES - c0)
            res = jnp.dot(hb_ref[...], w_ref[:, col0 + c0:col0 + c0 + w], preferred_element_type=F32)
            for j in range(w // LANES):
                kind = kinds[(c0 // LANES) + j]
                t = res[:, j * LANES:(j + 1) * LANES]
                if kind != V_KIND:
                    t = _rope(t, cos_ref[...], sin_ref[...], first_half)
                if kind == Q_KIND:
                    t = t * (HEAD_DIM ** -0.5)
                store(c0 + j * LANES, t)

    low_head = lane < HEAD_DIM

    def store_nat(c, t):
        if A_Q_DIM <= c < A_Q_DIM + A_KV_DIM:
            swapped = pltpu.roll(t, HEAD_DIM, 1)
            nat_ref[:, NAT_AK:NAT_AK + LANES] = jnp.where(low_head, t, swapped).astype(BF16)
            nat_ref[:, NAT_AK + LANES:NAT_AK + 2 * LANES] = jnp.where(low_head, swapped, t).astype(BF16)
        elif c < A_Q_DIM + 2 * A_KV_DIM:
            out = c if c < A_Q_DIM else NAT_AV
            nat_ref[:, out:out + LANES] = t.astype(BF16)
        else:
            out = c - 2 * A_KV_DIM
            nat_ref[:, out:out + LANES] = t.astype(BF16)

    project(0, _NAT_KINDS, c1_ref, s1_ref, store_nat)

    for c0 in range(0, GATE_COLS, 512):
        res = jnp.dot(hb_ref[...], w_ref[:, COL_GATE + c0:COL_GATE + c0 + 512], preferred_element_type=F32)
        gate_ref[:, c0:c0 + 512] = jax.nn.sigmoid(res).astype(BF16)

    for dil, col0, out_ref in ((4, COL_G1, g1_ref), (16, COL_G2, g2_ref)):
        n = tm // dil
        for r in range(dil):
            rows = slice(r * n, (r + 1) * n)
            for c in range(n_chunks):
                hb_ref[rows, c * LANES:(c + 1) * LANES] = hf_ref[c, pl.ds(r, n, stride=dil), :].astype(BF16)
            tab_ref[0, rows, :] = c1_ref[pl.ds(r, n, stride=dil), :]
            tab_ref[1, rows, :] = s1_ref[pl.ds(r, n, stride=dil), :]

        def store_grp(c, t, out_ref=out_ref, dil=dil, n=n):
            v = t.astype(BF16)
            for r in range(dil):
                out_ref[r, :, c:c + LANES] = v[r * n:(r + 1) * n]

        project(col0, _GRP_KINDS, tab_ref.at[0], tab_ref.at[1], store_grp)


def _in_proj(x2d, g, w_bf16, tables, batch, seq):
    tm = IN_TILE
    tiles_per_seq = seq // tm
    n_tok = batch * seq
    cos_t, sin_t = tables
    tab_spec = pl.BlockSpec((tm, LANES), lambda i: (i % tiles_per_seq, 0))
    return pl.pallas_call(
        _in_proj_kernel,
        grid=(n_tok // tm,),
        in_specs=[
            pl.BlockSpec((tm, D_MODEL), lambda i: (i, 0)),
            pl.BlockSpec((1, D_MODEL), lambda i: (0, 0)),
            pl.BlockSpec((D_MODEL, D_IN), lambda i: (0, 0), pipeline_mode=pl.Buffered(1)),
            tab_spec, tab_spec,
        ],
        out_specs=[
            pl.BlockSpec((tm, NAT_COLS), lambda i: (i, 0)),
            pl.BlockSpec((tm, GATE_COLS), lambda i: (i, 0)),
            pl.BlockSpec((None, 4, tm // 4, GRP_COLS), lambda i: (i // tiles_per_seq, 0, i % tiles_per_seq, 0)),
            pl.BlockSpec((None, 16, tm // 16, GRP_COLS), lambda i: (i // tiles_per_seq, 0, i % tiles_per_seq, 0)),
        ],
        out_shape=[
            jax.ShapeDtypeStruct((n_tok, NAT_COLS), BF16),
            jax.ShapeDtypeStruct((n_tok, GATE_COLS), BF16),
            jax.ShapeDtypeStruct((batch, 4, seq // 4, GRP_COLS), BF16),
            jax.ShapeDtypeStruct((batch, 16, seq // 16, GRP_COLS), BF16),
        ],
        scratch_shapes=[
            pltpu.VMEM((D_MODEL // LANES, tm, LANES), F32),
            pltpu.VMEM((tm, D_MODEL), BF16),
            pltpu.VMEM((2, tm, LANES), F32),
        ],
        compiler_params=_cparams("parallel"),
        name="in_proj",
    )(x2d, g, w_bf16, cos_t, sin_t)


def _masked_heads(q_pair, low_head, high_head):
    zero = jnp.zeros_like(q_pair)
    return [jnp.where(low_head, q_pair, zero), jnp.where(high_head, q_pair, zero)]


def _attn_a_kernel(sink_ref, q_ref, kp_ref, km_ref, kn_ref, vp_ref, vm_ref, vn_ref, o_ref, k_ref, vt_ref, *, seq):
    tq = q_ref.shape[0]
    hw = A_HALF_WINDOW
    n_sub = tq // hw
    i = pl.program_id(1)
    last_blk = seq // hw - 1
    k_ref[0:hw, :] = kp_ref[...]
    k_ref[hw:hw + tq, :] = km_ref[...]
    k_ref[hw + tq:tq + 2 * hw, :] = kn_ref[...]
    for r0, src in ((0, vp_ref), (hw, vm_ref), (hw + tq, vn_ref)):
        vt_ref[:, r0:r0 + src.shape[0]] = src[...].astype(F32).T.astype(BF16)
    grp_cols = A_GROUP * hw
    key = lax.broadcasted_iota(I32, (hw, grp_cols), 0)
    qry = lax.broadcasted_iota(I32, (hw, grp_cols), 1) % hw
    low_head = lax.broadcasted_iota(I32, (1, LANES), 1) < HEAD_DIM
    high_head = jnp.logical_not(low_head)
    for sb in range(n_sub):
        r0 = sb * hw
        blk = i * n_sub + sb
        mask_p = (key >= qry) if sb > 0 else (key >= qry + jnp.where(blk > 0, 0, hw))
        mask_n = (key <= qry) if sb < n_sub - 1 else (key <= qry - jnp.where(blk < last_blk, 0, hw))
        out_t = []
        for g in range(A_KV_HEADS):
            heads = range(g * A_GROUP, (g + 1) * A_GROUP)
            q_parts = []
            for c in range(g * A_GROUP // 2, (g + 1) * A_GROUP // 2):
                q_parts += _masked_heads(q_ref[r0:r0 + hw, c * LANES:(c + 1) * LANES], low_head, high_head)
            q = jnp.concatenate(q_parts, axis=0)
            sink = jnp.concatenate([jnp.full((1, hw), sink_ref[h], F32) for h in heads], axis=1)
            k = k_ref[r0:r0 + 3 * hw, g * LANES:(g + 1) * LANES]
            s = lax.dot_general(k, q, (((1,), (1,)), ((), ())), preferred_element_type=F32)
            sp = jnp.where(mask_p, s[0:hw], NEG_INF)
            so = s[hw:2 * hw]
            sn = jnp.where(mask_n, s[2 * hw:3 * hw], NEG_INF)
            m = jnp.max(jnp.maximum(jnp.maximum(sp, so), sn), axis=0, keepdims=True)
            m = jnp.maximum(m, sink)
            pp, po, pn = jnp.exp(sp - m), jnp.exp(so - m), jnp.exp(sn - m)
            denom = jnp.sum(pp + po + pn, axis=0, keepdims=True) + jnp.exp(sink - m)
            p = jnp.concatenate([pp, po, pn], axis=0).astype(BF16)
            vt = vt_ref[g * HEAD_DIM:(g + 1) * HEAD_DIM, r0:r0 + 3 * hw]
            o = jnp.dot(vt, p, preferred_element_type=F32) * (1.0 / denom)
            out_t += [o[:, j * hw:(j + 1) * hw] for j in range(A_GROUP)]
        o_ref[r0:r0 + hw, :] = jnp.concatenate(out_t, axis=0).T.astype(BF16)


def _attn_a(nat3d, sink):
    batch, seq, _ = nat3d.shape
    tq = A_Q_TILE
    hw = A_HALF_WINDOW
    per = tq // hw
    n_hw = seq // hw
    k_cols, v_cols = 2 * A_KV_DIM, A_KV_DIM
    k_blk, v_blk = NAT_AK // k_cols, NAT_AV // v_cols

    def prev_spec(cols, blk):
        return pl.BlockSpec((None, hw, cols), lambda b, i: (b, jnp.maximum(i * per - 1, 0), blk))

    def main_spec(cols, blk):
        return pl.BlockSpec((None, tq, cols), lambda b, i: (b, i, blk))

    def next_spec(cols, blk):
        return pl.BlockSpec((None, hw, cols), lambda b, i: (b, jnp.minimum((i + 1) * per, n_hw - 1), blk))

    return pl.pallas_call(
        functools.partial(_attn_a_kernel, seq=seq),
        grid=(batch, seq // tq),
        in_specs=[
            pl.BlockSpec(memory_space=pltpu.SMEM),
            main_spec(A_Q_DIM, 0),
            prev_spec(k_cols, k_blk), main_spec(k_cols, k_blk), next_spec(k_cols, k_blk),
            prev_spec(v_cols, v_blk), main_spec(v_cols, v_blk), next_spec(v_cols, v_blk),
        ],
        out_specs=pl.BlockSpec((None, tq, A_Q_DIM), lambda b, i: (b, i, 0)),
        out_shape=jax.ShapeDtypeStruct((batch, seq, A_Q_DIM), BF16),
        scratch_shapes=[pltpu.VMEM((tq + 2 * hw, k_cols), BF16), pltpu.VMEM((v_cols, tq + 2 * hw), BF16)],
        compiler_params=_cparams("parallel", "parallel"),
        name="attn_a",
    )(sink, nat3d, nat3d, nat3d, nat3d, nat3d, nat3d, nat3d)


def _attn_b_kernel(q_ref, kp_ref, km_ref, kn_ref, vp_ref, vm_ref, vn_ref, o_ref, lse_ref,
                   k_ref, vt_ref, *, sub_len):
    tq = q_ref.shape[0]
    hw = B_HALF_WINDOW
    t0 = pl.program_id(1) * tq
    k_ref[0:hw, :] = kp_ref[...]
    k_ref[hw:hw + tq, :] = km_ref[...]
    k_ref[hw + tq:tq + 2 * hw, :] = kn_ref[...]
    for r0, src in ((0, vp_ref), (hw, vm_ref), (hw + tq, vn_ref)):
        vt_ref[:, r0:r0 + src.shape[0]] = src[...].astype(F32).T.astype(BF16)
    kw = B_SUB + 2 * hw
    n_sub = tq // B_SUB
    all_cols = B_HEADS * B_SUB
    key = lax.broadcasted_iota(I32, (kw, all_cols), 0)
    qry = lax.broadcasted_iota(I32, (kw, all_cols), 1) % B_SUB
    in_band = jnp.abs(key - hw - qry) <= hw
    low_head = lax.broadcasted_iota(I32, (1, LANES), 1) < HEAD_DIM
    high_head = jnp.logical_not(low_head)
    for sb in range(n_sub):
        r0 = sb * B_SUB
        valid = in_band
        if sb == 0:
            valid = valid & (t0 - hw + key >= 0)
        if sb == n_sub - 1:
            valid = valid & (t0 + r0 - hw + key < sub_len)
        s_parts = []
        for c in range(B_HEADS // 2):
            cs = slice(c * LANES, (c + 1) * LANES)
            q = jnp.concatenate(_masked_heads(q_ref[r0:r0 + B_SUB, cs], low_head, high_head), axis=0)
            s_parts.append(lax.dot_general(k_ref[r0:r0 + kw, cs], q, (((1,), (1,)), ((), ())),
                                           preferred_element_type=F32))
        s = jnp.where(valid, jnp.concatenate(s_parts, axis=1), NEG_INF)
        m = jnp.max(s, axis=0, keepdims=True)
        p = jnp.exp(s - m)
        denom = jnp.sum(p, axis=0, keepdims=True)
        p = p.astype(BF16)
        inv = 1.0 / denom
        lse = m + jnp.log(denom)
        out_t, lse_t = [], []
        for h in range(B_HEADS):
            qs = slice(h * B_SUB, (h + 1) * B_SUB)
            vt = vt_ref[h * HEAD_DIM:(h + 1) * HEAD_DIM, r0:r0 + kw]
            out_t.append(jnp.dot(vt, p[:, qs], preferred_element_type=F32) * inv[:, qs])
            lse_t.append(jnp.broadcast_to(lse[:, qs], (HEAD_DIM, B_SUB)))
        o_ref[r0:r0 + B_SUB, :] = jnp.concatenate(out_t, axis=0).T
        lse_ref[r0:r0 + B_SUB, :] = jnp.concatenate(lse_t, axis=0).T


def _attn_b(arr3d, q_blk, name):
    n_sub, sub_len, _ = arr3d.shape
    tq = min(512, sub_len)
    hw = B_HALF_WINDOW
    per = tq // hw
    n_hw = sub_len // hw

    def main_spec(c):
        return pl.BlockSpec((None, tq, B_DIM), lambda g, i: (g, i, c))

    def prev_spec(c):
        return pl.BlockSpec((None, hw, B_DIM), lambda g, i: (g, jnp.maximum(i * per - 1, 0), c))

    def next_spec(c):
        return pl.BlockSpec((None, hw, B_DIM), lambda g, i: (g, jnp.minimum((i + 1) * per, n_hw - 1), c))

    out_spec = pl.BlockSpec((None, tq, B_DIM), lambda g, i: (g, i, 0))
    return pl.pallas_call(
        functools.partial(_attn_b_kernel, sub_len=sub_len),
        grid=(n_sub, sub_len // tq),
        in_specs=[main_spec(q_blk),
                  prev_spec(q_blk + 1), main_spec(q_blk + 1), next_spec(q_blk + 1),
                  prev_spec(q_blk + 2), main_spec(q_blk + 2), next_spec(q_blk + 2)],
        out_specs=[out_spec, out_spec],
        out_shape=[jax.ShapeDtypeStruct((n_sub, sub_len, B_DIM), F32)] * 2,
        scratch_shapes=[pltpu.VMEM((tq + 2 * hw, B_DIM), BF16), pltpu.VMEM((B_DIM, tq + 2 * hw), BF16)],
        compiler_params=_cparams("parallel", "parallel"),
        name=name,
    )(arr3d, arr3d, arr3d, arr3d, arr3d, arr3d, arr3d)


def _merge_kernel(x_ref, ya_ref, o0_ref, l0_ref, o1_ref, l1_ref, o2_ref, l2_ref, gate_ref,
                  wa_ref, wb_ref, wo_ref, g2_ref, wr_ref, br_ref,
                  x1_ref, h2_ref, code_ref, wcol_ref, cnt_ref,
                  so1_ref, sl1_ref, so2_ref, sl2_ref, yb_ref, run_ref, earlier_ref,
                  wab_ref, wbb_ref, wob_ref):
    tm = x_ref.shape[0]

    @pl.when(pl.program_id(0) == 0)
    def _():
        run_ref[...] = jnp.zeros_like(run_ref)
        wab_ref[...] = wa_ref[...].astype(BF16)
        wbb_ref[...] = wb_ref[...].astype(BF16)
        wob_ref[...] = wo_ref[...].astype(BF16)
        row = lax.broadcasted_iota(I32, (tm, tm), 0)
        col = lax.broadcasted_iota(I32, (tm, tm), 1)
        earlier_ref[...] = (row < col).astype(BF16)

    for dil, src_o, src_l, dst_o, dst_l in ((4, o1_ref, l1_ref, so1_ref, sl1_ref),
                                            (16, o2_ref, l2_ref, so2_ref, sl2_ref)):
        n = tm // dil
        for r in range(dil):
            for c in range(B_DIM // LANES):
                cs = slice(c * LANES, (c + 1) * LANES)
                dst_o[c, pl.ds(r, n, stride=dil), :] = src_o[r, :, cs]
                dst_l[c, pl.ds(r, n, stride=dil), :] = src_l[r, :, cs]

    for c in range(B_DIM // LANES):
        cs = slice(c * LANES, (c + 1) * LANES)
        l0, l1, l2 = l0_ref[:, cs], sl1_ref[c], sl2_ref[c]
        m = jnp.maximum(jnp.maximum(l0, l1), l2)
        e0, e1, e2 = jnp.exp(l0 - m), jnp.exp(l1 - m), jnp.exp(l2 - m)
        yb = (e0 * o0_ref[:, cs] + e1 * so1_ref[c] + e2 * so2_ref[c]) / (e0 + e1 + e2)
        yb_ref[:, cs] = yb.astype(BF16)

    ya_p = jnp.dot(ya_ref[...], wab_ref[...], preferred_element_type=F32)
    yb_p = jnp.dot(yb_ref[...], wbb_ref[...], preferred_element_type=F32)
    merged = gate_ref[:, 0:D_MODEL].astype(F32) * ya_p + gate_ref[:, D_MODEL:GATE_COLS].astype(F32) * yb_p
    x1 = x_ref[...] + jnp.dot(merged.astype(BF16), wob_ref[...], preferred_element_type=F32)
    x1_ref[...] = x1

    h2 = x1 * lax.rsqrt(jnp.mean(x1 * x1, axis=-1, keepdims=True) + NORM_EPS) * g2_ref[...]
    _store_row_tiles(h2_ref, h2)

    h_hi = h2.astype(BF16)
    h_lo = (h2 - h_hi.astype(F32)).astype(BF16)
    w_hi, w_lo = wr_ref[0:D_MODEL], wr_ref[D_MODEL:2 * D_MODEL]
    logits = (jnp.dot(h_hi, w_hi, preferred_element_type=F32)
              + jnp.dot(h_lo, w_hi, preferred_element_type=F32)
              + jnp.dot(h_hi, w_lo, preferred_element_type=F32)) + br_ref[...]

    logits_t = logits.T
    sub = lax.broadcasted_iota(I32, (EXPERTS_PER_GROUP, tm), 0).astype(F32)
    none = float(EXPERTS_PER_GROUP)
    gl = jnp.where(sub < MOE_GROUPS, logits_t[N_EXPERTS:N_EXPERTS + EXPERTS_PER_GROUP], -jnp.inf)
    gmax = jnp.max(gl, axis=0, keepdims=True)
    gidx = jnp.min(jnp.where(gl == gmax, sub, none), axis=0, keepdims=True)
    gw = 1.0 / jnp.sum(jnp.exp(gl - gmax), axis=0, keepdims=True)
    el = logits_t[0:EXPERTS_PER_GROUP]
    for g in range(1, MOE_GROUPS):
        el = jnp.where(gidx == g, logits_t[g * EXPERTS_PER_GROUP:(g + 1) * EXPERTS_PER_GROUP], el)
    v1 = jnp.max(el, axis=0, keepdims=True)
    i1 = jnp.min(jnp.where(el == v1, sub, none), axis=0, keepdims=True)
    el2 = jnp.where(sub == i1, -jnp.inf, el)
    v2 = jnp.max(el2, axis=0, keepdims=True)
    i2 = jnp.min(jnp.where(el2 == v2, sub, none), axis=0, keepdims=True)
    t = jnp.exp(v2 - v1)
    w1 = gw / (1.0 + t)
    w2 = gw * t / (1.0 + t)
    e1 = gidx * EXPERTS_PER_GROUP + i1
    e2 = gidx * EXPERTS_PER_GROUP + i2

    expert = lax.broadcasted_iota(I32, (N_EXPERTS, tm), 0).astype(F32)
    oh1 = (expert == e1).astype(F32)
    oh2 = (expert == e2).astype(F32)
    oh = oh1 + oh2
    run = run_ref[...]
    before = (jnp.dot(oh.astype(BF16), earlier_ref[...], preferred_element_type=F32)
              + jnp.concatenate([run] * (tm // LANES), axis=1))
    rank1 = jnp.sum(before * oh1, axis=0, keepdims=True)
    rank2 = jnp.sum(before * oh2, axis=0, keepdims=True)
    run_ref[...] = run + jnp.sum(oh, axis=1, keepdims=True)
    cnt_ref[...] = run_ref[...]

    scale = float(1 << RANK_BITS)
    zeros = jnp.zeros((6, tm), F32)
    code_ref[...] = jnp.concatenate([e1 * scale + rank1, e2 * scale + rank2, zeros], axis=0).astype(I32)
    w_rows = jnp.concatenate([w1, w2, jnp.zeros((LANES - 2, tm), F32)], axis=0)
    wcol_ref[...] = w_rows.T


def _merge(x2d, ya, o0, l0, o1, l1, o2, l2, gates, wa, wb, wo, g2, wr, br, batch, seq):
    tm = TOK_TILE
    tps = seq // tm
    n_tok = batch * seq

    def tok(c):
        return pl.BlockSpec((tm, c), lambda i: (i, 0))

    def full(a):
        return pl.BlockSpec(a.shape, lambda i: (0,) * a.ndim)

    def res_spec(d):
        return pl.BlockSpec((None, d, tm // d, B_DIM), lambda i: (i // tps, 0, i % tps, 0))

    return pl.pallas_call(
        _merge_kernel,
        grid=(n_tok // tm,),
        in_specs=[tok(D_MODEL), tok(A_Q_DIM), tok(B_DIM), tok(B_DIM),
                  res_spec(4), res_spec(4), res_spec(16), res_spec(16), tok(GATE_COLS),
                  full(wa), full(wb), full(wo), full(g2), full(wr), full(br)],
        out_specs=[tok(D_MODEL), pl.BlockSpec((tm * ROW_CHUNKS, LANES), lambda i: (i, 0)),
                   pl.BlockSpec((8, tm), lambda i: (0, i)),
                   tok(LANES),
                   pl.BlockSpec((N_EXPERTS, LANES), lambda i: (0, 0))],
        out_shape=[jax.ShapeDtypeStruct((n_tok, D_MODEL), F32),
                   jax.ShapeDtypeStruct((n_tok * ROW_CHUNKS, LANES), F32),
                   jax.ShapeDtypeStruct((8, n_tok), I32),
                   jax.ShapeDtypeStruct((n_tok, LANES), F32),
                   jax.ShapeDtypeStruct((N_EXPERTS, LANES), F32)],
        scratch_shapes=([pltpu.VMEM((B_DIM // LANES, tm, LANES), F32)] * 4
                        + [pltpu.VMEM((tm, B_DIM), BF16), pltpu.VMEM((N_EXPERTS, LANES), F32),
                           pltpu.VMEM((tm, tm), BF16),
                           pltpu.VMEM(wa.shape, BF16), pltpu.VMEM(wb.shape, BF16), pltpu.VMEM(wo.shape, BF16)]),
        compiler_params=_cparams("arbitrary"),
        name="merge_route",
    )(x2d, ya, o0, l0, o1, l1, o2, l2, gates, wa, wb, wo, g2, wr, br)


def _row_tile(ref, t):
    return ref.at[pl.ds(pl.multiple_of(t * ROW_CHUNKS, ROW_CHUNKS), ROW_CHUNKS)]


def _store_row_tiles(ref, val):
    rows = val.shape[0]
    for c in range(ROW_CHUNKS):
        ref[pl.ds(c, rows, stride=ROW_CHUNKS), :] = val[:, c * LANES:(c + 1) * LANES]


def _load_row_tiles_chunk(ref, c):
    return ref[pl.ds(c, ref.shape[0] // ROW_CHUNKS, stride=ROW_CHUNKS), :]


def _dispatch_kernel(slot_ref, h_ref, xs_in_ref, xs_ref, sem, *, n_tok):
    del xs_in_ref
    i = pl.program_id(0)
    rows = h_ref.shape[0] // ROW_CHUNKS

    def issue(j, carry):
        t = i * rows + j
        for k in range(2):
            pltpu.make_async_copy(_row_tile(h_ref, j), _row_tile(xs_ref, slot_ref[k * n_tok + t]), sem).start(priority=k)
        return carry

    lax.fori_loop(0, rows, issue, 0, unroll=DMA_UNROLL)
    for _ in range(2):
        pltpu.make_async_copy(h_ref, xs_ref.at[pl.ds(0, rows * ROW_CHUNKS)], sem).wait()


def _dispatch(slots, h2, xs_init):
    n_tok = h2.shape[0] // ROW_CHUNKS
    rows = DISPATCH_TILE
    return pl.pallas_call(
        functools.partial(_dispatch_kernel, n_tok=n_tok),
        grid_spec=pltpu.PrefetchScalarGridSpec(
            num_scalar_prefetch=1,
            grid=(n_tok // rows,),
            in_specs=[pl.BlockSpec((rows * ROW_CHUNKS, LANES), lambda i, s: (i, 0)),
                      pl.BlockSpec(memory_space=pl.ANY)],
            out_specs=pl.BlockSpec(memory_space=pl.ANY),
            scratch_shapes=[pltpu.SemaphoreType.DMA(())],
        ),
        out_shape=jax.ShapeDtypeStruct(xs_init.shape, xs_init.dtype),
        input_output_aliases={2: 0},
        compiler_params=_cparams("arbitrary"),
        name="dispatch",
    )(slots, h2, xs_init)


def _combine_kernel(slot_ref, x_ref, w_ref, g_ref, ys_ref, o_ref, buf_ref, y_ref, sem, *, n_tok, final):
    i = pl.program_id(0)
    n_steps = pl.num_programs(0)
    rows = x_ref.shape[0]

    def issue_tile(tile, slot):
        def issue(j, carry):
            t = tile * rows + j
            for k in range(2):
                pltpu.make_async_copy(_row_tile(ys_ref, slot_ref[k * n_tok + t]),
                                      _row_tile(buf_ref.at[slot, k], j), sem.at[slot]).start(priority=k)
            return carry

        lax.fori_loop(0, rows, issue, 0, unroll=DMA_UNROLL)

    @pl.when(i == 0)
    def _():
        issue_tile(0, 0)

    @pl.when(i + 1 < n_steps)
    def _():
        issue_tile(i + 1, (i + 1) % 2)

    cur = i % 2
    for k in range(2):
        pltpu.make_async_copy(ys_ref.at[pl.ds(0, rows * ROW_CHUNKS)], buf_ref.at[cur, k], sem.at[cur]).wait()
    w = w_ref[...]
    w1, w2 = w[:, 0:1], w[:, 1:2]
    for c in range(ROW_CHUNKS):
        cs = slice(c * LANES, (c + 1) * LANES)
        y1 = _load_row_tiles_chunk(buf_ref.at[cur, 0], c)
        y2 = _load_row_tiles_chunk(buf_ref.at[cur, 1], c)
        y_ref[:, cs] = x_ref[:, cs] + w1 * y1 + w2 * y2
    y = y_ref[...]
    if final:
        y = y * lax.rsqrt(jnp.mean(y * y, axis=-1, keepdims=True) + NORM_EPS) * g_ref[...]
    o_ref[...] = y


def _combine(slots, x1, wcol, g, ys, final):
    n_tok = x1.shape[0]
    rows = ROW_TILE
    return pl.pallas_call(
        functools.partial(_combine_kernel, n_tok=n_tok, final=final),
        grid_spec=pltpu.PrefetchScalarGridSpec(
            num_scalar_prefetch=1,
            grid=(n_tok // rows,),
            in_specs=[pl.BlockSpec((rows, D_MODEL), lambda i, s: (i, 0)),
                      pl.BlockSpec((rows, LANES), lambda i, s: (i, 0)),
                      pl.BlockSpec((1, D_MODEL), lambda i, s: (0, 0)),
                      pl.BlockSpec(memory_space=pl.ANY)],
            out_specs=pl.BlockSpec((rows, D_MODEL), lambda i, s: (i, 0)),
            scratch_shapes=[pltpu.VMEM((2, 2, rows * ROW_CHUNKS, LANES), F32),
                            pltpu.VMEM((rows, D_MODEL), F32),
                            pltpu.SemaphoreType.DMA((2,))],
        ),
        out_shape=jax.ShapeDtypeStruct((n_tok, D_MODEL), F32),
        compiler_params=_cparams("arbitrary"),
        name="combine",
    )(slots, x1, wcol, g, ys)


def _experts_kernel(tile_e_ref, n_used_ref, xs_ref, wg_ref, wu_ref, wd_ref, ys_ref, x_ref):
    del tile_e_ref

    @pl.when(pl.program_id(0) < n_used_ref[0])
    def _():
        for c in range(ROW_CHUNKS):
            x_ref[:, c * LANES:(c + 1) * LANES] = _load_row_tiles_chunk(xs_ref, c).astype(BF16)
        x = x_ref[...]
        a = jnp.dot(x, wg_ref[...].astype(BF16), preferred_element_type=F32)
        u = jnp.dot(x, wu_ref[...].astype(BF16), preferred_element_type=F32)
        z = (a * jax.nn.sigmoid(a)) * u
        _store_row_tiles(ys_ref, jnp.dot(z.astype(BF16), wd_ref[...].astype(BF16), preferred_element_type=F32))

    @pl.when(pl.program_id(0) >= n_used_ref[0])
    def _():
        ys_ref[...] = jnp.zeros_like(ys_ref)


def _experts(tile_e, n_used, xs, wg, wu, wd, layer):
    n_slots = xs.shape[0] // ROW_CHUNKS
    te = EXP_TILE

    def row_map(i, tile_e, n_used):
        return (jnp.minimum(i, n_used[0] - 1), 0)

    def out_map(i, tile_e, n_used):
        return (i, 0)

    def w_map(i, tile_e, n_used):
        return (layer, tile_e[i], 0, 0)

    return pl.pallas_call(
        _experts_kernel,
        grid_spec=pltpu.PrefetchScalarGridSpec(
            num_scalar_prefetch=2,
            grid=(n_slots // te,),
            in_specs=[pl.BlockSpec((te * ROW_CHUNKS, LANES), row_map),
                      pl.BlockSpec((None, None, D_MODEL, D_EXPERT), w_map),
                      pl.BlockSpec((None, None, D_MODEL, D_EXPERT), w_map),
                      pl.BlockSpec((None, None, D_EXPERT, D_MODEL), w_map)],
            out_specs=pl.BlockSpec((te * ROW_CHUNKS, LANES), out_map),
            scratch_shapes=[pltpu.VMEM((te, D_MODEL), BF16)],
        ),
        out_shape=jax.ShapeDtypeStruct((n_slots * ROW_CHUNKS, LANES), F32),
        compiler_params=_cparams("arbitrary"),
        name="experts",
    )(tile_e, n_used, xs, wg, wu, wd)


def kernel(x, attn_norm_g, w_in, a_sink, w_branch_a, w_branch_b, w_out, ffn_norm_g,
           w_router_group, b_router_group, w_router_expert, b_router_expert,
           w_exp_gate, w_exp_up, w_exp_down, final_norm_g):
    batch, seq, d_model = x.shape
    depth = w_in.shape[0]
    n_tok = batch * seq
    assert d_model == D_MODEL and w_in.shape[2] == D_IN
    assert seq % (16 * B_SUB) == 0 and seq % IN_TILE == 0 and n_tok % DISPATCH_TILE == 0
    assert n_tok < (1 << RANK_BITS)

    cos_t, sin_t = _rope_tables(seq)
    tables = (cos_t, sin_t)

    n_slots = 2 * n_tok + N_EXPERTS * EXP_TILE
    n_tiles = n_slots // EXP_TILE
    x2d = x.reshape(n_tok, D_MODEL)

    for l in range(depth):
        nat, gates, grp1, grp2 = _in_proj(x2d, attn_norm_g[l][None, :], w_in[l].astype(BF16), tables, batch, seq)
        nat3d = nat.reshape(batch, seq, NAT_COLS)
        ya = _attn_a(nat3d, a_sink[l]).reshape(n_tok, A_Q_DIM)
        o0, l0 = _attn_b(nat3d, NAT_B // B_DIM, "attn_b1")
        o1, l1 = _attn_b(grp1.reshape(batch * 4, seq // 4, GRP_COLS), 0, "attn_b4")
        o2, l2 = _attn_b(grp2.reshape(batch * 16, seq // 16, GRP_COLS), 0, "attn_b16")

        wr = jnp.zeros((D_MODEL, LANES), F32)
        wr = wr.at[:, 0:N_EXPERTS].set(w_router_expert[l]).at[:, N_EXPERTS:N_EXPERTS + MOE_GROUPS].set(w_router_group[l])
        br = jnp.zeros((1, LANES), F32)
        br = br.at[0, 0:N_EXPERTS].set(b_router_expert[l]).at[0, N_EXPERTS:N_EXPERTS + MOE_GROUPS].set(b_router_group[l])
        wr_hi = wr.astype(BF16)
        wr_lo = (wr - wr_hi.astype(F32)).astype(BF16)
        wr_stack = jnp.concatenate([wr_hi, wr_lo], axis=0)

        x1, h2, code, wcol, cnt = _merge(
            x2d, ya, o0.reshape(n_tok, B_DIM), l0.reshape(n_tok, B_DIM),
            o1.reshape(batch, 4, seq // 4, B_DIM), l1.reshape(batch, 4, seq // 4, B_DIM),
            o2.reshape(batch, 16, seq // 16, B_DIM), l2.reshape(batch, 16, seq // 16, B_DIM),
            gates, w_branch_a[l], w_branch_b[l], w_out[l],
            ffn_norm_g[l][None, :], wr_stack, br, batch, seq)

        counts = cnt[:, 0].astype(I32)
        padded = ((counts + EXP_TILE - 1) // EXP_TILE) * EXP_TILE
        ends = jnp.cumsum(padded)
        offs = ends - padded
        n_used = (ends[-1:] // EXP_TILE).astype(I32)
        tile_start = jnp.arange(n_tiles, dtype=I32) * EXP_TILE
        tile_e = jnp.minimum(jnp.sum((ends[None, :] <= tile_start[:, None]).astype(I32), axis=1), N_EXPERTS - 1)
        eid = code[0:2] >> RANK_BITS
        rank = code[0:2] & ((1 << RANK_BITS) - 1)
        expert_ids = jnp.arange(N_EXPERTS, dtype=I32)[:, None, None]
        slots = (rank + jnp.sum(jnp.where(eid[None] == expert_ids, offs[:, None, None], 0), axis=0)).reshape(-1)

        xs = _dispatch(slots, h2, jnp.zeros((n_slots * ROW_CHUNKS, LANES), F32) if l == 0 else xs)
        ys = _experts(tile_e, n_used, xs, w_exp_gate, w_exp_up, w_exp_down, l)
        x2d = _combine(slots, x1, wcol, final_norm_g[None, :], ys, final=(l == depth - 1))

    return x2d.reshape(batch, seq, D_MODEL)
```

```python
import functools

import jax
import jax.numpy as jnp
from jax import lax
from jax.experimental import pallas as pl
from jax.experimental.pallas import tpu as pltpu

F32 = jnp.float32
BF16 = jnp.bfloat16
I32 = jnp.int32

D_MODEL = 1024
HEAD_DIM = 64
HALF_HEAD = HEAD_DIM // 2
ROPE_THETA = 10000.0
NORM_EPS = 1e-6
NEG_INF = -1e30
LANES = 128

A_Q_HEADS = 8
A_KV_HEADS = 2
A_GROUP = A_Q_HEADS // A_KV_HEADS
A_HALF_WINDOW = 128
A_Q_DIM = A_Q_HEADS * HEAD_DIM
A_KV_DIM = A_KV_HEADS * HEAD_DIM

B_GROUPS = ((128, 1), (512, 4), (2048, 16))
B_HEADS = 4
B_DIM = B_HEADS * HEAD_DIM
B_HALF_WINDOW = 64

MOE_GROUPS = 4
EXPERTS_PER_GROUP = 8
N_EXPERTS = MOE_GROUPS * EXPERTS_PER_GROUP
D_EXPERT = 256

NAT_IN_COLS = A_Q_DIM + 2 * A_KV_DIM + 3 * B_DIM
NAT_B = A_Q_DIM
NAT_AK = NAT_B + 3 * B_DIM
NAT_AV = NAT_AK + 2 * A_KV_DIM
NAT_COLS = NAT_AV + A_KV_DIM
GRP_COLS = 3 * B_DIM
COL_G1 = NAT_IN_COLS
COL_G2 = COL_G1 + GRP_COLS
COL_GATE = COL_G2 + GRP_COLS
GATE_COLS = 2 * D_MODEL
D_IN = COL_GATE + GATE_COLS

TOK_TILE = 512
IN_TILE = 512
A_Q_TILE = 512
B_SUB = 128
EXP_TILE = 512
XS_SLOTS = 3
ROW_CHUNKS = D_MODEL // LANES
ROW_TILE = 256
DISPATCH_TILE = 1024
DMA_UNROLL = 8
RANK_BITS = 16
VMEM_LIMIT = 56 * 1024 * 1024


def _cparams(*sem):
    return pltpu.CompilerParams(dimension_semantics=sem, vmem_limit_bytes=VMEM_LIMIT)


def _rope_tables(seq_len):
    inv = 1.0 / (ROPE_THETA ** (jnp.arange(0, HEAD_DIM, 2, dtype=F32) / HEAD_DIM))
    ang = jnp.arange(seq_len, dtype=F32)[:, None] * inv[None, :]
    cos, sin = jnp.cos(ang), jnp.sin(ang)
    cos_t = jnp.concatenate([cos, cos, cos, cos], axis=-1)
    sin_t = jnp.concatenate([-sin, sin, -sin, sin], axis=-1)
    return cos_t, sin_t


def _residue_order(table, dilation, tile):
    s, c = table.shape
    return table.reshape(s // tile, tile // dilation, dilation, c).transpose(0, 2, 1, 3).reshape(s, c)


def _rope(t, cos, sin_signed, first_half):
    partner = jnp.where(first_half, pltpu.roll(t, LANES - HALF_HEAD, 1), pltpu.roll(t, HALF_HEAD, 1))
    return t * cos + partner * sin_signed


Q_KIND, K_KIND, V_KIND = 0, 1, 2
_NAT_KINDS = ([Q_KIND] * 4 + [K_KIND] + [V_KIND] + [Q_KIND] * 2 + [K_KIND] * 2 + [V_KIND] * 2)
_GRP_KINDS = [Q_KIND] * 2 + [K_KIND] * 2 + [V_KIND] * 2


def _in_proj_kernel(x_ref, g_ref, w_ref, c1_ref, s1_ref, c4_ref, s4_ref, c16_ref, s16_ref,
                    nat_ref, gate_ref, g1_ref, g2_ref, hf_ref, hb_ref, hd_ref):
    tm = x_ref.shape[0]
    x = x_ref[...]
    h = x * lax.rsqrt(jnp.mean(x * x, axis=-1, keepdims=True) + NORM_EPS) * g_ref[...]
    n_chunks = D_MODEL // LANES
    for c in range(n_chunks):
        hf_ref[c] = h[:, c * LANES:(c + 1) * LANES]
    hb_ref[...] = h.astype(BF16)
    lane = lax.broadcasted_iota(I32, (1, LANES), 1)
    first_half = (lane % HEAD_DIM) < HALF_HEAD

    def project(h_b, col0, kinds, cos_ref, sin_ref, store):
        width = 512
        for c0 in range(0, len(kinds) * LANES, width):
            w = min(width, len(kinds) * LANES - c0)
            res = jnp.dot(h_b, w_ref[:, col0 + c0:col0 + c0 + w], preferred_element_type=F32)
            for j in range(w // LANES):
                kind = kinds[(c0 // LANES) + j]
                t = res[:, j * LANES:(j + 1) * LANES]
                if kind != V_KIND:
                    t = _rope(t, cos_ref[...], sin_ref[...], first_half)
                if kind == Q_KIND:
                    t = t * (HEAD_DIM ** -0.5)
                store(c0 + j * LANES, t)

    low_head = lane < HEAD_DIM

    def store_nat(c, t):
        if A_Q_DIM <= c < A_Q_DIM + A_KV_DIM:
            swapped = pltpu.roll(t, HEAD_DIM, 1)
            nat_ref[:, NAT_AK:NAT_AK + LANES] = jnp.where(low_head, t, swapped).astype(BF16)
            nat_ref[:, NAT_AK + LANES:NAT_AK + 2 * LANES] = jnp.where(low_head, swapped, t).astype(BF16)
        elif c < A_Q_DIM + 2 * A_KV_DIM:
            out = c if c < A_Q_DIM else NAT_AV
            nat_ref[:, out:out + LANES] = t.astype(BF16)
        else:
            out = c - 2 * A_KV_DIM
            nat_ref[:, out:out + LANES] = t.astype(BF16)

    project(hb_ref[...], 0, _NAT_KINDS, c1_ref, s1_ref, store_nat)

    for c0 in range(0, GATE_COLS, 512):
        res = jnp.dot(hb_ref[...], w_ref[:, COL_GATE + c0:COL_GATE + c0 + 512], preferred_element_type=F32)
        gate_ref[:, c0:c0 + 512] = jax.nn.sigmoid(res).astype(BF16)

    for dil, col0, cos_ref, sin_ref, out_ref in ((4, COL_G1, c4_ref, s4_ref, g1_ref),
                                                 (16, COL_G2, c16_ref, s16_ref, g2_ref)):
        n = tm // dil
        for r in range(dil):
            for c in range(n_chunks):
                hd_ref[r * n:(r + 1) * n, c * LANES:(c + 1) * LANES] = (
                    hf_ref[c, pl.ds(r, n, stride=dil), :].astype(BF16))

        def store_grp(c, t, out_ref=out_ref, dil=dil, n=n):
            v = t.astype(BF16)
            for r in range(dil):
                out_ref[r, :, c:c + LANES] = v[r * n:(r + 1) * n]

        project(hd_ref[...], col0, _GRP_KINDS, cos_ref, sin_ref, store_grp)


def _in_proj(x2d, g, w_bf16, tables, batch, seq):
    tm = IN_TILE
    tiles_per_seq = seq // tm
    n_tok = batch * seq
    c1, s1, c4, s4, c16, s16 = tables
    tab_spec = pl.BlockSpec((tm, LANES), lambda i: (i % tiles_per_seq, 0))
    return pl.pallas_call(
        _in_proj_kernel,
        grid=(n_tok // tm,),
        in_specs=[
            pl.BlockSpec((tm, D_MODEL), lambda i: (i, 0)),
            pl.BlockSpec((1, D_MODEL), lambda i: (0, 0)),
            pl.BlockSpec((D_MODEL, D_IN), lambda i: (0, 0), pipeline_mode=pl.Buffered(1)),
            tab_spec, tab_spec, tab_spec, tab_spec, tab_spec, tab_spec,
        ],
        out_specs=[
            pl.BlockSpec((tm, NAT_COLS), lambda i: (i, 0)),
            pl.BlockSpec((tm, GATE_COLS), lambda i: (i, 0)),
            pl.BlockSpec((None, 4, tm // 4, GRP_COLS), lambda i: (i // tiles_per_seq, 0, i % tiles_per_seq, 0)),
            pl.BlockSpec((None, 16, tm // 16, GRP_COLS), lambda i: (i // tiles_per_seq, 0, i % tiles_per_seq, 0)),
        ],
        out_shape=[
            jax.ShapeDtypeStruct((n_tok, NAT_COLS), BF16),
            jax.ShapeDtypeStruct((n_tok, GATE_COLS), BF16),
            jax.ShapeDtypeStruct((batch, 4, seq // 4, GRP_COLS), BF16),
            jax.ShapeDtypeStruct((batch, 16, seq // 16, GRP_COLS), BF16),
        ],
        scratch_shapes=[
            pltpu.VMEM((D_MODEL // LANES, tm, LANES), F32),
            pltpu.VMEM((tm, D_MODEL), BF16),
            pltpu.VMEM((tm, D_MODEL), BF16),
        ],
        compiler_params=_cparams("parallel"),
        name="in_proj",
    )(x2d, g, w_bf16, c1, s1, c4, s4, c16, s16)


def _masked_heads(q_pair, low_head, high_head):
    zero = jnp.zeros_like(q_pair)
    return [jnp.where(low_head, q_pair, zero), jnp.where(high_head, q_pair, zero)]


def _attn_a_kernel(sink_ref, q_ref, kp_ref, km_ref, kn_ref, vp_ref, vm_ref, vn_ref, o_ref, k_ref, vt_ref, *, seq):
    tq = q_ref.shape[0]
    hw = A_HALF_WINDOW
    n_sub = tq // hw
    i = pl.program_id(1)
    last_blk = seq // hw - 1
    k_ref[0:hw, :] = kp_ref[...]
    k_ref[hw:hw + tq, :] = km_ref[...]
    k_ref[hw + tq:tq + 2 * hw, :] = kn_ref[...]
    for r0, src in ((0, vp_ref), (hw, vm_ref), (hw + tq, vn_ref)):
        vt_ref[:, r0:r0 + src.shape[0]] = src[...].astype(F32).T.astype(BF16)
    grp_cols = A_GROUP * hw
    key = lax.broadcasted_iota(I32, (hw, grp_cols), 0)
    qry = lax.broadcasted_iota(I32, (hw, grp_cols), 1) % hw
    low_head = lax.broadcasted_iota(I32, (1, LANES), 1) < HEAD_DIM
    high_head = jnp.logical_not(low_head)
    for sb in range(n_sub):
        r0 = sb * hw
        blk = i * n_sub + sb
        mask_p = (key >= qry) if sb > 0 else (key >= qry + jnp.where(blk > 0, 0, hw))
        mask_n = (key <= qry) if sb < n_sub - 1 else (key <= qry - jnp.where(blk < last_blk, 0, hw))
        out_t = []
        for g in range(A_KV_HEADS):
            heads = range(g * A_GROUP, (g + 1) * A_GROUP)
            q_parts = []
            for c in range(g * A_GROUP // 2, (g + 1) * A_GROUP // 2):
                q_parts += _masked_heads(q_ref[r0:r0 + hw, c * LANES:(c + 1) * LANES], low_head, high_head)
            q = jnp.concatenate(q_parts, axis=0)
            sink = jnp.concatenate([jnp.full((1, hw), sink_ref[h], F32) for h in heads], axis=1)
            k = k_ref[r0:r0 + 3 * hw, g * LANES:(g + 1) * LANES]
            s = lax.dot_general(k, q, (((1,), (1,)), ((), ())), preferred_element_type=F32)
            sp = jnp.where(mask_p, s[0:hw], NEG_INF)
            so = s[hw:2 * hw]
            sn = jnp.where(mask_n, s[2 * hw:3 * hw], NEG_INF)
            m = jnp.max(jnp.maximum(jnp.maximum(sp, so), sn), axis=0, keepdims=True)
            m = jnp.maximum(m, sink)
            pp, po, pn = jnp.exp(sp - m), jnp.exp(so - m), jnp.exp(sn - m)
            denom = jnp.sum(pp + po + pn, axis=0, keepdims=True) + jnp.exp(sink - m)
            p = jnp.concatenate([pp, po, pn], axis=0).astype(BF16)
            vt = vt_ref[g * HEAD_DIM:(g + 1) * HEAD_DIM, r0:r0 + 3 * hw]
            o = jnp.dot(vt, p, preferred_element_type=F32) * (1.0 / denom)
            out_t += [o[:, j * hw:(j + 1) * hw] for j in range(A_GROUP)]
        o_ref[r0:r0 + hw, :] = jnp.concatenate(out_t, axis=0).T.astype(BF16)


def _attn_a(nat3d, sink):
    batch, seq, _ = nat3d.shape
    tq = A_Q_TILE
    hw = A_HALF_WINDOW
    per = tq // hw
    n_hw = seq // hw
    k_cols, v_cols = 2 * A_KV_DIM, A_KV_DIM
    k_blk, v_blk = NAT_AK // k_cols, NAT_AV // v_cols

    def prev_spec(cols, blk):
        return pl.BlockSpec((None, hw, cols), lambda b, i: (b, jnp.maximum(i * per - 1, 0), blk))

    def main_spec(cols, blk):
        return pl.BlockSpec((None, tq, cols), lambda b, i: (b, i, blk))

    def next_spec(cols, blk):
        return pl.BlockSpec((None, hw, cols), lambda b, i: (b, jnp.minimum((i + 1) * per, n_hw - 1), blk))

    return pl.pallas_call(
        functools.partial(_attn_a_kernel, seq=seq),
        grid=(batch, seq // tq),
        in_specs=[
            pl.BlockSpec(memory_space=pltpu.SMEM),
            main_spec(A_Q_DIM, 0),
            prev_spec(k_cols, k_blk), main_spec(k_cols, k_blk), next_spec(k_cols, k_blk),
            prev_spec(v_cols, v_blk), main_spec(v_cols, v_blk), next_spec(v_cols, v_blk),
        ],
        out_specs=pl.BlockSpec((None, tq, A_Q_DIM), lambda b, i: (b, i, 0)),
        out_shape=jax.ShapeDtypeStruct((batch, seq, A_Q_DIM), BF16),
        scratch_shapes=[pltpu.VMEM((tq + 2 * hw, k_cols), BF16), pltpu.VMEM((v_cols, tq + 2 * hw), BF16)],
        compiler_params=_cparams("parallel", "parallel"),
        name="attn_a",
    )(sink, nat3d, nat3d, nat3d, nat3d, nat3d, nat3d, nat3d)


def _attn_b_kernel(q_ref, kp_ref, km_ref, kn_ref, vp_ref, vm_ref, vn_ref, o_ref, lse_ref,
                   k_ref, vt_ref, *, sub_len):
    tq = q_ref.shape[0]
    hw = B_HALF_WINDOW
    t0 = pl.program_id(1) * tq
    k_ref[0:hw, :] = kp_ref[...]
    k_ref[hw:hw + tq, :] = km_ref[...]
    k_ref[hw + tq:tq + 2 * hw, :] = kn_ref[...]
    for r0, src in ((0, vp_ref), (hw, vm_ref), (hw + tq, vn_ref)):
        vt_ref[:, r0:r0 + src.shape[0]] = src[...].astype(F32).T.astype(BF16)
    kw = B_SUB + 2 * hw
    n_sub = tq // B_SUB
    all_cols = B_HEADS * B_SUB
    key = lax.broadcasted_iota(I32, (kw, all_cols), 0)
    qry = lax.broadcasted_iota(I32, (kw, all_cols), 1) % B_SUB
    in_band = jnp.abs(key - hw - qry) <= hw
    low_head = lax.broadcasted_iota(I32, (1, LANES), 1) < HEAD_DIM
    high_head = jnp.logical_not(low_head)
    for sb in range(n_sub):
        r0 = sb * B_SUB
        valid = in_band
        if sb == 0:
            valid = valid & (t0 - hw + key >= 0)
        if sb == n_sub - 1:
            valid = valid & (t0 + r0 - hw + key < sub_len)
        s_parts = []
        for c in range(B_HEADS // 2):
            cs = slice(c * LANES, (c + 1) * LANES)
            q = jnp.concatenate(_masked_heads(q_ref[r0:r0 + B_SUB, cs], low_head, high_head), axis=0)
            s_parts.append(lax.dot_general(k_ref[r0:r0 + kw, cs], q, (((1,), (1,)), ((), ())),
                                           preferred_element_type=F32))
        s = jnp.where(valid, jnp.concatenate(s_parts, axis=1), NEG_INF)
        m = jnp.max(s, axis=0, keepdims=True)
        p = jnp.exp(s - m)
        denom = jnp.sum(p, axis=0, keepdims=True)
        p = p.astype(BF16)
        inv = 1.0 / denom
        lse = m + jnp.log(denom)
        out_t, lse_t = [], []
        for h in range(B_HEADS):
            qs = slice(h * B_SUB, (h + 1) * B_SUB)
            vt = vt_ref[h * HEAD_DIM:(h + 1) * HEAD_DIM, r0:r0 + kw]
            out_t.append(jnp.dot(vt, p[:, qs], preferred_element_type=F32) * inv[:, qs])
            lse_t.append(jnp.broadcast_to(lse[:, qs], (HEAD_DIM, B_SUB)))
        o_ref[r0:r0 + B_SUB, :] = jnp.concatenate(out_t, axis=0).T
        lse_ref[r0:r0 + B_SUB, :] = jnp.concatenate(lse_t, axis=0).T


def _attn_b(arr3d, q_blk, name):
    n_sub, sub_len, _ = arr3d.shape
    tq = min(512, sub_len)
    hw = B_HALF_WINDOW
    per = tq // hw
    n_hw = sub_len // hw

    def main_spec(c):
        return pl.BlockSpec((None, tq, B_DIM), lambda g, i: (g, i, c))

    def prev_spec(c):
        return pl.BlockSpec((None, hw, B_DIM), lambda g, i: (g, jnp.maximum(i * per - 1, 0), c))

    def next_spec(c):
        return pl.BlockSpec((None, hw, B_DIM), lambda g, i: (g, jnp.minimum((i + 1) * per, n_hw - 1), c))

    out_spec = pl.BlockSpec((None, tq, B_DIM), lambda g, i: (g, i, 0))
    return pl.pallas_call(
        functools.partial(_attn_b_kernel, sub_len=sub_len),
        grid=(n_sub, sub_len // tq),
        in_specs=[main_spec(q_blk),
                  prev_spec(q_blk + 1), main_spec(q_blk + 1), next_spec(q_blk + 1),
                  prev_spec(q_blk + 2), main_spec(q_blk + 2), next_spec(q_blk + 2)],
        out_specs=[out_spec, out_spec],
        out_shape=[jax.ShapeDtypeStruct((n_sub, sub_len, B_DIM), F32)] * 2,
        scratch_shapes=[pltpu.VMEM((tq + 2 * hw, B_DIM), BF16), pltpu.VMEM((B_DIM, tq + 2 * hw), BF16)],
        compiler_params=_cparams("parallel", "parallel"),
        name=name,
    )(arr3d, arr3d, arr3d, arr3d, arr3d, arr3d, arr3d)


def _merge_kernel(x_ref, ya_ref, o0_ref, l0_ref, o1_ref, l1_ref, o2_ref, l2_ref, gate_ref,
                  wa_ref, wb_ref, wo_ref, g2_ref, wr_ref, br_ref,
                  x1_ref, h2_ref, code_ref, wcol_ref, cnt_ref,
                  so1_ref, sl1_ref, so2_ref, sl2_ref, yb_ref, run_ref, earlier_ref,
                  wab_ref, wbb_ref, wob_ref):
    tm = x_ref.shape[0]

    @pl.when(pl.program_id(0) == 0)
    def _():
        run_ref[...] = jnp.zeros_like(run_ref)
        wab_ref[...] = wa_ref[...].astype(BF16)
        wbb_ref[...] = wb_ref[...].astype(BF16)
        wob_ref[...] = wo_ref[...].astype(BF16)
        row = lax.broadcasted_iota(I32, (tm, tm), 0)
        col = lax.broadcasted_iota(I32, (tm, tm), 1)
        earlier_ref[...] = (row < col).astype(BF16)

    for dil, src_o, src_l, dst_o, dst_l in ((4, o1_ref, l1_ref, so1_ref, sl1_ref),
                                            (16, o2_ref, l2_ref, so2_ref, sl2_ref)):
        n = tm // dil
        for r in range(dil):
            for c in range(B_DIM // LANES):
                cs = slice(c * LANES, (c + 1) * LANES)
                dst_o[c, pl.ds(r, n, stride=dil), :] = src_o[r, :, cs]
                dst_l[c, pl.ds(r, n, stride=dil), :] = src_l[r, :, cs]

    for c in range(B_DIM // LANES):
        cs = slice(c * LANES, (c + 1) * LANES)
        l0, l1, l2 = l0_ref[:, cs], sl1_ref[c], sl2_ref[c]
        m = jnp.maximum(jnp.maximum(l0, l1), l2)
        e0, e1, e2 = jnp.exp(l0 - m), jnp.exp(l1 - m), jnp.exp(l2 - m)
        yb = (e0 * o0_ref[:, cs] + e1 * so1_ref[c] + e2 * so2_ref[c]) / (e0 + e1 + e2)
        yb_ref[:, cs] = yb.astype(BF16)

    ya_p = jnp.dot(ya_ref[...], wab_ref[...], preferred_element_type=F32)
    yb_p = jnp.dot(yb_ref[...], wbb_ref[...], preferred_element_type=F32)
    merged = gate_ref[:, 0:D_MODEL].astype(F32) * ya_p + gate_ref[:, D_MODEL:GATE_COLS].astype(F32) * yb_p
    x1 = x_ref[...] + jnp.dot(merged.astype(BF16), wob_ref[...], preferred_element_type=F32)
    x1_ref[...] = x1

    h2 = x1 * lax.rsqrt(jnp.mean(x1 * x1, axis=-1, keepdims=True) + NORM_EPS) * g2_ref[...]
    _store_row_tiles(h2_ref, h2)

    h_hi = h2.astype(BF16)
    h_lo = (h2 - h_hi.astype(F32)).astype(BF16)
    w_hi, w_lo = wr_ref[0:D_MODEL], wr_ref[D_MODEL:2 * D_MODEL]
    logits = (jnp.dot(h_hi, w_hi, preferred_element_type=F32)
              + jnp.dot(h_lo, w_hi, preferred_element_type=F32)
              + jnp.dot(h_hi, w_lo, preferred_element_type=F32)) + br_ref[...]

    logits_t = logits.T
    sub = lax.broadcasted_iota(I32, (EXPERTS_PER_GROUP, tm), 0).astype(F32)
    none = float(EXPERTS_PER_GROUP)
    gl = jnp.where(sub < MOE_GROUPS, logits_t[N_EXPERTS:N_EXPERTS + EXPERTS_PER_GROUP], -jnp.inf)
    gmax = jnp.max(gl, axis=0, keepdims=True)
    gidx = jnp.min(jnp.where(gl == gmax, sub, none), axis=0, keepdims=True)
    gw = 1.0 / jnp.sum(jnp.exp(gl - gmax), axis=0, keepdims=True)
    el = logits_t[0:EXPERTS_PER_GROUP]
    for g in range(1, MOE_GROUPS):
        el = jnp.where(gidx == g, logits_t[g * EXPERTS_PER_GROUP:(g + 1) * EXPERTS_PER_GROUP], el)
    v1 = jnp.max(el, axis=0, keepdims=True)
    i1 = jnp.min(jnp.where(el == v1, sub, none), axis=0, keepdims=True)
    el2 = jnp.where(sub == i1, -jnp.inf, el)
    v2 = jnp.max(el2, axis=0, keepdims=True)
    i2 = jnp.min(jnp.where(el2 == v2, sub, none), axis=0, keepdims=True)
    t = jnp.exp(v2 - v1)
    w1 = gw / (1.0 + t)
    w2 = gw * t / (1.0 + t)
    e1 = gidx * EXPERTS_PER_GROUP + i1
    e2 = gidx * EXPERTS_PER_GROUP + i2

    expert = lax.broadcasted_iota(I32, (N_EXPERTS, tm), 0).astype(F32)
    oh1 = (expert == e1).astype(F32)
    oh2 = (expert == e2).astype(F32)
    oh = oh1 + oh2
    run = run_ref[...]
    before = (jnp.dot(oh.astype(BF16), earlier_ref[...], preferred_element_type=F32)
              + jnp.concatenate([run] * (tm // LANES), axis=1))
    rank1 = jnp.sum(before * oh1, axis=0, keepdims=True)
    rank2 = jnp.sum(before * oh2, axis=0, keepdims=True)
    run_ref[...] = run + jnp.sum(oh, axis=1, keepdims=True)
    cnt_ref[...] = run_ref[...]

    scale = float(1 << RANK_BITS)
    zeros = jnp.zeros((6, tm), F32)
    code_ref[...] = jnp.concatenate([e1 * scale + rank1, e2 * scale + rank2, zeros], axis=0).astype(I32)
    w_rows = jnp.concatenate([w1, w2, jnp.zeros((LANES - 2, tm), F32)], axis=0)
    wcol_ref[...] = w_rows.T


def _merge(x2d, ya, o0, l0, o1, l1, o2, l2, gates, wa, wb, wo, g2, wr, br, batch, seq):
    tm = TOK_TILE
    tps = seq // tm
    n_tok = batch * seq

    def tok(c):
        return pl.BlockSpec((tm, c), lambda i: (i, 0))

    def full(a):
        return pl.BlockSpec(a.shape, lambda i: (0,) * a.ndim)

    def res_spec(d):
        return pl.BlockSpec((None, d, tm // d, B_DIM), lambda i: (i // tps, 0, i % tps, 0))

    return pl.pallas_call(
        _merge_kernel,
        grid=(n_tok // tm,),
        in_specs=[tok(D_MODEL), tok(A_Q_DIM), tok(B_DIM), tok(B_DIM),
                  res_spec(4), res_spec(4), res_spec(16), res_spec(16), tok(GATE_COLS),
                  full(wa), full(wb), full(wo), full(g2), full(wr), full(br)],
        out_specs=[tok(D_MODEL), pl.BlockSpec((tm * ROW_CHUNKS, LANES), lambda i: (i, 0)),
                   pl.BlockSpec((8, tm), lambda i: (0, i)),
                   tok(LANES),
                   pl.BlockSpec((N_EXPERTS, LANES), lambda i: (0, 0))],
        out_shape=[jax.ShapeDtypeStruct((n_tok, D_MODEL), F32),
                   jax.ShapeDtypeStruct((n_tok * ROW_CHUNKS, LANES), F32),
                   jax.ShapeDtypeStruct((8, n_tok), I32),
                   jax.ShapeDtypeStruct((n_tok, LANES), F32),
                   jax.ShapeDtypeStruct((N_EXPERTS, LANES), F32)],
        scratch_shapes=([pltpu.VMEM((B_DIM // LANES, tm, LANES), F32)] * 4
                        + [pltpu.VMEM((tm, B_DIM), BF16), pltpu.VMEM((N_EXPERTS, LANES), F32),
                           pltpu.VMEM((tm, tm), BF16),
                           pltpu.VMEM(wa.shape, BF16), pltpu.VMEM(wb.shape, BF16), pltpu.VMEM(wo.shape, BF16)]),
        compiler_params=_cparams("arbitrary"),
        name="merge_route",
    )(x2d, ya, o0, l0, o1, l1, o2, l2, gates, wa, wb, wo, g2, wr, br)


def _row_tile(ref, t):
    return ref.at[pl.ds(pl.multiple_of(t * ROW_CHUNKS, ROW_CHUNKS), ROW_CHUNKS)]


def _store_row_tiles(ref, val):
    rows = val.shape[0]
    for c in range(ROW_CHUNKS):
        ref[pl.ds(c, rows, stride=ROW_CHUNKS), :] = val[:, c * LANES:(c + 1) * LANES]


def _load_row_tiles_chunk(ref, c):
    return ref[pl.ds(c, ref.shape[0] // ROW_CHUNKS, stride=ROW_CHUNKS), :]


def _dispatch_kernel(slot_ref, h_ref, xs_in_ref, xs_ref, sem, *, n_tok):
    del xs_in_ref
    i = pl.program_id(0)
    rows = h_ref.shape[0] // ROW_CHUNKS

    def issue(j, carry):
        t = i * rows + j
        for k in range(2):
            pltpu.make_async_copy(_row_tile(h_ref, j), _row_tile(xs_ref, slot_ref[k * n_tok + t]), sem).start(priority=k)
        return carry

    lax.fori_loop(0, rows, issue, 0, unroll=DMA_UNROLL)
    for _ in range(2):
        pltpu.make_async_copy(h_ref, xs_ref.at[pl.ds(0, rows * ROW_CHUNKS)], sem).wait()


def _dispatch(slots, h2, xs_init):
    n_tok = h2.shape[0] // ROW_CHUNKS
    rows = DISPATCH_TILE
    return pl.pallas_call(
        functools.partial(_dispatch_kernel, n_tok=n_tok),
        grid_spec=pltpu.PrefetchScalarGridSpec(
            num_scalar_prefetch=1,
            grid=(n_tok // rows,),
            in_specs=[pl.BlockSpec((rows * ROW_CHUNKS, LANES), lambda i, s: (i, 0)),
                      pl.BlockSpec(memory_space=pl.ANY)],
            out_specs=pl.BlockSpec(memory_space=pl.ANY),
            scratch_shapes=[pltpu.SemaphoreType.DMA(())],
        ),
        out_shape=jax.ShapeDtypeStruct(xs_init.shape, xs_init.dtype),
        input_output_aliases={2: 0},
        compiler_params=_cparams("arbitrary"),
        name="dispatch",
    )(slots, h2, xs_init)


def _combine_kernel(slot_ref, x_ref, w_ref, g_ref, ys_ref, o_ref, buf_ref, y_ref, sem, *, n_tok, final):
    i = pl.program_id(0)
    n_steps = pl.num_programs(0)
    rows = x_ref.shape[0]

    def issue_tile(tile, slot):
        def issue(j, carry):
            t = tile * rows + j
            for k in range(2):
                pltpu.make_async_copy(_row_tile(ys_ref, slot_ref[k * n_tok + t]),
                                      _row_tile(buf_ref.at[slot, k], j), sem.at[slot]).start(priority=k)
            return carry

        lax.fori_loop(0, rows, issue, 0, unroll=DMA_UNROLL)

    @pl.when(i == 0)
    def _():
        issue_tile(0, 0)

    @pl.when(i + 1 < n_steps)
    def _():
        issue_tile(i + 1, (i + 1) % 2)

    cur = i % 2
    for k in range(2):
        pltpu.make_async_copy(ys_ref.at[pl.ds(0, rows * ROW_CHUNKS)], buf_ref.at[cur, k], sem.at[cur]).wait()
    w = w_ref[...]
    w1, w2 = w[:, 0:1], w[:, 1:2]
    for c in range(ROW_CHUNKS):
        cs = slice(c * LANES, (c + 1) * LANES)
        y1 = _load_row_tiles_chunk(buf_ref.at[cur, 0], c)
        y2 = _load_row_tiles_chunk(buf_ref.at[cur, 1], c)
        y_ref[:, cs] = x_ref[:, cs] + w1 * y1 + w2 * y2
    y = y_ref[...]
    if final:
        y = y * lax.rsqrt(jnp.mean(y * y, axis=-1, keepdims=True) + NORM_EPS) * g_ref[...]
    o_ref[...] = y


def _combine(slots, x1, wcol, g, ys, final):
    n_tok = x1.shape[0]
    rows = ROW_TILE
    return pl.pallas_call(
        functools.partial(_combine_kernel, n_tok=n_tok, final=final),
        grid_spec=pltpu.PrefetchScalarGridSpec(
            num_scalar_prefetch=1,
            grid=(n_tok // rows,),
            in_specs=[pl.BlockSpec((rows, D_MODEL), lambda i, s: (i, 0)),
                      pl.BlockSpec((rows, LANES), lambda i, s: (i, 0)),
                      pl.BlockSpec((1, D_MODEL), lambda i, s: (0, 0)),
                      pl.BlockSpec(memory_space=pl.ANY)],
            out_specs=pl.BlockSpec((rows, D_MODEL), lambda i, s: (i, 0)),
            scratch_shapes=[pltpu.VMEM((2, 2, rows * ROW_CHUNKS, LANES), F32),
                            pltpu.VMEM((rows, D_MODEL), F32),
                            pltpu.SemaphoreType.DMA((2,))],
        ),
        out_shape=jax.ShapeDtypeStruct((n_tok, D_MODEL), F32),
        compiler_params=_cparams("arbitrary"),
        name="combine",
    )(slots, x1, wcol, g, ys)


def _experts_kernel(tile_e_ref, n_used_ref, xs_ref, wg_ref, wu_ref, wd_ref, ys_ref, xbuf_ref, x_ref, sem):
    del tile_e_ref
    i = pl.program_id(0)
    n_used = n_used_ref[0]
    tile_rows = xbuf_ref.shape[1]

    def fetch(tile):
        rows = pl.ds(pl.multiple_of(tile * tile_rows, tile_rows), tile_rows)
        slot = tile % XS_SLOTS
        return pltpu.make_async_copy(xs_ref.at[rows], xbuf_ref.at[slot], sem.at[slot])

    @pl.when(i == 0)
    def _():
        fetch(0).start()

        @pl.when(n_used > 1)
        def _():
            fetch(1).start()

    @pl.when(i < n_used)
    def _():
        @pl.when(i + XS_SLOTS - 1 < n_used)
        def _():
            fetch(i + XS_SLOTS - 1).start()

        fetch(i).wait()
        xb = xbuf_ref.at[i % XS_SLOTS]
        for c in range(ROW_CHUNKS):
            x_ref[:, c * LANES:(c + 1) * LANES] = _load_row_tiles_chunk(xb, c).astype(BF16)
        x = x_ref[...]
        a = jnp.dot(x, wg_ref[...].astype(BF16), preferred_element_type=F32)
        u = jnp.dot(x, wu_ref[...].astype(BF16), preferred_element_type=F32)
        z = (a * jax.nn.sigmoid(a)) * u
        _store_row_tiles(ys_ref, jnp.dot(z.astype(BF16), wd_ref[...].astype(BF16), preferred_element_type=F32))

    @pl.when(i >= n_used)
    def _():
        ys_ref[...] = jnp.zeros_like(ys_ref)


def _experts(tile_e, n_used, xs, wg, wu, wd, layer):
    n_slots = xs.shape[0] // ROW_CHUNKS
    te = EXP_TILE

    def out_map(i, tile_e, n_used):
        return (i, 0)

    def w_map(i, tile_e, n_used):
        return (layer, tile_e[i], 0, 0)

    return pl.pallas_call(
        _experts_kernel,
        grid_spec=pltpu.PrefetchScalarGridSpec(
            num_scalar_prefetch=2,
            grid=(n_slots // te,),
            in_specs=[pl.BlockSpec(memory_space=pl.ANY),
                      pl.BlockSpec((None, None, D_MODEL, D_EXPERT), w_map),
                      pl.BlockSpec((None, None, D_MODEL, D_EXPERT), w_map),
                      pl.BlockSpec((None, None, D_EXPERT, D_MODEL), w_map)],
            out_specs=pl.BlockSpec((te * ROW_CHUNKS, LANES), out_map),
            scratch_shapes=[pltpu.VMEM((XS_SLOTS, te * ROW_CHUNKS, LANES), F32),
                            pltpu.VMEM((te, D_MODEL), BF16),
                            pltpu.SemaphoreType.DMA((XS_SLOTS,))],
        ),
        out_shape=jax.ShapeDtypeStruct((n_slots * ROW_CHUNKS, LANES), F32),
        compiler_params=_cparams("arbitrary"),
        name="experts",
    )(tile_e, n_used, xs, wg, wu, wd)


def kernel(x, attn_norm_g, w_in, a_sink, w_branch_a, w_branch_b, w_out, ffn_norm_g,
           w_router_group, b_router_group, w_router_expert, b_router_expert,
           w_exp_gate, w_exp_up, w_exp_down, final_norm_g):
    batch, seq, d_model = x.shape
    depth = w_in.shape[0]
    n_tok = batch * seq
    assert d_model == D_MODEL and w_in.shape[2] == D_IN
    assert seq % (16 * B_SUB) == 0 and seq % IN_TILE == 0 and n_tok % DISPATCH_TILE == 0
    assert n_tok < (1 << RANK_BITS)

    cos_t, sin_t = _rope_tables(seq)
    tables = (cos_t, sin_t,
              _residue_order(cos_t, 4, IN_TILE), _residue_order(sin_t, 4, IN_TILE),
              _residue_order(cos_t, 16, IN_TILE), _residue_order(sin_t, 16, IN_TILE))

    n_slots = 2 * n_tok + N_EXPERTS * EXP_TILE
    n_tiles = n_slots // EXP_TILE
    x2d = x.reshape(n_tok, D_MODEL)

    for l in range(depth):
        nat, gates, grp1, grp2 = _in_proj(x2d, attn_norm_g[l][None, :], w_in[l].astype(BF16), tables, batch, seq)
        nat3d = nat.reshape(batch, seq, NAT_COLS)
        ya = _attn_a(nat3d, a_sink[l]).reshape(n_tok, A_Q_DIM)
        o0, l0 = _attn_b(nat3d, NAT_B // B_DIM, "attn_b1")
        o1, l1 = _attn_b(grp1.reshape(batch * 4, seq // 4, GRP_COLS), 0, "attn_b4")
        o2, l2 = _attn_b(grp2.reshape(batch * 16, seq // 16, GRP_COLS), 0, "attn_b16")

        wr = jnp.zeros((D_MODEL, LANES), F32)
        wr = wr.at[:, 0:N_EXPERTS].set(w_router_expert[l]).at[:, N_EXPERTS:N_EXPERTS + MOE_GROUPS].set(w_router_group[l])
        br = jnp.zeros((1, LANES), F32)
        br = br.at[0, 0:N_EXPERTS].set(b_router_expert[l]).at[0, N_EXPERTS:N_EXPERTS + MOE_GROUPS].set(b_router_group[l])
        wr_hi = wr.astype(BF16)
        wr_lo = (wr - wr_hi.astype(F32)).astype(BF16)
        wr_stack = jnp.concatenate([wr_hi, wr_lo], axis=0)

        x1, h2, code, wcol, cnt = _merge(
            x2d, ya, o0.reshape(n_tok, B_DIM), l0.reshape(n_tok, B_DIM),
            o1.reshape(batch, 4, seq // 4, B_DIM), l1.reshape(batch, 4, seq // 4, B_DIM),
            o2.reshape(batch, 16, seq // 16, B_DIM), l2.reshape(batch, 16, seq // 16, B_DIM),
            gates, w_branch_a[l], w_branch_b[l], w_out[l],
            ffn_norm_g[l][None, :], wr_stack, br, batch, seq)

        counts = cnt[:, 0].astype(I32)
        padded = ((counts + EXP_TILE - 1) // EXP_TILE) * EXP_TILE
        ends = jnp.cumsum(padded)
        offs = ends - padded
        n_used = (ends[-1:] // EXP_TILE).astype(I32)
        tile_start = jnp.arange(n_tiles, dtype=I32) * EXP_TILE
        tile_e = jnp.minimum(jnp.sum((ends[None, :] <= tile_start[:, None]).astype(I32), axis=1), N_EXPERTS - 1)
        eid = code[0:2] >> RANK_BITS
        rank = code[0:2] & ((1 << RANK_BITS) - 1)
        expert_ids = jnp.arange(N_EXPERTS, dtype=I32)[:, None, None]
        slots = (rank + jnp.sum(jnp.where(eid[None] == expert_ids, offs[:, None, None], 0), axis=0)).reshape(-1)

        xs = _dispatch(slots, h2, jnp.zeros((n_slots * ROW_CHUNKS, LANES), F32) if l == 0 else xs)
        ys = _experts(tile_e, n_used, xs, w_exp_gate, w_exp_up, w_exp_down, l)
        x2d = _combine(slots, x1, wcol, final_norm_g[None, :], ys, final=(l == depth - 1))

    return x2d.reshape(batch, seq, D_MODEL)
```

```python
import functools

import jax
import jax.numpy as jnp
from jax import lax
from jax.experimental import pallas as pl
from jax.experimental.pallas import tpu as pltpu

F32 = jnp.float32
BF16 = jnp.bfloat16
I32 = jnp.int32

D_MODEL = 1024
HEAD_DIM = 64
HALF_HEAD = HEAD_DIM // 2
ROPE_THETA = 10000.0
NORM_EPS = 1e-6
NEG_INF = -1e30
LANES = 128

A_Q_HEADS = 8
A_KV_HEADS = 2
A_GROUP = A_Q_HEADS // A_KV_HEADS
A_HALF_WINDOW = 128
A_Q_DIM = A_Q_HEADS * HEAD_DIM
A_KV_DIM = A_KV_HEADS * HEAD_DIM

B_GROUPS = ((128, 1), (512, 4), (2048, 16))
B_HEADS = 4
B_DIM = B_HEADS * HEAD_DIM
B_HALF_WINDOW = 64

MOE_GROUPS = 4
EXPERTS_PER_GROUP = 8
N_EXPERTS = MOE_GROUPS * EXPERTS_PER_GROUP
D_EXPERT = 256

NAT_IN_COLS = A_Q_DIM + 2 * A_KV_DIM + 3 * B_DIM
NAT_B = A_Q_DIM
NAT_AK = NAT_B + 3 * B_DIM
NAT_AV = NAT_AK + 2 * A_KV_DIM
NAT_COLS = NAT_AV + A_KV_DIM
GRP_COLS = 3 * B_DIM
COL_G1 = NAT_IN_COLS
COL_G2 = COL_G1 + GRP_COLS
COL_GATE = COL_G2 + GRP_COLS
GATE_COLS = 2 * D_MODEL
D_IN = COL_GATE + GATE_COLS

TOK_TILE = 512
IN_TILE = 512
A_Q_TILE = 512
B_SUB = 128
EXP_TILE = 256
XS_SLOTS = 3
ROW_CHUNKS = D_MODEL // LANES
ROW_TILE = 256
DISPATCH_TILE = 1024
DMA_UNROLL = 8
RANK_BITS = 16
VMEM_LIMIT = 56 * 1024 * 1024


def _cparams(*sem):
    return pltpu.CompilerParams(dimension_semantics=sem, vmem_limit_bytes=VMEM_LIMIT)


def _rope_tables(seq_len):
    inv = 1.0 / (ROPE_THETA ** (jnp.arange(0, HEAD_DIM, 2, dtype=F32) / HEAD_DIM))
    ang = jnp.arange(seq_len, dtype=F32)[:, None] * inv[None, :]
    cos, sin = jnp.cos(ang), jnp.sin(ang)
    cos_t = jnp.concatenate([cos, cos, cos, cos], axis=-1)
    sin_t = jnp.concatenate([-sin, sin, -sin, sin], axis=-1)
    return cos_t, sin_t


def _residue_order(table, dilation, tile):
    s, c = table.shape
    return table.reshape(s // tile, tile // dilation, dilation, c).transpose(0, 2, 1, 3).reshape(s, c)


def _rope(t, cos, sin_signed, first_half):
    partner = jnp.where(first_half, pltpu.roll(t, LANES - HALF_HEAD, 1), pltpu.roll(t, HALF_HEAD, 1))
    return t * cos + partner * sin_signed


Q_KIND, K_KIND, V_KIND = 0, 1, 2
_NAT_KINDS = ([Q_KIND] * 4 + [K_KIND] + [V_KIND] + [Q_KIND] * 2 + [K_KIND] * 2 + [V_KIND] * 2)
_GRP_KINDS = [Q_KIND] * 2 + [K_KIND] * 2 + [V_KIND] * 2


def _in_proj_kernel(x_ref, g_ref, w_ref, c1_ref, s1_ref, c4_ref, s4_ref, c16_ref, s16_ref,
                    nat_ref, gate_ref, g1_ref, g2_ref, hf_ref, hb_ref, hd_ref):
    tm = x_ref.shape[0]
    x = x_ref[...]
    h = x * lax.rsqrt(jnp.mean(x * x, axis=-1, keepdims=True) + NORM_EPS) * g_ref[...]
    n_chunks = D_MODEL // LANES
    for c in range(n_chunks):
        hf_ref[c] = h[:, c * LANES:(c + 1) * LANES]
    hb_ref[...] = h.astype(BF16)
    lane = lax.broadcasted_iota(I32, (1, LANES), 1)
    first_half = (lane % HEAD_DIM) < HALF_HEAD

    def project(h_b, col0, kinds, cos_ref, sin_ref, store):
        width = 512
        for c0 in range(0, len(kinds) * LANES, width):
            w = min(width, len(kinds) * LANES - c0)
            res = jnp.dot(h_b, w_ref[:, col0 + c0:col0 + c0 + w], preferred_element_type=F32)
            for j in range(w // LANES):
                kind = kinds[(c0 // LANES) + j]
                t = res[:, j * LANES:(j + 1) * LANES]
                if kind != V_KIND:
                    t = _rope(t, cos_ref[...], sin_ref[...], first_half)
                if kind == Q_KIND:
                    t = t * (HEAD_DIM ** -0.5)
                store(c0 + j * LANES, t)

    low_head = lane < HEAD_DIM

    def store_nat(c, t):
        if A_Q_DIM <= c < A_Q_DIM + A_KV_DIM:
            swapped = pltpu.roll(t, HEAD_DIM, 1)
            nat_ref[:, NAT_AK:NAT_AK + LANES] = jnp.where(low_head, t, swapped).astype(BF16)
            nat_ref[:, NAT_AK + LANES:NAT_AK + 2 * LANES] = jnp.where(low_head, swapped, t).astype(BF16)
        elif c < A_Q_DIM + 2 * A_KV_DIM:
            out = c if c < A_Q_DIM else NAT_AV
            nat_ref[:, out:out + LANES] = t.astype(BF16)
        else:
            out = c - 2 * A_KV_DIM
            nat_ref[:, out:out + LANES] = t.astype(BF16)

    project(hb_ref[...], 0, _NAT_KINDS, c1_ref, s1_ref, store_nat)

    for c0 in range(0, GATE_COLS, 512):
        res = jnp.dot(hb_ref[...], w_ref[:, COL_GATE + c0:COL_GATE + c0 + 512], preferred_element_type=F32)
        gate_ref[:, c0:c0 + 512] = jax.nn.sigmoid(res).astype(BF16)

    for dil, col0, cos_ref, sin_ref, out_ref in ((4, COL_G1, c4_ref, s4_ref, g1_ref),
                                                 (16, COL_G2, c16_ref, s16_ref, g2_ref)):
        n = tm // dil
        for r in range(dil):
            for c in range(n_chunks):
                hd_ref[r * n:(r + 1) * n, c * LANES:(c + 1) * LANES] = (
                    hf_ref[c, pl.ds(r, n, stride=dil), :].astype(BF16))

        def store_grp(c, t, out_ref=out_ref, dil=dil, n=n):
            v = t.astype(BF16)
            for r in range(dil):
                out_ref[r, :, c:c + LANES] = v[r * n:(r + 1) * n]

        project(hd_ref[...], col0, _GRP_KINDS, cos_ref, sin_ref, store_grp)


def _in_proj(x2d, g, w_bf16, tables, batch, seq):
    tm = IN_TILE
    tiles_per_seq = seq // tm
    n_tok = batch * seq
    c1, s1, c4, s4, c16, s16 = tables
    tab_spec = pl.BlockSpec((tm, LANES), lambda i: (i % tiles_per_seq, 0))
    return pl.pallas_call(
        _in_proj_kernel,
        grid=(n_tok // tm,),
        in_specs=[
            pl.BlockSpec((tm, D_MODEL), lambda i: (i, 0)),
            pl.BlockSpec((1, D_MODEL), lambda i: (0, 0)),
            pl.BlockSpec((D_MODEL, D_IN), lambda i: (0, 0), pipeline_mode=pl.Buffered(1)),
            tab_spec, tab_spec, tab_spec, tab_spec, tab_spec, tab_spec,
        ],
        out_specs=[
            pl.BlockSpec((tm, NAT_COLS), lambda i: (i, 0)),
            pl.BlockSpec((tm, GATE_COLS), lambda i: (i, 0)),
            pl.BlockSpec((None, 4, tm // 4, GRP_COLS), lambda i: (i // tiles_per_seq, 0, i % tiles_per_seq, 0)),
            pl.BlockSpec((None, 16, tm // 16, GRP_COLS), lambda i: (i // tiles_per_seq, 0, i % tiles_per_seq, 0)),
        ],
        out_shape=[
            jax.ShapeDtypeStruct((n_tok, NAT_COLS), BF16),
            jax.ShapeDtypeStruct((n_tok, GATE_COLS), BF16),
            jax.ShapeDtypeStruct((batch, 4, seq // 4, GRP_COLS), BF16),
            jax.ShapeDtypeStruct((batch, 16, seq // 16, GRP_COLS), BF16),
        ],
        scratch_shapes=[
            pltpu.VMEM((D_MODEL // LANES, tm, LANES), F32),
            pltpu.VMEM((tm, D_MODEL), BF16),
            pltpu.VMEM((tm, D_MODEL), BF16),
        ],
        compiler_params=_cparams("parallel"),
        name="in_proj",
    )(x2d, g, w_bf16, c1, s1, c4, s4, c16, s16)


def _masked_heads(q_pair, low_head, high_head):
    zero = jnp.zeros_like(q_pair)
    return [jnp.where(low_head, q_pair, zero), jnp.where(high_head, q_pair, zero)]


def _attn_a_kernel(sink_ref, q_ref, kp_ref, km_ref, kn_ref, vp_ref, vm_ref, vn_ref, o_ref, k_ref, vt_ref, *, seq):
    tq = q_ref.shape[0]
    hw = A_HALF_WINDOW
    n_sub = tq // hw
    i = pl.program_id(1)
    last_blk = seq // hw - 1
    k_ref[0:hw, :] = kp_ref[...]
    k_ref[hw:hw + tq, :] = km_ref[...]
    k_ref[hw + tq:tq + 2 * hw, :] = kn_ref[...]
    for r0, src in ((0, vp_ref), (hw, vm_ref), (hw + tq, vn_ref)):
        vt_ref[:, r0:r0 + src.shape[0]] = src[...].astype(F32).T.astype(BF16)
    grp_cols = A_GROUP * hw
    key = lax.broadcasted_iota(I32, (hw, grp_cols), 0)
    qry = lax.broadcasted_iota(I32, (hw, grp_cols), 1) % hw
    low_head = lax.broadcasted_iota(I32, (1, LANES), 1) < HEAD_DIM
    high_head = jnp.logical_not(low_head)
    for sb in range(n_sub):
        r0 = sb * hw
        blk = i * n_sub + sb
        mask_p = (key >= qry) if sb > 0 else (key >= qry + jnp.where(blk > 0, 0, hw))
        mask_n = (key <= qry) if sb < n_sub - 1 else (key <= qry - jnp.where(blk < last_blk, 0, hw))
        out_t = []
        for g in range(A_KV_HEADS):
            heads = range(g * A_GROUP, (g + 1) * A_GROUP)
            q_parts = []
            for c in range(g * A_GROUP // 2, (g + 1) * A_GROUP // 2):
                q_parts += _masked_heads(q_ref[r0:r0 + hw, c * LANES:(c + 1) * LANES], low_head, high_head)
            q = jnp.concatenate(q_parts, axis=0)
            sink = jnp.concatenate([jnp.full((1, hw), sink_ref[h], F32) for h in heads], axis=1)
            k = k_ref[r0:r0 + 3 * hw, g * LANES:(g + 1) * LANES]
            s = lax.dot_general(k, q, (((1,), (1,)), ((), ())), preferred_element_type=F32)
            sp = jnp.where(mask_p, s[0:hw], NEG_INF)
            so = s[hw:2 * hw]
            sn = jnp.where(mask_n, s[2 * hw:3 * hw], NEG_INF)
            m = jnp.max(jnp.maximum(jnp.maximum(sp, so), sn), axis=0, keepdims=True)
            m = jnp.maximum(m, sink)
            pp, po, pn = jnp.exp(sp - m), jnp.exp(so - m), jnp.exp(sn - m)
            denom = jnp.sum(pp + po + pn, axis=0, keepdims=True) + jnp.exp(sink - m)
            p = jnp.concatenate([pp, po, pn], axis=0).astype(BF16)
            vt = vt_ref[g * HEAD_DIM:(g + 1) * HEAD_DIM, r0:r0 + 3 * hw]
            o = jnp.dot(vt, p, preferred_element_type=F32) * (1.0 / denom)
            out_t += [o[:, j * hw:(j + 1) * hw] for j in range(A_GROUP)]
        o_ref[r0:r0 + hw, :] = jnp.concatenate(out_t, axis=0).T.astype(BF16)


def _attn_a(nat3d, sink):
    batch, seq, _ = nat3d.shape
    tq = A_Q_TILE
    hw = A_HALF_WINDOW
    per = tq // hw
    n_hw = seq // hw
    k_cols, v_cols = 2 * A_KV_DIM, A_KV_DIM
    k_blk, v_blk = NAT_AK // k_cols, NAT_AV // v_cols

    def prev_spec(cols, blk):
        return pl.BlockSpec((None, hw, cols), lambda b, i: (b, jnp.maximum(i * per - 1, 0), blk))

    def main_spec(cols, blk):
        return pl.BlockSpec((None, tq, cols), lambda b, i: (b, i, blk))

    def next_spec(cols, blk):
        return pl.BlockSpec((None, hw, cols), lambda b, i: (b, jnp.minimum((i + 1) * per, n_hw - 1), blk))

    return pl.pallas_call(
        functools.partial(_attn_a_kernel, seq=seq),
        grid=(batch, seq // tq),
        in_specs=[
            pl.BlockSpec(memory_space=pltpu.SMEM),
            main_spec(A_Q_DIM, 0),
            prev_spec(k_cols, k_blk), main_spec(k_cols, k_blk), next_spec(k_cols, k_blk),
            prev_spec(v_cols, v_blk), main_spec(v_cols, v_blk), next_spec(v_cols, v_blk),
        ],
        out_specs=pl.BlockSpec((None, tq, A_Q_DIM), lambda b, i: (b, i, 0)),
        out_shape=jax.ShapeDtypeStruct((batch, seq, A_Q_DIM), BF16),
        scratch_shapes=[pltpu.VMEM((tq + 2 * hw, k_cols), BF16), pltpu.VMEM((v_cols, tq + 2 * hw), BF16)],
        compiler_params=_cparams("parallel", "parallel"),
        name="attn_a",
    )(sink, nat3d, nat3d, nat3d, nat3d, nat3d, nat3d, nat3d)


def _attn_b_kernel(q_ref, kp_ref, km_ref, kn_ref, vp_ref, vm_ref, vn_ref, o_ref, lse_ref,
                   k_ref, vt_ref, *, sub_len):
    tq = q_ref.shape[0]
    hw = B_HALF_WINDOW
    t0 = pl.program_id(1) * tq
    k_ref[0:hw, :] = kp_ref[...]
    k_ref[hw:hw + tq, :] = km_ref[...]
    k_ref[hw + tq:tq + 2 * hw, :] = kn_ref[...]
    for r0, src in ((0, vp_ref), (hw, vm_ref), (hw + tq, vn_ref)):
        vt_ref[:, r0:r0 + src.shape[0]] = src[...].astype(F32).T.astype(BF16)
    kw = B_SUB + 2 * hw
    n_sub = tq // B_SUB
    all_cols = B_HEADS * B_SUB
    key = lax.broadcasted_iota(I32, (kw, all_cols), 0)
    qry = lax.broadcasted_iota(I32, (kw, all_cols), 1) % B_SUB
    in_band = jnp.abs(key - hw - qry) <= hw
    low_head = lax.broadcasted_iota(I32, (1, LANES), 1) < HEAD_DIM
    high_head = jnp.logical_not(low_head)
    for sb in range(n_sub):
        r0 = sb * B_SUB
        valid = in_band
        if sb == 0:
            valid = valid & (t0 - hw + key >= 0)
        if sb == n_sub - 1:
            valid = valid & (t0 + r0 - hw + key < sub_len)
        s_parts = []
        for c in range(B_HEADS // 2):
            cs = slice(c * LANES, (c + 1) * LANES)
            q = jnp.concatenate(_masked_heads(q_ref[r0:r0 + B_SUB, cs], low_head, high_head), axis=0)
            s_parts.append(lax.dot_general(k_ref[r0:r0 + kw, cs], q, (((1,), (1,)), ((), ())),
                                           preferred_element_type=F32))
        s = jnp.where(valid, jnp.concatenate(s_parts, axis=1), NEG_INF)
        m = jnp.max(s, axis=0, keepdims=True)
        p = jnp.exp(s - m)
        denom = jnp.sum(p, axis=0, keepdims=True)
        p = p.astype(BF16)
        inv = 1.0 / denom
        lse = m + jnp.log(denom)
        out_t, lse_t = [], []
        for h in range(B_HEADS):
            qs = slice(h * B_SUB, (h + 1) * B_SUB)
            vt = vt_ref[h * HEAD_DIM:(h + 1) * HEAD_DIM, r0:r0 + kw]
            out_t.append(jnp.dot(vt, p[:, qs], preferred_element_type=F32) * inv[:, qs])
            lse_t.append(jnp.broadcast_to(lse[:, qs], (HEAD_DIM, B_SUB)))
        o_ref[r0:r0 + B_SUB, :] = jnp.concatenate(out_t, axis=0).T
        lse_ref[r0:r0 + B_SUB, :] = jnp.concatenate(lse_t, axis=0).T


def _attn_b(arr3d, q_blk, name):
    n_sub, sub_len, _ = arr3d.shape
    tq = min(512, sub_len)
    hw = B_HALF_WINDOW
    per = tq // hw
    n_hw = sub_len // hw

    def main_spec(c):
        return pl.BlockSpec((None, tq, B_DIM), lambda g, i: (g, i, c))

    def prev_spec(c):
        return pl.BlockSpec((None, hw, B_DIM), lambda g, i: (g, jnp.maximum(i * per - 1, 0), c))

    def next_spec(c):
        return pl.BlockSpec((None, hw, B_DIM), lambda g, i: (g, jnp.minimum((i + 1) * per, n_hw - 1), c))

    out_spec = pl.BlockSpec((None, tq, B_DIM), lambda g, i: (g, i, 0))
    return pl.pallas_call(
        functools.partial(_attn_b_kernel, sub_len=sub_len),
        grid=(n_sub, sub_len // tq),
        in_specs=[main_spec(q_blk),
                  prev_spec(q_blk + 1), main_spec(q_blk + 1), next_spec(q_blk + 1),
                  prev_spec(q_blk + 2), main_spec(q_blk + 2), next_spec(q_blk + 2)],
        out_specs=[out_spec, out_spec],
        out_shape=[jax.ShapeDtypeStruct((n_sub, sub_len, B_DIM), F32)] * 2,
        scratch_shapes=[pltpu.VMEM((tq + 2 * hw, B_DIM), BF16), pltpu.VMEM((B_DIM, tq + 2 * hw), BF16)],
        compiler_params=_cparams("parallel", "parallel"),
        name=name,
    )(arr3d, arr3d, arr3d, arr3d, arr3d, arr3d, arr3d)


def _merge_kernel(x_ref, ya_ref, o0_ref, l0_ref, o1_ref, l1_ref, o2_ref, l2_ref, gate_ref,
                  wa_ref, wb_ref, wo_ref, g2_ref, wr_ref, br_ref,
                  x1_ref, h2_ref, code_ref, wcol_ref, cnt_ref,
                  so1_ref, sl1_ref, so2_ref, sl2_ref, yb_ref, run_ref, earlier_ref,
                  wab_ref, wbb_ref, wob_ref):
    tm = x_ref.shape[0]

    @pl.when(pl.program_id(0) == 0)
    def _():
        run_ref[...] = jnp.zeros_like(run_ref)
        wab_ref[...] = wa_ref[...].astype(BF16)
        wbb_ref[...] = wb_ref[...].astype(BF16)
        wob_ref[...] = wo_ref[...].astype(BF16)
        row = lax.broadcasted_iota(I32, (tm, tm), 0)
        col = lax.broadcasted_iota(I32, (tm, tm), 1)
        earlier_ref[...] = (row < col).astype(BF16)

    for dil, src_o, src_l, dst_o, dst_l in ((4, o1_ref, l1_ref, so1_ref, sl1_ref),
                                            (16, o2_ref, l2_ref, so2_ref, sl2_ref)):
        n = tm // dil
        for r in range(dil):
            for c in range(B_DIM // LANES):
                cs = slice(c * LANES, (c + 1) * LANES)
                dst_o[c, pl.ds(r, n, stride=dil), :] = src_o[r, :, cs]
                dst_l[c, pl.ds(r, n, stride=dil), :] = src_l[r, :, cs]

    for c in range(B_DIM // LANES):
        cs = slice(c * LANES, (c + 1) * LANES)
        l0, l1, l2 = l0_ref[:, cs], sl1_ref[c], sl2_ref[c]
        m = jnp.maximum(jnp.maximum(l0, l1), l2)
        e0, e1, e2 = jnp.exp(l0 - m), jnp.exp(l1 - m), jnp.exp(l2 - m)
        yb = (e0 * o0_ref[:, cs] + e1 * so1_ref[c] + e2 * so2_ref[c]) / (e0 + e1 + e2)
        yb_ref[:, cs] = yb.astype(BF16)

    ya_p = jnp.dot(ya_ref[...], wab_ref[...], preferred_element_type=F32)
    yb_p = jnp.dot(yb_ref[...], wbb_ref[...], preferred_element_type=F32)
    merged = gate_ref[:, 0:D_MODEL].astype(F32) * ya_p + gate_ref[:, D_MODEL:GATE_COLS].astype(F32) * yb_p
    x1 = x_ref[...] + jnp.dot(merged.astype(BF16), wob_ref[...], preferred_element_type=F32)
    x1_ref[...] = x1

    h2 = x1 * lax.rsqrt(jnp.mean(x1 * x1, axis=-1, keepdims=True) + NORM_EPS) * g2_ref[...]
    _store_row_tiles(h2_ref, h2)

    h_hi = h2.astype(BF16)
    h_lo = (h2 - h_hi.astype(F32)).astype(BF16)
    w_hi, w_lo = wr_ref[0:D_MODEL], wr_ref[D_MODEL:2 * D_MODEL]
    logits = (jnp.dot(h_hi, w_hi, preferred_element_type=F32)
              + jnp.dot(h_lo, w_hi, preferred_element_type=F32)
              + jnp.dot(h_hi, w_lo, preferred_element_type=F32)) + br_ref[...]

    logits_t = logits.T
    sub = lax.broadcasted_iota(I32, (EXPERTS_PER_GROUP, tm), 0).astype(F32)
    none = float(EXPERTS_PER_GROUP)
    gl = jnp.where(sub < MOE_GROUPS, logits_t[N_EXPERTS:N_EXPERTS + EXPERTS_PER_GROUP], -jnp.inf)
    gmax = jnp.max(gl, axis=0, keepdims=True)
    gidx = jnp.min(jnp.where(gl == gmax, sub, none), axis=0, keepdims=True)
    gw = 1.0 / jnp.sum(jnp.exp(gl - gmax), axis=0, keepdims=True)
    el = logits_t[0:EXPERTS_PER_GROUP]
    for g in range(1, MOE_GROUPS):
        el = jnp.where(gidx == g, logits_t[g * EXPERTS_PER_GROUP:(g + 1) * EXPERTS_PER_GROUP], el)
    v1 = jnp.max(el, axis=0, keepdims=True)
    i1 = jnp.min(jnp.where(el == v1, sub, none), axis=0, keepdims=True)
    el2 = jnp.where(sub == i1, -jnp.inf, el)
    v2 = jnp.max(el2, axis=0, keepdims=True)
    i2 = jnp.min(jnp.where(el2 == v2, sub, none), axis=0, keepdims=True)
    t = jnp.exp(v2 - v1)
    w1 = gw / (1.0 + t)
    w2 = gw * t / (1.0 + t)
    e1 = gidx * EXPERTS_PER_GROUP + i1
    e2 = gidx * EXPERTS_PER_GROUP + i2

    expert = lax.broadcasted_iota(I32, (N_EXPERTS, tm), 0).astype(F32)
    oh1 = (expert == e1).astype(F32)
    oh2 = (expert == e2).astype(F32)
    oh = oh1 + oh2
    run = run_ref[...]
    before = (jnp.dot(oh.astype(BF16), earlier_ref[...], preferred_element_type=F32)
              + jnp.concatenate([run] * (tm // LANES), axis=1))
    rank1 = jnp.sum(before * oh1, axis=0, keepdims=True)
    rank2 = jnp.sum(before * oh2, axis=0, keepdims=True)
    run_ref[...] = run + jnp.sum(oh, axis=1, keepdims=True)
    cnt_ref[...] = run_ref[...]

    scale = float(1 << RANK_BITS)
    zeros = jnp.zeros((6, tm), F32)
    code_ref[...] = jnp.concatenate([e1 * scale + rank1, e2 * scale + rank2, zeros], axis=0).astype(I32)
    w_rows = jnp.concatenate([w1, w2, jnp.zeros((LANES - 2, tm), F32)], axis=0)
    wcol_ref[...] = w_rows.T


def _merge(x2d, ya, o0, l0, o1, l1, o2, l2, gates, wa, wb, wo, g2, wr, br, batch, seq):
    tm = TOK_TILE
    tps = seq // tm
    n_tok = batch * seq

    def tok(c):
        return pl.BlockSpec((tm, c), lambda i: (i, 0))

    def full(a):
        return pl.BlockSpec(a.shape, lambda i: (0,) * a.ndim)

    def res_spec(d):
        return pl.BlockSpec((None, d, tm // d, B_DIM), lambda i: (i // tps, 0, i % tps, 0))

    return pl.pallas_call(
        _merge_kernel,
        grid=(n_tok // tm,),
        in_specs=[tok(D_MODEL), tok(A_Q_DIM), tok(B_DIM), tok(B_DIM),
                  res_spec(4), res_spec(4), res_spec(16), res_spec(16), tok(GATE_COLS),
                  full(wa), full(wb), full(wo), full(g2), full(wr), full(br)],
        out_specs=[tok(D_MODEL), pl.BlockSpec((tm * ROW_CHUNKS, LANES), lambda i: (i, 0)),
                   pl.BlockSpec((8, tm), lambda i: (0, i)),
                   tok(LANES),
                   pl.BlockSpec((N_EXPERTS, LANES), lambda i: (0, 0))],
        out_shape=[jax.ShapeDtypeStruct((n_tok, D_MODEL), F32),
                   jax.ShapeDtypeStruct((n_tok * ROW_CHUNKS, LANES), F32),
                   jax.ShapeDtypeStruct((8, n_tok), I32),
                   jax.ShapeDtypeStruct((n_tok, LANES), F32),
                   jax.ShapeDtypeStruct((N_EXPERTS, LANES), F32)],
        scratch_shapes=([pltpu.VMEM((B_DIM // LANES, tm, LANES), F32)] * 4
                        + [pltpu.VMEM((tm, B_DIM), BF16), pltpu.VMEM((N_EXPERTS, LANES), F32),
                           pltpu.VMEM((tm, tm), BF16),
                           pltpu.VMEM(wa.shape, BF16), pltpu.VMEM(wb.shape, BF16), pltpu.VMEM(wo.shape, BF16)]),
        compiler_params=_cparams("arbitrary"),
        name="merge_route",
    )(x2d, ya, o0, l0, o1, l1, o2, l2, gates, wa, wb, wo, g2, wr, br)


def _row_tile(ref, t):
    return ref.at[pl.ds(pl.multiple_of(t * ROW_CHUNKS, ROW_CHUNKS), ROW_CHUNKS)]


def _store_row_tiles(ref, val):
    rows = val.shape[0]
    for c in range(ROW_CHUNKS):
        ref[pl.ds(c, rows, stride=ROW_CHUNKS), :] = val[:, c * LANES:(c + 1) * LANES]


def _load_row_tiles_chunk(ref, c):
    return ref[pl.ds(c, ref.shape[0] // ROW_CHUNKS, stride=ROW_CHUNKS), :]


def _dispatch_kernel(slot_ref, fill_ref, h_ref, *rest, n_tok, reuse):
    if reuse:
        _, xs_ref, sem = rest
    else:
        xs_ref, zero_ref, sem, zsem = rest
    i = pl.program_id(0)
    rows = h_ref.shape[0] // ROW_CHUNKS

    if not reuse:
        @pl.when(i == 0)
        def _():
            tile_rows = zero_ref.shape[0]
            zero_ref[...] = jnp.zeros_like(zero_ref)

            def zero_copy(t):
                dst = xs_ref.at[pl.ds(pl.multiple_of(t * tile_rows, tile_rows), tile_rows)]
                return pltpu.make_async_copy(zero_ref, dst, zsem)

            def start(t, carry):
                @pl.when(fill_ref[t] != 0)
                def _():
                    zero_copy(t).start()
                return carry

            def wait(t, carry):
                @pl.when(fill_ref[t] != 0)
                def _():
                    zero_copy(t).wait()
                return carry

            n_tiles = xs_ref.shape[0] // tile_rows
            lax.fori_loop(0, n_tiles, start, 0)
            lax.fori_loop(0, n_tiles, wait, 0)

    def issue(j, carry):
        t = i * rows + j
        for k in range(2):
            pltpu.make_async_copy(_row_tile(h_ref, j), _row_tile(xs_ref, slot_ref[k * n_tok + t]), sem).start(priority=k)
        return carry

    lax.fori_loop(0, rows, issue, 0, unroll=DMA_UNROLL)
    for _ in range(2):
        pltpu.make_async_copy(h_ref, xs_ref.at[pl.ds(0, rows * ROW_CHUNKS)], sem).wait()


def _dispatch(slots, fill, h2, xs_prev, n_slots):
    n_tok = h2.shape[0] // ROW_CHUNKS
    rows = DISPATCH_TILE
    reuse = xs_prev is not None
    h_spec = pl.BlockSpec((rows * ROW_CHUNKS, LANES), lambda i, s, f: (i, 0))
    any_spec = pl.BlockSpec(memory_space=pl.ANY)
    scratch = [pltpu.SemaphoreType.DMA(())] if reuse else [
        pltpu.VMEM((EXP_TILE * ROW_CHUNKS, LANES), F32), pltpu.SemaphoreType.DMA(()), pltpu.SemaphoreType.DMA(())]
    return pl.pallas_call(
        functools.partial(_dispatch_kernel, n_tok=n_tok, reuse=reuse),
        grid_spec=pltpu.PrefetchScalarGridSpec(
            num_scalar_prefetch=2,
            grid=(n_tok // rows,),
            in_specs=[h_spec, any_spec] if reuse else [h_spec],
            out_specs=any_spec,
            scratch_shapes=scratch,
        ),
        out_shape=jax.ShapeDtypeStruct((n_slots * ROW_CHUNKS, LANES), F32),
        input_output_aliases={3: 0} if reuse else {},
        compiler_params=_cparams("arbitrary"),
        name="dispatch_reuse" if reuse else "dispatch",
    )(*((slots, fill, h2, xs_prev) if reuse else (slots, fill, h2)))


def _combine_kernel(slot_ref, x_ref, w_ref, g_ref, ys_ref, o_ref, buf_ref, y_ref, sem, *, n_tok, final):
    i = pl.program_id(0)
    n_steps = pl.num_programs(0)
    rows = x_ref.shape[0]

    def issue_tile(tile, slot):
        def issue(j, carry):
            t = tile * rows + j
            for k in range(2):
                pltpu.make_async_copy(_row_tile(ys_ref, slot_ref[k * n_tok + t]),
                                      _row_tile(buf_ref.at[slot, k], j), sem.at[slot]).start(priority=k)
            return carry

        lax.fori_loop(0, rows, issue, 0, unroll=DMA_UNROLL)

    @pl.when(i == 0)
    def _():
        issue_tile(0, 0)

    @pl.when(i + 1 < n_steps)
    def _():
        issue_tile(i + 1, (i + 1) % 2)

    cur = i % 2
    for k in range(2):
        pltpu.make_async_copy(ys_ref.at[pl.ds(0, rows * ROW_CHUNKS)], buf_ref.at[cur, k], sem.at[cur]).wait()
    w = w_ref[...]
    w1, w2 = w[:, 0:1], w[:, 1:2]
    for c in range(ROW_CHUNKS):
        cs = slice(c * LANES, (c + 1) * LANES)
        y1 = _load_row_tiles_chunk(buf_ref.at[cur, 0], c)
        y2 = _load_row_tiles_chunk(buf_ref.at[cur, 1], c)
        (y_ref if final else o_ref)[:, cs] = x_ref[:, cs] + w1 * y1 + w2 * y2
    if final:
        y = y_ref[...]
        o_ref[...] = y * lax.rsqrt(jnp.mean(y * y, axis=-1, keepdims=True) + NORM_EPS) * g_ref[...]


def _combine(slots, x1, wcol, g, ys, final):
    n_tok = x1.shape[0]
    rows = ROW_TILE
    return pl.pallas_call(
        functools.partial(_combine_kernel, n_tok=n_tok, final=final),
        grid_spec=pltpu.PrefetchScalarGridSpec(
            num_scalar_prefetch=1,
            grid=(n_tok // rows,),
            in_specs=[pl.BlockSpec((rows, D_MODEL), lambda i, s: (i, 0)),
                      pl.BlockSpec((rows, LANES), lambda i, s: (i, 0)),
                      pl.BlockSpec((1, D_MODEL), lambda i, s: (0, 0)),
                      pl.BlockSpec(memory_space=pl.ANY)],
            out_specs=pl.BlockSpec((rows, D_MODEL), lambda i, s: (i, 0)),
            scratch_shapes=[pltpu.VMEM((2, 2, rows * ROW_CHUNKS, LANES), F32),
                            pltpu.VMEM((rows, D_MODEL), F32),
                            pltpu.SemaphoreType.DMA((2,))],
        ),
        out_shape=jax.ShapeDtypeStruct((n_tok, D_MODEL), F32),
        compiler_params=_cparams("arbitrary"),
        name="combine",
    )(slots, x1, wcol, g, ys)


def _experts_kernel(tile_e_ref, n_used_ref, xs_ref, wg_ref, wu_ref, wd_ref, ys_ref, xbuf_ref, x_ref, sem):
    del tile_e_ref
    i = pl.program_id(0)
    n_used = n_used_ref[0]
    tile_rows = xbuf_ref.shape[1]

    def fetch(tile):
        rows = pl.ds(pl.multiple_of(tile * tile_rows, tile_rows), tile_rows)
        slot = tile % XS_SLOTS
        return pltpu.make_async_copy(xs_ref.at[rows], xbuf_ref.at[slot], sem.at[slot])

    @pl.when(i == 0)
    def _():
        fetch(0).start()

        @pl.when(n_used > 1)
        def _():
            fetch(1).start()

    @pl.when(i < n_used)
    def _():
        @pl.when(i + XS_SLOTS - 1 < n_used)
        def _():
            fetch(i + XS_SLOTS - 1).start()

        fetch(i).wait()
        xb = xbuf_ref.at[i % XS_SLOTS]
        for c in range(ROW_CHUNKS):
            x_ref[:, c * LANES:(c + 1) * LANES] = _load_row_tiles_chunk(xb, c).astype(BF16)
        x = x_ref[...]
        a = jnp.dot(x, wg_ref[...].astype(BF16), preferred_element_type=F32)
        u = jnp.dot(x, wu_ref[...].astype(BF16), preferred_element_type=F32)
        z = (a * jax.nn.sigmoid(a)) * u
        _store_row_tiles(ys_ref, jnp.dot(z.astype(BF16), wd_ref[...].astype(BF16), preferred_element_type=F32))

    @pl.when(i >= n_used)
    def _():
        ys_ref[...] = jnp.zeros_like(ys_ref)


def _experts(tile_e, n_used, xs, wg, wu, wd, layer):
    n_slots = xs.shape[0] // ROW_CHUNKS
    te = EXP_TILE

    def out_map(i, tile_e, n_used):
        return (i, 0)

    def w_map(i, tile_e, n_used):
        return (layer, tile_e[i], 0, 0)

    return pl.pallas_call(
        _experts_kernel,
        grid_spec=pltpu.PrefetchScalarGridSpec(
            num_scalar_prefetch=2,
            grid=(n_slots // te,),
            in_specs=[pl.BlockSpec(memory_space=pl.ANY),
                      pl.BlockSpec((None, None, D_MODEL, D_EXPERT), w_map),
                      pl.BlockSpec((None, None, D_MODEL, D_EXPERT), w_map),
                      pl.BlockSpec((None, None, D_EXPERT, D_MODEL), w_map)],
            out_specs=pl.BlockSpec((te * ROW_CHUNKS, LANES), out_map),
            scratch_shapes=[pltpu.VMEM((XS_SLOTS, te * ROW_CHUNKS, LANES), F32),
                            pltpu.VMEM((te, D_MODEL), BF16),
                            pltpu.SemaphoreType.DMA((XS_SLOTS,))],
        ),
        out_shape=jax.ShapeDtypeStruct((n_slots * ROW_CHUNKS, LANES), F32),
        compiler_params=_cparams("arbitrary"),
        name="experts",
    )(tile_e, n_used, xs, wg, wu, wd)


def kernel(x, attn_norm_g, w_in, a_sink, w_branch_a, w_branch_b, w_out, ffn_norm_g,
           w_router_group, b_router_group, w_router_expert, b_router_expert,
           w_exp_gate, w_exp_up, w_exp_down, final_norm_g):
    batch, seq, d_model = x.shape
    depth = w_in.shape[0]
    n_tok = batch * seq
    assert d_model == D_MODEL and w_in.shape[2] == D_IN
    assert seq % (16 * B_SUB) == 0 and seq % IN_TILE == 0 and n_tok % DISPATCH_TILE == 0
    assert n_tok < (1 << RANK_BITS)

    cos_t, sin_t = _rope_tables(seq)
    tables = (cos_t, sin_t,
              _residue_order(cos_t, 4, IN_TILE), _residue_order(sin_t, 4, IN_TILE),
              _residue_order(cos_t, 16, IN_TILE), _residue_order(sin_t, 16, IN_TILE))

    n_slots = 2 * n_tok + N_EXPERTS * EXP_TILE
    n_tiles = n_slots // EXP_TILE
    x2d = x.reshape(n_tok, D_MODEL)

    for l in range(depth):
        nat, gates, grp1, grp2 = _in_proj(x2d, attn_norm_g[l][None, :], w_in[l].astype(BF16), tables, batch, seq)
        nat3d = nat.reshape(batch, seq, NAT_COLS)
        ya = _attn_a(nat3d, a_sink[l]).reshape(n_tok, A_Q_DIM)
        o0, l0 = _attn_b(nat3d, NAT_B // B_DIM, "attn_b1")
        o1, l1 = _attn_b(grp1.reshape(batch * 4, seq // 4, GRP_COLS), 0, "attn_b4")
        o2, l2 = _attn_b(grp2.reshape(batch * 16, seq // 16, GRP_COLS), 0, "attn_b16")

        wr = jnp.zeros((D_MODEL, LANES), F32)
        wr = wr.at[:, 0:N_EXPERTS].set(w_router_expert[l]).at[:, N_EXPERTS:N_EXPERTS + MOE_GROUPS].set(w_router_group[l])
        br = jnp.zeros((1, LANES), F32)
        br = br.at[0, 0:N_EXPERTS].set(b_router_expert[l]).at[0, N_EXPERTS:N_EXPERTS + MOE_GROUPS].set(b_router_group[l])
        wr_hi = wr.astype(BF16)
        wr_lo = (wr - wr_hi.astype(F32)).astype(BF16)
        wr_stack = jnp.concatenate([wr_hi, wr_lo], axis=0)

        x1, h2, code, wcol, cnt = _merge(
            x2d, ya, o0.reshape(n_tok, B_DIM), l0.reshape(n_tok, B_DIM),
            o1.reshape(batch, 4, seq // 4, B_DIM), l1.reshape(batch, 4, seq // 4, B_DIM),
            o2.reshape(batch, 16, seq // 16, B_DIM), l2.reshape(batch, 16, seq // 16, B_DIM),
            gates, w_branch_a[l], w_branch_b[l], w_out[l],
            ffn_norm_g[l][None, :], wr_stack, br, batch, seq)

        counts = cnt[:, 0].astype(I32)
        padded = ((counts + EXP_TILE - 1) // EXP_TILE) * EXP_TILE
        ends = jnp.cumsum(padded)
        offs = ends - padded
        n_used = (ends[-1:] // EXP_TILE).astype(I32)
        tile_start = jnp.arange(n_tiles, dtype=I32) * EXP_TILE
        tile_e = jnp.minimum(jnp.sum((ends[None, :] <= tile_start[:, None]).astype(I32), axis=1), N_EXPERTS - 1)
        eid = code[0:2] >> RANK_BITS
        rank = code[0:2] & ((1 << RANK_BITS) - 1)
        expert_ids = jnp.arange(N_EXPERTS, dtype=I32)[:, None, None]
        slots = (rank + jnp.sum(jnp.where(eid[None] == expert_ids, offs[:, None, None], 0), axis=0)).reshape(-1)

        fill = ((tile_start + EXP_TILE == ends[tile_e]) | (tile_start >= ends[-1])).astype(I32)
        xs = _dispatch(slots, fill, h2, None if l == 0 else xs, n_slots)
        ys = _experts(tile_e, n_used, xs, w_exp_gate, w_exp_up, w_exp_down, l)
        x2d = _combine(slots, x1, wcol, final_norm_g[None, :], ys, final=(l == depth - 1))

    return x2d.reshape(batch, seq, D_MODEL)
```

```python
import functools

import jax
import jax.numpy as jnp
from jax import lax
from jax.experimental import pallas as pl
from jax.experimental.pallas import tpu as pltpu

F32 = jnp.float32
BF16 = jnp.bfloat16
I32 = jnp.int32

D_MODEL = 1024
HEAD_DIM = 64
HALF_HEAD = HEAD_DIM // 2
ROPE_THETA = 10000.0
NORM_EPS = 1e-6
NEG_INF = -1e30
LANES = 128

A_Q_HEADS = 8
A_KV_HEADS = 2
A_GROUP = A_Q_HEADS // A_KV_HEADS
A_HALF_WINDOW = 128
A_Q_DIM = A_Q_HEADS * HEAD_DIM
A_KV_DIM = A_KV_HEADS * HEAD_DIM

B_GROUPS = ((128, 1), (512, 4), (2048, 16))
B_HEADS = 4
B_DIM = B_HEADS * HEAD_DIM
B_HALF_WINDOW = 64

MOE_GROUPS = 4
EXPERTS_PER_GROUP = 8
N_EXPERTS = MOE_GROUPS * EXPERTS_PER_GROUP
D_EXPERT = 256

NAT_IN_COLS = A_Q_DIM + 2 * A_KV_DIM + 3 * B_DIM
NAT_B = A_Q_DIM
NAT_AK = NAT_B + 3 * B_DIM
NAT_AV = NAT_AK + 2 * A_KV_DIM
NAT_COLS = NAT_AV + A_KV_DIM
GRP_COLS = 3 * B_DIM
COL_G1 = NAT_IN_COLS
COL_G2 = COL_G1 + GRP_COLS
COL_GATE = COL_G2 + GRP_COLS
GATE_COLS = 2 * D_MODEL
D_IN = COL_GATE + GATE_COLS

TOK_TILE = 512
IN_TILE = 512
A_Q_TILE = 512
B_SUB = 128
EXP_TILE = 512
XS_SLOTS = 3
ROW_CHUNKS = D_MODEL // LANES
ROW_TILE = 256
DISPATCH_TILE = 1024
DMA_UNROLL = 8
RANK_BITS = 16
VMEM_LIMIT = 56 * 1024 * 1024


def _cparams(*sem):
    return pltpu.CompilerParams(dimension_semantics=sem, vmem_limit_bytes=VMEM_LIMIT)


def _rope_tables(seq_len):
    inv = 1.0 / (ROPE_THETA ** (jnp.arange(0, HEAD_DIM, 2, dtype=F32) / HEAD_DIM))
    ang = jnp.arange(seq_len, dtype=F32)[:, None] * inv[None, :]
    cos, sin = jnp.cos(ang), jnp.sin(ang)
    cos_t = jnp.concatenate([cos, cos, cos, cos], axis=-1)
    sin_t = jnp.concatenate([-sin, sin, -sin, sin], axis=-1)
    return cos_t, sin_t


def _residue_order(table, dilation, tile):
    s, c = table.shape
    return table.reshape(s // tile, tile // dilation, dilation, c).transpose(0, 2, 1, 3).reshape(s, c)


def _rope(t, cos, sin_signed, first_half):
    partner = jnp.where(first_half, pltpu.roll(t, LANES - HALF_HEAD, 1), pltpu.roll(t, HALF_HEAD, 1))
    return t * cos + partner * sin_signed


Q_KIND, K_KIND, V_KIND = 0, 1, 2
_NAT_KINDS = ([Q_KIND] * 4 + [K_KIND] + [V_KIND] + [Q_KIND] * 2 + [K_KIND] * 2 + [V_KIND] * 2)
_GRP_KINDS = [Q_KIND] * 2 + [K_KIND] * 2 + [V_KIND] * 2


def _in_proj_kernel(x_ref, g_ref, w_ref, c1_ref, s1_ref, c4_ref, s4_ref, c16_ref, s16_ref,
                    nat_ref, gate_ref, g1_ref, g2_ref, hf_ref, hb_ref, hd_ref):
    tm = x_ref.shape[0]
    x = x_ref[...]
    h = x * lax.rsqrt(jnp.mean(x * x, axis=-1, keepdims=True) + NORM_EPS) * g_ref[...]
    n_chunks = D_MODEL // LANES
    for c in range(n_chunks):
        hf_ref[c] = h[:, c * LANES:(c + 1) * LANES]
    hb_ref[...] = h.astype(BF16)
    lane = lax.broadcasted_iota(I32, (1, LANES), 1)
    first_half = (lane % HEAD_DIM) < HALF_HEAD

    def project(h_b, col0, kinds, cos_ref, sin_ref, store):
        width = 512
        for c0 in range(0, len(kinds) * LANES, width):
            w = min(width, len(kinds) * LANES - c0)
            res = jnp.dot(h_b, w_ref[:, col0 + c0:col0 + c0 + w], preferred_element_type=F32)
            for j in range(w // LANES):
                kind = kinds[(c0 // LANES) + j]
                t = res[:, j * LANES:(j + 1) * LANES]
                if kind != V_KIND:
                    t = _rope(t, cos_ref[...], sin_ref[...], first_half)
                if kind == Q_KIND:
                    t = t * (HEAD_DIM ** -0.5)
                store(c0 + j * LANES, t)

    low_head = lane < HEAD_DIM

    def store_nat(c, t):
        if A_Q_DIM <= c < A_Q_DIM + A_KV_DIM:
            swapped = pltpu.roll(t, HEAD_DIM, 1)
            nat_ref[:, NAT_AK:NAT_AK + LANES] = jnp.where(low_head, t, swapped).astype(BF16)
            nat_ref[:, NAT_AK + LANES:NAT_AK + 2 * LANES] = jnp.where(low_head, swapped, t).astype(BF16)
        elif c < A_Q_DIM + 2 * A_KV_DIM:
            out = c if c < A_Q_DIM else NAT_AV
            nat_ref[:, out:out + LANES] = t.astype(BF16)
        else:
            out = c - 2 * A_KV_DIM
            nat_ref[:, out:out + LANES] = t.astype(BF16)

    project(hb_ref[...], 0, _NAT_KINDS, c1_ref, s1_ref, store_nat)

    for c0 in range(0, GATE_COLS, 512):
        res = jnp.dot(hb_ref[...], w_ref[:, COL_GATE + c0:COL_GATE + c0 + 512], preferred_element_type=F32)
        gate_ref[:, c0:c0 + 512] = jax.nn.sigmoid(res).astype(BF16)

    for dil, col0, cos_ref, sin_ref, out_ref in ((4, COL_G1, c4_ref, s4_ref, g1_ref),
                                                 (16, COL_G2, c16_ref, s16_ref, g2_ref)):
        n = tm // dil
        for r in range(dil):
            for c in range(n_chunks):
                hd_ref[r * n:(r + 1) * n, c * LANES:(c + 1) * LANES] = (
                    hf_ref[c, pl.ds(r, n, stride=dil), :].astype(BF16))

        def store_grp(c, t, out_ref=out_ref, dil=dil, n=n):
            v = t.astype(BF16)
            for r in range(dil):
                out_ref[r, :, c:c + LANES] = v[r * n:(r + 1) * n]

        project(hd_ref[...], col0, _GRP_KINDS, cos_ref, sin_ref, store_grp)


def _in_proj(x2d, g, w_bf16, tables, batch, seq):
    tm = IN_TILE
    tiles_per_seq = seq // tm
    n_tok = batch * seq
    c1, s1, c4, s4, c16, s16 = tables
    tab_spec = pl.BlockSpec((tm, LANES), lambda i: (i % tiles_per_seq, 0))
    return pl.pallas_call(
        _in_proj_kernel,
        grid=(n_tok // tm,),
        in_specs=[
            pl.BlockSpec((tm, D_MODEL), lambda i: (i, 0)),
            pl.BlockSpec((1, D_MODEL), lambda i: (0, 0)),
            pl.BlockSpec((D_MODEL, D_IN), lambda i: (0, 0), pipeline_mode=pl.Buffered(1)),
            tab_spec, tab_spec, tab_spec, tab_spec, tab_spec, tab_spec,
        ],
        out_specs=[
            pl.BlockSpec((tm, NAT_COLS), lambda i: (i, 0)),
            pl.BlockSpec((tm, GATE_COLS), lambda i: (i, 0)),
            pl.BlockSpec((None, 4, tm // 4, GRP_COLS), lambda i: (i // tiles_per_seq, 0, i % tiles_per_seq, 0)),
            pl.BlockSpec((None, 16, tm // 16, GRP_COLS), lambda i: (i // tiles_per_seq, 0, i % tiles_per_seq, 0)),
        ],
        out_shape=[
            jax.ShapeDtypeStruct((n_tok, NAT_COLS), BF16),
            jax.ShapeDtypeStruct((n_tok, GATE_COLS), BF16),
            jax.ShapeDtypeStruct((batch, 4, seq // 4, GRP_COLS), BF16),
            jax.ShapeDtypeStruct((batch, 16, seq // 16, GRP_COLS), BF16),
        ],
        scratch_shapes=[
            pltpu.VMEM((D_MODEL // LANES, tm, LANES), F32),
            pltpu.VMEM((tm, D_MODEL), BF16),
            pltpu.VMEM((tm, D_MODEL), BF16),
        ],
        compiler_params=_cparams("parallel"),
        name="in_proj",
    )(x2d, g, w_bf16, c1, s1, c4, s4, c16, s16)


def _masked_heads(q_pair, low_head, high_head):
    zero = jnp.zeros_like(q_pair)
    return [jnp.where(low_head, q_pair, zero), jnp.where(high_head, q_pair, zero)]


def _attn_a_kernel(sink_ref, q_ref, kp_ref, km_ref, kn_ref, vp_ref, vm_ref, vn_ref, o_ref, k_ref, vt_ref, *, seq):
    tq = q_ref.shape[0]
    hw = A_HALF_WINDOW
    n_sub = tq // hw
    i = pl.program_id(1)
    last_blk = seq // hw - 1
    k_ref[0:hw, :] = kp_ref[...]
    k_ref[hw:hw + tq, :] = km_ref[...]
    k_ref[hw + tq:tq + 2 * hw, :] = kn_ref[...]
    for r0, src in ((0, vp_ref), (hw, vm_ref), (hw + tq, vn_ref)):
        vt_ref[:, r0:r0 + src.shape[0]] = src[...].astype(F32).T.astype(BF16)
    grp_cols = A_GROUP * hw
    key = lax.broadcasted_iota(I32, (hw, grp_cols), 0)
    qry = lax.broadcasted_iota(I32, (hw, grp_cols), 1) % hw
    low_head = lax.broadcasted_iota(I32, (1, LANES), 1) < HEAD_DIM
    high_head = jnp.logical_not(low_head)
    for sb in range(n_sub):
        r0 = sb * hw
        blk = i * n_sub + sb
        mask_p = (key >= qry) if sb > 0 else (key >= qry + jnp.where(blk > 0, 0, hw))
        mask_n = (key <= qry) if sb < n_sub - 1 else (key <= qry - jnp.where(blk < last_blk, 0, hw))
        out_t = []
        for g in range(A_KV_HEADS):
            heads = range(g * A_GROUP, (g + 1) * A_GROUP)
            q_parts = []
            for c in range(g * A_GROUP // 2, (g + 1) * A_GROUP // 2):
                q_parts += _masked_heads(q_ref[r0:r0 + hw, c * LANES:(c + 1) * LANES], low_head, high_head)
            q = jnp.concatenate(q_parts, axis=0)
            sink = jnp.concatenate([jnp.full((1, hw), sink_ref[h], F32) for h in heads], axis=1)
            k = k_ref[r0:r0 + 3 * hw, g * LANES:(g + 1) * LANES]
            s = lax.dot_general(k, q, (((1,), (1,)), ((), ())), preferred_element_type=F32)
            sp = jnp.where(mask_p, s[0:hw], NEG_INF)
            so = s[hw:2 * hw]
            sn = jnp.where(mask_n, s[2 * hw:3 * hw], NEG_INF)
            m = jnp.max(jnp.maximum(jnp.maximum(sp, so), sn), axis=0, keepdims=True)
            m = jnp.maximum(m, sink)
            pp, po, pn = jnp.exp(sp - m), jnp.exp(so - m), jnp.exp(sn - m)
            denom = jnp.sum(pp + po + pn, axis=0, keepdims=True) + jnp.exp(sink - m)
            p = jnp.concatenate([pp, po, pn], axis=0).astype(BF16)
            vt = vt_ref[g * HEAD_DIM:(g + 1) * HEAD_DIM, r0:r0 + 3 * hw]
            o = jnp.dot(vt, p, preferred_element_type=F32) * (1.0 / denom)
            out_t += [o[:, j * hw:(j + 1) * hw] for j in range(A_GROUP)]
        o_ref[r0:r0 + hw, :] = jnp.concatenate(out_t, axis=0).T.astype(BF16)


def _attn_a(nat3d, sink):
    batch, seq, _ = nat3d.shape
    tq = A_Q_TILE
    hw = A_HALF_WINDOW
    per = tq // hw
    n_hw = seq // hw
    k_cols, v_cols = 2 * A_KV_DIM, A_KV_DIM
    k_blk, v_blk = NAT_AK // k_cols, NAT_AV // v_cols

    def prev_spec(cols, blk):
        return pl.BlockSpec((None, hw, cols), lambda b, i: (b, jnp.maximum(i * per - 1, 0), blk))

    def main_spec(cols, blk):
        return pl.BlockSpec((None, tq, cols), lambda b, i: (b, i, blk))

    def next_spec(cols, blk):
        return pl.BlockSpec((None, hw, cols), lambda b, i: (b, jnp.minimum((i + 1) * per, n_hw - 1), blk))

    return pl.pallas_call(
        functools.partial(_attn_a_kernel, seq=seq),
        grid=(batch, seq // tq),
        in_specs=[
            pl.BlockSpec(memory_space=pltpu.SMEM),
            main_spec(A_Q_DIM, 0),
            prev_spec(k_cols, k_blk), main_spec(k_cols, k_blk), next_spec(k_cols, k_blk),
            prev_spec(v_cols, v_blk), main_spec(v_cols, v_blk), next_spec(v_cols, v_blk),
        ],
        out_specs=pl.BlockSpec((None, tq, A_Q_DIM), lambda b, i: (b, i, 0)),
        out_shape=jax.ShapeDtypeStruct((batch, seq, A_Q_DIM), BF16),
        scratch_shapes=[pltpu.VMEM((tq + 2 * hw, k_cols), BF16), pltpu.VMEM((v_cols, tq + 2 * hw), BF16)],
        compiler_params=_cparams("parallel", "parallel"),
        name="attn_a",
    )(sink, nat3d, nat3d, nat3d, nat3d, nat3d, nat3d, nat3d)


def _attn_b_kernel(q_ref, kp_ref, km_ref, kn_ref, vp_ref, vm_ref, vn_ref, o_ref, lse_ref,
                   k_ref, vt_ref, *, sub_len):
    tq = q_ref.shape[0]
    hw = B_HALF_WINDOW
    t0 = pl.program_id(1) * tq
    k_ref[0:hw, :] = kp_ref[...]
    k_ref[hw:hw + tq, :] = km_ref[...]
    k_ref[hw + tq:tq + 2 * hw, :] = kn_ref[...]
    for r0, src in ((0, vp_ref), (hw, vm_ref), (hw + tq, vn_ref)):
        vt_ref[:, r0:r0 + src.shape[0]] = src[...].astype(F32).T.astype(BF16)
    kw = B_SUB + 2 * hw
    n_sub = tq // B_SUB
    all_cols = B_HEADS * B_SUB
    key = lax.broadcasted_iota(I32, (kw, all_cols), 0)
    qry = lax.broadcasted_iota(I32, (kw, all_cols), 1) % B_SUB
    in_band = jnp.abs(key - hw - qry) <= hw
    low_head = lax.broadcasted_iota(I32, (1, LANES), 1) < HEAD_DIM
    high_head = jnp.logical_not(low_head)
    for sb in range(n_sub):
        r0 = sb * B_SUB
        valid = in_band
        if sb == 0:
            valid = valid & (t0 - hw + key >= 0)
        if sb == n_sub - 1:
            valid = valid & (t0 + r0 - hw + key < sub_len)
        s_parts = []
        for c in range(B_HEADS // 2):
            cs = slice(c * LANES, (c + 1) * LANES)
            q = jnp.concatenate(_masked_heads(q_ref[r0:r0 + B_SUB, cs], low_head, high_head), axis=0)
            s_parts.append(lax.dot_general(k_ref[r0:r0 + kw, cs], q, (((1,), (1,)), ((), ())),
                                           preferred_element_type=F32))
        s = jnp.where(valid, jnp.concatenate(s_parts, axis=1), NEG_INF)
        m = jnp.max(s, axis=0, keepdims=True)
        p = jnp.exp(s - m)
        denom = jnp.sum(p, axis=0, keepdims=True)
        p = p.astype(BF16)
        inv = 1.0 / denom
        lse = m + jnp.log(denom)
        out_t, lse_t = [], []
        for h in range(B_HEADS):
            qs = slice(h * B_SUB, (h + 1) * B_SUB)
            vt = vt_ref[h * HEAD_DIM:(h + 1) * HEAD_DIM, r0:r0 + kw]
            out_t.append(jnp.dot(vt, p[:, qs], preferred_element_type=F32) * inv[:, qs])
            lse_t.append(jnp.broadcast_to(lse[:, qs], (HEAD_DIM, B_SUB)))
        o_ref[r0:r0 + B_SUB, :] = jnp.concatenate(out_t, axis=0).T
        lse_ref[r0:r0 + B_SUB, :] = jnp.concatenate(lse_t, axis=0).T


def _attn_b(arr3d, q_blk, name):
    n_sub, sub_len, _ = arr3d.shape
    tq = min(512, sub_len)
    hw = B_HALF_WINDOW
    per = tq // hw
    n_hw = sub_len // hw

    def main_spec(c):
        return pl.BlockSpec((None, tq, B_DIM), lambda g, i: (g, i, c))

    def prev_spec(c):
        return pl.BlockSpec((None, hw, B_DIM), lambda g, i: (g, jnp.maximum(i * per - 1, 0), c))

    def next_spec(c):
        return pl.BlockSpec((None, hw, B_DIM), lambda g, i: (g, jnp.minimum((i + 1) * per, n_hw - 1), c))

    out_spec = pl.BlockSpec((None, tq, B_DIM), lambda g, i: (g, i, 0))
    return pl.pallas_call(
        functools.partial(_attn_b_kernel, sub_len=sub_len),
        grid=(n_sub, sub_len // tq),
        in_specs=[main_spec(q_blk),
                  prev_spec(q_blk + 1), main_spec(q_blk + 1), next_spec(q_blk + 1),
                  prev_spec(q_blk + 2), main_spec(q_blk + 2), next_spec(q_blk + 2)],
        out_specs=[out_spec, out_spec],
        out_shape=[jax.ShapeDtypeStruct((n_sub, sub_len, B_DIM), F32)] * 2,
        scratch_shapes=[pltpu.VMEM((tq + 2 * hw, B_DIM), BF16), pltpu.VMEM((B_DIM, tq + 2 * hw), BF16)],
        compiler_params=_cparams("parallel", "parallel"),
        name=name,
    )(arr3d, arr3d, arr3d, arr3d, arr3d, arr3d, arr3d)


def _merge_kernel(x_ref, ya_ref, o0_ref, l0_ref, o1_ref, l1_ref, o2_ref, l2_ref, gate_ref,
                  wa_ref, wb_ref, wo_ref, g2_ref, wr_ref, br_ref,
                  x1_ref, h2_ref, code_ref, wcol_ref, cnt_ref,
                  so1_ref, sl1_ref, so2_ref, sl2_ref, yb_ref, run_ref, earlier_ref,
                  wab_ref, wbb_ref, wob_ref):
    tm = x_ref.shape[0]

    @pl.when(pl.program_id(0) == 0)
    def _():
        run_ref[...] = jnp.zeros_like(run_ref)
        wab_ref[...] = wa_ref[...].astype(BF16)
        wbb_ref[...] = wb_ref[...].astype(BF16)
        wob_ref[...] = wo_ref[...].astype(BF16)
        row = lax.broadcasted_iota(I32, (tm, tm), 0)
        col = lax.broadcasted_iota(I32, (tm, tm), 1)
        earlier_ref[...] = (row < col).astype(BF16)

    for dil, src_o, src_l, dst_o, dst_l in ((4, o1_ref, l1_ref, so1_ref, sl1_ref),
                                            (16, o2_ref, l2_ref, so2_ref, sl2_ref)):
        n = tm // dil
        for r in range(dil):
            for c in range(B_DIM // LANES):
                cs = slice(c * LANES, (c + 1) * LANES)
                dst_o[c, pl.ds(r, n, stride=dil), :] = src_o[r, :, cs]
                dst_l[c, pl.ds(r, n, stride=dil), :] = src_l[r, :, cs]

    for c in range(B_DIM // LANES):
        cs = slice(c * LANES, (c + 1) * LANES)
        l0, l1, l2 = l0_ref[:, cs], sl1_ref[c], sl2_ref[c]
        m = jnp.maximum(jnp.maximum(l0, l1), l2)
        e0, e1, e2 = jnp.exp(l0 - m), jnp.exp(l1 - m), jnp.exp(l2 - m)
        yb = (e0 * o0_ref[:, cs] + e1 * so1_ref[c] + e2 * so2_ref[c]) / (e0 + e1 + e2)
        yb_ref[:, cs] = yb.astype(BF16)

    ya_p = jnp.dot(ya_ref[...], wab_ref[...], preferred_element_type=F32)
    yb_p = jnp.dot(yb_ref[...], wbb_ref[...], preferred_element_type=F32)
    merged = gate_ref[:, 0:D_MODEL].astype(F32) * ya_p + gate_ref[:, D_MODEL:GATE_COLS].astype(F32) * yb_p
    x1 = x_ref[...] + jnp.dot(merged.astype(BF16), wob_ref[...], preferred_element_type=F32)
    x1_ref[...] = x1

    h2 = x1 * lax.rsqrt(jnp.mean(x1 * x1, axis=-1, keepdims=True) + NORM_EPS) * g2_ref[...]
    _store_row_tiles(h2_ref, h2)

    h_hi = h2.astype(BF16)
    h_lo = (h2 - h_hi.astype(F32)).astype(BF16)
    w_hi, w_lo = wr_ref[0:D_MODEL], wr_ref[D_MODEL:2 * D_MODEL]
    logits = (jnp.dot(h_hi, w_hi, preferred_element_type=F32)
              + jnp.dot(h_lo, w_hi, preferred_element_type=F32)
              + jnp.dot(h_hi, w_lo, preferred_element_type=F32)) + br_ref[...]

    logits_t = logits.T
    sub = lax.broadcasted_iota(I32, (EXPERTS_PER_GROUP, tm), 0).astype(F32)
    none = float(EXPERTS_PER_GROUP)
    gl = jnp.where(sub < MOE_GROUPS, logits_t[N_EXPERTS:N_EXPERTS + EXPERTS_PER_GROUP], -jnp.inf)
    gmax = jnp.max(gl, axis=0, keepdims=True)
    gidx = jnp.min(jnp.where(gl == gmax, sub, none), axis=0, keepdims=True)
    gw = 1.0 / jnp.sum(jnp.exp(gl - gmax), axis=0, keepdims=True)
    el = logits_t[0:EXPERTS_PER_GROUP]
    for g in range(1, MOE_GROUPS):
        el = jnp.where(gidx == g, logits_t[g * EXPERTS_PER_GROUP:(g + 1) * EXPERTS_PER_GROUP], el)
    v1 = jnp.max(el, axis=0, keepdims=True)
    i1 = jnp.min(jnp.where(el == v1, sub, none), axis=0, keepdims=True)
    el2 = jnp.where(sub == i1, -jnp.inf, el)
    v2 = jnp.max(el2, axis=0, keepdims=True)
    i2 = jnp.min(jnp.where(el2 == v2, sub, none), axis=0, keepdims=True)
    t = jnp.exp(v2 - v1)
    w1 = gw / (1.0 + t)
    w2 = gw * t / (1.0 + t)
    e1 = gidx * EXPERTS_PER_GROUP + i1
    e2 = gidx * EXPERTS_PER_GROUP + i2

    expert = lax.broadcasted_iota(I32, (N_EXPERTS, tm), 0).astype(F32)
    oh1 = (expert == e1).astype(F32)
    oh2 = (expert == e2).astype(F32)
    oh = oh1 + oh2
    run = run_ref[...]
    before = (jnp.dot(oh.astype(BF16), earlier_ref[...], preferred_element_type=F32)
              + jnp.concatenate([run] * (tm // LANES), axis=1))
    rank1 = jnp.sum(before * oh1, axis=0, keepdims=True)
    rank2 = jnp.sum(before * oh2, axis=0, keepdims=True)
    run_ref[...] = run + jnp.sum(oh, axis=1, keepdims=True)
    cnt_ref[...] = run_ref[...]

    scale = float(1 << RANK_BITS)
    zeros = jnp.zeros((6, tm), F32)
    code_ref[...] = jnp.concatenate([e1 * scale + rank1, e2 * scale + rank2, zeros], axis=0).astype(I32)
    w_rows = jnp.concatenate([w1, w2, jnp.zeros((LANES - 2, tm), F32)], axis=0)
    wcol_ref[...] = w_rows.T


def _merge(x2d, ya, o0, l0, o1, l1, o2, l2, gates, wa, wb, wo, g2, wr, br, batch, seq):
    tm = TOK_TILE
    tps = seq // tm
    n_tok = batch * seq

    def tok(c):
        return pl.BlockSpec((tm, c), lambda i: (i, 0))

    def full(a):
        return pl.BlockSpec(a.shape, lambda i: (0,) * a.ndim)

    def res_spec(d):
        return pl.BlockSpec((None, d, tm // d, B_DIM), lambda i: (i // tps, 0, i % tps, 0))

    return pl.pallas_call(
        _merge_kernel,
        grid=(n_tok // tm,),
        in_specs=[tok(D_MODEL), tok(A_Q_DIM), tok(B_DIM), tok(B_DIM),
                  res_spec(4), res_spec(4), res_spec(16), res_spec(16), tok(GATE_COLS),
                  full(wa), full(wb), full(wo), full(g2), full(wr), full(br)],
        out_specs=[tok(D_MODEL), pl.BlockSpec((tm * ROW_CHUNKS, LANES), lambda i: (i, 0)),
                   pl.BlockSpec((8, tm), lambda i: (0, i)),
                   tok(LANES),
                   pl.BlockSpec((N_EXPERTS, LANES), lambda i: (0, 0))],
        out_shape=[jax.ShapeDtypeStruct((n_tok, D_MODEL), F32),
                   jax.ShapeDtypeStruct((n_tok * ROW_CHUNKS, LANES), F32),
                   jax.ShapeDtypeStruct((8, n_tok), I32),
                   jax.ShapeDtypeStruct((n_tok, LANES), F32),
                   jax.ShapeDtypeStruct((N_EXPERTS, LANES), F32)],
        scratch_shapes=([pltpu.VMEM((B_DIM // LANES, tm, LANES), F32)] * 4
                        + [pltpu.VMEM((tm, B_DIM), BF16), pltpu.VMEM((N_EXPERTS, LANES), F32),
                           pltpu.VMEM((tm, tm), BF16),
                           pltpu.VMEM(wa.shape, BF16), pltpu.VMEM(wb.shape, BF16), pltpu.VMEM(wo.shape, BF16)]),
        compiler_params=_cparams("arbitrary"),
        name="merge_route",
    )(x2d, ya, o0, l0, o1, l1, o2, l2, gates, wa, wb, wo, g2, wr, br)


def _row_tile(ref, t):
    return ref.at[pl.ds(pl.multiple_of(t * ROW_CHUNKS, ROW_CHUNKS), ROW_CHUNKS)]


def _store_row_tiles(ref, val):
    rows = val.shape[0]
    for c in range(ROW_CHUNKS):
        ref[pl.ds(c, rows, stride=ROW_CHUNKS), :] = val[:, c * LANES:(c + 1) * LANES]


def _load_row_tiles_chunk(ref, c):
    return ref[pl.ds(c, ref.shape[0] // ROW_CHUNKS, stride=ROW_CHUNKS), :]


def _dispatch_kernel(slot_ref, fill_ref, h_ref, *rest, n_tok, reuse):
    if reuse:
        _, xs_ref, sem = rest
    else:
        xs_ref, zero_ref, sem, zsem = rest
    i = pl.program_id(0)
    rows = h_ref.shape[0] // ROW_CHUNKS

    if not reuse:
        @pl.when(i == 0)
        def _():
            tile_rows = zero_ref.shape[0]
            zero_ref[...] = jnp.zeros_like(zero_ref)

            def zero_copy(t):
                dst = xs_ref.at[pl.ds(pl.multiple_of(t * tile_rows, tile_rows), tile_rows)]
                return pltpu.make_async_copy(zero_ref, dst, zsem)

            def start(t, carry):
                @pl.when(fill_ref[t] != 0)
                def _():
                    zero_copy(t).start()
                return carry

            def wait(t, carry):
                @pl.when(fill_ref[t] != 0)
                def _():
                    zero_copy(t).wait()
                return carry

            n_tiles = xs_ref.shape[0] // tile_rows
            lax.fori_loop(0, n_tiles, start, 0)
            lax.fori_loop(0, n_tiles, wait, 0)

    def issue(j, carry):
        t = i * rows + j
        for k in range(2):
            pltpu.make_async_copy(_row_tile(h_ref, j), _row_tile(xs_ref, slot_ref[k * n_tok + t]), sem).start(priority=k)
        return carry

    lax.fori_loop(0, rows, issue, 0, unroll=DMA_UNROLL)
    for _ in range(2):
        pltpu.make_async_copy(h_ref, xs_ref.at[pl.ds(0, rows * ROW_CHUNKS)], sem).wait()


def _dispatch(slots, fill, h2, xs_prev, n_slots):
    n_tok = h2.shape[0] // ROW_CHUNKS
    rows = DISPATCH_TILE
    reuse = xs_prev is not None
    h_spec = pl.BlockSpec((rows * ROW_CHUNKS, LANES), lambda i, s, f: (i, 0))
    any_spec = pl.BlockSpec(memory_space=pl.ANY)
    scratch = [pltpu.SemaphoreType.DMA(())] if reuse else [
        pltpu.VMEM((EXP_TILE * ROW_CHUNKS, LANES), F32), pltpu.SemaphoreType.DMA(()), pltpu.SemaphoreType.DMA(())]
    return pl.pallas_call(
        functools.partial(_dispatch_kernel, n_tok=n_tok, reuse=reuse),
        grid_spec=pltpu.PrefetchScalarGridSpec(
            num_scalar_prefetch=2,
            grid=(n_tok // rows,),
            in_specs=[h_spec, any_spec] if reuse else [h_spec],
            out_specs=any_spec,
            scratch_shapes=scratch,
        ),
        out_shape=jax.ShapeDtypeStruct((n_slots * ROW_CHUNKS, LANES), F32),
        input_output_aliases={3: 0} if reuse else {},
        compiler_params=_cparams("arbitrary"),
        name="dispatch_reuse" if reuse else "dispatch",
    )(*((slots, fill, h2, xs_prev) if reuse else (slots, fill, h2)))


def _combine_kernel(slot_ref, x_ref, w_ref, g_ref, ys_ref, o_ref, buf_ref, y_ref, sem, *, n_tok, final):
    i = pl.program_id(0)
    n_steps = pl.num_programs(0)
    rows = x_ref.shape[0]

    def issue_tile(tile, slot):
        def issue(j, carry):
            t = tile * rows + j
            for k in range(2):
                pltpu.make_async_copy(_row_tile(ys_ref, slot_ref[k * n_tok + t]),
                                      _row_tile(buf_ref.at[slot, k], j), sem.at[slot]).start(priority=k)
            return carry

        lax.fori_loop(0, rows, issue, 0, unroll=DMA_UNROLL)

    @pl.when(i == 0)
    def _():
        issue_tile(0, 0)

    @pl.when(i + 1 < n_steps)
    def _():
        issue_tile(i + 1, (i + 1) % 2)

    cur = i % 2
    for k in range(2):
        pltpu.make_async_copy(ys_ref.at[pl.ds(0, rows * ROW_CHUNKS)], buf_ref.at[cur, k], sem.at[cur]).wait()
    w = w_ref[...]
    w1, w2 = w[:, 0:1], w[:, 1:2]
    for c in range(ROW_CHUNKS):
        cs = slice(c * LANES, (c + 1) * LANES)
        y1 = _load_row_tiles_chunk(buf_ref.at[cur, 0], c)
        y2 = _load_row_tiles_chunk(buf_ref.at[cur, 1], c)
        (y_ref if final else o_ref)[:, cs] = x_ref[:, cs] + w1 * y1 + w2 * y2
    if final:
        y = y_ref[...]
        o_ref[...] = y * lax.rsqrt(jnp.mean(y * y, axis=-1, keepdims=True) + NORM_EPS) * g_ref[...]


def _combine(slots, x1, wcol, g, ys, final):
    n_tok = x1.shape[0]
    rows = ROW_TILE
    return pl.pallas_call(
        functools.partial(_combine_kernel, n_tok=n_tok, final=final),
        grid_spec=pltpu.PrefetchScalarGridSpec(
            num_scalar_prefetch=1,
            grid=(n_tok // rows,),
            in_specs=[pl.BlockSpec((rows, D_MODEL), lambda i, s: (i, 0)),
                      pl.BlockSpec((rows, LANES), lambda i, s: (i, 0)),
                      pl.BlockSpec((1, D_MODEL), lambda i, s: (0, 0)),
                      pl.BlockSpec(memory_space=pl.ANY)],
            out_specs=pl.BlockSpec((rows, D_MODEL), lambda i, s: (i, 0)),
            scratch_shapes=[pltpu.VMEM((2, 2, rows * ROW_CHUNKS, LANES), F32),
                            pltpu.VMEM((rows, D_MODEL), F32),
                            pltpu.SemaphoreType.DMA((2,))],
        ),
        out_shape=jax.ShapeDtypeStruct((n_tok, D_MODEL), F32),
        compiler_params=_cparams("arbitrary"),
        name="combine",
    )(slots, x1, wcol, g, ys)


def _experts_kernel(tile_e_ref, n_used_ref, xs_ref, wg_ref, wu_ref, wd_ref, ys_ref, xbuf_ref, x_ref, sem):
    del tile_e_ref
    i = pl.program_id(0)
    n_used = n_used_ref[0]
    tile_rows = xbuf_ref.shape[1]

    def fetch(tile):
        rows = pl.ds(pl.multiple_of(tile * tile_rows, tile_rows), tile_rows)
        slot = tile % XS_SLOTS
        return pltpu.make_async_copy(xs_ref.at[rows], xbuf_ref.at[slot], sem.at[slot])

    @pl.when(i == 0)
    def _():
        fetch(0).start()

        @pl.when(n_used > 1)
        def _():
            fetch(1).start()

    @pl.when(i < n_used)
    def _():
        @pl.when(i + XS_SLOTS - 1 < n_used)
        def _():
            fetch(i + XS_SLOTS - 1).start()

        fetch(i).wait()
        xb = xbuf_ref.at[i % XS_SLOTS]
        for c in range(ROW_CHUNKS):
            x_ref[:, c * LANES:(c + 1) * LANES] = _load_row_tiles_chunk(xb, c).astype(BF16)
        x = x_ref[...]
        a = jnp.dot(x, wg_ref[...].astype(BF16), preferred_element_type=F32)
        u = jnp.dot(x, wu_ref[...].astype(BF16), preferred_element_type=F32)
        z = (a * jax.nn.sigmoid(a)) * u
        _store_row_tiles(ys_ref, jnp.dot(z.astype(BF16), wd_ref[...].astype(BF16), preferred_element_type=F32))

    @pl.when(i >= n_used)
    def _():
        ys_ref[...] = jnp.zeros_like(ys_ref)


def _experts(tile_e, n_used, xs, wg, wu, wd, layer):
    n_slots = xs.shape[0] // ROW_CHUNKS
    te = EXP_TILE

    def out_map(i, tile_e, n_used):
        return (i, 0)

    def w_map(i, tile_e, n_used):
        return (layer, tile_e[i], 0, 0)

    return pl.pallas_call(
        _experts_kernel,
        grid_spec=pltpu.PrefetchScalarGridSpec(
            num_scalar_prefetch=2,
            grid=(n_slots // te,),
            in_specs=[pl.BlockSpec(memory_space=pl.ANY),
                      pl.BlockSpec((None, None, D_MODEL, D_EXPERT), w_map),
                      pl.BlockSpec((None, None, D_MODEL, D_EXPERT), w_map),
                      pl.BlockSpec((None, None, D_EXPERT, D_MODEL), w_map)],
            out_specs=pl.BlockSpec((te * ROW_CHUNKS, LANES), out_map),
            scratch_shapes=[pltpu.VMEM((XS_SLOTS, te * ROW_CHUNKS, LANES), F32),
                            pltpu.VMEM((te, D_MODEL), BF16),
                            pltpu.SemaphoreType.DMA((XS_SLOTS,))],
        ),
        out_shape=jax.ShapeDtypeStruct((n_slots * ROW_CHUNKS, LANES), F32),
        compiler_params=_cparams("arbitrary"),
        name="experts",
    )(tile_e, n_used, xs, wg, wu, wd)


def kernel(x, attn_norm_g, w_in, a_sink, w_branch_a, w_branch_b, w_out, ffn_norm_g,
           w_router_group, b_router_group, w_router_expert, b_router_expert,
           w_exp_gate, w_exp_up, w_exp_down, final_norm_g):
    batch, seq, d_model = x.shape
    depth = w_in.shape[0]
    n_tok = batch * seq
    assert d_model == D_MODEL and w_in.shape[2] == D_IN
    assert seq % (16 * B_SUB) == 0 and seq % IN_TILE == 0 and n_tok % DISPATCH_TILE == 0
    assert n_tok < (1 << RANK_BITS)

    cos_t, sin_t = _rope_tables(seq)
    tables = (cos_t, sin_t,
              _residue_order(cos_t, 4, IN_TILE), _residue_order(sin_t, 4, IN_TILE),
              _residue_order(cos_t, 16, IN_TILE), _residue_order(sin_t, 16, IN_TILE))

    n_slots = 2 * n_tok + N_EXPERTS * EXP_TILE
    n_tiles = n_slots // EXP_TILE
    x2d = x.reshape(n_tok, D_MODEL)

    for l in range(depth):
        nat, gates, grp1, grp2 = _in_proj(x2d, attn_norm_g[l][None, :], w_in[l].astype(BF16), tables, batch, seq)
        nat3d = nat.reshape(batch, seq, NAT_COLS)
        ya = _attn_a(nat3d, a_sink[l]).reshape(n_tok, A_Q_DIM)
        o0, l0 = _attn_b(nat3d, NAT_B // B_DIM, "attn_b1")
        o1, l1 = _attn_b(grp1.reshape(batch * 4, seq // 4, GRP_COLS), 0, "attn_b4")
        o2, l2 = _attn_b(grp2.reshape(batch * 16, seq // 16, GRP_COLS), 0, "attn_b16")

        wr = jnp.zeros((D_MODEL, LANES), F32)
        wr = wr.at[:, 0:N_EXPERTS].set(w_router_expert[l]).at[:, N_EXPERTS:N_EXPERTS + MOE_GROUPS].set(w_router_group[l])
        br = jnp.zeros((1, LANES), F32)
        br = br.at[0, 0:N_EXPERTS].set(b_router_expert[l]).at[0, N_EXPERTS:N_EXPERTS + MOE_GROUPS].set(b_router_group[l])
        wr_hi = wr.astype(BF16)
        wr_lo = (wr - wr_hi.astype(F32)).astype(BF16)
        wr_stack = jnp.concatenate([wr_hi, wr_lo], axis=0)

        x1, h2, code, wcol, cnt = _merge(
            x2d, ya, o0.reshape(n_tok, B_DIM), l0.reshape(n_tok, B_DIM),
            o1.reshape(batch, 4, seq // 4, B_DIM), l1.reshape(batch, 4, seq // 4, B_DIM),
            o2.reshape(batch, 16, seq // 16, B_DIM), l2.reshape(batch, 16, seq // 16, B_DIM),
            gates, w_branch_a[l], w_branch_b[l], w_out[l],
            ffn_norm_g[l][None, :], wr_stack, br, batch, seq)

        counts = cnt[:, 0].astype(I32)
        padded = ((counts + EXP_TILE - 1) // EXP_TILE) * EXP_TILE
        ends = jnp.cumsum(padded)
        offs = ends - padded
        n_used = (ends[-1:] // EXP_TILE).astype(I32)
        tile_start = jnp.arange(n_tiles, dtype=I32) * EXP_TILE
        tile_e = jnp.minimum(jnp.sum((ends[None, :] <= tile_start[:, None]).astype(I32), axis=1), N_EXPERTS - 1)
        eid = code[0:2] >> RANK_BITS
        rank = code[0:2] & ((1 << RANK_BITS) - 1)
        expert_ids = jnp.arange(N_EXPERTS, dtype=I32)[:, None, None]
        slots = (rank + jnp.sum(jnp.where(eid[None] == expert_ids, offs[:, None, None], 0), axis=0)).reshape(-1)

        fill = (jnp.any(tile_start[:, None] + EXP_TILE == ends[None, :], axis=1) | (tile_start >= ends[-1])).astype(I32)
        xs = _dispatch(slots, fill, h2, None if l == 0 else xs, n_slots)
        ys = _experts(tile_e, n_used, xs, w_exp_gate, w_exp_up, w_exp_down, l)
        x2d = _combine(slots, x1, wcol, final_norm_g[None, :], ys, final=(l == depth - 1))

    return x2d.reshape(batch, seq, D_MODEL)
```

```python
import functools

import jax
import jax.numpy as jnp
import numpy as np
from jax import lax
from jax.experimental import pallas as pl
from jax.experimental.pallas import tpu as pltpu

F32 = jnp.float32
BF16 = jnp.bfloat16
I32 = jnp.int32

D_MODEL = 1024
HEAD_DIM = 64
HALF_HEAD = HEAD_DIM // 2
ROPE_THETA = 10000.0
NORM_EPS = 1e-6
NEG_INF = -1e30
LANES = 128

A_Q_HEADS = 8
A_KV_HEADS = 2
A_GROUP = A_Q_HEADS // A_KV_HEADS
A_HALF_WINDOW = 128
A_Q_DIM = A_Q_HEADS * HEAD_DIM
A_KV_DIM = A_KV_HEADS * HEAD_DIM

B_GROUPS = ((128, 1), (512, 4), (2048, 16))
B_HEADS = 4
B_DIM = B_HEADS * HEAD_DIM
B_HALF_WINDOW = 64

MOE_GROUPS = 4
EXPERTS_PER_GROUP = 8
N_EXPERTS = MOE_GROUPS * EXPERTS_PER_GROUP
D_EXPERT = 256

NAT_IN_COLS = A_Q_DIM + 2 * A_KV_DIM + 3 * B_DIM
NAT_B = A_Q_DIM
NAT_AK = NAT_B + 3 * B_DIM
NAT_AV = NAT_AK + 2 * A_KV_DIM
NAT_COLS = NAT_AV + A_KV_DIM
GRP_COLS = 3 * B_DIM
COL_G1 = NAT_IN_COLS
COL_G2 = COL_G1 + GRP_COLS
COL_GATE = COL_G2 + GRP_COLS
GATE_COLS = 2 * D_MODEL
D_IN = COL_GATE + GATE_COLS

TOK_TILE = 512
IN_TILE = 512
A_Q_TILE = 512
B_SUB = 128
B_Q_TILE = 512
EXP_TILE = 512
XS_SLOTS = 3
ROW_CHUNKS = D_MODEL // LANES
ROW_TILE = 256
DISPATCH_TILE = 1024
DMA_UNROLL = 8
RANK_BITS = 16
VMEM_LIMIT = 56 * 1024 * 1024


def _cparams(*sem):
    return pltpu.CompilerParams(dimension_semantics=sem, vmem_limit_bytes=VMEM_LIMIT)


def _rope_tables(seq_len):
    inv = (1.0 / (np.float32(ROPE_THETA) ** (np.arange(0, HEAD_DIM, 2, dtype=np.float32) / np.float32(HEAD_DIM))))
    ang = np.arange(seq_len, dtype=np.float32)[:, None] * inv.astype(np.float32)[None, :]
    cos, sin = np.cos(ang).astype(np.float32), np.sin(ang).astype(np.float32)
    cos_t = np.concatenate([cos, cos, cos, cos], axis=-1)
    sin_t = np.concatenate([-sin, sin, -sin, sin], axis=-1)
    return cos_t, sin_t


def _residue_order(table, dilation, tile):
    s, c = table.shape
    return table.reshape(s // tile, tile // dilation, dilation, c).transpose(0, 2, 1, 3).reshape(s, c)


def _rope(t, cos, sin_signed, first_half):
    partner = jnp.where(first_half, pltpu.roll(t, LANES - HALF_HEAD, 1), pltpu.roll(t, HALF_HEAD, 1))
    return t * cos + partner * sin_signed


Q_KIND, K_KIND, V_KIND = 0, 1, 2
_NAT_KINDS = ([Q_KIND] * 4 + [K_KIND] + [V_KIND] + [Q_KIND] * 2 + [K_KIND] * 2 + [V_KIND] * 2)
_GRP_KINDS = [Q_KIND] * 2 + [K_KIND] * 2 + [V_KIND] * 2


def _in_proj_kernel(x_ref, g_ref, w_ref, c1_ref, s1_ref, c4_ref, s4_ref, c16_ref, s16_ref,
                    nat_ref, gate_ref, g1_ref, g2_ref, hf_ref, hb_ref, hd_ref):
    tm = x_ref.shape[0]
    x = x_ref[...]
    h = x * lax.rsqrt(jnp.mean(x * x, axis=-1, keepdims=True) + NORM_EPS) * g_ref[...]
    n_chunks = D_MODEL // LANES
    for c in range(n_chunks):
        hf_ref[c] = h[:, c * LANES:(c + 1) * LANES]
    hb_ref[...] = h.astype(BF16)
    lane = lax.broadcasted_iota(I32, (1, LANES), 1)
    first_half = (lane % HEAD_DIM) < HALF_HEAD

    def project(h_b, col0, kinds, cos_ref, sin_ref, store):
        width = 512
        for c0 in range(0, len(kinds) * LANES, width):
            w = min(width, len(kinds) * LANES - c0)
            res = jnp.dot(h_b, w_ref[:, col0 + c0:col0 + c0 + w], preferred_element_type=F32)
            for j in range(w // LANES):
                kind = kinds[(c0 // LANES) + j]
                t = res[:, j * LANES:(j + 1) * LANES]
                if kind != V_KIND:
                    t = _rope(t, cos_ref[...], sin_ref[...], first_half)
                if kind == Q_KIND:
                    t = t * (HEAD_DIM ** -0.5)
                store(c0 + j * LANES, t)

    low_head = lane < HEAD_DIM

    def store_nat(c, t):
        if A_Q_DIM <= c < A_Q_DIM + A_KV_DIM:
            swapped = pltpu.roll(t, HEAD_DIM, 1)
            nat_ref[:, NAT_AK:NAT_AK + LANES] = jnp.where(low_head, t, swapped).astype(BF16)
            nat_ref[:, NAT_AK + LANES:NAT_AK + 2 * LANES] = jnp.where(low_head, swapped, t).astype(BF16)
        elif c < A_Q_DIM + 2 * A_KV_DIM:
            out = c if c < A_Q_DIM else NAT_AV
            nat_ref[:, out:out + LANES] = t.astype(BF16)
        else:
            out = c - 2 * A_KV_DIM
            nat_ref[:, out:out + LANES] = t.astype(BF16)

    project(hb_ref[...], 0, _NAT_KINDS, c1_ref, s1_ref, store_nat)

    for c0 in range(0, GATE_COLS, 512):
        res = jnp.dot(hb_ref[...], w_ref[:, COL_GATE + c0:COL_GATE + c0 + 512], preferred_element_type=F32)
        gate_ref[:, c0:c0 + 512] = jax.nn.sigmoid(res).astype(BF16)

    for dil, col0, cos_ref, sin_ref, out_ref in ((4, COL_G1, c4_ref, s4_ref, g1_ref),
                                                 (16, COL_G2, c16_ref, s16_ref, g2_ref)):
        n = tm // dil
        for r in range(dil):
            for c in range(n_chunks):
                hd_ref[r * n:(r + 1) * n, c * LANES:(c + 1) * LANES] = (
                    hf_ref[c, pl.ds(r, n, stride=dil), :].astype(BF16))

        def store_grp(c, t, out_ref=out_ref, dil=dil, n=n):
            v = t.astype(BF16)
            for r in range(dil):
                out_ref[r, :, c:c + LANES] = v[r * n:(r + 1) * n]

        project(hd_ref[...], col0, _GRP_KINDS, cos_ref, sin_ref, store_grp)


def _in_proj(x2d, g, w_bf16, tables, batch, seq):
    tm = IN_TILE
    tiles_per_seq = seq // tm
    n_tok = batch * seq
    c1, s1, c4, s4, c16, s16 = tables
    tab_spec = pl.BlockSpec((tm, LANES), lambda i: (i % tiles_per_seq, 0))
    return pl.pallas_call(
        _in_proj_kernel,
        grid=(n_tok // tm,),
        in_specs=[
            pl.BlockSpec((tm, D_MODEL), lambda i: (i, 0)),
            pl.BlockSpec((1, D_MODEL), lambda i: (0, 0)),
            pl.BlockSpec((D_MODEL, D_IN), lambda i: (0, 0), pipeline_mode=pl.Buffered(1)),
            tab_spec, tab_spec, tab_spec, tab_spec, tab_spec, tab_spec,
        ],
        out_specs=[
            pl.BlockSpec((tm, NAT_COLS), lambda i: (i, 0)),
            pl.BlockSpec((tm, GATE_COLS), lambda i: (i, 0)),
            pl.BlockSpec((None, 4, tm // 4, GRP_COLS), lambda i: (i // tiles_per_seq, 0, i % tiles_per_seq, 0)),
            pl.BlockSpec((None, 16, tm // 16, GRP_COLS), lambda i: (i // tiles_per_seq, 0, i % tiles_per_seq, 0)),
        ],
        out_shape=[
            jax.ShapeDtypeStruct((n_tok, NAT_COLS), BF16),
            jax.ShapeDtypeStruct((n_tok, GATE_COLS), BF16),
            jax.ShapeDtypeStruct((batch, 4, seq // 4, GRP_COLS), BF16),
            jax.ShapeDtypeStruct((batch, 16, seq // 16, GRP_COLS), BF16),
        ],
        scratch_shapes=[
            pltpu.VMEM((D_MODEL // LANES, tm, LANES), F32),
            pltpu.VMEM((tm, D_MODEL), BF16),
            pltpu.VMEM((tm, D_MODEL), BF16),
        ],
        compiler_params=_cparams("parallel"),
        name="in_proj",
    )(x2d, g, w_bf16, c1, s1, c4, s4, c16, s16)


def _masked_heads(q_pair, low_head, high_head):
    zero = jnp.zeros_like(q_pair)
    return [jnp.where(low_head, q_pair, zero), jnp.where(high_head, q_pair, zero)]


def _attn_a_kernel(sink_ref, q_ref, kp_ref, km_ref, kn_ref, vp_ref, vm_ref, vn_ref, o_ref, k_ref, vt_ref, *, seq):
    tq = q_ref.shape[0]
    hw = A_HALF_WINDOW
    n_sub = tq // hw
    i = pl.program_id(1)
    last_blk = seq // hw - 1
    k_ref[0:hw, :] = kp_ref[...]
    k_ref[hw:hw + tq, :] = km_ref[...]
    k_ref[hw + tq:tq + 2 * hw, :] = kn_ref[...]
    for r0, src in ((0, vp_ref), (hw, vm_ref), (hw + tq, vn_ref)):
        vt_ref[:, r0:r0 + src.shape[0]] = src[...].astype(F32).T.astype(BF16)
    grp_cols = A_GROUP * hw
    key = lax.broadcasted_iota(I32, (hw, grp_cols), 0)
    qry = lax.broadcasted_iota(I32, (hw, grp_cols), 1) % hw
    low_head = lax.broadcasted_iota(I32, (1, LANES), 1) < HEAD_DIM
    high_head = jnp.logical_not(low_head)
    for sb in range(n_sub):
        r0 = sb * hw
        blk = i * n_sub + sb
        mask_p = (key >= qry) if sb > 0 else (key >= qry + jnp.where(blk > 0, 0, hw))
        mask_n = (key <= qry) if sb < n_sub - 1 else (key <= qry - jnp.where(blk < last_blk, 0, hw))
        out_t = []
        for g in range(A_KV_HEADS):
            heads = range(g * A_GROUP, (g + 1) * A_GROUP)
            q_parts = []
            for c in range(g * A_GROUP // 2, (g + 1) * A_GROUP // 2):
                q_parts += _masked_heads(q_ref[r0:r0 + hw, c * LANES:(c + 1) * LANES], low_head, high_head)
            q = jnp.concatenate(q_parts, axis=0)
            sink = jnp.concatenate([jnp.full((1, hw), sink_ref[h], F32) for h in heads], axis=1)
            k = k_ref[r0:r0 + 3 * hw, g * LANES:(g + 1) * LANES]
            s = lax.dot_general(k, q, (((1,), (1,)), ((), ())), preferred_element_type=F32)
            sp = jnp.where(mask_p, s[0:hw], NEG_INF)
            so = s[hw:2 * hw]
            sn = jnp.where(mask_n, s[2 * hw:3 * hw], NEG_INF)
            m = jnp.max(jnp.maximum(jnp.maximum(sp, so), sn), axis=0, keepdims=True)
            m = jnp.maximum(m, sink)
            pp, po, pn = jnp.exp(sp - m), jnp.exp(so - m), jnp.exp(sn - m)
            denom = jnp.sum(pp + po + pn, axis=0, keepdims=True) + jnp.exp(sink - m)
            p = jnp.concatenate([pp, po, pn], axis=0).astype(BF16)
            vt = vt_ref[g * HEAD_DIM:(g + 1) * HEAD_DIM, r0:r0 + 3 * hw]
            o = jnp.dot(vt, p, preferred_element_type=F32) * (1.0 / denom)
            out_t += [o[:, j * hw:(j + 1) * hw] for j in range(A_GROUP)]
        o_ref[r0:r0 + hw, :] = jnp.concatenate(out_t, axis=0).T.astype(BF16)


def _attn_a(nat3d, sink):
    batch, seq, _ = nat3d.shape
    tq = A_Q_TILE
    hw = A_HALF_WINDOW
    per = tq // hw
    n_hw = seq // hw
    k_cols, v_cols = 2 * A_KV_DIM, A_KV_DIM
    k_blk, v_blk = NAT_AK // k_cols, NAT_AV // v_cols

    def prev_spec(cols, blk):
        return pl.BlockSpec((None, hw, cols), lambda b, i: (b, jnp.maximum(i * per - 1, 0), blk))

    def main_spec(cols, blk):
        return pl.BlockSpec((None, tq, cols), lambda b, i: (b, i, blk))

    def next_spec(cols, blk):
        return pl.BlockSpec((None, hw, cols), lambda b, i: (b, jnp.minimum((i + 1) * per, n_hw - 1), blk))

    return pl.pallas_call(
        functools.partial(_attn_a_kernel, seq=seq),
        grid=(batch, seq // tq),
        in_specs=[
            pl.BlockSpec(memory_space=pltpu.SMEM),
            main_spec(A_Q_DIM, 0),
            prev_spec(k_cols, k_blk), main_spec(k_cols, k_blk), next_spec(k_cols, k_blk),
            prev_spec(v_cols, v_blk), main_spec(v_cols, v_blk), next_spec(v_cols, v_blk),
        ],
        out_specs=pl.BlockSpec((None, tq, A_Q_DIM), lambda b, i: (b, i, 0)),
        out_shape=jax.ShapeDtypeStruct((batch, seq, A_Q_DIM), BF16),
        scratch_shapes=[pltpu.VMEM((tq + 2 * hw, k_cols), BF16), pltpu.VMEM((v_cols, tq + 2 * hw), BF16)],
        compiler_params=_cparams("parallel", "parallel"),
        name="attn_a",
    )(sink, nat3d, nat3d, nat3d, nat3d, nat3d, nat3d, nat3d)


def _attn_b_kernel(q_ref, kp_ref, km_ref, kn_ref, vp_ref, vm_ref, vn_ref, o_ref, lse_ref,
                   k_ref, vt_ref, *, sub_len):
    for s in range(q_ref.shape[0]):
        _attn_b_one(q_ref.at[s], kp_ref.at[s], km_ref.at[s], kn_ref.at[s], vp_ref.at[s], vm_ref.at[s], vn_ref.at[s],
                    o_ref.at[s], lse_ref.at[s], k_ref, vt_ref, sub_len)


def _attn_b_one(q_ref, kp_ref, km_ref, kn_ref, vp_ref, vm_ref, vn_ref, o_ref, lse_ref, k_ref, vt_ref, sub_len):
    tq = q_ref.shape[0]
    hw = B_HALF_WINDOW
    t0 = pl.program_id(1) * tq
    k_ref[0:hw, :] = kp_ref[...]
    k_ref[hw:hw + tq, :] = km_ref[...]
    k_ref[hw + tq:tq + 2 * hw, :] = kn_ref[...]
    for r0, src in ((0, vp_ref), (hw, vm_ref), (hw + tq, vn_ref)):
        vt_ref[:, r0:r0 + src.shape[0]] = src[...].astype(F32).T.astype(BF16)
    kw = B_SUB + 2 * hw
    n_sub = tq // B_SUB
    all_cols = B_HEADS * B_SUB
    key = lax.broadcasted_iota(I32, (kw, all_cols), 0)
    qry = lax.broadcasted_iota(I32, (kw, all_cols), 1) % B_SUB
    in_band = jnp.abs(key - hw - qry) <= hw
    low_head = lax.broadcasted_iota(I32, (1, LANES), 1) < HEAD_DIM
    high_head = jnp.logical_not(low_head)
    for sb in range(n_sub):
        r0 = sb * B_SUB
        valid = in_band
        if sb == 0:
            valid = valid & (t0 - hw + key >= 0)
        if sb == n_sub - 1:
            valid = valid & (t0 + r0 - hw + key < sub_len)
        s_parts = []
        for c in range(B_HEADS // 2):
            cs = slice(c * LANES, (c + 1) * LANES)
            q = jnp.concatenate(_masked_heads(q_ref[r0:r0 + B_SUB, cs], low_head, high_head), axis=0)
            s_parts.append(lax.dot_general(k_ref[r0:r0 + kw, cs], q, (((1,), (1,)), ((), ())),
                                           preferred_element_type=F32))
        s = jnp.where(valid, jnp.concatenate(s_parts, axis=1), NEG_INF)
        m = jnp.max(s, axis=0, keepdims=True)
        p = jnp.exp(s - m)
        denom = jnp.sum(p, axis=0, keepdims=True)
        p = p.astype(BF16)
        inv = 1.0 / denom
        lse = m + jnp.log(denom)
        out_t, lse_t = [], []
        for h in range(B_HEADS):
            qs = slice(h * B_SUB, (h + 1) * B_SUB)
            vt = vt_ref[h * HEAD_DIM:(h + 1) * HEAD_DIM, r0:r0 + kw]
            out_t.append(jnp.dot(vt, p[:, qs], preferred_element_type=F32) * inv[:, qs])
            lse_t.append(jnp.broadcast_to(lse[:, qs], (HEAD_DIM, B_SUB)))
        o_ref[r0:r0 + B_SUB, :] = jnp.concatenate(out_t, axis=0).T
        lse_ref[r0:r0 + B_SUB, :] = jnp.concatenate(lse_t, axis=0).T


def _attn_b(arr3d, q_blk, name):
    n_sub, sub_len, _ = arr3d.shape
    tq = min(B_Q_TILE, sub_len)
    n_per = B_Q_TILE // tq
    hw = B_HALF_WINDOW
    per = tq // hw
    n_hw = sub_len // hw

    def main_spec(c):
        return pl.BlockSpec((n_per, tq, B_DIM), lambda g, i: (g, i, c))

    def prev_spec(c):
        return pl.BlockSpec((n_per, hw, B_DIM), lambda g, i: (g, jnp.maximum(i * per - 1, 0), c))

    def next_spec(c):
        return pl.BlockSpec((n_per, hw, B_DIM), lambda g, i: (g, jnp.minimum((i + 1) * per, n_hw - 1), c))

    out_spec = pl.BlockSpec((n_per, tq, B_DIM), lambda g, i: (g, i, 0))
    return pl.pallas_call(
        functools.partial(_attn_b_kernel, sub_len=sub_len),
        grid=(n_sub // n_per, sub_len // tq),
        in_specs=[main_spec(q_blk),
                  prev_spec(q_blk + 1), main_spec(q_blk + 1), next_spec(q_blk + 1),
                  prev_spec(q_blk + 2), main_spec(q_blk + 2), next_spec(q_blk + 2)],
        out_specs=[out_spec, out_spec],
        out_shape=[jax.ShapeDtypeStruct((n_sub, sub_len, B_DIM), F32)] * 2,
        scratch_shapes=[pltpu.VMEM((tq + 2 * hw, B_DIM), BF16), pltpu.VMEM((B_DIM, tq + 2 * hw), BF16)],
        compiler_params=_cparams("parallel", "parallel"),
        name=name,
    )(arr3d, arr3d, arr3d, arr3d, arr3d, arr3d, arr3d)


def _merge_kernel(x_ref, ya_ref, o0_ref, l0_ref, o1_ref, l1_ref, o2_ref, l2_ref, gate_ref,
                  wa_ref, wb_ref, wo_ref, g2_ref, wr_ref, br_ref,
                  x1_ref, h2_ref, code_ref, wcol_ref, cnt_ref,
                  so1_ref, sl1_ref, so2_ref, sl2_ref, yb_ref, run_ref, earlier_ref,
                  wab_ref, wbb_ref, wob_ref):
    tm = x_ref.shape[0]

    @pl.when(pl.program_id(0) == 0)
    def _():
        run_ref[...] = jnp.zeros_like(run_ref)
        wab_ref[...] = wa_ref[...].astype(BF16)
        wbb_ref[...] = wb_ref[...].astype(BF16)
        wob_ref[...] = wo_ref[...].astype(BF16)
        row = lax.broadcasted_iota(I32, (tm, tm), 0)
        col = lax.broadcasted_iota(I32, (tm, tm), 1)
        earlier_ref[...] = (row < col).astype(BF16)

    for dil, src_o, src_l, dst_o, dst_l in ((4, o1_ref, l1_ref, so1_ref, sl1_ref),
                                            (16, o2_ref, l2_ref, so2_ref, sl2_ref)):
        n = tm // dil
        for r in range(dil):
            for c in range(B_DIM // LANES):
                cs = slice(c * LANES, (c + 1) * LANES)
                dst_o[c, pl.ds(r, n, stride=dil), :] = src_o[r, :, cs]
                dst_l[c, pl.ds(r, n, stride=dil), :] = src_l[r, :, cs]

    for c in range(B_DIM // LANES):
        cs = slice(c * LANES, (c + 1) * LANES)
        l0, l1, l2 = l0_ref[:, cs], sl1_ref[c], sl2_ref[c]
        m = jnp.maximum(jnp.maximum(l0, l1), l2)
        e0, e1, e2 = jnp.exp(l0 - m), jnp.exp(l1 - m), jnp.exp(l2 - m)
        yb = (e0 * o0_ref[:, cs] + e1 * so1_ref[c] + e2 * so2_ref[c]) / (e0 + e1 + e2)
        yb_ref[:, cs] = yb.astype(BF16)

    ya_p = jnp.dot(ya_ref[...], wab_ref[...], preferred_element_type=F32)
    yb_p = jnp.dot(yb_ref[...], wbb_ref[...], preferred_element_type=F32)
    merged = gate_ref[:, 0:D_MODEL].astype(F32) * ya_p + gate_ref[:, D_MODEL:GATE_COLS].astype(F32) * yb_p
    x1 = x_ref[...] + jnp.dot(merged.astype(BF16), wob_ref[...], preferred_element_type=F32)
    x1_ref[...] = x1

    h2 = x1 * lax.rsqrt(jnp.mean(x1 * x1, axis=-1, keepdims=True) + NORM_EPS) * g2_ref[...]
    _store_row_tiles(h2_ref, h2)

    h_hi = h2.astype(BF16)
    h_lo = (h2 - h_hi.astype(F32)).astype(BF16)
    w_hi, w_lo = wr_ref[0:D_MODEL], wr_ref[D_MODEL:2 * D_MODEL]
    logits = (jnp.dot(h_hi, w_hi, preferred_element_type=F32)
              + jnp.dot(h_lo, w_hi, preferred_element_type=F32)
              + jnp.dot(h_hi, w_lo, preferred_element_type=F32)) + br_ref[...]

    logits_t = logits.T
    sub = lax.broadcasted_iota(I32, (EXPERTS_PER_GROUP, tm), 0).astype(F32)
    none = float(EXPERTS_PER_GROUP)
    gl = jnp.where(sub < MOE_GROUPS, logits_t[N_EXPERTS:N_EXPERTS + EXPERTS_PER_GROUP], -jnp.inf)
    gmax = jnp.max(gl, axis=0, keepdims=True)
    gidx = jnp.min(jnp.where(gl == gmax, sub, none), axis=0, keepdims=True)
    gw = 1.0 / jnp.sum(jnp.exp(gl - gmax), axis=0, keepdims=True)
    el = logits_t[0:EXPERTS_PER_GROUP]
    for g in range(1, MOE_GROUPS):
        el = jnp.where(gidx == g, logits_t[g * EXPERTS_PER_GROUP:(g + 1) * EXPERTS_PER_GROUP], el)
    v1 = jnp.max(el, axis=0, keepdims=True)
    i1 = jnp.min(jnp.where(el == v1, sub, none), axis=0, keepdims=True)
    el2 = jnp.where(sub == i1, -jnp.inf, el)
    v2 = jnp.max(el2, axis=0, keepdims=True)
    i2 = jnp.min(jnp.where(el2 == v2, sub, none), axis=0, keepdims=True)
    t = jnp.exp(v2 - v1)
    w1 = gw / (1.0 + t)
    w2 = gw * t / (1.0 + t)
    e1 = gidx * EXPERTS_PER_GROUP + i1
    e2 = gidx * EXPERTS_PER_GROUP + i2

    expert = lax.broadcasted_iota(I32, (N_EXPERTS, tm), 0).astype(F32)
    oh1 = (expert == e1).astype(F32)
    oh2 = (expert == e2).astype(F32)
    oh = oh1 + oh2
    run = run_ref[...]
    before = (jnp.dot(oh.astype(BF16), earlier_ref[...], preferred_element_type=F32)
              + jnp.concatenate([run] * (tm // LANES), axis=1))
    rank1 = jnp.sum(before * oh1, axis=0, keepdims=True)
    rank2 = jnp.sum(before * oh2, axis=0, keepdims=True)
    run_ref[...] = run + jnp.sum(oh, axis=1, keepdims=True)
    cnt_ref[...] = run_ref[...]

    scale = float(1 << RANK_BITS)
    zeros = jnp.zeros((6, tm), F32)
    code_ref[...] = jnp.concatenate([e1 * scale + rank1, e2 * scale + rank2, zeros], axis=0).astype(I32)
    w_rows = jnp.concatenate([w1, w2, jnp.zeros((LANES - 2, tm), F32)], axis=0)
    wcol_ref[...] = w_rows.T


def _merge(x2d, ya, o0, l0, o1, l1, o2, l2, gates, wa, wb, wo, g2, wr, br, batch, seq):
    tm = TOK_TILE
    tps = seq // tm
    n_tok = batch * seq

    def tok(c):
        return pl.BlockSpec((tm, c), lambda i: (i, 0))

    def full(a):
        return pl.BlockSpec(a.shape, lambda i: (0,) * a.ndim)

    def res_spec(d):
        return pl.BlockSpec((None, d, tm // d, B_DIM), lambda i: (i // tps, 0, i % tps, 0))

    return pl.pallas_call(
        _merge_kernel,
        grid=(n_tok // tm,),
        in_specs=[tok(D_MODEL), tok(A_Q_DIM), tok(B_DIM), tok(B_DIM),
                  res_spec(4), res_spec(4), res_spec(16), res_spec(16), tok(GATE_COLS),
                  full(wa), full(wb), full(wo), full(g2), full(wr), full(br)],
        out_specs=[tok(D_MODEL), pl.BlockSpec((tm * ROW_CHUNKS, LANES), lambda i: (i, 0)),
                   pl.BlockSpec((8, tm), lambda i: (0, i)),
                   tok(LANES),
                   pl.BlockSpec((N_EXPERTS, LANES), lambda i: (0, 0))],
        out_shape=[jax.ShapeDtypeStruct((n_tok, D_MODEL), F32),
                   jax.ShapeDtypeStruct((n_tok * ROW_CHUNKS, LANES), F32),
                   jax.ShapeDtypeStruct((8, n_tok), I32),
                   jax.ShapeDtypeStruct((n_tok, LANES), F32),
                   jax.ShapeDtypeStruct((N_EXPERTS, LANES), F32)],
        scratch_shapes=([pltpu.VMEM((B_DIM // LANES, tm, LANES), F32)] * 4
                        + [pltpu.VMEM((tm, B_DIM), BF16), pltpu.VMEM((N_EXPERTS, LANES), F32),
                           pltpu.VMEM((tm, tm), BF16),
                           pltpu.VMEM(wa.shape, BF16), pltpu.VMEM(wb.shape, BF16), pltpu.VMEM(wo.shape, BF16)]),
        compiler_params=_cparams("arbitrary"),
        name="merge_route",
    )(x2d, ya, o0, l0, o1, l1, o2, l2, gates, wa, wb, wo, g2, wr, br)


def _row_tile(ref, t):
    return ref.at[pl.ds(pl.multiple_of(t * ROW_CHUNKS, ROW_CHUNKS), ROW_CHUNKS)]


def _store_row_tiles(ref, val):
    rows = val.shape[0]
    for c in range(ROW_CHUNKS):
        ref[pl.ds(c, rows, stride=ROW_CHUNKS), :] = val[:, c * LANES:(c + 1) * LANES]


def _load_row_tiles_chunk(ref, c):
    return ref[pl.ds(c, ref.shape[0] // ROW_CHUNKS, stride=ROW_CHUNKS), :]


def _dispatch_kernel(slot_ref, fill_ref, h_ref, *rest, n_tok, reuse):
    if reuse:
        _, xs_ref, sem = rest
    else:
        xs_ref, zero_ref, sem, zsem = rest
    i = pl.program_id(0)
    rows = h_ref.shape[0] // ROW_CHUNKS

    if not reuse:
        @pl.when(i == 0)
        def _():
            tile_rows = zero_ref.shape[0]
            zero_ref[...] = jnp.zeros_like(zero_ref)

            def zero_copy(t):
                dst = xs_ref.at[pl.ds(pl.multiple_of(t * tile_rows, tile_rows), tile_rows)]
                return pltpu.make_async_copy(zero_ref, dst, zsem)

            def start(t, carry):
                @pl.when(fill_ref[t] != 0)
                def _():
                    zero_copy(t).start()
                return carry

            def wait(t, carry):
                @pl.when(fill_ref[t] != 0)
                def _():
                    zero_copy(t).wait()
                return carry

            n_tiles = xs_ref.shape[0] // tile_rows
            lax.fori_loop(0, n_tiles, start, 0)
            lax.fori_loop(0, n_tiles, wait, 0)

    def issue(j, carry):
        t = i * rows + j
        for k in range(2):
            pltpu.make_async_copy(_row_tile(h_ref, j), _row_tile(xs_ref, slot_ref[k * n_tok + t]), sem).start(priority=k)
        return carry

    lax.fori_loop(0, rows, issue, 0, unroll=DMA_UNROLL)
    for _ in range(2):
        pltpu.make_async_copy(h_ref, xs_ref.at[pl.ds(0, rows * ROW_CHUNKS)], sem).wait()


def _dispatch(slots, fill, h2, xs_prev, n_slots):
    n_tok = h2.shape[0] // ROW_CHUNKS
    rows = DISPATCH_TILE
    reuse = xs_prev is not None
    h_spec = pl.BlockSpec((rows * ROW_CHUNKS, LANES), lambda i, s, f: (i, 0))
    any_spec = pl.BlockSpec(memory_space=pl.ANY)
    scratch = [pltpu.SemaphoreType.DMA(())] if reuse else [
        pltpu.VMEM((EXP_TILE * ROW_CHUNKS, LANES), F32), pltpu.SemaphoreType.DMA(()), pltpu.SemaphoreType.DMA(())]
    return pl.pallas_call(
        functools.partial(_dispatch_kernel, n_tok=n_tok, reuse=reuse),
        grid_spec=pltpu.PrefetchScalarGridSpec(
            num_scalar_prefetch=2,
            grid=(n_tok // rows,),
            in_specs=[h_spec, any_spec] if reuse else [h_spec],
            out_specs=any_spec,
            scratch_shapes=scratch,
        ),
        out_shape=jax.ShapeDtypeStruct((n_slots * ROW_CHUNKS, LANES), F32),
        input_output_aliases={3: 0} if reuse else {},
        compiler_params=_cparams("arbitrary"),
        name="dispatch_reuse" if reuse else "dispatch",
    )(*((slots, fill, h2, xs_prev) if reuse else (slots, fill, h2)))


def _combine_kernel(slot_ref, x_ref, w_ref, g_ref, ys_ref, o_ref, buf_ref, y_ref, sem, *, n_tok, final):
    i = pl.program_id(0)
    n_steps = pl.num_programs(0)
    rows = x_ref.shape[0]

    def issue_tile(tile, slot):
        def issue(j, carry):
            t = tile * rows + j
            for k in range(2):
                pltpu.make_async_copy(_row_tile(ys_ref, slot_ref[k * n_tok + t]),
                                      _row_tile(buf_ref.at[slot, k], j), sem.at[slot]).start(priority=k)
            return carry

        lax.fori_loop(0, rows, issue, 0, unroll=DMA_UNROLL)

    @pl.when(i == 0)
    def _():
        issue_tile(0, 0)

    @pl.when(i + 1 < n_steps)
    def _():
        issue_tile(i + 1, (i + 1) % 2)

    cur = i % 2
    for k in range(2):
        pltpu.make_async_copy(ys_ref.at[pl.ds(0, rows * ROW_CHUNKS)], buf_ref.at[cur, k], sem.at[cur]).wait()
    w = w_ref[...]
    w1, w2 = w[:, 0:1], w[:, 1:2]
    for c in range(ROW_CHUNKS):
        cs = slice(c * LANES, (c + 1) * LANES)
        y1 = _load_row_tiles_chunk(buf_ref.at[cur, 0], c)
        y2 = _load_row_tiles_chunk(buf_ref.at[cur, 1], c)
        (y_ref if final else o_ref)[:, cs] = x_ref[:, cs] + w1 * y1 + w2 * y2
    if final:
        y = y_ref[...]
        o_ref[...] = y * lax.rsqrt(jnp.mean(y * y, axis=-1, keepdims=True) + NORM_EPS) * g_ref[...]


def _combine(slots, x1, wcol, g, ys, final):
    n_tok = x1.shape[0]
    rows = ROW_TILE
    return pl.pallas_call(
        functools.partial(_combine_kernel, n_tok=n_tok, final=final),
        grid_spec=pltpu.PrefetchScalarGridSpec(
            num_scalar_prefetch=1,
            grid=(n_tok // rows,),
            in_specs=[pl.BlockSpec((rows, D_MODEL), lambda i, s: (i, 0)),
                      pl.BlockSpec((rows, LANES), lambda i, s: (i, 0)),
                      pl.BlockSpec((1, D_MODEL), lambda i, s: (0, 0)),
                      pl.BlockSpec(memory_space=pl.ANY)],
            out_specs=pl.BlockSpec((rows, D_MODEL), lambda i, s: (i, 0)),
            scratch_shapes=[pltpu.VMEM((2, 2, rows * ROW_CHUNKS, LANES), F32),
                            pltpu.VMEM((rows, D_MODEL), F32),
                            pltpu.SemaphoreType.DMA((2,))],
        ),
        out_shape=jax.ShapeDtypeStruct((n_tok, D_MODEL), F32),
        compiler_params=_cparams("arbitrary"),
        name="combine",
    )(slots, x1, wcol, g, ys)


def _experts_kernel(tile_e_ref, n_used_ref, xs_ref, wg_ref, wu_ref, wd_ref, ys_ref, xbuf_ref, x_ref, sem):
    del tile_e_ref
    i = pl.program_id(0)
    n_used = n_used_ref[0]
    tile_rows = xbuf_ref.shape[1]

    def fetch(tile):
        rows = pl.ds(pl.multiple_of(tile * tile_rows, tile_rows), tile_rows)
        slot = tile % XS_SLOTS
        return pltpu.make_async_copy(xs_ref.at[rows], xbuf_ref.at[slot], sem.at[slot])

    @pl.when(i == 0)
    def _():
        fetch(0).start()

        @pl.when(n_used > 1)
        def _():
            fetch(1).start()

    @pl.when(i < n_used)
    def _():
        @pl.when(i + XS_SLOTS - 1 < n_used)
        def _():
            fetch(i + XS_SLOTS - 1).start()

        fetch(i).wait()
        xb = xbuf_ref.at[i % XS_SLOTS]
        for c in range(ROW_CHUNKS):
            x_ref[:, c * LANES:(c + 1) * LANES] = _load_row_tiles_chunk(xb, c).astype(BF16)
        x = x_ref[...]
        a = jnp.dot(x, wg_ref[...].astype(BF16), preferred_element_type=F32)
        u = jnp.dot(x, wu_ref[...].astype(BF16), preferred_element_type=F32)
        z = (a * jax.nn.sigmoid(a)) * u
        _store_row_tiles(ys_ref, jnp.dot(z.astype(BF16), wd_ref[...].astype(BF16), preferred_element_type=F32))

    @pl.when(i >= n_used)
    def _():
        ys_ref[...] = jnp.zeros_like(ys_ref)


def _experts(tile_e, n_used, xs, wg, wu, wd, layer):
    n_slots = xs.shape[0] // ROW_CHUNKS
    te = EXP_TILE

    def out_map(i, tile_e, n_used):
        return (i, 0)

    def w_map(i, tile_e, n_used):
        return (layer, tile_e[i], 0, 0)

    return pl.pallas_call(
        _experts_kernel,
        grid_spec=pltpu.PrefetchScalarGridSpec(
            num_scalar_prefetch=2,
            grid=(n_slots // te,),
            in_specs=[pl.BlockSpec(memory_space=pl.ANY),
                      pl.BlockSpec((None, None, D_MODEL, D_EXPERT), w_map),
                      pl.BlockSpec((None, None, D_MODEL, D_EXPERT), w_map),
                      pl.BlockSpec((None, None, D_EXPERT, D_MODEL), w_map)],
            out_specs=pl.BlockSpec((te * ROW_CHUNKS, LANES), out_map),
            scratch_shapes=[pltpu.VMEM((XS_SLOTS, te * ROW_CHUNKS, LANES), F32),
                            pltpu.VMEM((te, D_MODEL), BF16),
                            pltpu.SemaphoreType.DMA((XS_SLOTS,))],
        ),
        out_shape=jax.ShapeDtypeStruct((n_slots * ROW_CHUNKS, LANES), F32),
        compiler_params=_cparams("arbitrary"),
        name="experts",
    )(tile_e, n_used, xs, wg, wu, wd)


def kernel(x, attn_norm_g, w_in, a_sink, w_branch_a, w_branch_b, w_out, ffn_norm_g,
           w_router_group, b_router_group, w_router_expert, b_router_expert,
           w_exp_gate, w_exp_up, w_exp_down, final_norm_g):
    batch, seq, d_model = x.shape
    depth = w_in.shape[0]
    n_tok = batch * seq
    assert d_model == D_MODEL and w_in.shape[2] == D_IN
    assert seq % (16 * B_SUB) == 0 and seq % IN_TILE == 0 and n_tok % DISPATCH_TILE == 0
    assert n_tok < (1 << RANK_BITS)

    cos_t, sin_t = _rope_tables(seq)
    tables = (cos_t, sin_t,
              _residue_order(cos_t, 4, IN_TILE), _residue_order(sin_t, 4, IN_TILE),
              _residue_order(cos_t, 16, IN_TILE), _residue_order(sin_t, 16, IN_TILE))

    n_slots = 2 * n_tok + N_EXPERTS * EXP_TILE
    n_tiles = n_slots // EXP_TILE
    x2d = x.reshape(n_tok, D_MODEL)

    for l in range(depth):
        nat, gates, grp1, grp2 = _in_proj(x2d, attn_norm_g[l][None, :], w_in[l].astype(BF16), tables, batch, seq)
        nat3d = nat.reshape(batch, seq, NAT_COLS)
        ya = _attn_a(nat3d, a_sink[l]).reshape(n_tok, A_Q_DIM)
        o0, l0 = _attn_b(nat3d, NAT_B // B_DIM, "attn_b1")
        o1, l1 = _attn_b(grp1.reshape(batch * 4, seq // 4, GRP_COLS), 0, "attn_b4")
        o2, l2 = _attn_b(grp2.reshape(batch * 16, seq // 16, GRP_COLS), 0, "attn_b16")

        wr = jnp.zeros((D_MODEL, LANES), F32)
        wr = wr.at[:, 0:N_EXPERTS].set(w_router_expert[l]).at[:, N_EXPERTS:N_EXPERTS + MOE_GROUPS].set(w_router_group[l])
        br = jnp.zeros((1, LANES), F32)
        br = br.at[0, 0:N_EXPERTS].set(b_router_expert[l]).at[0, N_EXPERTS:N_EXPERTS + MOE_GROUPS].set(b_router_group[l])
        wr_hi = wr.astype(BF16)
        wr_lo = (wr - wr_hi.astype(F32)).astype(BF16)
        wr_stack = jnp.concatenate([wr_hi, wr_lo], axis=0)

        x1, h2, code, wcol, cnt = _merge(
            x2d, ya, o0.reshape(n_tok, B_DIM), l0.reshape(n_tok, B_DIM),
            o1.reshape(batch, 4, seq // 4, B_DIM), l1.reshape(batch, 4, seq // 4, B_DIM),
            o2.reshape(batch, 16, seq // 16, B_DIM), l2.reshape(batch, 16, seq // 16, B_DIM),
            gates, w_branch_a[l], w_branch_b[l], w_out[l],
            ffn_norm_g[l][None, :], wr_stack, br, batch, seq)

        counts = cnt[:, 0].astype(I32)
        padded = ((counts + EXP_TILE - 1) // EXP_TILE) * EXP_TILE
        ends = jnp.cumsum(padded)
        offs = ends - padded
        n_used = (ends[-1:] // EXP_TILE).astype(I32)
        tile_start = jnp.arange(n_tiles, dtype=I32) * EXP_TILE
        tile_e = jnp.minimum(jnp.sum((ends[None, :] <= tile_start[:, None]).astype(I32), axis=1), N_EXPERTS - 1)
        eid = code[0:2] >> RANK_BITS
        rank = code[0:2] & ((1 << RANK_BITS) - 1)
        expert_ids = jnp.arange(N_EXPERTS, dtype=I32)[:, None, None]
        slots = (rank + jnp.sum(jnp.where(eid[None] == expert_ids, offs[:, None, None], 0), axis=0)).reshape(-1)

        fill = (jnp.any(tile_start[:, None] + EXP_TILE == ends[None, :], axis=1) | (tile_start >= ends[-1])).astype(I32)
        xs = _dispatch(slots, fill, h2, None if l == 0 else xs, n_slots)
        ys = _experts(tile_e, n_used, xs, w_exp_gate, w_exp_up, w_exp_down, l)
        x2d = _combine(slots, x1, wcol, final_norm_g[None, :], ys, final=(l == depth - 1))

    return x2d.reshape(batch, seq, D_MODEL)
```

```python
import functools

import jax
import jax.numpy as jnp
import numpy as np
from jax import lax
from jax.experimental import pallas as pl
from jax.experimental.pallas import tpu as pltpu

F32 = jnp.float32
BF16 = jnp.bfloat16
I32 = jnp.int32

D_MODEL = 1024
HEAD_DIM = 64
HALF_HEAD = HEAD_DIM // 2
ROPE_THETA = 10000.0
NORM_EPS = 1e-6
NEG_INF = -1e30
LANES = 128

A_Q_HEADS = 8
A_KV_HEADS = 2
A_GROUP = A_Q_HEADS // A_KV_HEADS
A_HALF_WINDOW = 128
A_Q_DIM = A_Q_HEADS * HEAD_DIM
A_KV_DIM = A_KV_HEADS * HEAD_DIM

B_GROUPS = ((128, 1), (512, 4), (2048, 16))
B_HEADS = 4
B_DIM = B_HEADS * HEAD_DIM
B_HALF_WINDOW = 64

MOE_GROUPS = 4
EXPERTS_PER_GROUP = 8
N_EXPERTS = MOE_GROUPS * EXPERTS_PER_GROUP
D_EXPERT = 256

NAT_IN_COLS = A_Q_DIM + 2 * A_KV_DIM + 3 * B_DIM
NAT_B = A_Q_DIM
NAT_AK = NAT_B + 3 * B_DIM
NAT_AV = NAT_AK + 2 * A_KV_DIM
NAT_COLS = NAT_AV + A_KV_DIM
GRP_COLS = 3 * B_DIM
COL_G1 = NAT_IN_COLS
COL_G2 = COL_G1 + GRP_COLS
COL_GATE = COL_G2 + GRP_COLS
GATE_COLS = 2 * D_MODEL
D_IN = COL_GATE + GATE_COLS

TOK_TILE = 512
IN_TILE = 512
W_CHUNK = 512
A_Q_TILE = 512
B_SUB = 128
B_Q_TILE = 512
EXP_TILE = 512
XS_SLOTS = 3
ROW_CHUNKS = D_MODEL // LANES
ROW_TILE = 512
DISPATCH_TILE = 2048
DMA_UNROLL = 8
RANK_BITS = 16
VMEM_LIMIT = 56 * 1024 * 1024


def _cparams(*sem):
    return pltpu.CompilerParams(dimension_semantics=sem, vmem_limit_bytes=VMEM_LIMIT)


def _rope_tables(seq_len):
    inv = (1.0 / (np.float32(ROPE_THETA) ** (np.arange(0, HEAD_DIM, 2, dtype=np.float32) / np.float32(HEAD_DIM))))
    ang = np.arange(seq_len, dtype=np.float32)[:, None] * inv.astype(np.float32)[None, :]
    cos, sin = np.cos(ang).astype(np.float32), np.sin(ang).astype(np.float32)
    cos_t = np.concatenate([cos, cos, cos, cos], axis=-1)
    sin_t = np.concatenate([-sin, sin, -sin, sin], axis=-1)
    return cos_t, sin_t


def _residue_order(table, dilation, tile):
    s, c = table.shape
    return table.reshape(s // tile, tile // dilation, dilation, c).transpose(0, 2, 1, 3).reshape(s, c)


def _rope(t, cos, sin_signed, first_half):
    partner = jnp.where(first_half, pltpu.roll(t, LANES - HALF_HEAD, 1), pltpu.roll(t, HALF_HEAD, 1))
    return t * cos + partner * sin_signed


Q_KIND, K_KIND, V_KIND = 0, 1, 2
_NAT_KINDS = ([Q_KIND] * 4 + [K_KIND] + [V_KIND] + [Q_KIND] * 2 + [K_KIND] * 2 + [V_KIND] * 2)
_GRP_KINDS = [Q_KIND] * 2 + [K_KIND] * 2 + [V_KIND] * 2


def _in_proj_kernel(x_ref, g_ref, w_hbm_ref, c1_ref, s1_ref, c4_ref, s4_ref, c16_ref, s16_ref,
                    nat_ref, gate_ref, g1_ref, g2_ref, hf_ref, hb_ref, hd_ref, w_ref, stage_ref, sem, *, layer):
    tm = x_ref.shape[0]

    @pl.when(pl.program_id(0) == 0)
    def _():
        def chunk(j):
            cols = pl.ds(j * W_CHUNK, W_CHUNK)
            return pltpu.make_async_copy(w_hbm_ref.at[layer, :, cols], stage_ref.at[j % 2], sem.at[j % 2])

        n = D_IN // W_CHUNK
        chunk(0).start()
        for j in range(n):
            if j + 1 < n:
                chunk(j + 1).start()
            chunk(j).wait()
            w_ref[:, j * W_CHUNK:(j + 1) * W_CHUNK] = stage_ref[j % 2].astype(BF16)

    x = x_ref[...]
    h = x * lax.rsqrt(jnp.mean(x * x, axis=-1, keepdims=True) + NORM_EPS) * g_ref[...]
    n_chunks = D_MODEL // LANES
    for c in range(n_chunks):
        hf_ref[c] = h[:, c * LANES:(c + 1) * LANES]
    hb_ref[...] = h.astype(BF16)
    lane = lax.broadcasted_iota(I32, (1, LANES), 1)
    first_half = (lane % HEAD_DIM) < HALF_HEAD

    def project(h_b, col0, kinds, cos_ref, sin_ref, store):
        width = 512
        for c0 in range(0, len(kinds) * LANES, width):
            w = min(width, len(kinds) * LANES - c0)
            res = jnp.dot(h_b, w_ref[:, col0 + c0:col0 + c0 + w], preferred_element_type=F32)
            for j in range(w // LANES):
                kind = kinds[(c0 // LANES) + j]
                t = res[:, j * LANES:(j + 1) * LANES]
                if kind != V_KIND:
                    t = _rope(t, cos_ref[...], sin_ref[...], first_half)
                if kind == Q_KIND:
                    t = t * (HEAD_DIM ** -0.5)
                store(c0 + j * LANES, t)

    low_head = lane < HEAD_DIM

    def store_nat(c, t):
        if A_Q_DIM <= c < A_Q_DIM + A_KV_DIM:
            swapped = pltpu.roll(t, HEAD_DIM, 1)
            nat_ref[:, NAT_AK:NAT_AK + LANES] = jnp.where(low_head, t, swapped).astype(BF16)
            nat_ref[:, NAT_AK + LANES:NAT_AK + 2 * LANES] = jnp.where(low_head, swapped, t).astype(BF16)
        elif c < A_Q_DIM + 2 * A_KV_DIM:
            out = c if c < A_Q_DIM else NAT_AV
            nat_ref[:, out:out + LANES] = t.astype(BF16)
        else:
            out = c - 2 * A_KV_DIM
            nat_ref[:, out:out + LANES] = t.astype(BF16)

    project(hb_ref[...], 0, _NAT_KINDS, c1_ref, s1_ref, store_nat)

    for c0 in range(0, GATE_COLS, 512):
        res = jnp.dot(hb_ref[...], w_ref[:, COL_GATE + c0:COL_GATE + c0 + 512], preferred_element_type=F32)
        gate_ref[:, c0:c0 + 512] = jax.nn.sigmoid(res).astype(BF16)

    for dil, col0, cos_ref, sin_ref, out_ref in ((4, COL_G1, c4_ref, s4_ref, g1_ref),
                                                 (16, COL_G2, c16_ref, s16_ref, g2_ref)):
        n = tm // dil
        for r in range(dil):
            for c in range(n_chunks):
                hd_ref[r * n:(r + 1) * n, c * LANES:(c + 1) * LANES] = (
                    hf_ref[c, pl.ds(r, n, stride=dil), :].astype(BF16))

        def store_grp(c, t, out_ref=out_ref, dil=dil, n=n):
            v = t.astype(BF16)
            for r in range(dil):
                out_ref[r, :, c:c + LANES] = v[r * n:(r + 1) * n]

        project(hd_ref[...], col0, _GRP_KINDS, cos_ref, sin_ref, store_grp)


def _in_proj(x2d, g, w_in, layer, tables, batch, seq):
    tm = IN_TILE
    tiles_per_seq = seq // tm
    n_tok = batch * seq
    c1, s1, c4, s4, c16, s16 = tables
    tab_spec = pl.BlockSpec((tm, LANES), lambda i: (i % tiles_per_seq, 0))
    return pl.pallas_call(
        functools.partial(_in_proj_kernel, layer=layer),
        grid=(n_tok // tm,),
        in_specs=[
            pl.BlockSpec((tm, D_MODEL), lambda i: (i, 0)),
            pl.BlockSpec((1, D_MODEL), lambda i: (0, 0)),
            pl.BlockSpec(memory_space=pl.ANY),
            tab_spec, tab_spec, tab_spec, tab_spec, tab_spec, tab_spec,
        ],
        out_specs=[
            pl.BlockSpec((tm, NAT_COLS), lambda i: (i, 0)),
            pl.BlockSpec((tm, GATE_COLS), lambda i: (i, 0)),
            pl.BlockSpec((None, 4, tm // 4, GRP_COLS), lambda i: (i // tiles_per_seq, 0, i % tiles_per_seq, 0)),
            pl.BlockSpec((None, 16, tm // 16, GRP_COLS), lambda i: (i // tiles_per_seq, 0, i % tiles_per_seq, 0)),
        ],
        out_shape=[
            jax.ShapeDtypeStruct((n_tok, NAT_COLS), BF16),
            jax.ShapeDtypeStruct((n_tok, GATE_COLS), BF16),
            jax.ShapeDtypeStruct((batch, 4, seq // 4, GRP_COLS), BF16),
            jax.ShapeDtypeStruct((batch, 16, seq // 16, GRP_COLS), BF16),
        ],
        scratch_shapes=[
            pltpu.VMEM((D_MODEL // LANES, tm, LANES), F32),
            pltpu.VMEM((tm, D_MODEL), BF16),
            pltpu.VMEM((tm, D_MODEL), BF16),
            pltpu.VMEM((D_MODEL, D_IN), BF16),
            pltpu.VMEM((2, D_MODEL, W_CHUNK), F32),
            pltpu.SemaphoreType.DMA((2,)),
        ],
        compiler_params=_cparams("arbitrary"),
        name="in_proj",
    )(x2d, g, w_in, c1, s1, c4, s4, c16, s16)


def _masked_heads(q_pair, low_head, high_head):
    zero = jnp.zeros_like(q_pair)
    return [jnp.where(low_head, q_pair, zero), jnp.where(high_head, q_pair, zero)]


def _attn_a_kernel(sink_ref, q_ref, kp_ref, km_ref, kn_ref, vp_ref, vm_ref, vn_ref, o_ref, k_ref, vt_ref, *, seq):
    tq = q_ref.shape[0]
    hw = A_HALF_WINDOW
    n_sub = tq // hw
    i = pl.program_id(1)
    last_blk = seq // hw - 1
    k_ref[0:hw, :] = kp_ref[...]
    k_ref[hw:hw + tq, :] = km_ref[...]
    k_ref[hw + tq:tq + 2 * hw, :] = kn_ref[...]
    for r0, src in ((0, vp_ref), (hw, vm_ref), (hw + tq, vn_ref)):
        vt_ref[:, r0:r0 + src.shape[0]] = src[...].astype(F32).T.astype(BF16)
    grp_cols = A_GROUP * hw
    key = lax.broadcasted_iota(I32, (hw, grp_cols), 0)
    qry = lax.broadcasted_iota(I32, (hw, grp_cols), 1) % hw
    low_head = lax.broadcasted_iota(I32, (1, LANES), 1) < HEAD_DIM
    high_head = jnp.logical_not(low_head)
    for sb in range(n_sub):
        r0 = sb * hw
        blk = i * n_sub + sb
        mask_p = (key >= qry) if sb > 0 else (key >= qry + jnp.where(blk > 0, 0, hw))
        mask_n = (key <= qry) if sb < n_sub - 1 else (key <= qry - jnp.where(blk < last_blk, 0, hw))
        out_t = []
        for g in range(A_KV_HEADS):
            heads = range(g * A_GROUP, (g + 1) * A_GROUP)
            q_parts = []
            for c in range(g * A_GROUP // 2, (g + 1) * A_GROUP // 2):
                q_parts += _masked_heads(q_ref[r0:r0 + hw, c * LANES:(c + 1) * LANES], low_head, high_head)
            q = jnp.concatenate(q_parts, axis=0)
            sink = jnp.concatenate([jnp.full((1, hw), sink_ref[h], F32) for h in heads], axis=1)
            k = k_ref[r0:r0 + 3 * hw, g * LANES:(g + 1) * LANES]
            s = lax.dot_general(k, q, (((1,), (1,)), ((), ())), preferred_element_type=F32)
            sp = jnp.where(mask_p, s[0:hw], NEG_INF)
            so = s[hw:2 * hw]
            sn = jnp.where(mask_n, s[2 * hw:3 * hw], NEG_INF)
            m = jnp.max(jnp.maximum(jnp.maximum(sp, so), sn), axis=0, keepdims=True)
            m = jnp.maximum(m, sink)
            pp, po, pn = jnp.exp(sp - m), jnp.exp(so - m), jnp.exp(sn - m)
            denom = jnp.sum(pp + po + pn, axis=0, keepdims=True) + jnp.exp(sink - m)
            p = jnp.concatenate([pp, po, pn], axis=0).astype(BF16)
            vt = vt_ref[g * HEAD_DIM:(g + 1) * HEAD_DIM, r0:r0 + 3 * hw]
            o = jnp.dot(vt, p, preferred_element_type=F32) * (1.0 / denom)
            out_t += [o[:, j * hw:(j + 1) * hw] for j in range(A_GROUP)]
        o_ref[r0:r0 + hw, :] = jnp.concatenate(out_t, axis=0).T.astype(BF16)


def _attn_a(nat3d, sink):
    batch, seq, _ = nat3d.shape
    tq = A_Q_TILE
    hw = A_HALF_WINDOW
    per = tq // hw
    n_hw = seq // hw
    k_cols, v_cols = 2 * A_KV_DIM, A_KV_DIM
    k_blk, v_blk = NAT_AK // k_cols, NAT_AV // v_cols

    def prev_spec(cols, blk):
        return pl.BlockSpec((None, hw, cols), lambda b, i: (b, jnp.maximum(i * per - 1, 0), blk))

    def main_spec(cols, blk):
        return pl.BlockSpec((None, tq, cols), lambda b, i: (b, i, blk))

    def next_spec(cols, blk):
        return pl.BlockSpec((None, hw, cols), lambda b, i: (b, jnp.minimum((i + 1) * per, n_hw - 1), blk))

    return pl.pallas_call(
        functools.partial(_attn_a_kernel, seq=seq),
        grid=(batch, seq // tq),
        in_specs=[
            pl.BlockSpec(memory_space=pltpu.SMEM),
            main_spec(A_Q_DIM, 0),
            prev_spec(k_cols, k_blk), main_spec(k_cols, k_blk), next_spec(k_cols, k_blk),
            prev_spec(v_cols, v_blk), main_spec(v_cols, v_blk), next_spec(v_cols, v_blk),
        ],
        out_specs=pl.BlockSpec((None, tq, A_Q_DIM), lambda b, i: (b, i, 0)),
        out_shape=jax.ShapeDtypeStruct((batch, seq, A_Q_DIM), BF16),
        scratch_shapes=[pltpu.VMEM((tq + 2 * hw, k_cols), BF16), pltpu.VMEM((v_cols, tq + 2 * hw), BF16)],
        compiler_params=_cparams("parallel", "parallel"),
        name="attn_a",
    )(sink, nat3d, nat3d, nat3d, nat3d, nat3d, nat3d, nat3d)


def _attn_b_kernel(q_ref, kp_ref, km_ref, kn_ref, vp_ref, vm_ref, vn_ref, o_ref, lse_ref,
                   k_ref, vt_ref, *, sub_len):
    for s in range(q_ref.shape[0]):
        _attn_b_one(q_ref.at[s], kp_ref.at[s], km_ref.at[s], kn_ref.at[s], vp_ref.at[s], vm_ref.at[s], vn_ref.at[s],
                    o_ref.at[s], lse_ref.at[s], k_ref, vt_ref, sub_len)


def _attn_b_one(q_ref, kp_ref, km_ref, kn_ref, vp_ref, vm_ref, vn_ref, o_ref, lse_ref, k_ref, vt_ref, sub_len):
    tq = q_ref.shape[0]
    hw = B_HALF_WINDOW
    t0 = pl.program_id(1) * tq
    k_ref[0:hw, :] = kp_ref[...]
    k_ref[hw:hw + tq, :] = km_ref[...]
    k_ref[hw + tq:tq + 2 * hw, :] = kn_ref[...]
    for r0, src in ((0, vp_ref), (hw, vm_ref), (hw + tq, vn_ref)):
        vt_ref[:, r0:r0 + src.shape[0]] = src[...].astype(F32).T.astype(BF16)
    kw = B_SUB + 2 * hw
    n_sub = tq // B_SUB
    all_cols = B_HEADS * B_SUB
    key = lax.broadcasted_iota(I32, (kw, all_cols), 0)
    qry = lax.broadcasted_iota(I32, (kw, all_cols), 1) % B_SUB
    in_band = jnp.abs(key - hw - qry) <= hw
    low_head = lax.broadcasted_iota(I32, (1, LANES), 1) < HEAD_DIM
    high_head = jnp.logical_not(low_head)
    for sb in range(n_sub):
        r0 = sb * B_SUB
        valid = in_band
        if sb == 0:
            valid = valid & (t0 - hw + key >= 0)
        if sb == n_sub - 1:
            valid = valid & (t0 + r0 - hw + key < sub_len)
        s_parts = []
        for c in range(B_HEADS // 2):
            cs = slice(c * LANES, (c + 1) * LANES)
            q = jnp.concatenate(_masked_heads(q_ref[r0:r0 + B_SUB, cs], low_head, high_head), axis=0)
            s_parts.append(lax.dot_general(k_ref[r0:r0 + kw, cs], q, (((1,), (1,)), ((), ())),
                                           preferred_element_type=F32))
        s = jnp.where(valid, jnp.concatenate(s_parts, axis=1), NEG_INF)
        m = jnp.max(s, axis=0, keepdims=True)
        p = jnp.exp(s - m)
        denom = jnp.sum(p, axis=0, keepdims=True)
        p = p.astype(BF16)
        inv = 1.0 / denom
        lse = m + jnp.log(denom)
        out_t, lse_t = [], []
        for h in range(B_HEADS):
            qs = slice(h * B_SUB, (h + 1) * B_SUB)
            vt = vt_ref[h * HEAD_DIM:(h + 1) * HEAD_DIM, r0:r0 + kw]
            out_t.append(jnp.dot(vt, p[:, qs], preferred_element_type=F32) * inv[:, qs])
            lse_t.append(jnp.broadcast_to(lse[:, qs], (HEAD_DIM, B_SUB)))
        o_ref[r0:r0 + B_SUB, :] = jnp.concatenate(out_t, axis=0).T
        lse_ref[r0:r0 + B_SUB, :] = jnp.concatenate(lse_t, axis=0).T


def _attn_b(arr3d, q_blk, name):
    n_sub, sub_len, _ = arr3d.shape
    tq = min(B_Q_TILE, sub_len)
    n_per = B_Q_TILE // tq
    hw = B_HALF_WINDOW
    per = tq // hw
    n_hw = sub_len // hw

    def main_spec(c):
        return pl.BlockSpec((n_per, tq, B_DIM), lambda g, i: (g, i, c))

    def prev_spec(c):
        return pl.BlockSpec((n_per, hw, B_DIM), lambda g, i: (g, jnp.maximum(i * per - 1, 0), c))

    def next_spec(c):
        return pl.BlockSpec((n_per, hw, B_DIM), lambda g, i: (g, jnp.minimum((i + 1) * per, n_hw - 1), c))

    out_spec = pl.BlockSpec((n_per, tq, B_DIM), lambda g, i: (g, i, 0))
    return pl.pallas_call(
        functools.partial(_attn_b_kernel, sub_len=sub_len),
        grid=(n_sub // n_per, sub_len // tq),
        in_specs=[main_spec(q_blk),
                  prev_spec(q_blk + 1), main_spec(q_blk + 1), next_spec(q_blk + 1),
                  prev_spec(q_blk + 2), main_spec(q_blk + 2), next_spec(q_blk + 2)],
        out_specs=[out_spec, out_spec],
        out_shape=[jax.ShapeDtypeStruct((n_sub, sub_len, B_DIM), F32)] * 2,
        scratch_shapes=[pltpu.VMEM((tq + 2 * hw, B_DIM), BF16), pltpu.VMEM((B_DIM, tq + 2 * hw), BF16)],
        compiler_params=_cparams("parallel", "parallel"),
        name=name,
    )(arr3d, arr3d, arr3d, arr3d, arr3d, arr3d, arr3d)


def _merge_kernel(x_ref, ya_ref, o0_ref, l0_ref, o1_ref, l1_ref, o2_ref, l2_ref, gate_ref,
                  wa_ref, wb_ref, wo_ref, g2_ref, wr_ref, br_ref,
                  x1_ref, h2_ref, code_ref, wcol_ref, cnt_ref,
                  so1_ref, sl1_ref, so2_ref, sl2_ref, yb_ref, run_ref, earlier_ref,
                  wab_ref, wbb_ref, wob_ref):
    tm = x_ref.shape[0]

    @pl.when(pl.program_id(0) == 0)
    def _():
        run_ref[...] = jnp.zeros_like(run_ref)
        wab_ref[...] = wa_ref[...].astype(BF16)
        wbb_ref[...] = wb_ref[...].astype(BF16)
        wob_ref[...] = wo_ref[...].astype(BF16)
        row = lax.broadcasted_iota(I32, (tm, tm), 0)
        col = lax.broadcasted_iota(I32, (tm, tm), 1)
        earlier_ref[...] = (row < col).astype(BF16)

    for dil, src_o, src_l, dst_o, dst_l in ((4, o1_ref, l1_ref, so1_ref, sl1_ref),
                                            (16, o2_ref, l2_ref, so2_ref, sl2_ref)):
        n = tm // dil
        for r in range(dil):
            for c in range(B_DIM // LANES):
                cs = slice(c * LANES, (c + 1) * LANES)
                dst_o[c, pl.ds(r, n, stride=dil), :] = src_o[r, :, cs]
                dst_l[c, pl.ds(r, n, stride=dil), :] = src_l[r, :, cs]

    for c in range(B_DIM // LANES):
        cs = slice(c * LANES, (c + 1) * LANES)
        l0, l1, l2 = l0_ref[:, cs], sl1_ref[c], sl2_ref[c]
        m = jnp.maximum(jnp.maximum(l0, l1), l2)
        e0, e1, e2 = jnp.exp(l0 - m), jnp.exp(l1 - m), jnp.exp(l2 - m)
        yb = (e0 * o0_ref[:, cs] + e1 * so1_ref[c] + e2 * so2_ref[c]) / (e0 + e1 + e2)
        yb_ref[:, cs] = yb.astype(BF16)

    ya_p = jnp.dot(ya_ref[...], wab_ref[...], preferred_element_type=F32)
    yb_p = jnp.dot(yb_ref[...], wbb_ref[...], preferred_element_type=F32)
    merged = gate_ref[:, 0:D_MODEL].astype(F32) * ya_p + gate_ref[:, D_MODEL:GATE_COLS].astype(F32) * yb_p
    x1 = x_ref[...] + jnp.dot(merged.astype(BF16), wob_ref[...], preferred_element_type=F32)
    x1_ref[...] = x1

    h2 = x1 * lax.rsqrt(jnp.mean(x1 * x1, axis=-1, keepdims=True) + NORM_EPS) * g2_ref[...]
    _store_row_tiles(h2_ref, h2)

    h_hi = h2.astype(BF16)
    h_lo = (h2 - h_hi.astype(F32)).astype(BF16)
    hi_terms = jnp.dot(h_hi, wr_ref[...], preferred_element_type=F32)
    logits = (hi_terms[:, 0:LANES] + hi_terms[:, LANES:2 * LANES]
              + jnp.dot(h_lo, wr_ref[:, 0:LANES], preferred_element_type=F32)) + br_ref[...]

    logits_t = logits.T
    sub = lax.broadcasted_iota(I32, (EXPERTS_PER_GROUP, tm), 0).astype(F32)
    none = float(EXPERTS_PER_GROUP)
    gl = jnp.where(sub < MOE_GROUPS, logits_t[N_EXPERTS:N_EXPERTS + EXPERTS_PER_GROUP], -jnp.inf)
    gmax = jnp.max(gl, axis=0, keepdims=True)
    gidx = jnp.min(jnp.where(gl == gmax, sub, none), axis=0, keepdims=True)
    gw = 1.0 / jnp.sum(jnp.exp(gl - gmax), axis=0, keepdims=True)
    el = logits_t[0:EXPERTS_PER_GROUP]
    for g in range(1, MOE_GROUPS):
        el = jnp.where(gidx == g, logits_t[g * EXPERTS_PER_GROUP:(g + 1) * EXPERTS_PER_GROUP], el)
    v1 = jnp.max(el, axis=0, keepdims=True)
    i1 = jnp.min(jnp.where(el == v1, sub, none), axis=0, keepdims=True)
    el2 = jnp.where(sub == i1, -jnp.inf, el)
    v2 = jnp.max(el2, axis=0, keepdims=True)
    i2 = jnp.min(jnp.where(el2 == v2, sub, none), axis=0, keepdims=True)
    t = jnp.exp(v2 - v1)
    w1 = gw / (1.0 + t)
    w2 = gw * t / (1.0 + t)
    e1 = gidx * EXPERTS_PER_GROUP + i1
    e2 = gidx * EXPERTS_PER_GROUP + i2

    expert = lax.broadcasted_iota(I32, (N_EXPERTS, tm), 0).astype(F32)
    oh1 = (expert == e1).astype(F32)
    oh2 = (expert == e2).astype(F32)
    oh = oh1 + oh2
    run = run_ref[...]
    before = (jnp.dot(oh.astype(BF16), earlier_ref[...], preferred_element_type=F32)
              + jnp.concatenate([run] * (tm // LANES), axis=1))
    rank1 = jnp.sum(before * oh1, axis=0, keepdims=True)
    rank2 = jnp.sum(before * oh2, axis=0, keepdims=True)
    run_ref[...] = run + jnp.sum(oh, axis=1, keepdims=True)
    cnt_ref[...] = run_ref[...]

    scale = float(1 << RANK_BITS)
    zeros = jnp.zeros((6, tm), F32)
    code_ref[...] = jnp.concatenate([e1 * scale + rank1, e2 * scale + rank2, zeros], axis=0).astype(I32)
    w_rows = jnp.concatenate([w1, w2, jnp.zeros((LANES - 2, tm), F32)], axis=0)
    wcol_ref[...] = w_rows.T


def _merge(x2d, ya, o0, l0, o1, l1, o2, l2, gates, wa, wb, wo, g2, wr, br, batch, seq):
    tm = TOK_TILE
    tps = seq // tm
    n_tok = batch * seq

    def tok(c):
        return pl.BlockSpec((tm, c), lambda i: (i, 0))

    def full(a):
        return pl.BlockSpec(a.shape, lambda i: (0,) * a.ndim)

    def res_spec(d):
        return pl.BlockSpec((None, d, tm // d, B_DIM), lambda i: (i // tps, 0, i % tps, 0))

    return pl.pallas_call(
        _merge_kernel,
        grid=(n_tok // tm,),
        in_specs=[tok(D_MODEL), tok(A_Q_DIM), tok(B_DIM), tok(B_DIM),
                  res_spec(4), res_spec(4), res_spec(16), res_spec(16), tok(GATE_COLS),
                  full(wa), full(wb), full(wo), full(g2), full(wr), full(br)],
        out_specs=[tok(D_MODEL), pl.BlockSpec((tm * ROW_CHUNKS, LANES), lambda i: (i, 0)),
                   pl.BlockSpec((8, tm), lambda i: (0, i)),
                   tok(LANES),
                   pl.BlockSpec((N_EXPERTS, LANES), lambda i: (0, 0))],
        out_shape=[jax.ShapeDtypeStruct((n_tok, D_MODEL), F32),
                   jax.ShapeDtypeStruct((n_tok * ROW_CHUNKS, LANES), F32),
                   jax.ShapeDtypeStruct((8, n_tok), I32),
                   jax.ShapeDtypeStruct((n_tok, LANES), F32),
                   jax.ShapeDtypeStruct((N_EXPERTS, LANES), F32)],
        scratch_shapes=([pltpu.VMEM((B_DIM // LANES, tm, LANES), F32)] * 4
                        + [pltpu.VMEM((tm, B_DIM), BF16), pltpu.VMEM((N_EXPERTS, LANES), F32),
                           pltpu.VMEM((tm, tm), BF16),
                           pltpu.VMEM(wa.shape, BF16), pltpu.VMEM(wb.shape, BF16), pltpu.VMEM(wo.shape, BF16)]),
        compiler_params=_cparams("arbitrary"),
        name="merge_route",
    )(x2d, ya, o0, l0, o1, l1, o2, l2, gates, wa, wb, wo, g2, wr, br)


def _row_tile(ref, t):
    return ref.at[pl.ds(pl.multiple_of(t * ROW_CHUNKS, ROW_CHUNKS), ROW_CHUNKS)]


def _store_row_tiles(ref, val):
    rows = val.shape[0]
    for c in range(ROW_CHUNKS):
        ref[pl.ds(c, rows, stride=ROW_CHUNKS), :] = val[:, c * LANES:(c + 1) * LANES]


def _load_row_tiles_chunk(ref, c):
    return ref[pl.ds(c, ref.shape[0] // ROW_CHUNKS, stride=ROW_CHUNKS), :]


def _dispatch_kernel(slot_ref, fill_ref, h_ref, *rest, n_tok, reuse):
    if reuse:
        _, xs_ref, sem = rest
    else:
        xs_ref, zero_ref, sem, zsem = rest
    i = pl.program_id(0)
    rows = h_ref.shape[0] // ROW_CHUNKS

    if not reuse:
        @pl.when(i == 0)
        def _():
            tile_rows = zero_ref.shape[0]
            zero_ref[...] = jnp.zeros_like(zero_ref)

            def zero_copy(t):
                dst = xs_ref.at[pl.ds(pl.multiple_of(t * tile_rows, tile_rows), tile_rows)]
                return pltpu.make_async_copy(zero_ref, dst, zsem)

            def start(t, carry):
                @pl.when(fill_ref[t] != 0)
                def _():
                    zero_copy(t).start()
                return carry

            def wait(t, carry):
                @pl.when(fill_ref[t] != 0)
                def _():
                    zero_copy(t).wait()
                return carry

            n_tiles = xs_ref.shape[0] // tile_rows
            lax.fori_loop(0, n_tiles, start, 0)
            lax.fori_loop(0, n_tiles, wait, 0)

    def issue(j, carry):
        t = i * rows + j
        for k in range(2):
            pltpu.make_async_copy(_row_tile(h_ref, j), _row_tile(xs_ref, slot_ref[k * n_tok + t]), sem).start(priority=k)
        return carry

    lax.fori_loop(0, rows, issue, 0, unroll=DMA_UNROLL)
    for _ in range(2):
        pltpu.make_async_copy(h_ref, xs_ref.at[pl.ds(0, rows * ROW_CHUNKS)], sem).wait()


def _dispatch(slots, fill, h2, xs_prev, n_slots):
    n_tok = h2.shape[0] // ROW_CHUNKS
    rows = DISPATCH_TILE
    reuse = xs_prev is not None
    h_spec = pl.BlockSpec((rows * ROW_CHUNKS, LANES), lambda i, s, f: (i, 0))
    any_spec = pl.BlockSpec(memory_space=pl.ANY)
    scratch = [pltpu.SemaphoreType.DMA(())] if reuse else [
        pltpu.VMEM((EXP_TILE * ROW_CHUNKS, LANES), F32), pltpu.SemaphoreType.DMA(()), pltpu.SemaphoreType.DMA(())]
    return pl.pallas_call(
        functools.partial(_dispatch_kernel, n_tok=n_tok, reuse=reuse),
        grid_spec=pltpu.PrefetchScalarGridSpec(
            num_scalar_prefetch=2,
            grid=(n_tok // rows,),
            in_specs=[h_spec, any_spec] if reuse else [h_spec],
            out_specs=any_spec,
            scratch_shapes=scratch,
        ),
        out_shape=jax.ShapeDtypeStruct((n_slots * ROW_CHUNKS, LANES), F32),
        input_output_aliases={3: 0} if reuse else {},
        compiler_params=_cparams("arbitrary"),
        name="dispatch_reuse" if reuse else "dispatch",
    )(*((slots, fill, h2, xs_prev) if reuse else (slots, fill, h2)))


def _combine_kernel(slot_ref, x_ref, w_ref, g_ref, ys_ref, o_ref, buf_ref, y_ref, sem, *, n_tok, final):
    i = pl.program_id(0)
    n_steps = pl.num_programs(0)
    rows = x_ref.shape[0]

    def issue_tile(tile, slot):
        def issue(j, carry):
            t = tile * rows + j
            for k in range(2):
                pltpu.make_async_copy(_row_tile(ys_ref, slot_ref[k * n_tok + t]),
                                      _row_tile(buf_ref.at[slot, k], j), sem.at[slot]).start(priority=k)
            return carry

        lax.fori_loop(0, rows, issue, 0, unroll=DMA_UNROLL)

    @pl.when(i == 0)
    def _():
        issue_tile(0, 0)

    @pl.when(i + 1 < n_steps)
    def _():
        issue_tile(i + 1, (i + 1) % 2)

    cur = i % 2
    for k in range(2):
        pltpu.make_async_copy(ys_ref.at[pl.ds(0, rows * ROW_CHUNKS)], buf_ref.at[cur, k], sem.at[cur]).wait()
    w = w_ref[...]
    w1, w2 = w[:, 0:1], w[:, 1:2]
    for c in range(ROW_CHUNKS):
        cs = slice(c * LANES, (c + 1) * LANES)
        y1 = _load_row_tiles_chunk(buf_ref.at[cur, 0], c)
        y2 = _load_row_tiles_chunk(buf_ref.at[cur, 1], c)
        (y_ref if final else o_ref)[:, cs] = x_ref[:, cs] + w1 * y1 + w2 * y2
    if final:
        y = y_ref[...]
        o_ref[...] = y * lax.rsqrt(jnp.mean(y * y, axis=-1, keepdims=True) + NORM_EPS) * g_ref[...]


def _combine(slots, x1, wcol, g, ys, final):
    n_tok = x1.shape[0]
    rows = ROW_TILE
    return pl.pallas_call(
        functools.partial(_combine_kernel, n_tok=n_tok, final=final),
        grid_spec=pltpu.PrefetchScalarGridSpec(
            num_scalar_prefetch=1,
            grid=(n_tok // rows,),
            in_specs=[pl.BlockSpec((rows, D_MODEL), lambda i, s: (i, 0)),
                      pl.BlockSpec((rows, LANES), lambda i, s: (i, 0)),
                      pl.BlockSpec((1, D_MODEL), lambda i, s: (0, 0)),
                      pl.BlockSpec(memory_space=pl.ANY)],
            out_specs=pl.BlockSpec((rows, D_MODEL), lambda i, s: (i, 0)),
            scratch_shapes=[pltpu.VMEM((2, 2, rows * ROW_CHUNKS, LANES), F32),
                            pltpu.VMEM((rows, D_MODEL), F32),
                            pltpu.SemaphoreType.DMA((2,))],
        ),
        out_shape=jax.ShapeDtypeStruct((n_tok, D_MODEL), F32),
        compiler_params=_cparams("arbitrary"),
        name="combine",
    )(slots, x1, wcol, g, ys)


def _experts_kernel(tile_e_ref, n_used_ref, xs_ref, wg_ref, wu_ref, wd_ref, ys_ref, xbuf_ref, x_ref, sem):
    del tile_e_ref
    i = pl.program_id(0)
    n_used = n_used_ref[0]
    tile_rows = xbuf_ref.shape[1]

    def fetch(tile):
        rows = pl.ds(pl.multiple_of(tile * tile_rows, tile_rows), tile_rows)
        slot = tile % XS_SLOTS
        return pltpu.make_async_copy(xs_ref.at[rows], xbuf_ref.at[slot], sem.at[slot])

    @pl.when(i == 0)
    def _():
        fetch(0).start()

        @pl.when(n_used > 1)
        def _():
            fetch(1).start()

    @pl.when(i < n_used)
    def _():
        @pl.when(i + XS_SLOTS - 1 < n_used)
        def _():
            fetch(i + XS_SLOTS - 1).start()

        fetch(i).wait()
        xb = xbuf_ref.at[i % XS_SLOTS]
        for c in range(ROW_CHUNKS):
            x_ref[:, c * LANES:(c + 1) * LANES] = _load_row_tiles_chunk(xb, c).astype(BF16)
        x = x_ref[...]
        a = jnp.dot(x, wg_ref[...].astype(BF16), preferred_element_type=F32)
        u = jnp.dot(x, wu_ref[...].astype(BF16), preferred_element_type=F32)
        z = (a * jax.nn.sigmoid(a)) * u
        _store_row_tiles(ys_ref, jnp.dot(z.astype(BF16), wd_ref[...].astype(BF16), preferred_element_type=F32))

    @pl.when(i >= n_used)
    def _():
        ys_ref[...] = jnp.zeros_like(ys_ref)


def _experts(tile_e, n_used, xs, wg, wu, wd, layer):
    n_slots = xs.shape[0] // ROW_CHUNKS
    te = EXP_TILE

    def out_map(i, tile_e, n_used):
        return (i, 0)

    def w_map(i, tile_e, n_used):
        return (layer, tile_e[i], 0, 0)

    return pl.pallas_call(
        _experts_kernel,
        grid_spec=pltpu.PrefetchScalarGridSpec(
            num_scalar_prefetch=2,
            grid=(n_slots // te,),
            in_specs=[pl.BlockSpec(memory_space=pl.ANY),
                      pl.BlockSpec((None, None, D_MODEL, D_EXPERT), w_map),
                      pl.BlockSpec((None, None, D_MODEL, D_EXPERT), w_map),
                      pl.BlockSpec((None, None, D_EXPERT, D_MODEL), w_map)],
            out_specs=pl.BlockSpec((te * ROW_CHUNKS, LANES), out_map),
            scratch_shapes=[pltpu.VMEM((XS_SLOTS, te * ROW_CHUNKS, LANES), F32),
                            pltpu.VMEM((te, D_MODEL), BF16),
                            pltpu.SemaphoreType.DMA((XS_SLOTS,))],
        ),
        out_shape=jax.ShapeDtypeStruct((n_slots * ROW_CHUNKS, LANES), F32),
        compiler_params=_cparams("arbitrary"),
        name="experts",
    )(tile_e, n_used, xs, wg, wu, wd)


def kernel(x, attn_norm_g, w_in, a_sink, w_branch_a, w_branch_b, w_out, ffn_norm_g,
           w_router_group, b_router_group, w_router_expert, b_router_expert,
           w_exp_gate, w_exp_up, w_exp_down, final_norm_g):
    batch, seq, d_model = x.shape
    depth = w_in.shape[0]
    n_tok = batch * seq
    assert d_model == D_MODEL and w_in.shape[2] == D_IN
    assert seq % (16 * B_SUB) == 0 and seq % IN_TILE == 0 and n_tok % DISPATCH_TILE == 0
    assert n_tok < (1 << RANK_BITS)

    cos_t, sin_t = _rope_tables(seq)
    tables = (cos_t, sin_t,
              _residue_order(cos_t, 4, IN_TILE), _residue_order(sin_t, 4, IN_TILE),
              _residue_order(cos_t, 16, IN_TILE), _residue_order(sin_t, 16, IN_TILE))

    n_slots = 2 * n_tok + N_EXPERTS * EXP_TILE
    n_tiles = n_slots // EXP_TILE
    x2d = x.reshape(n_tok, D_MODEL)

    for l in range(depth):
        nat, gates, grp1, grp2 = _in_proj(x2d, attn_norm_g[l][None, :], w_in, l, tables, batch, seq)
        nat3d = nat.reshape(batch, seq, NAT_COLS)
        ya = _attn_a(nat3d, a_sink[l]).reshape(n_tok, A_Q_DIM)
        o0, l0 = _attn_b(nat3d, NAT_B // B_DIM, "attn_b1")
        o1, l1 = _attn_b(grp1.reshape(batch * 4, seq // 4, GRP_COLS), 0, "attn_b4")
        o2, l2 = _attn_b(grp2.reshape(batch * 16, seq // 16, GRP_COLS), 0, "attn_b16")

        wr = jnp.zeros((D_MODEL, LANES), F32)
        wr = wr.at[:, 0:N_EXPERTS].set(w_router_expert[l]).at[:, N_EXPERTS:N_EXPERTS + MOE_GROUPS].set(w_router_group[l])
        br = jnp.zeros((1, LANES), F32)
        br = br.at[0, 0:N_EXPERTS].set(b_router_expert[l]).at[0, N_EXPERTS:N_EXPERTS + MOE_GROUPS].set(b_router_group[l])
        wr_hi = wr.astype(BF16)
        wr_lo = (wr - wr_hi.astype(F32)).astype(BF16)
        wr_stack = jnp.concatenate([wr_hi, wr_lo], axis=1)

        x1, h2, code, wcol, cnt = _merge(
            x2d, ya, o0.reshape(n_tok, B_DIM), l0.reshape(n_tok, B_DIM),
            o1.reshape(batch, 4, seq // 4, B_DIM), l1.reshape(batch, 4, seq // 4, B_DIM),
            o2.reshape(batch, 16, seq // 16, B_DIM), l2.reshape(batch, 16, seq // 16, B_DIM),
            gates, w_branch_a[l], w_branch_b[l], w_out[l],
            ffn_norm_g[l][None, :], wr_stack, br, batch, seq)

        counts = cnt[:, 0].astype(I32)
        padded = ((counts + EXP_TILE - 1) // EXP_TILE) * EXP_TILE
        ends = jnp.cumsum(padded)
        offs = ends - padded
        n_used = (ends[-1:] // EXP_TILE).astype(I32)
        tile_start = jnp.arange(n_tiles, dtype=I32) * EXP_TILE
        tile_e = jnp.minimum(jnp.sum((ends[None, :] <= tile_start[:, None]).astype(I32), axis=1), N_EXPERTS - 1)
        eid = code[0:2] >> RANK_BITS
        rank = code[0:2] & ((1 << RANK_BITS) - 1)
        expert_ids = jnp.arange(N_EXPERTS, dtype=I32)[:, None, None]
        slots = (rank + jnp.sum(jnp.where(eid[None] == expert_ids, offs[:, None, None], 0), axis=0)).reshape(-1)

        fill = (jnp.any(tile_start[:, None] + EXP_TILE == ends[None, :], axis=1) | (tile_start >= ends[-1])).astype(I32)
        xs = _dispatch(slots, fill, h2, None if l == 0 else xs, n_slots)
        ys = _experts(tile_e, n_used, xs, w_exp_gate, w_exp_up, w_exp_down, l)
        x2d = _combine(slots, x1, wcol, final_norm_g[None, :], ys, final=(l == depth - 1))

    return x2d.reshape(batch, seq, D_MODEL)
```

```python
import functools

import jax
import jax.numpy as jnp
import numpy as np
from jax import lax
from jax.experimental import pallas as pl
from jax.experimental.pallas import tpu as pltpu

F32 = jnp.float32
BF16 = jnp.bfloat16
I32 = jnp.int32

D_MODEL = 1024
HEAD_DIM = 64
HALF_HEAD = HEAD_DIM // 2
ROPE_THETA = 10000.0
NORM_EPS = 1e-6
NEG_INF = -1e30
LANES = 128

A_Q_HEADS = 8
A_KV_HEADS = 2
A_GROUP = A_Q_HEADS // A_KV_HEADS
A_HALF_WINDOW = 128
A_Q_DIM = A_Q_HEADS * HEAD_DIM
A_KV_DIM = A_KV_HEADS * HEAD_DIM

B_GROUPS = ((128, 1), (512, 4), (2048, 16))
B_HEADS = 4
B_DIM = B_HEADS * HEAD_DIM
B_HALF_WINDOW = 64

MOE_GROUPS = 4
EXPERTS_PER_GROUP = 8
N_EXPERTS = MOE_GROUPS * EXPERTS_PER_GROUP
D_EXPERT = 256

NAT_IN_COLS = A_Q_DIM + 2 * A_KV_DIM + 3 * B_DIM
NAT_B = A_Q_DIM
NAT_AK = NAT_B + 3 * B_DIM
NAT_AV = NAT_AK + 2 * A_KV_DIM
NAT_COLS = NAT_AV + A_KV_DIM
GRP_COLS = 3 * B_DIM
COL_G1 = NAT_IN_COLS
COL_G2 = COL_G1 + GRP_COLS
COL_GATE = COL_G2 + GRP_COLS
GATE_COLS = 2 * D_MODEL
D_IN = COL_GATE + GATE_COLS

TOK_TILE = 512
IN_TILE = 512
W_CHUNK = 512
A_Q_TILE = 512
B_SUB = 128
T_ROWS = 512
B_Q_TILE = 2048
EXP_TILE = 512
XS_SLOTS = 3
ROW_CHUNKS = D_MODEL // LANES
ROW_TILE = 512
DISPATCH_TILE = 2048
DMA_UNROLL = 8
RANK_BITS = 16
VMEM_LIMIT = 56 * 1024 * 1024


def _cparams(*sem):
    return pltpu.CompilerParams(dimension_semantics=sem, vmem_limit_bytes=VMEM_LIMIT)


def _rope_tables(seq_len):
    inv = (1.0 / (np.float32(ROPE_THETA) ** (np.arange(0, HEAD_DIM, 2, dtype=np.float32) / np.float32(HEAD_DIM))))
    ang = np.arange(seq_len, dtype=np.float32)[:, None] * inv.astype(np.float32)[None, :]
    cos, sin = np.cos(ang).astype(np.float32), np.sin(ang).astype(np.float32)
    cos_t = np.concatenate([cos, cos, cos, cos], axis=-1)
    sin_t = np.concatenate([-sin, sin, -sin, sin], axis=-1)
    return cos_t, sin_t


def _residue_order(table, dilation, tile):
    s, c = table.shape
    return table.reshape(s // tile, tile // dilation, dilation, c).transpose(0, 2, 1, 3).reshape(s, c)


def _rope(t, cos, sin_signed, first_half):
    partner = jnp.where(first_half, pltpu.roll(t, LANES - HALF_HEAD, 1), pltpu.roll(t, HALF_HEAD, 1))
    return t * cos + partner * sin_signed


Q_KIND, K_KIND, V_KIND = 0, 1, 2
_NAT_KINDS = ([Q_KIND] * 4 + [K_KIND] + [V_KIND] + [Q_KIND] * 2 + [K_KIND] * 2 + [V_KIND] * 2)
_GRP_KINDS = [Q_KIND] * 2 + [K_KIND] * 2 + [V_KIND] * 2


def _in_proj_kernel(x_ref, g_ref, w_hbm_ref, c1_ref, s1_ref, c4_ref, s4_ref, c16_ref, s16_ref,
                    nat_ref, gate_ref, g1_ref, g2_ref, hf_ref, hb_ref, hd_ref, w_ref, stage_ref, sem, *, layer):
    tm = x_ref.shape[0]

    @pl.when(pl.program_id(0) == 0)
    def _():
        def chunk(j):
            cols = pl.ds(j * W_CHUNK, W_CHUNK)
            return pltpu.make_async_copy(w_hbm_ref.at[layer, :, cols], stage_ref.at[j % 2], sem.at[j % 2])

        n = D_IN // W_CHUNK
        chunk(0).start()
        for j in range(n):
            if j + 1 < n:
                chunk(j + 1).start()
            chunk(j).wait()
            w_ref[:, j * W_CHUNK:(j + 1) * W_CHUNK] = stage_ref[j % 2].astype(BF16)

    x = x_ref[...]
    h = x * lax.rsqrt(jnp.mean(x * x, axis=-1, keepdims=True) + NORM_EPS) * g_ref[...]
    n_chunks = D_MODEL // LANES
    for c in range(n_chunks):
        hf_ref[c] = h[:, c * LANES:(c + 1) * LANES]
    hb_ref[...] = h.astype(BF16)
    lane = lax.broadcasted_iota(I32, (1, LANES), 1)
    first_half = (lane % HEAD_DIM) < HALF_HEAD

    def project(h_b, col0, kinds, cos_ref, sin_ref, store):
        width = W_CHUNK
        for c0 in range(0, len(kinds) * LANES, width):
            w = min(width, len(kinds) * LANES - c0)
            res = jnp.dot(h_b, w_ref[:, col0 + c0:col0 + c0 + w], preferred_element_type=F32)
            for j in range(w // LANES):
                kind = kinds[(c0 // LANES) + j]
                t = res[:, j * LANES:(j + 1) * LANES]
                if kind != V_KIND:
                    t = _rope(t, cos_ref[...], sin_ref[...], first_half)
                if kind == Q_KIND:
                    t = t * (HEAD_DIM ** -0.5)
                store(c0 + j * LANES, t)

    low_head = lane < HEAD_DIM

    def store_nat(c, t):
        if A_Q_DIM <= c < A_Q_DIM + A_KV_DIM:
            swapped = pltpu.roll(t, HEAD_DIM, 1)
            nat_ref[:, NAT_AK:NAT_AK + LANES] = jnp.where(low_head, t, swapped).astype(BF16)
            nat_ref[:, NAT_AK + LANES:NAT_AK + 2 * LANES] = jnp.where(low_head, swapped, t).astype(BF16)
        elif c < A_Q_DIM + 2 * A_KV_DIM:
            out = c if c < A_Q_DIM else NAT_AV
            nat_ref[:, out:out + LANES] = t.astype(BF16)
        else:
            out = c - 2 * A_KV_DIM
            nat_ref[:, out:out + LANES] = t.astype(BF16)

    project(hb_ref[...], 0, _NAT_KINDS, c1_ref, s1_ref, store_nat)

    for c0 in range(0, GATE_COLS, W_CHUNK):
        res = jnp.dot(hb_ref[...], w_ref[:, COL_GATE + c0:COL_GATE + c0 + W_CHUNK], preferred_element_type=F32)
        gate_ref[:, c0:c0 + W_CHUNK] = jax.nn.sigmoid(res).astype(BF16)

    for dil, col0, cos_ref, sin_ref, out_ref in ((4, COL_G1, c4_ref, s4_ref, g1_ref),
                                                 (16, COL_G2, c16_ref, s16_ref, g2_ref)):
        n = tm // dil
        for r in range(dil):
            for c in range(n_chunks):
                hd_ref[r * n:(r + 1) * n, c * LANES:(c + 1) * LANES] = (
                    hf_ref[c, pl.ds(r, n, stride=dil), :].astype(BF16))

        def store_grp(c, t, out_ref=out_ref, dil=dil, n=n):
            v = t.astype(BF16)
            for r in range(dil):
                out_ref[r, :, c:c + LANES] = v[r * n:(r + 1) * n]

        project(hd_ref[...], col0, _GRP_KINDS, cos_ref, sin_ref, store_grp)


def _in_proj(x2d, g, w_in, layer, tables, batch, seq):
    tm = IN_TILE
    tiles_per_seq = seq // tm
    n_tok = batch * seq
    c1, s1, c4, s4, c16, s16 = tables
    tab_spec = pl.BlockSpec((tm, LANES), lambda i: (i % tiles_per_seq, 0))
    return pl.pallas_call(
        functools.partial(_in_proj_kernel, layer=layer),
        grid=(n_tok // tm,),
        in_specs=[
            pl.BlockSpec((tm, D_MODEL), lambda i: (i, 0)),
            pl.BlockSpec((1, D_MODEL), lambda i: (0, 0)),
            pl.BlockSpec(memory_space=pl.ANY),
            tab_spec, tab_spec, tab_spec, tab_spec, tab_spec, tab_spec,
        ],
        out_specs=[
            pl.BlockSpec((tm, NAT_COLS), lambda i: (i, 0)),
            pl.BlockSpec((tm, GATE_COLS), lambda i: (i, 0)),
            pl.BlockSpec((None, 4, tm // 4, GRP_COLS), lambda i: (i // tiles_per_seq, 0, i % tiles_per_seq, 0)),
            pl.BlockSpec((None, 16, tm // 16, GRP_COLS), lambda i: (i // tiles_per_seq, 0, i % tiles_per_seq, 0)),
        ],
        out_shape=[
            jax.ShapeDtypeStruct((n_tok, NAT_COLS), BF16),
            jax.ShapeDtypeStruct((n_tok, GATE_COLS), BF16),
            jax.ShapeDtypeStruct((batch, 4, seq // 4, GRP_COLS), BF16),
            jax.ShapeDtypeStruct((batch, 16, seq // 16, GRP_COLS), BF16),
        ],
        scratch_shapes=[
            pltpu.VMEM((D_MODEL // LANES, tm, LANES), F32),
            pltpu.VMEM((tm, D_MODEL), BF16),
            pltpu.VMEM((tm, D_MODEL), BF16),
            pltpu.VMEM((D_MODEL, D_IN), BF16),
            pltpu.VMEM((2, D_MODEL, W_CHUNK), F32),
            pltpu.SemaphoreType.DMA((2,)),
        ],
        compiler_params=_cparams("arbitrary"),
        name="in_proj",
    )(x2d, g, w_in, c1, s1, c4, s4, c16, s16)


def _store_transposed(vt_ref, blocks):
    for r0, src in blocks:
        n = src.shape[0]
        for j in range(0, n, T_ROWS):
            m = min(T_ROWS, n - j)
            vt_ref[:, r0 + j:r0 + j + m] = src[j:j + m, :].astype(F32).T.astype(BF16)


def _masked_heads(q_pair, low_head, high_head):
    zero = jnp.zeros_like(q_pair)
    return [jnp.where(low_head, q_pair, zero), jnp.where(high_head, q_pair, zero)]


def _attn_a_kernel(sink_ref, q_ref, kp_ref, km_ref, kn_ref, vp_ref, vm_ref, vn_ref, o_ref, k_ref, vt_ref, *, seq):
    tq = q_ref.shape[0]
    hw = A_HALF_WINDOW
    n_sub = tq // hw
    i = pl.program_id(1)
    last_blk = seq // hw - 1
    k_ref[0:hw, :] = kp_ref[...]
    k_ref[hw:hw + tq, :] = km_ref[...]
    k_ref[hw + tq:tq + 2 * hw, :] = kn_ref[...]
    _store_transposed(vt_ref, ((0, vp_ref), (hw, vm_ref), (hw + tq, vn_ref)))
    grp_cols = A_GROUP * hw
    key = lax.broadcasted_iota(I32, (hw, grp_cols), 0)
    qry = lax.broadcasted_iota(I32, (hw, grp_cols), 1) % hw
    low_head = lax.broadcasted_iota(I32, (1, LANES), 1) < HEAD_DIM
    high_head = jnp.logical_not(low_head)
    for sb in range(n_sub):
        r0 = sb * hw
        blk = i * n_sub + sb
        mask_p = (key >= qry) if sb > 0 else (key >= qry + jnp.where(blk > 0, 0, hw))
        mask_n = (key <= qry) if sb < n_sub - 1 else (key <= qry - jnp.where(blk < last_blk, 0, hw))
        out_t = []
        for g in range(A_KV_HEADS):
            heads = range(g * A_GROUP, (g + 1) * A_GROUP)
            q_parts = []
            for c in range(g * A_GROUP // 2, (g + 1) * A_GROUP // 2):
                q_parts += _masked_heads(q_ref[r0:r0 + hw, c * LANES:(c + 1) * LANES], low_head, high_head)
            q = jnp.concatenate(q_parts, axis=0)
            sink = jnp.concatenate([jnp.full((1, hw), sink_ref[h], F32) for h in heads], axis=1)
            k = k_ref[r0:r0 + 3 * hw, g * LANES:(g + 1) * LANES]
            s = lax.dot_general(k, q, (((1,), (1,)), ((), ())), preferred_element_type=F32)
            sp = jnp.where(mask_p, s[0:hw], NEG_INF)
            so = s[hw:2 * hw]
            sn = jnp.where(mask_n, s[2 * hw:3 * hw], NEG_INF)
            m = jnp.max(jnp.maximum(jnp.maximum(sp, so), sn), axis=0, keepdims=True)
            m = jnp.maximum(m, sink)
            pp, po, pn = jnp.exp(sp - m), jnp.exp(so - m), jnp.exp(sn - m)
            denom = jnp.sum(pp + po + pn, axis=0, keepdims=True) + jnp.exp(sink - m)
            p = jnp.concatenate([pp, po, pn], axis=0).astype(BF16)
            vt = vt_ref[g * HEAD_DIM:(g + 1) * HEAD_DIM, r0:r0 + 3 * hw]
            o = jnp.dot(vt, p, preferred_element_type=F32) * (1.0 / denom)
            out_t += [o[:, j * hw:(j + 1) * hw] for j in range(A_GROUP)]
        o_ref[r0:r0 + hw, :] = jnp.concatenate(out_t, axis=0).T.astype(BF16)


def _attn_a(nat3d, sink):
    batch, seq, _ = nat3d.shape
    tq = A_Q_TILE
    hw = A_HALF_WINDOW
    per = tq // hw
    n_hw = seq // hw
    k_cols, v_cols = 2 * A_KV_DIM, A_KV_DIM
    k_blk, v_blk = NAT_AK // k_cols, NAT_AV // v_cols

    def prev_spec(cols, blk):
        return pl.BlockSpec((None, hw, cols), lambda b, i: (b, jnp.maximum(i * per - 1, 0), blk))

    def main_spec(cols, blk):
        return pl.BlockSpec((None, tq, cols), lambda b, i: (b, i, blk))

    def next_spec(cols, blk):
        return pl.BlockSpec((None, hw, cols), lambda b, i: (b, jnp.minimum((i + 1) * per, n_hw - 1), blk))

    return pl.pallas_call(
        functools.partial(_attn_a_kernel, seq=seq),
        grid=(batch, seq // tq),
        in_specs=[
            pl.BlockSpec(memory_space=pltpu.SMEM),
            main_spec(A_Q_DIM, 0),
            prev_spec(k_cols, k_blk), main_spec(k_cols, k_blk), next_spec(k_cols, k_blk),
            prev_spec(v_cols, v_blk), main_spec(v_cols, v_blk), next_spec(v_cols, v_blk),
        ],
        out_specs=pl.BlockSpec((None, tq, A_Q_DIM), lambda b, i: (b, i, 0)),
        out_shape=jax.ShapeDtypeStruct((batch, seq, A_Q_DIM), BF16),
        scratch_shapes=[pltpu.VMEM((tq + 2 * hw, k_cols), BF16), pltpu.VMEM((v_cols, tq + 2 * hw), BF16)],
        compiler_params=_cparams("parallel", "parallel"),
        name="attn_a",
    )(sink, nat3d, nat3d, nat3d, nat3d, nat3d, nat3d, nat3d)


def _attn_b_kernel(q_ref, kp_ref, km_ref, kn_ref, vp_ref, vm_ref, vn_ref, o_ref, lse_ref,
                   k_ref, vt_ref, *, sub_len):
    for s in range(q_ref.shape[0]):
        _attn_b_one(q_ref.at[s], kp_ref.at[s], km_ref.at[s], kn_ref.at[s], vp_ref.at[s], vm_ref.at[s], vn_ref.at[s],
                    o_ref.at[s], lse_ref.at[s], k_ref.at[s], vt_ref.at[s], sub_len)


def _attn_b_one(q_ref, kp_ref, km_ref, kn_ref, vp_ref, vm_ref, vn_ref, o_ref, lse_ref, k_ref, vt_ref, sub_len):
    tq = q_ref.shape[0]
    hw = B_HALF_WINDOW
    t0 = pl.program_id(1) * tq
    k_ref[0:hw, :] = kp_ref[...]
    k_ref[hw:hw + tq, :] = km_ref[...]
    k_ref[hw + tq:tq + 2 * hw, :] = kn_ref[...]
    _store_transposed(vt_ref, ((0, vp_ref), (hw, vm_ref), (hw + tq, vn_ref)))
    kw = B_SUB + 2 * hw
    n_sub = tq // B_SUB
    all_cols = B_HEADS * B_SUB
    key = lax.broadcasted_iota(I32, (kw, all_cols), 0)
    qry = lax.broadcasted_iota(I32, (kw, all_cols), 1) % B_SUB
    in_band = jnp.abs(key - hw - qry) <= hw
    low_head = lax.broadcasted_iota(I32, (1, LANES), 1) < HEAD_DIM
    high_head = jnp.logical_not(low_head)
    for sb in range(n_sub):
        r0 = sb * B_SUB
        valid = in_band
        if sb == 0:
            valid = valid & (t0 - hw + key >= 0)
        if sb == n_sub - 1:
            valid = valid & (t0 + r0 - hw + key < sub_len)
        s_parts = []
        for c in range(B_HEADS // 2):
            cs = slice(c * LANES, (c + 1) * LANES)
            q = jnp.concatenate(_masked_heads(q_ref[r0:r0 + B_SUB, cs], low_head, high_head), axis=0)
            s_parts.append(lax.dot_general(k_ref[r0:r0 + kw, cs], q, (((1,), (1,)), ((), ())),
                                           preferred_element_type=F32))
        s = jnp.where(valid, jnp.concatenate(s_parts, axis=1), NEG_INF)
        m = jnp.max(s, axis=0, keepdims=True)
        p = jnp.exp(s - m)
        denom = jnp.sum(p, axis=0, keepdims=True)
        p = p.astype(BF16)
        inv = 1.0 / denom
        lse = m + jnp.log(denom)
        out_t, lse_t = [], []
        for h in range(B_HEADS):
            qs = slice(h * B_SUB, (h + 1) * B_SUB)
            vt = vt_ref[h * HEAD_DIM:(h + 1) * HEAD_DIM, r0:r0 + kw]
            out_t.append(jnp.dot(vt, p[:, qs], preferred_element_type=F32) * inv[:, qs])
            lse_t.append(jnp.broadcast_to(lse[:, qs], (HEAD_DIM, B_SUB)))
        o_ref[r0:r0 + B_SUB, :] = jnp.concatenate(out_t, axis=0).T
        lse_ref[r0:r0 + B_SUB, :] = jnp.concatenate(lse_t, axis=0).T


def _attn_b(arr3d, q_blk, name):
    n_sub, sub_len, _ = arr3d.shape
    tq = min(B_Q_TILE, sub_len)
    n_per = B_Q_TILE // tq
    hw = B_HALF_WINDOW
    per = tq // hw
    n_hw = sub_len // hw

    def main_spec(c):
        return pl.BlockSpec((n_per, tq, B_DIM), lambda g, i: (g, i, c))

    def prev_spec(c):
        return pl.BlockSpec((n_per, hw, B_DIM), lambda g, i: (g, jnp.maximum(i * per - 1, 0), c))

    def next_spec(c):
        return pl.BlockSpec((n_per, hw, B_DIM), lambda g, i: (g, jnp.minimum((i + 1) * per, n_hw - 1), c))

    out_spec = pl.BlockSpec((n_per, tq, B_DIM), lambda g, i: (g, i, 0))
    return pl.pallas_call(
        functools.partial(_attn_b_kernel, sub_len=sub_len),
        grid=(n_sub // n_per, sub_len // tq),
        in_specs=[main_spec(q_blk),
                  prev_spec(q_blk + 1), main_spec(q_blk + 1), next_spec(q_blk + 1),
                  prev_spec(q_blk + 2), main_spec(q_blk + 2), next_spec(q_blk + 2)],
        out_specs=[out_spec, out_spec],
        out_shape=[jax.ShapeDtypeStruct((n_sub, sub_len, B_DIM), F32)] * 2,
        scratch_shapes=[pltpu.VMEM((n_per, tq + 2 * hw, B_DIM), BF16),
                        pltpu.VMEM((n_per, B_DIM, tq + 2 * hw), BF16)],
        compiler_params=_cparams("parallel", "parallel"),
        name=name,
    )(arr3d, arr3d, arr3d, arr3d, arr3d, arr3d, arr3d)


def _merge_kernel(x_ref, ya_ref, o0_ref, l0_ref, o1_ref, l1_ref, o2_ref, l2_ref, gate_ref,
                  wa_ref, wb_ref, wo_ref, g2_ref, wr_ref, br_ref,
                  x1_ref, h2_ref, code_ref, wcol_ref, cnt_ref,
                  so1_ref, sl1_ref, so2_ref, sl2_ref, yb_ref, run_ref, earlier_ref,
                  wab_ref, wbb_ref, wob_ref):
    tm = x_ref.shape[0]

    @pl.when(pl.program_id(0) == 0)
    def _():
        run_ref[...] = jnp.zeros_like(run_ref)
        wab_ref[...] = wa_ref[...].astype(BF16)
        wbb_ref[...] = wb_ref[...].astype(BF16)
        wob_ref[...] = wo_ref[...].astype(BF16)
        row = lax.broadcasted_iota(I32, (tm, tm), 0)
        col = lax.broadcasted_iota(I32, (tm, tm), 1)
        earlier_ref[...] = (row < col).astype(BF16)

    for dil, src_o, src_l, dst_o, dst_l in ((4, o1_ref, l1_ref, so1_ref, sl1_ref),
                                            (16, o2_ref, l2_ref, so2_ref, sl2_ref)):
        n = tm // dil
        for r in range(dil):
            for c in range(B_DIM // LANES):
                cs = slice(c * LANES, (c + 1) * LANES)
                dst_o[c, pl.ds(r, n, stride=dil), :] = src_o[r, :, cs]
                dst_l[c, pl.ds(r, n, stride=dil), :] = src_l[r, :, cs]

    for c in range(B_DIM // LANES):
        cs = slice(c * LANES, (c + 1) * LANES)
        l0, l1, l2 = l0_ref[:, cs], sl1_ref[c], sl2_ref[c]
        m = jnp.maximum(jnp.maximum(l0, l1), l2)
        e0, e1, e2 = jnp.exp(l0 - m), jnp.exp(l1 - m), jnp.exp(l2 - m)
        yb = (e0 * o0_ref[:, cs] + e1 * so1_ref[c] + e2 * so2_ref[c]) / (e0 + e1 + e2)
        yb_ref[:, cs] = yb.astype(BF16)

    ya_p = jnp.dot(ya_ref[...], wab_ref[...], preferred_element_type=F32)
    yb_p = jnp.dot(yb_ref[...], wbb_ref[...], preferred_element_type=F32)
    merged = gate_ref[:, 0:D_MODEL].astype(F32) * ya_p + gate_ref[:, D_MODEL:GATE_COLS].astype(F32) * yb_p
    x1 = x_ref[...] + jnp.dot(merged.astype(BF16), wob_ref[...], preferred_element_type=F32)
    x1_ref[...] = x1

    h2 = x1 * lax.rsqrt(jnp.mean(x1 * x1, axis=-1, keepdims=True) + NORM_EPS) * g2_ref[...]
    _store_row_tiles(h2_ref, h2)

    h_hi = h2.astype(BF16)
    h_lo = (h2 - h_hi.astype(F32)).astype(BF16)
    hi_terms = jnp.dot(h_hi, wr_ref[...], preferred_element_type=F32)
    logits = (hi_terms[:, 0:LANES] + hi_terms[:, LANES:2 * LANES]
              + jnp.dot(h_lo, wr_ref[:, 0:LANES], preferred_element_type=F32)) + br_ref[...]

    logits_t = logits.T
    sub = lax.broadcasted_iota(I32, (EXPERTS_PER_GROUP, tm), 0).astype(F32)
    none = float(EXPERTS_PER_GROUP)
    gl = jnp.where(sub < MOE_GROUPS, logits_t[N_EXPERTS:N_EXPERTS + EXPERTS_PER_GROUP], -jnp.inf)
    gmax = jnp.max(gl, axis=0, keepdims=True)
    gidx = jnp.min(jnp.where(gl == gmax, sub, none), axis=0, keepdims=True)
    gw = 1.0 / jnp.sum(jnp.exp(gl - gmax), axis=0, keepdims=True)
    el = logits_t[0:EXPERTS_PER_GROUP]
    for g in range(1, MOE_GROUPS):
        el = jnp.where(gidx == g, logits_t[g * EXPERTS_PER_GROUP:(g + 1) * EXPERTS_PER_GROUP], el)
    v1 = jnp.max(el, axis=0, keepdims=True)
    i1 = jnp.min(jnp.where(el == v1, sub, none), axis=0, keepdims=True)
    el2 = jnp.where(sub == i1, -jnp.inf, el)
    v2 = jnp.max(el2, axis=0, keepdims=True)
    i2 = jnp.min(jnp.where(el2 == v2, sub, none), axis=0, keepdims=True)
    t = jnp.exp(v2 - v1)
    w1 = gw / (1.0 + t)
    w2 = gw * t / (1.0 + t)
    e1 = gidx * EXPERTS_PER_GROUP + i1
    e2 = gidx * EXPERTS_PER_GROUP + i2

    expert = lax.broadcasted_iota(I32, (N_EXPERTS, tm), 0).astype(F32)
    oh1 = (expert == e1).astype(F32)
    oh2 = (expert == e2).astype(F32)
    oh = oh1 + oh2
    run = run_ref[...]
    before = (jnp.dot(oh.astype(BF16), earlier_ref[...], preferred_element_type=F32)
              + jnp.concatenate([run] * (tm // LANES), axis=1))
    rank1 = jnp.sum(before * oh1, axis=0, keepdims=True)
    rank2 = jnp.sum(before * oh2, axis=0, keepdims=True)
    run_ref[...] = run + jnp.sum(oh, axis=1, keepdims=True)
    cnt_ref[...] = run_ref[...]

    scale = float(1 << RANK_BITS)
    zeros = jnp.zeros((6, tm), F32)
    code_ref[...] = jnp.concatenate([e1 * scale + rank1, e2 * scale + rank2, zeros], axis=0).astype(I32)
    w_rows = jnp.concatenate([w1, w2, jnp.zeros((LANES - 2, tm), F32)], axis=0)
    wcol_ref[...] = w_rows.T


def _merge(x2d, ya, o0, l0, o1, l1, o2, l2, gates, wa, wb, wo, g2, wr, br, batch, seq):
    tm = TOK_TILE
    tps = seq // tm
    n_tok = batch * seq

    def tok(c):
        return pl.BlockSpec((tm, c), lambda i: (i, 0))

    def full(a):
        return pl.BlockSpec(a.shape, lambda i: (0,) * a.ndim)

    def res_spec(d):
        return pl.BlockSpec((None, d, tm // d, B_DIM), lambda i: (i // tps, 0, i % tps, 0))

    return pl.pallas_call(
        _merge_kernel,
        grid=(n_tok // tm,),
        in_specs=[tok(D_MODEL), tok(A_Q_DIM), tok(B_DIM), tok(B_DIM),
                  res_spec(4), res_spec(4), res_spec(16), res_spec(16), tok(GATE_COLS),
                  full(wa), full(wb), full(wo), full(g2), full(wr), full(br)],
        out_specs=[tok(D_MODEL), pl.BlockSpec((tm * ROW_CHUNKS, LANES), lambda i: (i, 0)),
                   pl.BlockSpec((8, tm), lambda i: (0, i)),
                   tok(LANES),
                   pl.BlockSpec((N_EXPERTS, LANES), lambda i: (0, 0))],
        out_shape=[jax.ShapeDtypeStruct((n_tok, D_MODEL), F32),
                   jax.ShapeDtypeStruct((n_tok * ROW_CHUNKS, LANES), F32),
                   jax.ShapeDtypeStruct((8, n_tok), I32),
                   jax.ShapeDtypeStruct((n_tok, LANES), F32),
                   jax.ShapeDtypeStruct((N_EXPERTS, LANES), F32)],
        scratch_shapes=([pltpu.VMEM((B_DIM // LANES, tm, LANES), F32)] * 4
                        + [pltpu.VMEM((tm, B_DIM), BF16), pltpu.VMEM((N_EXPERTS, LANES), F32),
                           pltpu.VMEM((tm, tm), BF16),
                           pltpu.VMEM(wa.shape, BF16), pltpu.VMEM(wb.shape, BF16), pltpu.VMEM(wo.shape, BF16)]),
        compiler_params=_cparams("arbitrary"),
        name="merge_route",
    )(x2d, ya, o0, l0, o1, l1, o2, l2, gates, wa, wb, wo, g2, wr, br)


def _row_tile(ref, t):
    return ref.at[pl.ds(pl.multiple_of(t * ROW_CHUNKS, ROW_CHUNKS), ROW_CHUNKS)]


def _store_row_tiles(ref, val):
    rows = val.shape[0]
    for c in range(ROW_CHUNKS):
        ref[pl.ds(c, rows, stride=ROW_CHUNKS), :] = val[:, c * LANES:(c + 1) * LANES]


def _load_row_tiles_chunk(ref, c):
    return ref[pl.ds(c, ref.shape[0] // ROW_CHUNKS, stride=ROW_CHUNKS), :]


def _dispatch_kernel(slot_ref, fill_ref, h_ref, *rest, n_tok, reuse):
    if reuse:
        _, xs_ref, sem = rest
    else:
        xs_ref, zero_ref, sem, zsem = rest
    i = pl.program_id(0)
    rows = h_ref.shape[0] // ROW_CHUNKS

    if not reuse:
        @pl.when(i == 0)
        def _():
            tile_rows = zero_ref.shape[0]
            zero_ref[...] = jnp.zeros_like(zero_ref)

            def zero_copy(t):
                dst = xs_ref.at[pl.ds(pl.multiple_of(t * tile_rows, tile_rows), tile_rows)]
                return pltpu.make_async_copy(zero_ref, dst, zsem)

            def start(t, carry):
                @pl.when(fill_ref[t] != 0)
                def _():
                    zero_copy(t).start()
                return carry

            def wait(t, carry):
                @pl.when(fill_ref[t] != 0)
                def _():
                    zero_copy(t).wait()
                return carry

            n_tiles = xs_ref.shape[0] // tile_rows
            lax.fori_loop(0, n_tiles, start, 0)
            lax.fori_loop(0, n_tiles, wait, 0)

    def issue(j, carry):
        t = i * rows + j
        for k in range(2):
            pltpu.make_async_copy(_row_tile(h_ref, j), _row_tile(xs_ref, slot_ref[k * n_tok + t]), sem).start(priority=k)
        return carry

    lax.fori_loop(0, rows, issue, 0, unroll=DMA_UNROLL)
    for _ in range(2):
        pltpu.make_async_copy(h_ref, xs_ref.at[pl.ds(0, rows * ROW_CHUNKS)], sem).wait()


def _dispatch(slots, fill, h2, xs_prev, n_slots):
    n_tok = h2.shape[0] // ROW_CHUNKS
    rows = DISPATCH_TILE
    reuse = xs_prev is not None
    h_spec = pl.BlockSpec((rows * ROW_CHUNKS, LANES), lambda i, s, f: (i, 0))
    any_spec = pl.BlockSpec(memory_space=pl.ANY)
    scratch = [pltpu.SemaphoreType.DMA(())] if reuse else [
        pltpu.VMEM((EXP_TILE * ROW_CHUNKS, LANES), F32), pltpu.SemaphoreType.DMA(()), pltpu.SemaphoreType.DMA(())]
    return pl.pallas_call(
        functools.partial(_dispatch_kernel, n_tok=n_tok, reuse=reuse),
        grid_spec=pltpu.PrefetchScalarGridSpec(
            num_scalar_prefetch=2,
            grid=(n_tok // rows,),
            in_specs=[h_spec, any_spec] if reuse else [h_spec],
            out_specs=any_spec,
            scratch_shapes=scratch,
        ),
        out_shape=jax.ShapeDtypeStruct((n_slots * ROW_CHUNKS, LANES), F32),
        input_output_aliases={3: 0} if reuse else {},
        compiler_params=_cparams("arbitrary"),
        name="dispatch_reuse" if reuse else "dispatch",
    )(*((slots, fill, h2, xs_prev) if reuse else (slots, fill, h2)))


def _combine_kernel(slot_ref, x_ref, w_ref, g_ref, ys_ref, o_ref, buf_ref, y_ref, sem, *, n_tok, final):
    i = pl.program_id(0)
    n_steps = pl.num_programs(0)
    rows = x_ref.shape[0]

    def issue_tile(tile, slot):
        def issue(j, carry):
            t = tile * rows + j
            for k in range(2):
                pltpu.make_async_copy(_row_tile(ys_ref, slot_ref[k * n_tok + t]),
                                      _row_tile(buf_ref.at[slot, k], j), sem.at[slot]).start(priority=k)
            return carry

        lax.fori_loop(0, rows, issue, 0, unroll=DMA_UNROLL)

    @pl.when(i == 0)
    def _():
        issue_tile(0, 0)

    @pl.when(i + 1 < n_steps)
    def _():
        issue_tile(i + 1, (i + 1) % 2)

    cur = i % 2
    for k in range(2):
        pltpu.make_async_copy(ys_ref.at[pl.ds(0, rows * ROW_CHUNKS)], buf_ref.at[cur, k], sem.at[cur]).wait()
    w = w_ref[...]
    w1, w2 = w[:, 0:1], w[:, 1:2]
    for c in range(ROW_CHUNKS):
        cs = slice(c * LANES, (c + 1) * LANES)
        y1 = _load_row_tiles_chunk(buf_ref.at[cur, 0], c)
        y2 = _load_row_tiles_chunk(buf_ref.at[cur, 1], c)
        (y_ref if final else o_ref)[:, cs] = x_ref[:, cs] + w1 * y1 + w2 * y2
    if final:
        y = y_ref[...]
        o_ref[...] = y * lax.rsqrt(jnp.mean(y * y, axis=-1, keepdims=True) + NORM_EPS) * g_ref[...]


def _combine(slots, x1, wcol, g, ys, final):
    n_tok = x1.shape[0]
    rows = ROW_TILE
    return pl.pallas_call(
        functools.partial(_combine_kernel, n_tok=n_tok, final=final),
        grid_spec=pltpu.PrefetchScalarGridSpec(
            num_scalar_prefetch=1,
            grid=(n_tok // rows,),
            in_specs=[pl.BlockSpec((rows, D_MODEL), lambda i, s: (i, 0)),
                      pl.BlockSpec((rows, LANES), lambda i, s: (i, 0)),
                      pl.BlockSpec((1, D_MODEL), lambda i, s: (0, 0)),
                      pl.BlockSpec(memory_space=pl.ANY)],
            out_specs=pl.BlockSpec((rows, D_MODEL), lambda i, s: (i, 0)),
            scratch_shapes=[pltpu.VMEM((2, 2, rows * ROW_CHUNKS, LANES), F32),
                            pltpu.VMEM((rows, D_MODEL), F32),
                            pltpu.SemaphoreType.DMA((2,))],
        ),
        out_shape=jax.ShapeDtypeStruct((n_tok, D_MODEL), F32),
        compiler_params=_cparams("arbitrary"),
        name="combine",
    )(slots, x1, wcol, g, ys)


def _experts_kernel(tile_e_ref, n_used_ref, xs_ref, wg_ref, wu_ref, wd_ref, ys_ref, xbuf_ref, x_ref, sem):
    del tile_e_ref
    i = pl.program_id(0)
    n_used = n_used_ref[0]
    tile_rows = xbuf_ref.shape[1]

    def fetch(tile):
        rows = pl.ds(pl.multiple_of(tile * tile_rows, tile_rows), tile_rows)
        slot = tile % XS_SLOTS
        return pltpu.make_async_copy(xs_ref.at[rows], xbuf_ref.at[slot], sem.at[slot])

    @pl.when(i == 0)
    def _():
        fetch(0).start()

        @pl.when(n_used > 1)
        def _():
            fetch(1).start()

    @pl.when(i < n_used)
    def _():
        @pl.when(i + XS_SLOTS - 1 < n_used)
        def _():
            fetch(i + XS_SLOTS - 1).start()

        fetch(i).wait()
        xb = xbuf_ref.at[i % XS_SLOTS]
        for c in range(ROW_CHUNKS):
            x_ref[:, c * LANES:(c + 1) * LANES] = _load_row_tiles_chunk(xb, c).astype(BF16)
        x = x_ref[...]
        a = jnp.dot(x, wg_ref[...].astype(BF16), preferred_element_type=F32)
        u = jnp.dot(x, wu_ref[...].astype(BF16), preferred_element_type=F32)
        z = (a * jax.nn.sigmoid(a)) * u
        _store_row_tiles(ys_ref, jnp.dot(z.astype(BF16), wd_ref[...].astype(BF16), preferred_element_type=F32))

    @pl.when(i >= n_used)
    def _():
        ys_ref[...] = jnp.zeros_like(ys_ref)


def _experts(tile_e, n_used, xs, wg, wu, wd, layer):
    n_slots = xs.shape[0] // ROW_CHUNKS
    te = EXP_TILE

    def out_map(i, tile_e, n_used):
        return (i, 0)

    def w_map(i, tile_e, n_used):
        return (layer, tile_e[i], 0, 0)

    return pl.pallas_call(
        _experts_kernel,
        grid_spec=pltpu.PrefetchScalarGridSpec(
            num_scalar_prefetch=2,
            grid=(n_slots // te,),
            in_specs=[pl.BlockSpec(memory_space=pl.ANY),
                      pl.BlockSpec((None, None, D_MODEL, D_EXPERT), w_map),
                      pl.BlockSpec((None, None, D_MODEL, D_EXPERT), w_map),
                      pl.BlockSpec((None, None, D_EXPERT, D_MODEL), w_map)],
            out_specs=pl.BlockSpec((te * ROW_CHUNKS, LANES), out_map),
            scratch_shapes=[pltpu.VMEM((XS_SLOTS, te * ROW_CHUNKS, LANES), F32),
                            pltpu.VMEM((te, D_MODEL), BF16),
                            pltpu.SemaphoreType.DMA((XS_SLOTS,))],
        ),
        out_shape=jax.ShapeDtypeStruct((n_slots * ROW_CHUNKS, LANES), F32),
        compiler_params=_cparams("arbitrary"),
        name="experts",
    )(tile_e, n_used, xs, wg, wu, wd)


def kernel(x, attn_norm_g, w_in, a_sink, w_branch_a, w_branch_b, w_out, ffn_norm_g,
           w_router_group, b_router_group, w_router_expert, b_router_expert,
           w_exp_gate, w_exp_up, w_exp_down, final_norm_g):
    batch, seq, d_model = x.shape
    depth = w_in.shape[0]
    n_tok = batch * seq
    assert d_model == D_MODEL and w_in.shape[2] == D_IN
    assert seq % (16 * B_SUB) == 0 and seq % IN_TILE == 0 and n_tok % DISPATCH_TILE == 0
    assert n_tok < (1 << RANK_BITS)

    cos_t, sin_t = _rope_tables(seq)
    tables = (cos_t, sin_t,
              _residue_order(cos_t, 4, IN_TILE), _residue_order(sin_t, 4, IN_TILE),
              _residue_order(cos_t, 16, IN_TILE), _residue_order(sin_t, 16, IN_TILE))

    n_slots = 2 * n_tok + N_EXPERTS * EXP_TILE
    n_tiles = n_slots // EXP_TILE
    x2d = x.reshape(n_tok, D_MODEL)

    for l in range(depth):
        nat, gates, grp1, grp2 = _in_proj(x2d, attn_norm_g[l][None, :], w_in, l, tables, batch, seq)
        nat3d = nat.reshape(batch, seq, NAT_COLS)
        ya = _attn_a(nat3d, a_sink[l]).reshape(n_tok, A_Q_DIM)
        o0, l0 = _attn_b(nat3d, NAT_B // B_DIM, "attn_b1")
        o1, l1 = _attn_b(grp1.reshape(batch * 4, seq // 4, GRP_COLS), 0, "attn_b4")
        o2, l2 = _attn_b(grp2.reshape(batch * 16, seq // 16, GRP_COLS), 0, "attn_b16")

        wr = jnp.zeros((D_MODEL, LANES), F32)
        wr = wr.at[:, 0:N_EXPERTS].set(w_router_expert[l]).at[:, N_EXPERTS:N_EXPERTS + MOE_GROUPS].set(w_router_group[l])
        br = jnp.zeros((1, LANES), F32)
        br = br.at[0, 0:N_EXPERTS].set(b_router_expert[l]).at[0, N_EXPERTS:N_EXPERTS + MOE_GROUPS].set(b_router_group[l])
        wr_hi = wr.astype(BF16)
        wr_lo = (wr - wr_hi.astype(F32)).astype(BF16)
        wr_stack = jnp.concatenate([wr_hi, wr_lo], axis=1)

        x1, h2, code, wcol, cnt = _merge(
            x2d, ya, o0.reshape(n_tok, B_DIM), l0.reshape(n_tok, B_DIM),
            o1.reshape(batch, 4, seq // 4, B_DIM), l1.reshape(batch, 4, seq // 4, B_DIM),
            o2.reshape(batch, 16, seq // 16, B_DIM), l2.reshape(batch, 16, seq // 16, B_DIM),
            gates, w_branch_a[l], w_branch_b[l], w_out[l],
            ffn_norm_g[l][None, :], wr_stack, br, batch, seq)

        counts = cnt[:, 0].astype(I32)
        padded = ((counts + EXP_TILE - 1) // EXP_TILE) * EXP_TILE
        ends = jnp.cumsum(padded)
        offs = ends - padded
        n_used = (ends[-1:] // EXP_TILE).astype(I32)
        tile_start = jnp.arange(n_tiles, dtype=I32) * EXP_TILE
        tile_e = jnp.minimum(jnp.sum((ends[None, :] <= tile_start[:, None]).astype(I32), axis=1), N_EXPERTS - 1)
        eid = code[0:2] >> RANK_BITS
        rank = code[0:2] & ((1 << RANK_BITS) - 1)
        expert_ids = jnp.arange(N_EXPERTS, dtype=I32)[:, None, None]
        slots = (rank + jnp.sum(jnp.where(eid[None] == expert_ids, offs[:, None, None], 0), axis=0)).reshape(-1)

        fill = (jnp.any(tile_start[:, None] + EXP_TILE == ends[None, :], axis=1) | (tile_start >= ends[-1])).astype(I32)
        xs = _dispatch(slots, fill, h2, None if l == 0 else xs, n_slots)
        ys = _experts(tile_e, n_used, xs, w_exp_gate, w_exp_up, w_exp_down, l)
        x2d = _combine(slots, x1, wcol, final_norm_g[None, :], ys, final=(l == depth - 1))

    return x2d.reshape(batch, seq, D_MODEL)
```

```python
import functools

import jax
import jax.numpy as jnp
import numpy as np
from jax import lax
from jax.experimental import pallas as pl
from jax.experimental.pallas import tpu as pltpu

F32 = jnp.float32
BF16 = jnp.bfloat16
I32 = jnp.int32

D_MODEL = 1024
HEAD_DIM = 64
HALF_HEAD = HEAD_DIM // 2
ROPE_THETA = 10000.0
NORM_EPS = 1e-6
NEG_INF = -1e30
LANES = 128

A_Q_HEADS = 8
A_KV_HEADS = 2
A_GROUP = A_Q_HEADS // A_KV_HEADS
A_HALF_WINDOW = 128
A_Q_DIM = A_Q_HEADS * HEAD_DIM
A_KV_DIM = A_KV_HEADS * HEAD_DIM

B_GROUPS = ((128, 1), (512, 4), (2048, 16))
B_HEADS = 4
B_DIM = B_HEADS * HEAD_DIM
B_HALF_WINDOW = 64

MOE_GROUPS = 4
EXPERTS_PER_GROUP = 8
N_EXPERTS = MOE_GROUPS * EXPERTS_PER_GROUP
D_EXPERT = 256

NAT_IN_COLS = A_Q_DIM + 2 * A_KV_DIM + 3 * B_DIM
NAT_B = A_Q_DIM
NAT_AK = NAT_B + 3 * B_DIM
NAT_AV = NAT_AK + 2 * A_KV_DIM
NAT_COLS = NAT_AV + A_KV_DIM
GRP_COLS = 3 * B_DIM
COL_G1 = NAT_IN_COLS
COL_G2 = COL_G1 + GRP_COLS
COL_GATE = COL_G2 + GRP_COLS
GATE_COLS = 2 * D_MODEL
D_IN = COL_GATE + GATE_COLS

TOK_TILE = 512
IN_TILE = 512
W_CHUNK = 512
A_Q_TILE = 2048
B_SUB = 128
T_ROWS = 512
B_Q_TILE = 2048
EXP_TILE = 512
XS_SLOTS = 3
ROW_CHUNKS = D_MODEL // LANES
ROW_TILE = 512
DISPATCH_TILE = 2048
DMA_UNROLL = 8
RANK_BITS = 16
VMEM_LIMIT = 56 * 1024 * 1024


def _cparams(*sem):
    return pltpu.CompilerParams(dimension_semantics=sem, vmem_limit_bytes=VMEM_LIMIT)


def _rope_tables(seq_len):
    inv = (1.0 / (np.float32(ROPE_THETA) ** (np.arange(0, HEAD_DIM, 2, dtype=np.float32) / np.float32(HEAD_DIM))))
    ang = np.arange(seq_len, dtype=np.float32)[:, None] * inv.astype(np.float32)[None, :]
    cos, sin = np.cos(ang).astype(np.float32), np.sin(ang).astype(np.float32)
    cos_t = np.concatenate([cos, cos, cos, cos], axis=-1)
    sin_t = np.concatenate([-sin, sin, -sin, sin], axis=-1)
    return cos_t, sin_t


def _residue_order(table, dilation, tile):
    s, c = table.shape
    return table.reshape(s // tile, tile // dilation, dilation, c).transpose(0, 2, 1, 3).reshape(s, c)


def _rope(t, cos, sin_signed, first_half):
    partner = jnp.where(first_half, pltpu.roll(t, LANES - HALF_HEAD, 1), pltpu.roll(t, HALF_HEAD, 1))
    return t * cos + partner * sin_signed


Q_KIND, K_KIND, V_KIND = 0, 1, 2
_NAT_KINDS = ([Q_KIND] * 4 + [K_KIND] + [V_KIND] + [Q_KIND] * 2 + [K_KIND] * 2 + [V_KIND] * 2)
_GRP_KINDS = [Q_KIND] * 2 + [K_KIND] * 2 + [V_KIND] * 2


def _in_proj_kernel(x_ref, g_ref, w_hbm_ref, c1_ref, s1_ref, c4_ref, s4_ref, c16_ref, s16_ref,
                    nat_ref, gate_ref, g1_ref, g2_ref, hf_ref, hb_ref, hd2_ref, w_ref, stage_ref, sem, *, layer):
    tm = x_ref.shape[0]

    @pl.when(pl.program_id(0) == 0)
    def _():
        def chunk(j):
            cols = pl.ds(j * W_CHUNK, W_CHUNK)
            return pltpu.make_async_copy(w_hbm_ref.at[layer, :, cols], stage_ref.at[j % 2], sem.at[j % 2])

        n = D_IN // W_CHUNK
        chunk(0).start()
        for j in range(n):
            if j + 1 < n:
                chunk(j + 1).start()
            chunk(j).wait()
            w_ref[:, j * W_CHUNK:(j + 1) * W_CHUNK] = stage_ref[j % 2].astype(BF16)

    x = x_ref[...]
    h = x * lax.rsqrt(jnp.mean(x * x, axis=-1, keepdims=True) + NORM_EPS) * g_ref[...]
    n_chunks = D_MODEL // LANES
    for c in range(n_chunks):
        hf_ref[c] = h[:, c * LANES:(c + 1) * LANES]
    hb_ref[...] = h.astype(BF16)
    lane = lax.broadcasted_iota(I32, (1, LANES), 1)
    first_half = (lane % HEAD_DIM) < HALF_HEAD

    def project(h_b, col0, kinds, cos_ref, sin_ref, store):
        width = W_CHUNK
        for c0 in range(0, len(kinds) * LANES, width):
            w = min(width, len(kinds) * LANES - c0)
            res = jnp.dot(h_b, w_ref[:, col0 + c0:col0 + c0 + w], preferred_element_type=F32)
            for j in range(w // LANES):
                kind = kinds[(c0 // LANES) + j]
                t = res[:, j * LANES:(j + 1) * LANES]
                if kind != V_KIND:
                    t = _rope(t, cos_ref[...], sin_ref[...], first_half)
                if kind == Q_KIND:
                    t = t * (HEAD_DIM ** -0.5)
                store(c0 + j * LANES, t)

    low_head = lane < HEAD_DIM

    def store_nat(c, t):
        if A_Q_DIM <= c < A_Q_DIM + A_KV_DIM:
            swapped = pltpu.roll(t, HEAD_DIM, 1)
            nat_ref[:, NAT_AK:NAT_AK + LANES] = jnp.where(low_head, t, swapped).astype(BF16)
            nat_ref[:, NAT_AK + LANES:NAT_AK + 2 * LANES] = jnp.where(low_head, swapped, t).astype(BF16)
        elif c < A_Q_DIM + 2 * A_KV_DIM:
            out = c if c < A_Q_DIM else NAT_AV
            nat_ref[:, out:out + LANES] = t.astype(BF16)
        else:
            out = c - 2 * A_KV_DIM
            nat_ref[:, out:out + LANES] = t.astype(BF16)

    project(hb_ref[...], 0, _NAT_KINDS, c1_ref, s1_ref, store_nat)

    for c0 in range(0, GATE_COLS, W_CHUNK):
        res = jnp.dot(hb_ref[...], w_ref[:, COL_GATE + c0:COL_GATE + c0 + W_CHUNK], preferred_element_type=F32)
        gate_ref[:, c0:c0 + W_CHUNK] = jax.nn.sigmoid(res).astype(BF16)

    for slot, (dil, col0, cos_ref, sin_ref, out_ref) in enumerate(((4, COL_G1, c4_ref, s4_ref, g1_ref),
                                                                   (16, COL_G2, c16_ref, s16_ref, g2_ref))):
        n = tm // dil
        hd_ref = hd2_ref.at[slot]
        for r in range(dil):
            for c in range(n_chunks):
                hd_ref[r * n:(r + 1) * n, c * LANES:(c + 1) * LANES] = (
                    hf_ref[c, pl.ds(r, n, stride=dil), :].astype(BF16))

        def store_grp(c, t, out_ref=out_ref, dil=dil, n=n):
            v = t.astype(BF16)
            for r in range(dil):
                out_ref[r, :, c:c + LANES] = v[r * n:(r + 1) * n]

        project(hd_ref[...], col0, _GRP_KINDS, cos_ref, sin_ref, store_grp)


def _in_proj(x2d, g, w_in, layer, tables, batch, seq):
    tm = IN_TILE
    tiles_per_seq = seq // tm
    n_tok = batch * seq
    c1, s1, c4, s4, c16, s16 = tables
    tab_spec = pl.BlockSpec((tm, LANES), lambda i: (i % tiles_per_seq, 0))
    return pl.pallas_call(
        functools.partial(_in_proj_kernel, layer=layer),
        grid=(n_tok // tm,),
        in_specs=[
            pl.BlockSpec((tm, D_MODEL), lambda i: (i, 0)),
            pl.BlockSpec((1, D_MODEL), lambda i: (0, 0)),
            pl.BlockSpec(memory_space=pl.ANY),
            tab_spec, tab_spec, tab_spec, tab_spec, tab_spec, tab_spec,
        ],
        out_specs=[
            pl.BlockSpec((tm, NAT_COLS), lambda i: (i, 0)),
            pl.BlockSpec((tm, GATE_COLS), lambda i: (i, 0)),
            pl.BlockSpec((None, 4, tm // 4, GRP_COLS), lambda i: (i // tiles_per_seq, 0, i % tiles_per_seq, 0)),
            pl.BlockSpec((None, 16, tm // 16, GRP_COLS), lambda i: (i // tiles_per_seq, 0, i % tiles_per_seq, 0)),
        ],
        out_shape=[
            jax.ShapeDtypeStruct((n_tok, NAT_COLS), BF16),
            jax.ShapeDtypeStruct((n_tok, GATE_COLS), BF16),
            jax.ShapeDtypeStruct((batch, 4, seq // 4, GRP_COLS), BF16),
            jax.ShapeDtypeStruct((batch, 16, seq // 16, GRP_COLS), BF16),
        ],
        scratch_shapes=[
            pltpu.VMEM((D_MODEL // LANES, tm, LANES), F32),
            pltpu.VMEM((tm, D_MODEL), BF16),
            pltpu.VMEM((2, tm, D_MODEL), BF16),
            pltpu.VMEM((D_MODEL, D_IN), BF16),
            pltpu.VMEM((2, D_MODEL, W_CHUNK), F32),
            pltpu.SemaphoreType.DMA((2,)),
        ],
        compiler_params=_cparams("arbitrary"),
        name="in_proj",
    )(x2d, g, w_in, c1, s1, c4, s4, c16, s16)


def _store_transposed(vt_ref, blocks):
    for r0, src in blocks:
        n = src.shape[0]
        for j in range(0, n, T_ROWS):
            m = min(T_ROWS, n - j)
            vt_ref[:, r0 + j:r0 + j + m] = src[j:j + m, :].astype(F32).T.astype(BF16)


def _masked_heads(q_pair, low_head, high_head):
    zero = jnp.zeros_like(q_pair)
    return [jnp.where(low_head, q_pair, zero), jnp.where(high_head, q_pair, zero)]


def _attn_a_kernel(sink_ref, q_ref, kp_ref, km_ref, kn_ref, vp_ref, vm_ref, vn_ref, o_ref, k_ref, vt_ref, *, seq):
    tq = q_ref.shape[0]
    hw = A_HALF_WINDOW
    n_sub = tq // hw
    i = pl.program_id(1)
    last_blk = seq // hw - 1
    k_ref[0:hw, :] = kp_ref[...]
    k_ref[hw:hw + tq, :] = km_ref[...]
    k_ref[hw + tq:tq + 2 * hw, :] = kn_ref[...]
    _store_transposed(vt_ref, ((0, vp_ref), (hw, vm_ref), (hw + tq, vn_ref)))
    grp_cols = A_GROUP * hw
    key = lax.broadcasted_iota(I32, (hw, grp_cols), 0)
    qry = lax.broadcasted_iota(I32, (hw, grp_cols), 1) % hw
    low_head = lax.broadcasted_iota(I32, (1, LANES), 1) < HEAD_DIM
    high_head = jnp.logical_not(low_head)
    for sb in range(n_sub):
        r0 = sb * hw
        blk = i * n_sub + sb
        mask_p = (key >= qry) if sb > 0 else (key >= qry + jnp.where(blk > 0, 0, hw))
        mask_n = (key <= qry) if sb < n_sub - 1 else (key <= qry - jnp.where(blk < last_blk, 0, hw))
        out_t = []
        for g in range(A_KV_HEADS):
            heads = range(g * A_GROUP, (g + 1) * A_GROUP)
            q_parts = []
            for c in range(g * A_GROUP // 2, (g + 1) * A_GROUP // 2):
                q_parts += _masked_heads(q_ref[r0:r0 + hw, c * LANES:(c + 1) * LANES], low_head, high_head)
            q = jnp.concatenate(q_parts, axis=0)
            sink = jnp.concatenate([jnp.full((1, hw), sink_ref[h], F32) for h in heads], axis=1)
            k = k_ref[r0:r0 + 3 * hw, g * LANES:(g + 1) * LANES]
            s = lax.dot_general(k, q, (((1,), (1,)), ((), ())), preferred_element_type=F32)
            sp = jnp.where(mask_p, s[0:hw], NEG_INF)
            so = s[hw:2 * hw]
            sn = jnp.where(mask_n, s[2 * hw:3 * hw], NEG_INF)
            m = jnp.max(jnp.maximum(jnp.maximum(sp, so), sn), axis=0, keepdims=True)
            m = jnp.maximum(m, sink)
            pp, po, pn = jnp.exp(sp - m), jnp.exp(so - m), jnp.exp(sn - m)
            denom = jnp.sum(pp + po + pn, axis=0, keepdims=True) + jnp.exp(sink - m)
            p = jnp.concatenate([pp, po, pn], axis=0).astype(BF16)
            vt = vt_ref[g * HEAD_DIM:(g + 1) * HEAD_DIM, r0:r0 + 3 * hw]
            o = jnp.dot(vt, p, preferred_element_type=F32) * (1.0 / denom)
            out_t += [o[:, j * hw:(j + 1) * hw] for j in range(A_GROUP)]
        o_ref[r0:r0 + hw, :] = jnp.concatenate(out_t, axis=0).T.astype(BF16)


def _attn_a(nat3d, sink):
    batch, seq, _ = nat3d.shape
    tq = A_Q_TILE
    hw = A_HALF_WINDOW
    per = tq // hw
    n_hw = seq // hw
    k_cols, v_cols = 2 * A_KV_DIM, A_KV_DIM
    k_blk, v_blk = NAT_AK // k_cols, NAT_AV // v_cols

    def prev_spec(cols, blk):
        return pl.BlockSpec((None, hw, cols), lambda b, i: (b, jnp.maximum(i * per - 1, 0), blk))

    def main_spec(cols, blk):
        return pl.BlockSpec((None, tq, cols), lambda b, i: (b, i, blk))

    def next_spec(cols, blk):
        return pl.BlockSpec((None, hw, cols), lambda b, i: (b, jnp.minimum((i + 1) * per, n_hw - 1), blk))

    return pl.pallas_call(
        functools.partial(_attn_a_kernel, seq=seq),
        grid=(batch, seq // tq),
        in_specs=[
            pl.BlockSpec(memory_space=pltpu.SMEM),
            main_spec(A_Q_DIM, 0),
            prev_spec(k_cols, k_blk), main_spec(k_cols, k_blk), next_spec(k_cols, k_blk),
            prev_spec(v_cols, v_blk), main_spec(v_cols, v_blk), next_spec(v_cols, v_blk),
        ],
        out_specs=pl.BlockSpec((None, tq, A_Q_DIM), lambda b, i: (b, i, 0)),
        out_shape=jax.ShapeDtypeStruct((batch, seq, A_Q_DIM), BF16),
        scratch_shapes=[pltpu.VMEM((tq + 2 * hw, k_cols), BF16), pltpu.VMEM((v_cols, tq + 2 * hw), BF16)],
        compiler_params=_cparams("parallel", "parallel"),
        name="attn_a",
    )(sink, nat3d, nat3d, nat3d, nat3d, nat3d, nat3d, nat3d)


def _attn_b_kernel(q_ref, kp_ref, km_ref, kn_ref, vp_ref, vm_ref, vn_ref, o_ref, lse_ref,
                   k_ref, vt_ref, *, sub_len):
    for s in range(q_ref.shape[0]):
        _attn_b_one(q_ref.at[s], kp_ref.at[s], km_ref.at[s], kn_ref.at[s], vp_ref.at[s], vm_ref.at[s], vn_ref.at[s],
                    o_ref.at[s], lse_ref.at[s], k_ref.at[s], vt_ref.at[s], sub_len)


def _attn_b_one(q_ref, kp_ref, km_ref, kn_ref, vp_ref, vm_ref, vn_ref, o_ref, lse_ref, k_ref, vt_ref, sub_len):
    tq = q_ref.shape[0]
    hw = B_HALF_WINDOW
    t0 = pl.program_id(1) * tq
    k_ref[0:hw, :] = kp_ref[...]
    k_ref[hw:hw + tq, :] = km_ref[...]
    k_ref[hw + tq:tq + 2 * hw, :] = kn_ref[...]
    _store_transposed(vt_ref, ((0, vp_ref), (hw, vm_ref), (hw + tq, vn_ref)))
    kw = B_SUB + 2 * hw
    n_sub = tq // B_SUB
    all_cols = B_HEADS * B_SUB
    key = lax.broadcasted_iota(I32, (kw, all_cols), 0)
    qry = lax.broadcasted_iota(I32, (kw, all_cols), 1) % B_SUB
    in_band = jnp.abs(key - hw - qry) <= hw
    low_head = lax.broadcasted_iota(I32, (1, LANES), 1) < HEAD_DIM
    high_head = jnp.logical_not(low_head)
    for sb in range(n_sub):
        r0 = sb * B_SUB
        valid = in_band
        if sb == 0:
            valid = valid & (t0 - hw + key >= 0)
        if sb == n_sub - 1:
            valid = valid & (t0 + r0 - hw + key < sub_len)
        s_parts = []
        for c in range(B_HEADS // 2):
            cs = slice(c * LANES, (c + 1) * LANES)
            q = jnp.concatenate(_masked_heads(q_ref[r0:r0 + B_SUB, cs], low_head, high_head), axis=0)
            s_parts.append(lax.dot_general(k_ref[r0:r0 + kw, cs], q, (((1,), (1,)), ((), ())),
                                           preferred_element_type=F32))
        s = jnp.where(valid, jnp.concatenate(s_parts, axis=1), NEG_INF)
        m = jnp.max(s, axis=0, keepdims=True)
        p = jnp.exp(s - m)
        denom = jnp.sum(p, axis=0, keepdims=True)
        p = p.astype(BF16)
        inv = 1.0 / denom
        lse = m + jnp.log(denom)
        out_t, lse_t = [], []
        for h in range(B_HEADS):
            qs = slice(h * B_SUB, (h + 1) * B_SUB)
            vt = vt_ref[h * HEAD_DIM:(h + 1) * HEAD_DIM, r0:r0 + kw]
            out_t.append(jnp.dot(vt, p[:, qs], preferred_element_type=F32) * inv[:, qs])
            lse_t.append(jnp.broadcast_to(lse[:, qs], (HEAD_DIM, B_SUB)))
        o_ref[r0:r0 + B_SUB, :] = jnp.concatenate(out_t, axis=0).T
        lse_ref[r0:r0 + B_SUB, :] = jnp.concatenate(lse_t, axis=0).T


def _attn_b(arr3d, q_blk, name):
    n_sub, sub_len, _ = arr3d.shape
    tq = min(B_Q_TILE, sub_len)
    n_per = B_Q_TILE // tq
    hw = B_HALF_WINDOW
    per = tq // hw
    n_hw = sub_len // hw

    def main_spec(c):
        return pl.BlockSpec((n_per, tq, B_DIM), lambda g, i: (g, i, c))

    def prev_spec(c):
        return pl.BlockSpec((n_per, hw, B_DIM), lambda g, i: (g, jnp.maximum(i * per - 1, 0), c))

    def next_spec(c):
        return pl.BlockSpec((n_per, hw, B_DIM), lambda g, i: (g, jnp.minimum((i + 1) * per, n_hw - 1), c))

    out_spec = pl.BlockSpec((n_per, tq, B_DIM), lambda g, i: (g, i, 0))
    return pl.pallas_call(
        functools.partial(_attn_b_kernel, sub_len=sub_len),
        grid=(n_sub // n_per, sub_len // tq),
        in_specs=[main_spec(q_blk),
                  prev_spec(q_blk + 1), main_spec(q_blk + 1), next_spec(q_blk + 1),
                  prev_spec(q_blk + 2), main_spec(q_blk + 2), next_spec(q_blk + 2)],
        out_specs=[out_spec, out_spec],
        out_shape=[jax.ShapeDtypeStruct((n_sub, sub_len, B_DIM), F32)] * 2,
        scratch_shapes=[pltpu.VMEM((n_per, tq + 2 * hw, B_DIM), BF16),
                        pltpu.VMEM((n_per, B_DIM, tq + 2 * hw), BF16)],
        compiler_params=_cparams("parallel", "parallel"),
        name=name,
    )(arr3d, arr3d, arr3d, arr3d, arr3d, arr3d, arr3d)


def _merge_kernel(x_ref, ya_ref, o0_ref, l0_ref, o1_ref, l1_ref, o2_ref, l2_ref, gate_ref,
                  wa_ref, wb_ref, wo_ref, g2_ref, wr_ref, br_ref,
                  x1_ref, h2_ref, code_ref, wcol_ref, cnt_ref,
                  so1_ref, sl1_ref, so2_ref, sl2_ref, yb_ref, run_ref, earlier_ref,
                  wab_ref, wbb_ref, wob_ref):
    tm = x_ref.shape[0]

    @pl.when(pl.program_id(0) == 0)
    def _():
        run_ref[...] = jnp.zeros_like(run_ref)
        wab_ref[...] = wa_ref[...].astype(BF16)
        wbb_ref[...] = wb_ref[...].astype(BF16)
        wob_ref[...] = wo_ref[...].astype(BF16)
        row = lax.broadcasted_iota(I32, (tm, tm), 0)
        col = lax.broadcasted_iota(I32, (tm, tm), 1)
        earlier_ref[...] = (row < col).astype(BF16)

    for dil, src_o, src_l, dst_o, dst_l in ((4, o1_ref, l1_ref, so1_ref, sl1_ref),
                                            (16, o2_ref, l2_ref, so2_ref, sl2_ref)):
        n = tm // dil
        for r in range(dil):
            for c in range(B_DIM // LANES):
                cs = slice(c * LANES, (c + 1) * LANES)
                dst_o[c, pl.ds(r, n, stride=dil), :] = src_o[r, :, cs]
                dst_l[c, pl.ds(r, n, stride=dil), :] = src_l[r, :, cs]

    for c in range(B_DIM // LANES):
        cs = slice(c * LANES, (c + 1) * LANES)
        l0, l1, l2 = l0_ref[:, cs], sl1_ref[c], sl2_ref[c]
        m = jnp.maximum(jnp.maximum(l0, l1), l2)
        e0, e1, e2 = jnp.exp(l0 - m), jnp.exp(l1 - m), jnp.exp(l2 - m)
        yb = (e0 * o0_ref[:, cs] + e1 * so1_ref[c] + e2 * so2_ref[c]) / (e0 + e1 + e2)
        yb_ref[:, cs] = yb.astype(BF16)

    ya_p = jnp.dot(ya_ref[...], wab_ref[...], preferred_element_type=F32)
    yb_p = jnp.dot(yb_ref[...], wbb_ref[...], preferred_element_type=F32)
    merged = gate_ref[:, 0:D_MODEL].astype(F32) * ya_p + gate_ref[:, D_MODEL:GATE_COLS].astype(F32) * yb_p
    x1 = x_ref[...] + jnp.dot(merged.astype(BF16), wob_ref[...], preferred_element_type=F32)
    x1_ref[...] = x1

    h2 = x1 * lax.rsqrt(jnp.mean(x1 * x1, axis=-1, keepdims=True) + NORM_EPS) * g2_ref[...]
    _store_row_tiles(h2_ref, h2)

    h_hi = h2.astype(BF16)
    h_lo = (h2 - h_hi.astype(F32)).astype(BF16)
    hi_terms = jnp.dot(h_hi, wr_ref[...], preferred_element_type=F32)
    logits = (hi_terms[:, 0:LANES] + hi_terms[:, LANES:2 * LANES]
              + jnp.dot(h_lo, wr_ref[:, 0:LANES], preferred_element_type=F32)) + br_ref[...]

    logits_t = logits.T
    sub = lax.broadcasted_iota(I32, (EXPERTS_PER_GROUP, tm), 0).astype(F32)
    none = float(EXPERTS_PER_GROUP)
    gl = jnp.where(sub < MOE_GROUPS, logits_t[N_EXPERTS:N_EXPERTS + EXPERTS_PER_GROUP], -jnp.inf)
    gmax = jnp.max(gl, axis=0, keepdims=True)
    gidx = jnp.min(jnp.where(gl == gmax, sub, none), axis=0, keepdims=True)
    gw = 1.0 / jnp.sum(jnp.exp(gl - gmax), axis=0, keepdims=True)
    el = logits_t[0:EXPERTS_PER_GROUP]
    for g in range(1, MOE_GROUPS):
        el = jnp.where(gidx == g, logits_t[g * EXPERTS_PER_GROUP:(g + 1) * EXPERTS_PER_GROUP], el)
    v1 = jnp.max(el, axis=0, keepdims=True)
    i1 = jnp.min(jnp.where(el == v1, sub, none), axis=0, keepdims=True)
    el2 = jnp.where(sub == i1, -jnp.inf, el)
    v2 = jnp.max(el2, axis=0, keepdims=True)
    i2 = jnp.min(jnp.where(el2 == v2, sub, none), axis=0, keepdims=True)
    t = jnp.exp(v2 - v1)
    w1 = gw / (1.0 + t)
    w2 = gw * t / (1.0 + t)
    e1 = gidx * EXPERTS_PER_GROUP + i1
    e2 = gidx * EXPERTS_PER_GROUP + i2

    expert = lax.broadcasted_iota(I32, (N_EXPERTS, tm), 0).astype(F32)
    oh1 = (expert == e1).astype(F32)
    oh2 = (expert == e2).astype(F32)
    oh = oh1 + oh2
    run = run_ref[...]
    before = (jnp.dot(oh.astype(BF16), earlier_ref[...], preferred_element_type=F32)
              + jnp.concatenate([run] * (tm // LANES), axis=1))
    rank1 = jnp.sum(before * oh1, axis=0, keepdims=True)
    rank2 = jnp.sum(before * oh2, axis=0, keepdims=True)
    run_ref[...] = run + jnp.sum(oh, axis=1, keepdims=True)
    cnt_ref[...] = run_ref[...]

    scale = float(1 << RANK_BITS)
    zeros = jnp.zeros((6, tm), F32)
    code_ref[...] = jnp.concatenate([e1 * scale + rank1, e2 * scale + rank2, zeros], axis=0).astype(I32)
    w_rows = jnp.concatenate([w1, w2, jnp.zeros((LANES - 2, tm), F32)], axis=0)
    wcol_ref[...] = w_rows.T


def _merge(x2d, ya, o0, l0, o1, l1, o2, l2, gates, wa, wb, wo, g2, wr, br, batch, seq):
    tm = TOK_TILE
    tps = seq // tm
    n_tok = batch * seq

    def tok(c):
        return pl.BlockSpec((tm, c), lambda i: (i, 0))

    def full(a):
        return pl.BlockSpec(a.shape, lambda i: (0,) * a.ndim)

    def res_spec(d):
        return pl.BlockSpec((None, d, tm // d, B_DIM), lambda i: (i // tps, 0, i % tps, 0))

    return pl.pallas_call(
        _merge_kernel,
        grid=(n_tok // tm,),
        in_specs=[tok(D_MODEL), tok(A_Q_DIM), tok(B_DIM), tok(B_DIM),
                  res_spec(4), res_spec(4), res_spec(16), res_spec(16), tok(GATE_COLS),
                  full(wa), full(wb), full(wo), full(g2), full(wr), full(br)],
        out_specs=[tok(D_MODEL), pl.BlockSpec((tm * ROW_CHUNKS, LANES), lambda i: (i, 0)),
                   pl.BlockSpec((8, tm), lambda i: (0, i)),
                   tok(LANES),
                   pl.BlockSpec((N_EXPERTS, LANES), lambda i: (0, 0))],
        out_shape=[jax.ShapeDtypeStruct((n_tok, D_MODEL), F32),
                   jax.ShapeDtypeStruct((n_tok * ROW_CHUNKS, LANES), F32),
                   jax.ShapeDtypeStruct((8, n_tok), I32),
                   jax.ShapeDtypeStruct((n_tok, LANES), F32),
                   jax.ShapeDtypeStruct((N_EXPERTS, LANES), F32)],
        scratch_shapes=([pltpu.VMEM((B_DIM // LANES, tm, LANES), F32)] * 4
                        + [pltpu.VMEM((tm, B_DIM), BF16), pltpu.VMEM((N_EXPERTS, LANES), F32),
                           pltpu.VMEM((tm, tm), BF16),
                           pltpu.VMEM(wa.shape, BF16), pltpu.VMEM(wb.shape, BF16), pltpu.VMEM(wo.shape, BF16)]),
        compiler_params=_cparams("arbitrary"),
        name="merge_route",
    )(x2d, ya, o0, l0, o1, l1, o2, l2, gates, wa, wb, wo, g2, wr, br)


def _row_tile(ref, t):
    return ref.at[pl.ds(pl.multiple_of(t * ROW_CHUNKS, ROW_CHUNKS), ROW_CHUNKS)]


def _store_row_tiles(ref, val):
    rows = val.shape[0]
    for c in range(ROW_CHUNKS):
        ref[pl.ds(c, rows, stride=ROW_CHUNKS), :] = val[:, c * LANES:(c + 1) * LANES]


def _load_row_tiles_chunk(ref, c):
    return ref[pl.ds(c, ref.shape[0] // ROW_CHUNKS, stride=ROW_CHUNKS), :]


def _dispatch_kernel(slot_ref, fill_ref, h_ref, *rest, n_tok, reuse):
    if reuse:
        _, xs_ref, sem = rest
    else:
        xs_ref, zero_ref, sem, zsem = rest
    i = pl.program_id(0)
    rows = h_ref.shape[0] // ROW_CHUNKS

    if not reuse:
        @pl.when(i == 0)
        def _():
            tile_rows = zero_ref.shape[0]
            zero_ref[...] = jnp.zeros_like(zero_ref)

            def zero_copy(t):
                dst = xs_ref.at[pl.ds(pl.multiple_of(t * tile_rows, tile_rows), tile_rows)]
                return pltpu.make_async_copy(zero_ref, dst, zsem)

            def start(t, carry):
                @pl.when(fill_ref[t] != 0)
                def _():
                    zero_copy(t).start()
                return carry

            def wait(t, carry):
                @pl.when(fill_ref[t] != 0)
                def _():
                    zero_copy(t).wait()
                return carry

            n_tiles = xs_ref.shape[0] // tile_rows
            lax.fori_loop(0, n_tiles, start, 0)
            lax.fori_loop(0, n_tiles, wait, 0)

    def issue(j, carry):
        t = i * rows + j
        for k in range(2):
            pltpu.make_async_copy(_row_tile(h_ref, j), _row_tile(xs_ref, slot_ref[k * n_tok + t]), sem).start(priority=k)
        return carry

    lax.fori_loop(0, rows, issue, 0, unroll=DMA_UNROLL)
    for _ in range(2):
        pltpu.make_async_copy(h_ref, xs_ref.at[pl.ds(0, rows * ROW_CHUNKS)], sem).wait()


def _dispatch(slots, fill, h2, xs_prev, n_slots):
    n_tok = h2.shape[0] // ROW_CHUNKS
    rows = DISPATCH_TILE
    reuse = xs_prev is not None
    h_spec = pl.BlockSpec((rows * ROW_CHUNKS, LANES), lambda i, s, f: (i, 0))
    any_spec = pl.BlockSpec(memory_space=pl.ANY)
    scratch = [pltpu.SemaphoreType.DMA(())] if reuse else [
        pltpu.VMEM((EXP_TILE * ROW_CHUNKS, LANES), F32), pltpu.SemaphoreType.DMA(()), pltpu.SemaphoreType.DMA(())]
    return pl.pallas_call(
        functools.partial(_dispatch_kernel, n_tok=n_tok, reuse=reuse),
        grid_spec=pltpu.PrefetchScalarGridSpec(
            num_scalar_prefetch=2,
            grid=(n_tok // rows,),
            in_specs=[h_spec, any_spec] if reuse else [h_spec],
            out_specs=any_spec,
            scratch_shapes=scratch,
        ),
        out_shape=jax.ShapeDtypeStruct((n_slots * ROW_CHUNKS, LANES), F32),
        input_output_aliases={3: 0} if reuse else {},
        compiler_params=_cparams("arbitrary"),
        name="dispatch_reuse" if reuse else "dispatch",
    )(*((slots, fill, h2, xs_prev) if reuse else (slots, fill, h2)))


def _combine_kernel(slot_ref, x_ref, w_ref, g_ref, ys_ref, o_ref, buf_ref, y_ref, sem, *, n_tok, final):
    i = pl.program_id(0)
    n_steps = pl.num_programs(0)
    rows = x_ref.shape[0]

    def issue_tile(tile, slot):
        def issue(j, carry):
            t = tile * rows + j
            for k in range(2):
                pltpu.make_async_copy(_row_tile(ys_ref, slot_ref[k * n_tok + t]),
                                      _row_tile(buf_ref.at[slot, k], j), sem.at[slot]).start(priority=k)
            return carry

        lax.fori_loop(0, rows, issue, 0, unroll=DMA_UNROLL)

    @pl.when(i == 0)
    def _():
        issue_tile(0, 0)

    @pl.when(i + 1 < n_steps)
    def _():
        issue_tile(i + 1, (i + 1) % 2)

    cur = i % 2
    for k in range(2):
        pltpu.make_async_copy(ys_ref.at[pl.ds(0, rows * ROW_CHUNKS)], buf_ref.at[cur, k], sem.at[cur]).wait()
    w = w_ref[...]
    w1, w2 = w[:, 0:1], w[:, 1:2]
    for c in range(ROW_CHUNKS):
        cs = slice(c * LANES, (c + 1) * LANES)
        y1 = _load_row_tiles_chunk(buf_ref.at[cur, 0], c)
        y2 = _load_row_tiles_chunk(buf_ref.at[cur, 1], c)
        (y_ref if final else o_ref)[:, cs] = x_ref[:, cs] + w1 * y1 + w2 * y2
    if final:
        y = y_ref[...]
        o_ref[...] = y * lax.rsqrt(jnp.mean(y * y, axis=-1, keepdims=True) + NORM_EPS) * g_ref[...]


def _combine(slots, x1, wcol, g, ys, final):
    n_tok = x1.shape[0]
    rows = ROW_TILE
    return pl.pallas_call(
        functools.partial(_combine_kernel, n_tok=n_tok, final=final),
        grid_spec=pltpu.PrefetchScalarGridSpec(
            num_scalar_prefetch=1,
            grid=(n_tok // rows,),
            in_specs=[pl.BlockSpec((rows, D_MODEL), lambda i, s: (i, 0)),
                      pl.BlockSpec((rows, LANES), lambda i, s: (i, 0)),
                      pl.BlockSpec((1, D_MODEL), lambda i, s: (0, 0)),
                      pl.BlockSpec(memory_space=pl.ANY)],
            out_specs=pl.BlockSpec((rows, D_MODEL), lambda i, s: (i, 0)),
            scratch_shapes=[pltpu.VMEM((2, 2, rows * ROW_CHUNKS, LANES), F32),
                            pltpu.VMEM((rows, D_MODEL), F32),
                            pltpu.SemaphoreType.DMA((2,))],
        ),
        out_shape=jax.ShapeDtypeStruct((n_tok, D_MODEL), F32),
        compiler_params=_cparams("arbitrary"),
        name="combine",
    )(slots, x1, wcol, g, ys)


def _experts_kernel(tile_e_ref, n_used_ref, xs_ref, wg_ref, wu_ref, wd_ref, ys_ref, xbuf_ref, x_ref, sem):
    del tile_e_ref
    i = pl.program_id(0)
    n_used = n_used_ref[0]
    tile_rows = xbuf_ref.shape[1]

    def fetch(tile):
        rows = pl.ds(pl.multiple_of(tile * tile_rows, tile_rows), tile_rows)
        slot = tile % XS_SLOTS
        return pltpu.make_async_copy(xs_ref.at[rows], xbuf_ref.at[slot], sem.at[slot])

    @pl.when(i == 0)
    def _():
        fetch(0).start()

        @pl.when(n_used > 1)
        def _():
            fetch(1).start()

    @pl.when(i < n_used)
    def _():
        @pl.when(i + XS_SLOTS - 1 < n_used)
        def _():
            fetch(i + XS_SLOTS - 1).start()

        fetch(i).wait()
        xb = xbuf_ref.at[i % XS_SLOTS]
        for c in range(ROW_CHUNKS):
            x_ref[:, c * LANES:(c + 1) * LANES] = _load_row_tiles_chunk(xb, c).astype(BF16)
        x = x_ref[...]
        a = jnp.dot(x, wg_ref[...].astype(BF16), preferred_element_type=F32)
        u = jnp.dot(x, wu_ref[...].astype(BF16), preferred_element_type=F32)
        z = (a * jax.nn.sigmoid(a)) * u
        _store_row_tiles(ys_ref, jnp.dot(z.astype(BF16), wd_ref[...].astype(BF16), preferred_element_type=F32))

    @pl.when(i >= n_used)
    def _():
        ys_ref[...] = jnp.zeros_like(ys_ref)


def _experts(tile_e, n_used, xs, wg, wu, wd, layer):
    n_slots = xs.shape[0] // ROW_CHUNKS
    te = EXP_TILE

    def out_map(i, tile_e, n_used):
        return (i, 0)

    def w_map(i, tile_e, n_used):
        return (layer, tile_e[i], 0, 0)

    return pl.pallas_call(
        _experts_kernel,
        grid_spec=pltpu.PrefetchScalarGridSpec(
            num_scalar_prefetch=2,
            grid=(n_slots // te,),
            in_specs=[pl.BlockSpec(memory_space=pl.ANY),
                      pl.BlockSpec((None, None, D_MODEL, D_EXPERT), w_map),
                      pl.BlockSpec((None, None, D_MODEL, D_EXPERT), w_map),
                      pl.BlockSpec((None, None, D_EXPERT, D_MODEL), w_map)],
            out_specs=pl.BlockSpec((te * ROW_CHUNKS, LANES), out_map),
            scratch_shapes=[pltpu.VMEM((XS_SLOTS, te * ROW_CHUNKS, LANES), F32),
                            pltpu.VMEM((te, D_MODEL), BF16),
                            pltpu.SemaphoreType.DMA((XS_SLOTS,))],
        ),
        out_shape=jax.ShapeDtypeStruct((n_slots * ROW_CHUNKS, LANES), F32),
        compiler_params=_cparams("arbitrary"),
        name="experts",
    )(tile_e, n_used, xs, wg, wu, wd)


def kernel(x, attn_norm_g, w_in, a_sink, w_branch_a, w_branch_b, w_out, ffn_norm_g,
           w_router_group, b_router_group, w_router_expert, b_router_expert,
           w_exp_gate, w_exp_up, w_exp_down, final_norm_g):
    batch, seq, d_model = x.shape
    depth = w_in.shape[0]
    n_tok = batch * seq
    assert d_model == D_MODEL and w_in.shape[2] == D_IN
    assert seq % (16 * B_SUB) == 0 and seq % IN_TILE == 0 and n_tok % DISPATCH_TILE == 0
    assert n_tok < (1 << RANK_BITS)

    cos_t, sin_t = _rope_tables(seq)
    tables = (cos_t, sin_t,
              _residue_order(cos_t, 4, IN_TILE), _residue_order(sin_t, 4, IN_TILE),
              _residue_order(cos_t, 16, IN_TILE), _residue_order(sin_t, 16, IN_TILE))

    n_slots = 2 * n_tok + N_EXPERTS * EXP_TILE
    n_tiles = n_slots // EXP_TILE
    x2d = x.reshape(n_tok, D_MODEL)

    for l in range(depth):
        nat, gates, grp1, grp2 = _in_proj(x2d, attn_norm_g[l][None, :], w_in, l, tables, batch, seq)
        nat3d = nat.reshape(batch, seq, NAT_COLS)
        ya = _attn_a(nat3d, a_sink[l]).reshape(n_tok, A_Q_DIM)
        o0, l0 = _attn_b(nat3d, NAT_B // B_DIM, "attn_b1")
        o1, l1 = _attn_b(grp1.reshape(batch * 4, seq // 4, GRP_COLS), 0, "attn_b4")
        o2, l2 = _attn_b(grp2.reshape(batch * 16, seq // 16, GRP_COLS), 0, "attn_b16")

        wr = jnp.zeros((D_MODEL, LANES), F32)
        wr = wr.at[:, 0:N_EXPERTS].set(w_router_expert[l]).at[:, N_EXPERTS:N_EXPERTS + MOE_GROUPS].set(w_router_group[l])
        br = jnp.zeros((1, LANES), F32)
        br = br.at[0, 0:N_EXPERTS].set(b_router_expert[l]).at[0, N_EXPERTS:N_EXPERTS + MOE_GROUPS].set(b_router_group[l])
        wr_hi = wr.astype(BF16)
        wr_lo = (wr - wr_hi.astype(F32)).astype(BF16)
        wr_stack = jnp.concatenate([wr_hi, wr_lo], axis=1)

        x1, h2, code, wcol, cnt = _merge(
            x2d, ya, o0.reshape(n_tok, B_DIM), l0.reshape(n_tok, B_DIM),
            o1.reshape(batch, 4, seq // 4, B_DIM), l1.reshape(batch, 4, seq // 4, B_DIM),
            o2.reshape(batch, 16, seq // 16, B_DIM), l2.reshape(batch, 16, seq // 16, B_DIM),
            gates, w_branch_a[l], w_branch_b[l], w_out[l],
            ffn_norm_g[l][None, :], wr_stack, br, batch, seq)

        counts = cnt[:, 0].astype(I32)
        padded = ((counts + EXP_TILE - 1) // EXP_TILE) * EXP_TILE
        ends = jnp.cumsum(padded)
        offs = ends - padded
        n_used = (ends[-1:] // EXP_TILE).astype(I32)
        tile_start = jnp.arange(n_tiles, dtype=I32) * EXP_TILE
        tile_e = jnp.minimum(jnp.sum((ends[None, :] <= tile_start[:, None]).astype(I32), axis=1), N_EXPERTS - 1)
        eid = code[0:2] >> RANK_BITS
        rank = code[0:2] & ((1 << RANK_BITS) - 1)
        expert_ids = jnp.arange(N_EXPERTS, dtype=I32)[:, None, None]
        slots = (rank + jnp.sum(jnp.where(eid[None] == expert_ids, offs[:, None, None], 0), axis=0)).reshape(-1)

        fill = (jnp.any(tile_start[:, None] + EXP_TILE == ends[None, :], axis=1) | (tile_start >= ends[-1])).astype(I32)
        xs = _dispatch(slots, fill, h2, None if l == 0 else xs, n_slots)
        ys = _experts(tile_e, n_used, xs, w_exp_gate, w_exp_up, w_exp_down, l)
        x2d = _combine(slots, x1, wcol, final_norm_g[None, :], ys, final=(l == depth - 1))

    return x2d.reshape(batch, seq, D_MODEL)
```

```python
import functools

import jax
import jax.numpy as jnp
import numpy as np
from jax import lax
from jax.experimental import pallas as pl
from jax.experimental.pallas import tpu as pltpu

F32 = jnp.float32
BF16 = jnp.bfloat16
I32 = jnp.int32

D_MODEL = 1024
HEAD_DIM = 64
HALF_HEAD = HEAD_DIM // 2
ROPE_THETA = 10000.0
NORM_EPS = 1e-6
NEG_INF = -1e30
LANES = 128

A_Q_HEADS = 8
A_KV_HEADS = 2
A_GROUP = A_Q_HEADS // A_KV_HEADS
A_HALF_WINDOW = 128
A_Q_DIM = A_Q_HEADS * HEAD_DIM
A_KV_DIM = A_KV_HEADS * HEAD_DIM

B_GROUPS = ((128, 1), (512, 4), (2048, 16))
B_HEADS = 4
B_DIM = B_HEADS * HEAD_DIM
B_HALF_WINDOW = 64

MOE_GROUPS = 4
EXPERTS_PER_GROUP = 8
N_EXPERTS = MOE_GROUPS * EXPERTS_PER_GROUP
D_EXPERT = 256

NAT_IN_COLS = A_Q_DIM + 2 * A_KV_DIM + 3 * B_DIM
NAT_B = A_Q_DIM
NAT_AK = NAT_B + 3 * B_DIM
NAT_AV = NAT_AK + 2 * A_KV_DIM
NAT_COLS = NAT_AV + A_KV_DIM
GRP_COLS = 3 * B_DIM
COL_G1 = NAT_IN_COLS
COL_G2 = COL_G1 + GRP_COLS
COL_GATE = COL_G2 + GRP_COLS
GATE_COLS = 2 * D_MODEL
D_IN = COL_GATE + GATE_COLS

TOK_TILE = 512
IN_TILE = 512
W_CHUNK = 512
A_Q_TILE = 2048
B_SUB = 128
T_ROWS = 512
B_Q_TILE = 2048
EXP_TILE = 512
XS_SLOTS = 3
ROW_CHUNKS = D_MODEL // LANES
ROW_TILE = 512
DISPATCH_TILE = 2048
DMA_UNROLL = 8
RANK_BITS = 16
VMEM_LIMIT = 56 * 1024 * 1024


def _cparams(*sem):
    return pltpu.CompilerParams(dimension_semantics=sem, vmem_limit_bytes=VMEM_LIMIT)


def _rope_tables(seq_len):
    inv = (1.0 / (np.float32(ROPE_THETA) ** (np.arange(0, HEAD_DIM, 2, dtype=np.float32) / np.float32(HEAD_DIM))))
    ang = np.arange(seq_len, dtype=np.float32)[:, None] * inv.astype(np.float32)[None, :]
    cos, sin = np.cos(ang).astype(np.float32), np.sin(ang).astype(np.float32)
    cos_t = np.concatenate([cos, cos, cos, cos], axis=-1)
    sin_t = np.concatenate([-sin, sin, -sin, sin], axis=-1)
    return cos_t, sin_t


def _residue_order(table, dilation, tile):
    s, c = table.shape
    return table.reshape(s // tile, tile // dilation, dilation, c).transpose(0, 2, 1, 3).reshape(s, c)


def _rope(t, cos, sin_signed, first_half):
    partner = jnp.where(first_half, pltpu.roll(t, LANES - HALF_HEAD, 1), pltpu.roll(t, HALF_HEAD, 1))
    return t * cos + partner * sin_signed


Q_KIND, K_KIND, V_KIND = 0, 1, 2
_NAT_KINDS = ([Q_KIND] * 4 + [K_KIND] + [V_KIND] + [Q_KIND] * 2 + [K_KIND] * 2 + [V_KIND] * 2)
_GRP_KINDS = [Q_KIND] * 2 + [K_KIND] * 2 + [V_KIND] * 2


def _in_proj_kernel(x_ref, g_ref, w_hbm_ref, c1_ref, s1_ref, c4_ref, s4_ref, c16_ref, s16_ref,
                    nat_ref, gate_ref, g1_ref, g2_ref, hf_ref, h4_ref, hb_ref, hd2_ref, w_ref, stage_ref, sem,
                    *, layer):
    tm = x_ref.shape[0]

    @pl.when(pl.program_id(0) == 0)
    def _():
        def chunk(j):
            cols = pl.ds(j * W_CHUNK, W_CHUNK)
            return pltpu.make_async_copy(w_hbm_ref.at[layer, :, cols], stage_ref.at[j % 2], sem.at[j % 2])

        n = D_IN // W_CHUNK
        chunk(0).start()
        for j in range(n):
            if j + 1 < n:
                chunk(j + 1).start()
            chunk(j).wait()
            w_ref[:, j * W_CHUNK:(j + 1) * W_CHUNK] = stage_ref[j % 2].astype(BF16)

    x = x_ref[...]
    h = x * lax.rsqrt(jnp.mean(x * x, axis=-1, keepdims=True) + NORM_EPS) * g_ref[...]
    n_chunks = D_MODEL // LANES
    for c in range(n_chunks):
        hf_ref[c] = h[:, c * LANES:(c + 1) * LANES]
    hb_ref[...] = h.astype(BF16)
    lane = lax.broadcasted_iota(I32, (1, LANES), 1)
    first_half = (lane % HEAD_DIM) < HALF_HEAD

    def project(h_b, col0, kinds, cos_ref, sin_ref, store):
        width = W_CHUNK
        for c0 in range(0, len(kinds) * LANES, width):
            w = min(width, len(kinds) * LANES - c0)
            res = jnp.dot(h_b, w_ref[:, col0 + c0:col0 + c0 + w], preferred_element_type=F32)
            for j in range(w // LANES):
                kind = kinds[(c0 // LANES) + j]
                t = res[:, j * LANES:(j + 1) * LANES]
                if kind != V_KIND:
                    t = _rope(t, cos_ref[...], sin_ref[...], first_half)
                if kind == Q_KIND:
                    t = t * (HEAD_DIM ** -0.5)
                store(c0 + j * LANES, t)

    low_head = lane < HEAD_DIM

    def store_nat(c, t):
        if A_Q_DIM <= c < A_Q_DIM + A_KV_DIM:
            swapped = pltpu.roll(t, HEAD_DIM, 1)
            nat_ref[:, NAT_AK:NAT_AK + LANES] = jnp.where(low_head, t, swapped).astype(BF16)
            nat_ref[:, NAT_AK + LANES:NAT_AK + 2 * LANES] = jnp.where(low_head, swapped, t).astype(BF16)
        elif c < A_Q_DIM + 2 * A_KV_DIM:
            out = c if c < A_Q_DIM else NAT_AV
            nat_ref[:, out:out + LANES] = t.astype(BF16)
        else:
            out = c - 2 * A_KV_DIM
            nat_ref[:, out:out + LANES] = t.astype(BF16)

    project(hb_ref[...], 0, _NAT_KINDS, c1_ref, s1_ref, store_nat)

    for c0 in range(0, GATE_COLS, W_CHUNK):
        res = jnp.dot(hb_ref[...], w_ref[:, COL_GATE + c0:COL_GATE + c0 + W_CHUNK], preferred_element_type=F32)
        gate_ref[:, c0:c0 + W_CHUNK] = jax.nn.sigmoid(res).astype(BF16)

    for slot, (dil, col0, cos_ref, sin_ref, out_ref) in enumerate(((4, COL_G1, c4_ref, s4_ref, g1_ref),
                                                                   (16, COL_G2, c16_ref, s16_ref, g2_ref))):
        n = tm // dil
        hd_ref = hd2_ref.at[slot]
        n4 = tm // 4
        for r in range(dil):
            for c in range(n_chunks):
                cs = slice(c * LANES, (c + 1) * LANES)
                if dil == 4:
                    rows4 = hf_ref[c, pl.ds(r, n, stride=4), :]
                    h4_ref[c, r * n:(r + 1) * n, :] = rows4
                    hd_ref[r * n:(r + 1) * n, cs] = rows4.astype(BF16)
                else:
                    r4, q = r % 4, r // 4
                    hd_ref[r * n:(r + 1) * n, cs] = h4_ref[c, pl.ds(r4 * n4 + q, n, stride=4), :].astype(BF16)

        def store_grp(c, t, out_ref=out_ref, dil=dil, n=n):
            v = t.astype(BF16)
            for r in range(dil):
                out_ref[r, :, c:c + LANES] = v[r * n:(r + 1) * n]

        project(hd_ref[...], col0, _GRP_KINDS, cos_ref, sin_ref, store_grp)


def _in_proj(x2d, g, w_in, layer, tables, batch, seq):
    tm = IN_TILE
    tiles_per_seq = seq // tm
    n_tok = batch * seq
    c1, s1, c4, s4, c16, s16 = tables
    tab_spec = pl.BlockSpec((tm, LANES), lambda i: (i % tiles_per_seq, 0))
    return pl.pallas_call(
        functools.partial(_in_proj_kernel, layer=layer),
        grid=(n_tok // tm,),
        in_specs=[
            pl.BlockSpec((tm, D_MODEL), lambda i: (i, 0)),
            pl.BlockSpec((1, D_MODEL), lambda i: (0, 0)),
            pl.BlockSpec(memory_space=pl.ANY),
            tab_spec, tab_spec, tab_spec, tab_spec, tab_spec, tab_spec,
        ],
        out_specs=[
            pl.BlockSpec((tm, NAT_COLS), lambda i: (i, 0)),
            pl.BlockSpec((tm, GATE_COLS), lambda i: (i, 0)),
            pl.BlockSpec((None, 4, tm // 4, GRP_COLS), lambda i: (i // tiles_per_seq, 0, i % tiles_per_seq, 0)),
            pl.BlockSpec((None, 16, tm // 16, GRP_COLS), lambda i: (i // tiles_per_seq, 0, i % tiles_per_seq, 0)),
        ],
        out_shape=[
            jax.ShapeDtypeStruct((n_tok, NAT_COLS), BF16),
            jax.ShapeDtypeStruct((n_tok, GATE_COLS), BF16),
            jax.ShapeDtypeStruct((batch, 4, seq // 4, GRP_COLS), BF16),
            jax.ShapeDtypeStruct((batch, 16, seq // 16, GRP_COLS), BF16),
        ],
        scratch_shapes=[
            pltpu.VMEM((D_MODEL // LANES, tm, LANES), F32),
            pltpu.VMEM((D_MODEL // LANES, tm, LANES), F32),
            pltpu.VMEM((tm, D_MODEL), BF16),
            pltpu.VMEM((2, tm, D_MODEL), BF16),
            pltpu.VMEM((D_MODEL, D_IN), BF16),
            pltpu.VMEM((2, D_MODEL, W_CHUNK), F32),
            pltpu.SemaphoreType.DMA((2,)),
        ],
        compiler_params=_cparams("arbitrary"),
        name="in_proj",
    )(x2d, g, w_in, c1, s1, c4, s4, c16, s16)


def _store_transposed(vt_ref, blocks):
    for r0, src in blocks:
        n = src.shape[0]
        for j in range(0, n, T_ROWS):
            m = min(T_ROWS, n - j)
            vt_ref[:, r0 + j:r0 + j + m] = src[j:j + m, :].astype(F32).T.astype(BF16)


def _masked_heads(q_pair, low_head, high_head):
    zero = jnp.zeros_like(q_pair)
    return [jnp.where(low_head, q_pair, zero), jnp.where(high_head, q_pair, zero)]


def _attn_a_kernel(sink_ref, q_ref, kp_ref, km_ref, kn_ref, vp_ref, vm_ref, vn_ref, o_ref, k_ref, vt_ref, *, seq):
    tq = q_ref.shape[0]
    hw = A_HALF_WINDOW
    n_sub = tq // hw
    i = pl.program_id(1)
    last_blk = seq // hw - 1
    k_ref[0:hw, :] = kp_ref[...]
    k_ref[hw:hw + tq, :] = km_ref[...]
    k_ref[hw + tq:tq + 2 * hw, :] = kn_ref[...]
    _store_transposed(vt_ref, ((0, vp_ref), (hw, vm_ref), (hw + tq, vn_ref)))
    grp_cols = A_GROUP * hw
    key = lax.broadcasted_iota(I32, (hw, grp_cols), 0)
    qry = lax.broadcasted_iota(I32, (hw, grp_cols), 1) % hw
    low_head = lax.broadcasted_iota(I32, (1, LANES), 1) < HEAD_DIM
    high_head = jnp.logical_not(low_head)
    for sb in range(n_sub):
        r0 = sb * hw
        blk = i * n_sub + sb
        mask_p = (key >= qry) if sb > 0 else (key >= qry + jnp.where(blk > 0, 0, hw))
        mask_n = (key <= qry) if sb < n_sub - 1 else (key <= qry - jnp.where(blk < last_blk, 0, hw))
        out_t = []
        for g in range(A_KV_HEADS):
            heads = range(g * A_GROUP, (g + 1) * A_GROUP)
            q_parts = []
            for c in range(g * A_GROUP // 2, (g + 1) * A_GROUP // 2):
                q_parts += _masked_heads(q_ref[r0:r0 + hw, c * LANES:(c + 1) * LANES], low_head, high_head)
            q = jnp.concatenate(q_parts, axis=0)
            sink = jnp.concatenate([jnp.full((1, hw), sink_ref[h], F32) for h in heads], axis=1)
            k = k_ref[r0:r0 + 3 * hw, g * LANES:(g + 1) * LANES]
            s = lax.dot_general(k, q, (((1,), (1,)), ((), ())), preferred_element_type=F32)
            sp = jnp.where(mask_p, s[0:hw], NEG_INF)
            so = s[hw:2 * hw]
            sn = jnp.where(mask_n, s[2 * hw:3 * hw], NEG_INF)
            m = jnp.max(jnp.maximum(jnp.maximum(sp, so), sn), axis=0, keepdims=True)
            m = jnp.maximum(m, sink)
            pp, po, pn = jnp.exp(sp - m), jnp.exp(so - m), jnp.exp(sn - m)
            denom = jnp.sum(pp + po + pn, axis=0, keepdims=True) + jnp.exp(sink - m)
            p = jnp.concatenate([pp, po, pn], axis=0).astype(BF16)
            vt = vt_ref[g * HEAD_DIM:(g + 1) * HEAD_DIM, r0:r0 + 3 * hw]
            o = jnp.dot(vt, p, preferred_element_type=F32) * (1.0 / denom)
            out_t += [o[:, j * hw:(j + 1) * hw] for j in range(A_GROUP)]
        o_ref[r0:r0 + hw, :] = jnp.concatenate(out_t, axis=0).T.astype(BF16)


def _attn_a(nat3d, sink):
    batch, seq, _ = nat3d.shape
    tq = A_Q_TILE
    hw = A_HALF_WINDOW
    per = tq // hw
    n_hw = seq // hw
    k_cols, v_cols = 2 * A_KV_DIM, A_KV_DIM
    k_blk, v_blk = NAT_AK // k_cols, NAT_AV // v_cols

    def prev_spec(cols, blk):
        return pl.BlockSpec((None, hw, cols), lambda b, i: (b, jnp.maximum(i * per - 1, 0), blk))

    def main_spec(cols, blk):
        return pl.BlockSpec((None, tq, cols), lambda b, i: (b, i, blk))

    def next_spec(cols, blk):
        return pl.BlockSpec((None, hw, cols), lambda b, i: (b, jnp.minimum((i + 1) * per, n_hw - 1), blk))

    return pl.pallas_call(
        functools.partial(_attn_a_kernel, seq=seq),
        grid=(batch, seq // tq),
        in_specs=[
            pl.BlockSpec(memory_space=pltpu.SMEM),
            main_spec(A_Q_DIM, 0),
            prev_spec(k_cols, k_blk), main_spec(k_cols, k_blk), next_spec(k_cols, k_blk),
            prev_spec(v_cols, v_blk), main_spec(v_cols, v_blk), next_spec(v_cols, v_blk),
        ],
        out_specs=pl.BlockSpec((None, tq, A_Q_DIM), lambda b, i: (b, i, 0)),
        out_shape=jax.ShapeDtypeStruct((batch, seq, A_Q_DIM), BF16),
        scratch_shapes=[pltpu.VMEM((tq + 2 * hw, k_cols), BF16), pltpu.VMEM((v_cols, tq + 2 * hw), BF16)],
        compiler_params=_cparams("parallel", "parallel"),
        name="attn_a",
    )(sink, nat3d, nat3d, nat3d, nat3d, nat3d, nat3d, nat3d)


def _attn_b_kernel(q_ref, kp_ref, km_ref, kn_ref, vp_ref, vm_ref, vn_ref, o_ref, lse_ref,
                   k_ref, vt_ref, *, sub_len):
    for s in range(q_ref.shape[0]):
        _attn_b_one(q_ref.at[s], kp_ref.at[s], km_ref.at[s], kn_ref.at[s], vp_ref.at[s], vm_ref.at[s], vn_ref.at[s],
                    o_ref.at[s], lse_ref.at[s], k_ref.at[s], vt_ref.at[s], sub_len)


def _attn_b_one(q_ref, kp_ref, km_ref, kn_ref, vp_ref, vm_ref, vn_ref, o_ref, lse_ref, k_ref, vt_ref, sub_len):
    tq = q_ref.shape[0]
    hw = B_HALF_WINDOW
    t0 = pl.program_id(1) * tq
    k_ref[0:hw, :] = kp_ref[...]
    k_ref[hw:hw + tq, :] = km_ref[...]
    k_ref[hw + tq:tq + 2 * hw, :] = kn_ref[...]
    _store_transposed(vt_ref, ((0, vp_ref), (hw, vm_ref), (hw + tq, vn_ref)))
    kw = B_SUB + 2 * hw
    n_sub = tq // B_SUB
    all_cols = B_HEADS * B_SUB
    key = lax.broadcasted_iota(I32, (kw, all_cols), 0)
    qry = lax.broadcasted_iota(I32, (kw, all_cols), 1) % B_SUB
    in_band = jnp.abs(key - hw - qry) <= hw
    low_head = lax.broadcasted_iota(I32, (1, LANES), 1) < HEAD_DIM
    high_head = jnp.logical_not(low_head)
    for sb in range(n_sub):
        r0 = sb * B_SUB
        valid = in_band
        if sb == 0:
            valid = valid & (t0 - hw + key >= 0)
        if sb == n_sub - 1:
            valid = valid & (t0 + r0 - hw + key < sub_len)
        s_parts = []
        for c in range(B_HEADS // 2):
            cs = slice(c * LANES, (c + 1) * LANES)
            q = jnp.concatenate(_masked_heads(q_ref[r0:r0 + B_SUB, cs], low_head, high_head), axis=0)
            s_parts.append(lax.dot_general(k_ref[r0:r0 + kw, cs], q, (((1,), (1,)), ((), ())),
                                           preferred_element_type=F32))
        s = jnp.where(valid, jnp.concatenate(s_parts, axis=1), NEG_INF)
        m = jnp.max(s, axis=0, keepdims=True)
        p = jnp.exp(s - m)
        denom = jnp.sum(p, axis=0, keepdims=True)
        p = p.astype(BF16)
        inv = 1.0 / denom
        lse = m + jnp.log(denom)
        out_t, lse_t = [], []
        for h in range(B_HEADS):
            qs = slice(h * B_SUB, (h + 1) * B_SUB)
            vt = vt_ref[h * HEAD_DIM:(h + 1) * HEAD_DIM, r0:r0 + kw]
            out_t.append(jnp.dot(vt, p[:, qs], preferred_element_type=F32) * inv[:, qs])
            lse_t.append(jnp.broadcast_to(lse[:, qs], (HEAD_DIM, B_SUB)))
        o_ref[r0:r0 + B_SUB, :] = jnp.concatenate(out_t, axis=0).T
        lse_ref[r0:r0 + B_SUB, :] = jnp.concatenate(lse_t, axis=0).T


def _attn_b(arr3d, q_blk, name):
    n_sub, sub_len, _ = arr3d.shape
    tq = min(B_Q_TILE, sub_len)
    n_per = B_Q_TILE // tq
    hw = B_HALF_WINDOW
    per = tq // hw
    n_hw = sub_len // hw

    def main_spec(c):
        return pl.BlockSpec((n_per, tq, B_DIM), lambda g, i: (g, i, c))

    def prev_spec(c):
        return pl.BlockSpec((n_per, hw, B_DIM), lambda g, i: (g, jnp.maximum(i * per - 1, 0), c))

    def next_spec(c):
        return pl.BlockSpec((n_per, hw, B_DIM), lambda g, i: (g, jnp.minimum((i + 1) * per, n_hw - 1), c))

    out_spec = pl.BlockSpec((n_per, tq, B_DIM), lambda g, i: (g, i, 0))
    return pl.pallas_call(
        functools.partial(_attn_b_kernel, sub_len=sub_len),
        grid=(n_sub // n_per, sub_len // tq),
        in_specs=[main_spec(q_blk),
                  prev_spec(q_blk + 1), main_spec(q_blk + 1), next_spec(q_blk + 1),
                  prev_spec(q_blk + 2), main_spec(q_blk + 2), next_spec(q_blk + 2)],
        out_specs=[out_spec, out_spec],
        out_shape=[jax.ShapeDtypeStruct((n_sub, sub_len, B_DIM), F32)] * 2,
        scratch_shapes=[pltpu.VMEM((n_per, tq + 2 * hw, B_DIM), BF16),
                        pltpu.VMEM((n_per, B_DIM, tq + 2 * hw), BF16)],
        compiler_params=_cparams("parallel", "parallel"),
        name=name,
    )(arr3d, arr3d, arr3d, arr3d, arr3d, arr3d, arr3d)


def _merge_kernel(x_ref, ya_ref, o0_ref, l0_ref, o1_ref, l1_ref, o2_ref, l2_ref, gate_ref,
                  wa_ref, wb_ref, wo_ref, g2_ref, wr_ref, br_ref,
                  x1_ref, h2_ref, code_ref, wcol_ref, cnt_ref,
                  so1_ref, sl1_ref, so2_ref, sl2_ref, yb_ref, run_ref, earlier_ref,
                  wab_ref, wbb_ref, wob_ref):
    tm = x_ref.shape[0]

    @pl.when(pl.program_id(0) == 0)
    def _():
        run_ref[...] = jnp.zeros_like(run_ref)
        wab_ref[...] = wa_ref[...].astype(BF16)
        wbb_ref[...] = wb_ref[...].astype(BF16)
        wob_ref[...] = wo_ref[...].astype(BF16)
        row = lax.broadcasted_iota(I32, (tm, tm), 0)
        col = lax.broadcasted_iota(I32, (tm, tm), 1)
        earlier_ref[...] = (row < col).astype(BF16)

    for dil, src_o, src_l, dst_o, dst_l in ((4, o1_ref, l1_ref, so1_ref, sl1_ref),
                                            (16, o2_ref, l2_ref, so2_ref, sl2_ref)):
        n = tm // dil
        for r in range(dil):
            for c in range(B_DIM // LANES):
                cs = slice(c * LANES, (c + 1) * LANES)
                dst_o[c, pl.ds(r, n, stride=dil), :] = src_o[r, :, cs]
                dst_l[c, pl.ds(r, n, stride=dil), :] = src_l[r, :, cs]

    for c in range(B_DIM // LANES):
        cs = slice(c * LANES, (c + 1) * LANES)
        l0, l1, l2 = l0_ref[:, cs], sl1_ref[c], sl2_ref[c]
        m = jnp.maximum(jnp.maximum(l0, l1), l2)
        e0, e1, e2 = jnp.exp(l0 - m), jnp.exp(l1 - m), jnp.exp(l2 - m)
        yb = (e0 * o0_ref[:, cs] + e1 * so1_ref[c] + e2 * so2_ref[c]) / (e0 + e1 + e2)
        yb_ref[:, cs] = yb.astype(BF16)

    ya_p = jnp.dot(ya_ref[...], wab_ref[...], preferred_element_type=F32)
    yb_p = jnp.dot(yb_ref[...], wbb_ref[...], preferred_element_type=F32)
    merged = gate_ref[:, 0:D_MODEL].astype(F32) * ya_p + gate_ref[:, D_MODEL:GATE_COLS].astype(F32) * yb_p
    x1 = x_ref[...] + jnp.dot(merged.astype(BF16), wob_ref[...], preferred_element_type=F32)
    x1_ref[...] = x1

    h2 = x1 * lax.rsqrt(jnp.mean(x1 * x1, axis=-1, keepdims=True) + NORM_EPS) * g2_ref[...]
    _store_row_tiles(h2_ref, h2)

    h_hi = h2.astype(BF16)
    h_lo = (h2 - h_hi.astype(F32)).astype(BF16)
    hi_terms = jnp.dot(h_hi, wr_ref[...], preferred_element_type=F32)
    logits = (hi_terms[:, 0:LANES] + hi_terms[:, LANES:2 * LANES]
              + jnp.dot(h_lo, wr_ref[:, 0:LANES], preferred_element_type=F32)) + br_ref[...]

    logits_t = logits.T
    sub = lax.broadcasted_iota(I32, (EXPERTS_PER_GROUP, tm), 0).astype(F32)
    none = float(EXPERTS_PER_GROUP)
    gl = jnp.where(sub < MOE_GROUPS, logits_t[N_EXPERTS:N_EXPERTS + EXPERTS_PER_GROUP], -jnp.inf)
    gmax = jnp.max(gl, axis=0, keepdims=True)
    gidx = jnp.min(jnp.where(gl == gmax, sub, none), axis=0, keepdims=True)
    gw = 1.0 / jnp.sum(jnp.exp(gl - gmax), axis=0, keepdims=True)
    el = logits_t[0:EXPERTS_PER_GROUP]
    for g in range(1, MOE_GROUPS):
        el = jnp.where(gidx == g, logits_t[g * EXPERTS_PER_GROUP:(g + 1) * EXPERTS_PER_GROUP], el)
    v1 = jnp.max(el, axis=0, keepdims=True)
    i1 = jnp.min(jnp.where(el == v1, sub, none), axis=0, keepdims=True)
    el2 = jnp.where(sub == i1, -jnp.inf, el)
    v2 = jnp.max(el2, axis=0, keepdims=True)
    i2 = jnp.min(jnp.where(el2 == v2, sub, none), axis=0, keepdims=True)
    t = jnp.exp(v2 - v1)
    w1 = gw / (1.0 + t)
    w2 = gw * t / (1.0 + t)
    e1 = gidx * EXPERTS_PER_GROUP + i1
    e2 = gidx * EXPERTS_PER_GROUP + i2

    expert = lax.broadcasted_iota(I32, (N_EXPERTS, tm), 0).astype(F32)
    oh1 = (expert == e1).astype(F32)
    oh2 = (expert == e2).astype(F32)
    oh = oh1 + oh2
    run = run_ref[...]
    before = (jnp.dot(oh.astype(BF16), earlier_ref[...], preferred_element_type=F32)
              + jnp.concatenate([run] * (tm // LANES), axis=1))
    rank1 = jnp.sum(before * oh1, axis=0, keepdims=True)
    rank2 = jnp.sum(before * oh2, axis=0, keepdims=True)
    run_ref[...] = run + jnp.sum(oh, axis=1, keepdims=True)
    cnt_ref[...] = run_ref[...]

    scale = float(1 << RANK_BITS)
    zeros = jnp.zeros((6, tm), F32)
    code_ref[...] = jnp.concatenate([e1 * scale + rank1, e2 * scale + rank2, zeros], axis=0).astype(I32)
    w_rows = jnp.concatenate([w1, w2, jnp.zeros((LANES - 2, tm), F32)], axis=0)
    wcol_ref[...] = w_rows.T


def _merge(x2d, ya, o0, l0, o1, l1, o2, l2, gates, wa, wb, wo, g2, wr, br, batch, seq):
    tm = TOK_TILE
    tps = seq // tm
    n_tok = batch * seq

    def tok(c):
        return pl.BlockSpec((tm, c), lambda i: (i, 0))

    def full(a):
        return pl.BlockSpec(a.shape, lambda i: (0,) * a.ndim)

    def res_spec(d):
        return pl.BlockSpec((None, d, tm // d, B_DIM), lambda i: (i // tps, 0, i % tps, 0))

    return pl.pallas_call(
        _merge_kernel,
        grid=(n_tok // tm,),
        in_specs=[tok(D_MODEL), tok(A_Q_DIM), tok(B_DIM), tok(B_DIM),
                  res_spec(4), res_spec(4), res_spec(16), res_spec(16), tok(GATE_COLS),
                  full(wa), full(wb), full(wo), full(g2), full(wr), full(br)],
        out_specs=[tok(D_MODEL), pl.BlockSpec((tm * ROW_CHUNKS, LANES), lambda i: (i, 0)),
                   pl.BlockSpec((8, tm), lambda i: (0, i)),
                   tok(LANES),
                   pl.BlockSpec((N_EXPERTS, LANES), lambda i: (0, 0))],
        out_shape=[jax.ShapeDtypeStruct((n_tok, D_MODEL), F32),
                   jax.ShapeDtypeStruct((n_tok * ROW_CHUNKS, LANES), F32),
                   jax.ShapeDtypeStruct((8, n_tok), I32),
                   jax.ShapeDtypeStruct((n_tok, LANES), F32),
                   jax.ShapeDtypeStruct((N_EXPERTS, LANES), F32)],
        scratch_shapes=([pltpu.VMEM((B_DIM // LANES, tm, LANES), F32)] * 4
                        + [pltpu.VMEM((tm, B_DIM), BF16), pltpu.VMEM((N_EXPERTS, LANES), F32),
                           pltpu.VMEM((tm, tm), BF16),
                           pltpu.VMEM(wa.shape, BF16), pltpu.VMEM(wb.shape, BF16), pltpu.VMEM(wo.shape, BF16)]),
        compiler_params=_cparams("arbitrary"),
        name="merge_route",
    )(x2d, ya, o0, l0, o1, l1, o2, l2, gates, wa, wb, wo, g2, wr, br)


def _row_tile(ref, t):
    return ref.at[pl.ds(pl.multiple_of(t * ROW_CHUNKS, ROW_CHUNKS), ROW_CHUNKS)]


def _store_row_tiles(ref, val):
    rows = val.shape[0]
    for c in range(ROW_CHUNKS):
        ref[pl.ds(c, rows, stride=ROW_CHUNKS), :] = val[:, c * LANES:(c + 1) * LANES]


def _load_row_tiles_chunk(ref, c):
    return ref[pl.ds(c, ref.shape[0] // ROW_CHUNKS, stride=ROW_CHUNKS), :]


def _dispatch_kernel(slot_ref, fill_ref, h_ref, *rest, n_tok, reuse):
    if reuse:
        _, xs_ref, sem = rest
    else:
        xs_ref, zero_ref, sem, zsem = rest
    i = pl.program_id(0)
    rows = h_ref.shape[0] // ROW_CHUNKS

    if not reuse:
        @pl.when(i == 0)
        def _():
            tile_rows = zero_ref.shape[0]
            zero_ref[...] = jnp.zeros_like(zero_ref)

            def zero_copy(t):
                dst = xs_ref.at[pl.ds(pl.multiple_of(t * tile_rows, tile_rows), tile_rows)]
                return pltpu.make_async_copy(zero_ref, dst, zsem)

            def start(t, carry):
                @pl.when(fill_ref[t] != 0)
                def _():
                    zero_copy(t).start()
                return carry

            def wait(t, carry):
                @pl.when(fill_ref[t] != 0)
                def _():
                    zero_copy(t).wait()
                return carry

            n_tiles = xs_ref.shape[0] // tile_rows
            lax.fori_loop(0, n_tiles, start, 0)
            lax.fori_loop(0, n_tiles, wait, 0)

    def issue(j, carry):
        t = i * rows + j
        for k in range(2):
            pltpu.make_async_copy(_row_tile(h_ref, j), _row_tile(xs_ref, slot_ref[k * n_tok + t]), sem).start(priority=k)
        return carry

    lax.fori_loop(0, rows, issue, 0, unroll=DMA_UNROLL)
    for _ in range(2):
        pltpu.make_async_copy(h_ref, xs_ref.at[pl.ds(0, rows * ROW_CHUNKS)], sem).wait()


def _dispatch(slots, fill, h2, xs_prev, n_slots):
    n_tok = h2.shape[0] // ROW_CHUNKS
    rows = DISPATCH_TILE
    reuse = xs_prev is not None
    h_spec = pl.BlockSpec((rows * ROW_CHUNKS, LANES), lambda i, s, f: (i, 0))
    any_spec = pl.BlockSpec(memory_space=pl.ANY)
    scratch = [pltpu.SemaphoreType.DMA(())] if reuse else [
        pltpu.VMEM((EXP_TILE * ROW_CHUNKS, LANES), F32), pltpu.SemaphoreType.DMA(()), pltpu.SemaphoreType.DMA(())]
    return pl.pallas_call(
        functools.partial(_dispatch_kernel, n_tok=n_tok, reuse=reuse),
        grid_spec=pltpu.PrefetchScalarGridSpec(
            num_scalar_prefetch=2,
            grid=(n_tok // rows,),
            in_specs=[h_spec, any_spec] if reuse else [h_spec],
            out_specs=any_spec,
            scratch_shapes=scratch,
        ),
        out_shape=jax.ShapeDtypeStruct((n_slots * ROW_CHUNKS, LANES), F32),
        input_output_aliases={3: 0} if reuse else {},
        compiler_params=_cparams("arbitrary"),
        name="dispatch_reuse" if reuse else "dispatch",
    )(*((slots, fill, h2, xs_prev) if reuse else (slots, fill, h2)))


def _combine_kernel(slot_ref, x_ref, w_ref, g_ref, ys_ref, o_ref, buf_ref, y_ref, sem, *, n_tok, final):
    i = pl.program_id(0)
    n_steps = pl.num_programs(0)
    rows = x_ref.shape[0]

    def issue_tile(tile, slot):
        def issue(j, carry):
            t = tile * rows + j
            for k in range(2):
                pltpu.make_async_copy(_row_tile(ys_ref, slot_ref[k * n_tok + t]),
                                      _row_tile(buf_ref.at[slot, k], j), sem.at[slot]).start(priority=k)
            return carry

        lax.fori_loop(0, rows, issue, 0, unroll=DMA_UNROLL)

    @pl.when(i == 0)
    def _():
        issue_tile(0, 0)

    @pl.when(i + 1 < n_steps)
    def _():
        issue_tile(i + 1, (i + 1) % 2)

    cur = i % 2
    for k in range(2):
        pltpu.make_async_copy(ys_ref.at[pl.ds(0, rows * ROW_CHUNKS)], buf_ref.at[cur, k], sem.at[cur]).wait()
    w = w_ref[...]
    w1, w2 = w[:, 0:1], w[:, 1:2]
    for c in range(ROW_CHUNKS):
        cs = slice(c * LANES, (c + 1) * LANES)
        y1 = _load_row_tiles_chunk(buf_ref.at[cur, 0], c)
        y2 = _load_row_tiles_chunk(buf_ref.at[cur, 1], c)
        (y_ref if final else o_ref)[:, cs] = x_ref[:, cs] + w1 * y1 + w2 * y2
    if final:
        y = y_ref[...]
        o_ref[...] = y * lax.rsqrt(jnp.mean(y * y, axis=-1, keepdims=True) + NORM_EPS) * g_ref[...]


def _combine(slots, x1, wcol, g, ys, final):
    n_tok = x1.shape[0]
    rows = ROW_TILE
    return pl.pallas_call(
        functools.partial(_combine_kernel, n_tok=n_tok, final=final),
        grid_spec=pltpu.PrefetchScalarGridSpec(
            num_scalar_prefetch=1,
            grid=(n_tok // rows,),
            in_specs=[pl.BlockSpec((rows, D_MODEL), lambda i, s: (i, 0)),
                      pl.BlockSpec((rows, LANES), lambda i, s: (i, 0)),
                      pl.BlockSpec((1, D_MODEL), lambda i, s: (0, 0)),
                      pl.BlockSpec(memory_space=pl.ANY)],
            out_specs=pl.BlockSpec((rows, D_MODEL), lambda i, s: (i, 0)),
            scratch_shapes=[pltpu.VMEM((2, 2, rows * ROW_CHUNKS, LANES), F32),
                            pltpu.VMEM((rows, D_MODEL), F32),
                            pltpu.SemaphoreType.DMA((2,))],
        ),
        out_shape=jax.ShapeDtypeStruct((n_tok, D_MODEL), F32),
        compiler_params=_cparams("arbitrary"),
        name="combine",
    )(slots, x1, wcol, g, ys)


def _experts_kernel(tile_e_ref, n_used_ref, xs_ref, wg_ref, wu_ref, wd_ref, ys_ref, xbuf_ref, x_ref, sem):
    del tile_e_ref
    i = pl.program_id(0)
    n_used = n_used_ref[0]
    tile_rows = xbuf_ref.shape[1]

    def fetch(tile):
        rows = pl.ds(pl.multiple_of(tile * tile_rows, tile_rows), tile_rows)
        slot = tile % XS_SLOTS
        return pltpu.make_async_copy(xs_ref.at[rows], xbuf_ref.at[slot], sem.at[slot])

    @pl.when(i == 0)
    def _():
        fetch(0).start()

        @pl.when(n_used > 1)
        def _():
            fetch(1).start()

    @pl.when(i < n_used)
    def _():
        @pl.when(i + XS_SLOTS - 1 < n_used)
        def _():
            fetch(i + XS_SLOTS - 1).start()

        fetch(i).wait()
        xb = xbuf_ref.at[i % XS_SLOTS]
        for c in range(ROW_CHUNKS):
            x_ref[:, c * LANES:(c + 1) * LANES] = _load_row_tiles_chunk(xb, c).astype(BF16)
        x = x_ref[...]
        a = jnp.dot(x, wg_ref[...].astype(BF16), preferred_element_type=F32)
        u = jnp.dot(x, wu_ref[...].astype(BF16), preferred_element_type=F32)
        z = (a * jax.nn.sigmoid(a)) * u
        _store_row_tiles(ys_ref, jnp.dot(z.astype(BF16), wd_ref[...].astype(BF16), preferred_element_type=F32))

    @pl.when(i >= n_used)
    def _():
        ys_ref[...] = jnp.zeros_like(ys_ref)


def _experts(tile_e, n_used, xs, wg, wu, wd, layer):
    n_slots = xs.shape[0] // ROW_CHUNKS
    te = EXP_TILE

    def out_map(i, tile_e, n_used):
        return (i, 0)

    def w_map(i, tile_e, n_used):
        return (layer, tile_e[i], 0, 0)

    return pl.pallas_call(
        _experts_kernel,
        grid_spec=pltpu.PrefetchScalarGridSpec(
            num_scalar_prefetch=2,
            grid=(n_slots // te,),
            in_specs=[pl.BlockSpec(memory_space=pl.ANY),
                      pl.BlockSpec((None, None, D_MODEL, D_EXPERT), w_map),
                      pl.BlockSpec((None, None, D_MODEL, D_EXPERT), w_map),
                      pl.BlockSpec((None, None, D_EXPERT, D_MODEL), w_map)],
            out_specs=pl.BlockSpec((te * ROW_CHUNKS, LANES), out_map),
            scratch_shapes=[pltpu.VMEM((XS_SLOTS, te * ROW_CHUNKS, LANES), F32),
                            pltpu.VMEM((te, D_MODEL), BF16),
                            pltpu.SemaphoreType.DMA((XS_SLOTS,))],
        ),
        out_shape=jax.ShapeDtypeStruct((n_slots * ROW_CHUNKS, LANES), F32),
        compiler_params=_cparams("arbitrary"),
        name="experts",
    )(tile_e, n_used, xs, wg, wu, wd)


def kernel(x, attn_norm_g, w_in, a_sink, w_branch_a, w_branch_b, w_out, ffn_norm_g,
           w_router_group, b_router_group, w_router_expert, b_router_expert,
           w_exp_gate, w_exp_up, w_exp_down, final_norm_g):
    batch, seq, d_model = x.shape
    depth = w_in.shape[0]
    n_tok = batch * seq
    assert d_model == D_MODEL and w_in.shape[2] == D_IN
    assert seq % (16 * B_SUB) == 0 and seq % IN_TILE == 0 and n_tok % DISPATCH_TILE == 0
    assert n_tok < (1 << RANK_BITS)

    cos_t, sin_t = _rope_tables(seq)
    tables = (cos_t, sin_t,
              _residue_order(cos_t, 4, IN_TILE), _residue_order(sin_t, 4, IN_TILE),
              _residue_order(cos_t, 16, IN_TILE), _residue_order(sin_t, 16, IN_TILE))

    n_slots = 2 * n_tok + N_EXPERTS * EXP_TILE
    n_tiles = n_slots // EXP_TILE
    x2d = x.reshape(n_tok, D_MODEL)

    for l in range(depth):
        nat, gates, grp1, grp2 = _in_proj(x2d, attn_norm_g[l][None, :], w_in, l, tables, batch, seq)
        nat3d = nat.reshape(batch, seq, NAT_COLS)
        ya = _attn_a(nat3d, a_sink[l]).reshape(n_tok, A_Q_DIM)
        o0, l0 = _attn_b(nat3d, NAT_B // B_DIM, "attn_b1")
        o1, l1 = _attn_b(grp1.reshape(batch * 4, seq // 4, GRP_COLS), 0, "attn_b4")
        o2, l2 = _attn_b(grp2.reshape(batch * 16, seq // 16, GRP_COLS), 0, "attn_b16")

        wr = jnp.zeros((D_MODEL, LANES), F32)
        wr = wr.at[:, 0:N_EXPERTS].set(w_router_expert[l]).at[:, N_EXPERTS:N_EXPERTS + MOE_GROUPS].set(w_router_group[l])
        br = jnp.zeros((1, LANES), F32)
        br = br.at[0, 0:N_EXPERTS].set(b_router_expert[l]).at[0, N_EXPERTS:N_EXPERTS + MOE_GROUPS].set(b_router_group[l])
        wr_hi = wr.astype(BF16)
        wr_lo = (wr - wr_hi.astype(F32)).astype(BF16)
        wr_stack = jnp.concatenate([wr_hi, wr_lo], axis=1)

        x1, h2, code, wcol, cnt = _merge(
            x2d, ya, o0.reshape(n_tok, B_DIM), l0.reshape(n_tok, B_DIM),
            o1.reshape(batch, 4, seq // 4, B_DIM), l1.reshape(batch, 4, seq // 4, B_DIM),
            o2.reshape(batch, 16, seq // 16, B_DIM), l2.reshape(batch, 16, seq // 16, B_DIM),
            gates, w_branch_a[l], w_branch_b[l], w_out[l],
            ffn_norm_g[l][None, :], wr_stack, br, batch, seq)

        counts = cnt[:, 0].astype(I32)
        padded = ((counts + EXP_TILE - 1) // EXP_TILE) * EXP_TILE
        ends = jnp.cumsum(padded)
        offs = ends - padded
        n_used = (ends[-1:] // EXP_TILE).astype(I32)
        tile_start = jnp.arange(n_tiles, dtype=I32) * EXP_TILE
        tile_e = jnp.minimum(jnp.sum((ends[None, :] <= tile_start[:, None]).astype(I32), axis=1), N_EXPERTS - 1)
        eid = code[0:2] >> RANK_BITS
        rank = code[0:2] & ((1 << RANK_BITS) - 1)
        expert_ids = jnp.arange(N_EXPERTS, dtype=I32)[:, None, None]
        slots = (rank + jnp.sum(jnp.where(eid[None] == expert_ids, offs[:, None, None], 0), axis=0)).reshape(-1)

        fill = (jnp.any(tile_start[:, None] + EXP_TILE == ends[None, :], axis=1) | (tile_start >= ends[-1])).astype(I32)
        xs = _dispatch(slots, fill, h2, None if l == 0 else xs, n_slots)
        ys = _experts(tile_e, n_used, xs, w_exp_gate, w_exp_up, w_exp_down, l)
        x2d = _combine(slots, x1, wcol, final_norm_g[None, :], ys, final=(l == depth - 1))

    return x2d.reshape(batch, seq, D_MODEL)
```

```python
import functools

import jax
import jax.numpy as jnp
import numpy as np
from jax import lax
from jax.experimental import pallas as pl
from jax.experimental.pallas import tpu as pltpu

F32 = jnp.float32
BF16 = jnp.bfloat16
I32 = jnp.int32

D_MODEL = 1024
HEAD_DIM = 64
HALF_HEAD = HEAD_DIM // 2
ROPE_THETA = 10000.0
NORM_EPS = 1e-6
NEG_INF = -1e30
LANES = 128

A_Q_HEADS = 8
A_KV_HEADS = 2
A_GROUP = A_Q_HEADS // A_KV_HEADS
A_HALF_WINDOW = 128
A_Q_DIM = A_Q_HEADS * HEAD_DIM
A_KV_DIM = A_KV_HEADS * HEAD_DIM

B_GROUPS = ((128, 1), (512, 4), (2048, 16))
B_HEADS = 4
B_DIM = B_HEADS * HEAD_DIM
B_HALF_WINDOW = 64

MOE_GROUPS = 4
EXPERTS_PER_GROUP = 8
N_EXPERTS = MOE_GROUPS * EXPERTS_PER_GROUP
D_EXPERT = 256

NAT_IN_COLS = A_Q_DIM + 2 * A_KV_DIM + 3 * B_DIM
NAT_B = A_Q_DIM
NAT_AK = NAT_B + 3 * B_DIM
NAT_AV = NAT_AK + 2 * A_KV_DIM
NAT_COLS = NAT_AV + A_KV_DIM
GRP_COLS = 3 * B_DIM
COL_G1 = NAT_IN_COLS
COL_G2 = COL_G1 + GRP_COLS
COL_GATE = COL_G2 + GRP_COLS
GATE_COLS = 2 * D_MODEL
D_IN = COL_GATE + GATE_COLS

TOK_TILE = 512
IN_TILE = 512
W_CHUNK = 512
A_Q_TILE = 4096
B_SUB = 128
T_ROWS = 512
B_Q_TILE = 4096
EXP_TILE = 512
XS_SLOTS = 3
ROW_CHUNKS = D_MODEL // LANES
ROW_TILE = 512
DISPATCH_TILE = 2048
DMA_UNROLL = 16
RANK_BITS = 16
VMEM_LIMIT = 56 * 1024 * 1024


def _cparams(*sem):
    return pltpu.CompilerParams(dimension_semantics=sem, vmem_limit_bytes=VMEM_LIMIT)


def _rope_tables(seq_len):
    inv = (1.0 / (np.float32(ROPE_THETA) ** (np.arange(0, HEAD_DIM, 2, dtype=np.float32) / np.float32(HEAD_DIM))))
    ang = np.arange(seq_len, dtype=np.float32)[:, None] * inv.astype(np.float32)[None, :]
    cos, sin = np.cos(ang).astype(np.float32), np.sin(ang).astype(np.float32)
    cos_t = np.concatenate([cos, cos, cos, cos], axis=-1)
    sin_t = np.concatenate([-sin, sin, -sin, sin], axis=-1)
    return cos_t, sin_t


def _residue_order(table, dilation, tile):
    s, c = table.shape
    return table.reshape(s // tile, tile // dilation, dilation, c).transpose(0, 2, 1, 3).reshape(s, c)


def _rope(t, cos, sin_signed, first_half):
    partner = jnp.where(first_half, pltpu.roll(t, LANES - HALF_HEAD, 1), pltpu.roll(t, HALF_HEAD, 1))
    return t * cos + partner * sin_signed


Q_KIND, K_KIND, V_KIND = 0, 1, 2
_NAT_KINDS = ([Q_KIND] * 4 + [K_KIND] + [V_KIND] + [Q_KIND] * 2 + [K_KIND] * 2 + [V_KIND] * 2)
_GRP_KINDS = [Q_KIND] * 2 + [K_KIND] * 2 + [V_KIND] * 2


def _in_proj_kernel(x_ref, g_ref, w_hbm_ref, c1_ref, s1_ref, c4_ref, s4_ref, c16_ref, s16_ref,
                    nat_ref, gate_ref, g1_ref, g2_ref, hf_ref, h4_ref, hb_ref, hd2_ref, w_ref, stage_ref, sem,
                    *, layer):
    tm = x_ref.shape[0]

    @pl.when(pl.program_id(0) == 0)
    def _():
        def chunk(j):
            cols = pl.ds(j * W_CHUNK, W_CHUNK)
            return pltpu.make_async_copy(w_hbm_ref.at[layer, :, cols], stage_ref.at[j % 2], sem.at[j % 2])

        n = D_IN // W_CHUNK
        chunk(0).start()
        for j in range(n):
            if j + 1 < n:
                chunk(j + 1).start()
            chunk(j).wait()
            w_ref[:, j * W_CHUNK:(j + 1) * W_CHUNK] = stage_ref[j % 2].astype(BF16)

    x = x_ref[...]
    h = x * lax.rsqrt(jnp.mean(x * x, axis=-1, keepdims=True) + NORM_EPS) * g_ref[...]
    n_chunks = D_MODEL // LANES
    for c in range(n_chunks):
        hf_ref[c] = h[:, c * LANES:(c + 1) * LANES]
    hb_ref[...] = h.astype(BF16)
    lane = lax.broadcasted_iota(I32, (1, LANES), 1)
    first_half = (lane % HEAD_DIM) < HALF_HEAD

    def project(h_b, col0, kinds, cos_ref, sin_ref, store):
        width = W_CHUNK
        for c0 in range(0, len(kinds) * LANES, width):
            w = min(width, len(kinds) * LANES - c0)
            res = jnp.dot(h_b, w_ref[:, col0 + c0:col0 + c0 + w], preferred_element_type=F32)
            for j in range(w // LANES):
                kind = kinds[(c0 // LANES) + j]
                t = res[:, j * LANES:(j + 1) * LANES]
                if kind != V_KIND:
                    t = _rope(t, cos_ref[...], sin_ref[...], first_half)
                if kind == Q_KIND:
                    t = t * (HEAD_DIM ** -0.5)
                store(c0 + j * LANES, t)

    low_head = lane < HEAD_DIM

    def store_nat(c, t):
        if A_Q_DIM <= c < A_Q_DIM + A_KV_DIM:
            swapped = pltpu.roll(t, HEAD_DIM, 1)
            nat_ref[:, NAT_AK:NAT_AK + LANES] = jnp.where(low_head, t, swapped).astype(BF16)
            nat_ref[:, NAT_AK + LANES:NAT_AK + 2 * LANES] = jnp.where(low_head, swapped, t).astype(BF16)
        elif c < A_Q_DIM + 2 * A_KV_DIM:
            out = c if c < A_Q_DIM else NAT_AV
            nat_ref[:, out:out + LANES] = t.astype(BF16)
        else:
            out = c - 2 * A_KV_DIM
            nat_ref[:, out:out + LANES] = t.astype(BF16)

    project(hb_ref[...], 0, _NAT_KINDS, c1_ref, s1_ref, store_nat)

    for c0 in range(0, GATE_COLS, W_CHUNK):
        res = jnp.dot(hb_ref[...], w_ref[:, COL_GATE + c0:COL_GATE + c0 + W_CHUNK], preferred_element_type=F32)
        gate_ref[:, c0:c0 + W_CHUNK] = jax.nn.sigmoid(res).astype(BF16)

    for slot, (dil, col0, cos_ref, sin_ref, out_ref) in enumerate(((4, COL_G1, c4_ref, s4_ref, g1_ref),
                                                                   (16, COL_G2, c16_ref, s16_ref, g2_ref))):
        n = tm // dil
        hd_ref = hd2_ref.at[slot]
        n4 = tm // 4
        for r in range(dil):
            for c in range(n_chunks):
                cs = slice(c * LANES, (c + 1) * LANES)
                if dil == 4:
                    rows4 = hf_ref[c, pl.ds(r, n, stride=4), :]
                    h4_ref[c, r * n:(r + 1) * n, :] = rows4
                    hd_ref[r * n:(r + 1) * n, cs] = rows4.astype(BF16)
                else:
                    r4, q = r % 4, r // 4
                    hd_ref[r * n:(r + 1) * n, cs] = h4_ref[c, pl.ds(r4 * n4 + q, n, stride=4), :].astype(BF16)

        def store_grp(c, t, out_ref=out_ref, dil=dil, n=n):
            v = t.astype(BF16)
            for r in range(dil):
                out_ref[r, :, c:c + LANES] = v[r * n:(r + 1) * n]

        project(hd_ref[...], col0, _GRP_KINDS, cos_ref, sin_ref, store_grp)


def _in_proj(x2d, g, w_in, layer, tables, batch, seq):
    tm = IN_TILE
    tiles_per_seq = seq // tm
    n_tok = batch * seq
    c1, s1, c4, s4, c16, s16 = tables
    tab_spec = pl.BlockSpec((tm, LANES), lambda i: (i % tiles_per_seq, 0))
    return pl.pallas_call(
        functools.partial(_in_proj_kernel, layer=layer),
        grid=(n_tok // tm,),
        in_specs=[
            pl.BlockSpec((tm, D_MODEL), lambda i: (i, 0)),
            pl.BlockSpec((1, D_MODEL), lambda i: (0, 0)),
            pl.BlockSpec(memory_space=pl.ANY),
            tab_spec, tab_spec, tab_spec, tab_spec, tab_spec, tab_spec,
        ],
        out_specs=[
            pl.BlockSpec((tm, NAT_COLS), lambda i: (i, 0)),
            pl.BlockSpec((tm, GATE_COLS), lambda i: (i, 0)),
            pl.BlockSpec((None, 4, tm // 4, GRP_COLS), lambda i: (i // tiles_per_seq, 0, i % tiles_per_seq, 0)),
            pl.BlockSpec((None, 16, tm // 16, GRP_COLS), lambda i: (i // tiles_per_seq, 0, i % tiles_per_seq, 0)),
        ],
        out_shape=[
            jax.ShapeDtypeStruct((n_tok, NAT_COLS), BF16),
            jax.ShapeDtypeStruct((n_tok, GATE_COLS), BF16),
            jax.ShapeDtypeStruct((batch, 4, seq // 4, GRP_COLS), BF16),
            jax.ShapeDtypeStruct((batch, 16, seq // 16, GRP_COLS), BF16),
        ],
        scratch_shapes=[
            pltpu.VMEM((D_MODEL // LANES, tm, LANES), F32),
            pltpu.VMEM((D_MODEL // LANES, tm, LANES), F32),
            pltpu.VMEM((tm, D_MODEL), BF16),
            pltpu.VMEM((2, tm, D_MODEL), BF16),
            pltpu.VMEM((D_MODEL, D_IN), BF16),
            pltpu.VMEM((2, D_MODEL, W_CHUNK), F32),
            pltpu.SemaphoreType.DMA((2,)),
        ],
        compiler_params=_cparams("arbitrary"),
        name="in_proj",
    )(x2d, g, w_in, c1, s1, c4, s4, c16, s16)


def _store_transposed(vt_ref, blocks):
    for r0, src in blocks:
        n = src.shape[0]
        for j in range(0, n, T_ROWS):
            m = min(T_ROWS, n - j)
            vt_ref[:, r0 + j:r0 + j + m] = src[j:j + m, :].astype(F32).T.astype(BF16)


def _masked_heads(q_pair, low_head, high_head):
    zero = jnp.zeros_like(q_pair)
    return [jnp.where(low_head, q_pair, zero), jnp.where(high_head, q_pair, zero)]


def _attn_a_kernel(sink_ref, q_ref, kp_ref, km_ref, kn_ref, vp_ref, vm_ref, vn_ref, o_ref, k_ref, vt_ref, *, seq):
    tq = q_ref.shape[0]
    hw = A_HALF_WINDOW
    n_sub = tq // hw
    i = pl.program_id(1)
    last_blk = seq // hw - 1
    k_ref[0:hw, :] = kp_ref[...]
    k_ref[hw:hw + tq, :] = km_ref[...]
    k_ref[hw + tq:tq + 2 * hw, :] = kn_ref[...]
    _store_transposed(vt_ref, ((0, vp_ref), (hw, vm_ref), (hw + tq, vn_ref)))
    grp_cols = A_GROUP * hw
    key = lax.broadcasted_iota(I32, (hw, grp_cols), 0)
    qry = lax.broadcasted_iota(I32, (hw, grp_cols), 1) % hw
    low_head = lax.broadcasted_iota(I32, (1, LANES), 1) < HEAD_DIM
    high_head = jnp.logical_not(low_head)
    for sb in range(n_sub):
        r0 = sb * hw
        blk = i * n_sub + sb
        mask_p = (key >= qry) if sb > 0 else (key >= qry + jnp.where(blk > 0, 0, hw))
        mask_n = (key <= qry) if sb < n_sub - 1 else (key <= qry - jnp.where(blk < last_blk, 0, hw))
        out_t = []
        for g in range(A_KV_HEADS):
            heads = range(g * A_GROUP, (g + 1) * A_GROUP)
            q_parts = []
            for c in range(g * A_GROUP // 2, (g + 1) * A_GROUP // 2):
                q_parts += _masked_heads(q_ref[r0:r0 + hw, c * LANES:(c + 1) * LANES], low_head, high_head)
            q = jnp.concatenate(q_parts, axis=0)
            sink = jnp.concatenate([jnp.full((1, hw), sink_ref[h], F32) for h in heads], axis=1)
            k = k_ref[r0:r0 + 3 * hw, g * LANES:(g + 1) * LANES]
            s = lax.dot_general(k, q, (((1,), (1,)), ((), ())), preferred_element_type=F32)
            sp = jnp.where(mask_p, s[0:hw], NEG_INF)
            so = s[hw:2 * hw]
            sn = jnp.where(mask_n, s[2 * hw:3 * hw], NEG_INF)
            m = jnp.max(jnp.maximum(jnp.maximum(sp, so), sn), axis=0, keepdims=True)
            m = jnp.maximum(m, sink)
            pp, po, pn = jnp.exp(sp - m), jnp.exp(so - m), jnp.exp(sn - m)
            denom = jnp.sum(pp + po + pn, axis=0, keepdims=True) + jnp.exp(sink - m)
            p = jnp.concatenate([pp, po, pn], axis=0).astype(BF16)
            vt = vt_ref[g * HEAD_DIM:(g + 1) * HEAD_DIM, r0:r0 + 3 * hw]
            o = jnp.dot(vt, p, preferred_element_type=F32) * (1.0 / denom)
            out_t += [o[:, j * hw:(j + 1) * hw] for j in range(A_GROUP)]
        o_ref[r0:r0 + hw, :] = jnp.concatenate(out_t, axis=0).T.astype(BF16)


def _attn_a(nat3d, sink):
    batch, seq, _ = nat3d.shape
    tq = min(A_Q_TILE, seq)
    hw = A_HALF_WINDOW
    per = tq // hw
    n_hw = seq // hw
    k_cols, v_cols = 2 * A_KV_DIM, A_KV_DIM
    k_blk, v_blk = NAT_AK // k_cols, NAT_AV // v_cols

    def prev_spec(cols, blk):
        return pl.BlockSpec((None, hw, cols), lambda b, i: (b, jnp.maximum(i * per - 1, 0), blk))

    def main_spec(cols, blk):
        return pl.BlockSpec((None, tq, cols), lambda b, i: (b, i, blk))

    def next_spec(cols, blk):
        return pl.BlockSpec((None, hw, cols), lambda b, i: (b, jnp.minimum((i + 1) * per, n_hw - 1), blk))

    return pl.pallas_call(
        functools.partial(_attn_a_kernel, seq=seq),
        grid=(batch, seq // tq),
        in_specs=[
            pl.BlockSpec(memory_space=pltpu.SMEM),
            main_spec(A_Q_DIM, 0),
            prev_spec(k_cols, k_blk), main_spec(k_cols, k_blk), next_spec(k_cols, k_blk),
            prev_spec(v_cols, v_blk), main_spec(v_cols, v_blk), next_spec(v_cols, v_blk),
        ],
        out_specs=pl.BlockSpec((None, tq, A_Q_DIM), lambda b, i: (b, i, 0)),
        out_shape=jax.ShapeDtypeStruct((batch, seq, A_Q_DIM), BF16),
        scratch_shapes=[pltpu.VMEM((tq + 2 * hw, k_cols), BF16), pltpu.VMEM((v_cols, tq + 2 * hw), BF16)],
        compiler_params=_cparams("parallel", "parallel"),
        name="attn_a",
    )(sink, nat3d, nat3d, nat3d, nat3d, nat3d, nat3d, nat3d)


def _attn_b_kernel(q_ref, kp_ref, km_ref, kn_ref, vp_ref, vm_ref, vn_ref, o_ref, lse_ref,
                   k_ref, vt_ref, *, sub_len):
    for s in range(q_ref.shape[0]):
        _attn_b_one(q_ref.at[s], kp_ref.at[s], km_ref.at[s], kn_ref.at[s], vp_ref.at[s], vm_ref.at[s], vn_ref.at[s],
                    o_ref.at[s], lse_ref.at[s], k_ref.at[s], vt_ref.at[s], sub_len)


def _attn_b_one(q_ref, kp_ref, km_ref, kn_ref, vp_ref, vm_ref, vn_ref, o_ref, lse_ref, k_ref, vt_ref, sub_len):
    tq = q_ref.shape[0]
    hw = B_HALF_WINDOW
    t0 = pl.program_id(1) * tq
    k_ref[0:hw, :] = kp_ref[...]
    k_ref[hw:hw + tq, :] = km_ref[...]
    k_ref[hw + tq:tq + 2 * hw, :] = kn_ref[...]
    _store_transposed(vt_ref, ((0, vp_ref), (hw, vm_ref), (hw + tq, vn_ref)))
    kw = B_SUB + 2 * hw
    n_sub = tq // B_SUB
    all_cols = B_HEADS * B_SUB
    key = lax.broadcasted_iota(I32, (kw, all_cols), 0)
    qry = lax.broadcasted_iota(I32, (kw, all_cols), 1) % B_SUB
    in_band = jnp.abs(key - hw - qry) <= hw
    low_head = lax.broadcasted_iota(I32, (1, LANES), 1) < HEAD_DIM
    high_head = jnp.logical_not(low_head)
    for sb in range(n_sub):
        r0 = sb * B_SUB
        valid = in_band
        if sb == 0:
            valid = valid & (t0 - hw + key >= 0)
        if sb == n_sub - 1:
            valid = valid & (t0 + r0 - hw + key < sub_len)
        s_parts = []
        for c in range(B_HEADS // 2):
            cs = slice(c * LANES, (c + 1) * LANES)
            q = jnp.concatenate(_masked_heads(q_ref[r0:r0 + B_SUB, cs], low_head, high_head), axis=0)
            s_parts.append(lax.dot_general(k_ref[r0:r0 + kw, cs], q, (((1,), (1,)), ((), ())),
                                           preferred_element_type=F32))
        s = jnp.where(valid, jnp.concatenate(s_parts, axis=1), NEG_INF)
        m = jnp.max(s, axis=0, keepdims=True)
        p = jnp.exp(s - m)
        denom = jnp.sum(p, axis=0, keepdims=True)
        p = p.astype(BF16)
        inv = 1.0 / denom
        lse = m + jnp.log(denom)
        out_t, lse_t = [], []
        for h in range(B_HEADS):
            qs = slice(h * B_SUB, (h + 1) * B_SUB)
            vt = vt_ref[h * HEAD_DIM:(h + 1) * HEAD_DIM, r0:r0 + kw]
            out_t.append(jnp.dot(vt, p[:, qs], preferred_element_type=F32) * inv[:, qs])
            lse_t.append(jnp.broadcast_to(lse[:, qs], (HEAD_DIM, B_SUB)))
        o_ref[r0:r0 + B_SUB, :] = jnp.concatenate(out_t, axis=0).T
        lse_ref[r0:r0 + B_SUB, :] = jnp.concatenate(lse_t, axis=0).T


def _attn_b(arr3d, q_blk, name):
    n_sub, sub_len, _ = arr3d.shape
    tq = min(B_Q_TILE, sub_len)
    n_per = min(B_Q_TILE // tq, n_sub)
    assert n_sub % n_per == 0
    hw = B_HALF_WINDOW
    per = tq // hw
    n_hw = sub_len // hw

    def main_spec(c):
        return pl.BlockSpec((n_per, tq, B_DIM), lambda g, i: (g, i, c))

    def prev_spec(c):
        return pl.BlockSpec((n_per, hw, B_DIM), lambda g, i: (g, jnp.maximum(i * per - 1, 0), c))

    def next_spec(c):
        return pl.BlockSpec((n_per, hw, B_DIM), lambda g, i: (g, jnp.minimum((i + 1) * per, n_hw - 1), c))

    out_spec = pl.BlockSpec((n_per, tq, B_DIM), lambda g, i: (g, i, 0))
    return pl.pallas_call(
        functools.partial(_attn_b_kernel, sub_len=sub_len),
        grid=(n_sub // n_per, sub_len // tq),
        in_specs=[main_spec(q_blk),
                  prev_spec(q_blk + 1), main_spec(q_blk + 1), next_spec(q_blk + 1),
                  prev_spec(q_blk + 2), main_spec(q_blk + 2), next_spec(q_blk + 2)],
        out_specs=[out_spec, out_spec],
        out_shape=[jax.ShapeDtypeStruct((n_sub, sub_len, B_DIM), F32)] * 2,
        scratch_shapes=[pltpu.VMEM((n_per, tq + 2 * hw, B_DIM), BF16),
                        pltpu.VMEM((n_per, B_DIM, tq + 2 * hw), BF16)],
        compiler_params=_cparams("parallel", "parallel"),
        name=name,
    )(arr3d, arr3d, arr3d, arr3d, arr3d, arr3d, arr3d)


def _merge_kernel(x_ref, ya_ref, o0_ref, l0_ref, o1_ref, l1_ref, o2_ref, l2_ref, gate_ref,
                  wa_ref, wb_ref, wo_ref, g2_ref, wr_ref, br_ref,
                  x1_ref, h2_ref, code_ref, wcol_ref, cnt_ref,
                  so1_ref, sl1_ref, so2_ref, sl2_ref, yb_ref, run_ref, earlier_ref,
                  wab_ref, wbb_ref, wob_ref):
    tm = x_ref.shape[0]

    @pl.when(pl.program_id(0) == 0)
    def _():
        run_ref[...] = jnp.zeros_like(run_ref)
        wab_ref[...] = wa_ref[...].astype(BF16)
        wbb_ref[...] = wb_ref[...].astype(BF16)
        wob_ref[...] = wo_ref[...].astype(BF16)
        row = lax.broadcasted_iota(I32, (tm, tm), 0)
        col = lax.broadcasted_iota(I32, (tm, tm), 1)
        earlier_ref[...] = (row < col).astype(BF16)

    for dil, src_o, src_l, dst_o, dst_l in ((4, o1_ref, l1_ref, so1_ref, sl1_ref),
                                            (16, o2_ref, l2_ref, so2_ref, sl2_ref)):
        n = tm // dil
        for r in range(dil):
            for c in range(B_DIM // LANES):
                cs = slice(c * LANES, (c + 1) * LANES)
                dst_o[c, pl.ds(r, n, stride=dil), :] = src_o[r, :, cs]
                dst_l[c, pl.ds(r, n, stride=dil), :] = src_l[r, :, cs]

    for c in range(B_DIM // LANES):
        cs = slice(c * LANES, (c + 1) * LANES)
        l0, l1, l2 = l0_ref[:, cs], sl1_ref[c], sl2_ref[c]
        m = jnp.maximum(jnp.maximum(l0, l1), l2)
        e0, e1, e2 = jnp.exp(l0 - m), jnp.exp(l1 - m), jnp.exp(l2 - m)
        yb = (e0 * o0_ref[:, cs] + e1 * so1_ref[c] + e2 * so2_ref[c]) / (e0 + e1 + e2)
        yb_ref[:, cs] = yb.astype(BF16)

    ya_p = jnp.dot(ya_ref[...], wab_ref[...], preferred_element_type=F32)
    yb_p = jnp.dot(yb_ref[...], wbb_ref[...], preferred_element_type=F32)
    merged = gate_ref[:, 0:D_MODEL].astype(F32) * ya_p + gate_ref[:, D_MODEL:GATE_COLS].astype(F32) * yb_p
    x1 = x_ref[...] + jnp.dot(merged.astype(BF16), wob_ref[...], preferred_element_type=F32)
    x1_ref[...] = x1

    h2 = x1 * lax.rsqrt(jnp.mean(x1 * x1, axis=-1, keepdims=True) + NORM_EPS) * g2_ref[...]
    _store_row_tiles(h2_ref, h2)

    h_hi = h2.astype(BF16)
    h_lo = (h2 - h_hi.astype(F32)).astype(BF16)
    hi_terms = jnp.dot(h_hi, wr_ref[...], preferred_element_type=F32)
    logits = (hi_terms[:, 0:LANES] + hi_terms[:, LANES:2 * LANES]
              + jnp.dot(h_lo, wr_ref[:, 0:LANES], preferred_element_type=F32)) + br_ref[...]

    logits_t = logits.T
    sub = lax.broadcasted_iota(I32, (EXPERTS_PER_GROUP, tm), 0).astype(F32)
    none = float(EXPERTS_PER_GROUP)
    gl = jnp.where(sub < MOE_GROUPS, logits_t[N_EXPERTS:N_EXPERTS + EXPERTS_PER_GROUP], -jnp.inf)
    gmax = jnp.max(gl, axis=0, keepdims=True)
    gidx = jnp.min(jnp.where(gl == gmax, sub, none), axis=0, keepdims=True)
    gw = 1.0 / jnp.sum(jnp.exp(gl - gmax), axis=0, keepdims=True)
    el = logits_t[0:EXPERTS_PER_GROUP]
    for g in range(1, MOE_GROUPS):
        el = jnp.where(gidx == g, logits_t[g * EXPERTS_PER_GROUP:(g + 1) * EXPERTS_PER_GROUP], el)
    v1 = jnp.max(el, axis=0, keepdims=True)
    i1 = jnp.min(jnp.where(el == v1, sub, none), axis=0, keepdims=True)
    el2 = jnp.where(sub == i1, -jnp.inf, el)
    v2 = jnp.max(el2, axis=0, keepdims=True)
    i2 = jnp.min(jnp.where(el2 == v2, sub, none), axis=0, keepdims=True)
    t = jnp.exp(v2 - v1)
    w1 = gw / (1.0 + t)
    w2 = gw * t / (1.0 + t)
    e1 = gidx * EXPERTS_PER_GROUP + i1
    e2 = gidx * EXPERTS_PER_GROUP + i2

    expert = lax.broadcasted_iota(I32, (N_EXPERTS, tm), 0).astype(F32)
    oh1 = (expert == e1).astype(F32)
    oh2 = (expert == e2).astype(F32)
    oh = oh1 + oh2
    run = run_ref[...]
    before = (jnp.dot(oh.astype(BF16), earlier_ref[...], preferred_element_type=F32)
              + jnp.concatenate([run] * (tm // LANES), axis=1))
    rank1 = jnp.sum(before * oh1, axis=0, keepdims=True)
    rank2 = jnp.sum(before * oh2, axis=0, keepdims=True)
    run_ref[...] = run + jnp.sum(oh, axis=1, keepdims=True)
    cnt_ref[...] = run_ref[...]

    scale = float(1 << RANK_BITS)
    zeros = jnp.zeros((6, tm), F32)
    code_ref[...] = jnp.concatenate([e1 * scale + rank1, e2 * scale + rank2, zeros], axis=0).astype(I32)
    w_rows = jnp.concatenate([w1, w2, jnp.zeros((LANES - 2, tm), F32)], axis=0)
    wcol_ref[...] = w_rows.T


def _merge(x2d, ya, o0, l0, o1, l1, o2, l2, gates, wa, wb, wo, g2, wr, br, batch, seq):
    tm = TOK_TILE
    tps = seq // tm
    n_tok = batch * seq

    def tok(c):
        return pl.BlockSpec((tm, c), lambda i: (i, 0))

    def full(a):
        return pl.BlockSpec(a.shape, lambda i: (0,) * a.ndim)

    def res_spec(d):
        return pl.BlockSpec((None, d, tm // d, B_DIM), lambda i: (i // tps, 0, i % tps, 0))

    return pl.pallas_call(
        _merge_kernel,
        grid=(n_tok // tm,),
        in_specs=[tok(D_MODEL), tok(A_Q_DIM), tok(B_DIM), tok(B_DIM),
                  res_spec(4), res_spec(4), res_spec(16), res_spec(16), tok(GATE_COLS),
                  full(wa), full(wb), full(wo), full(g2), full(wr), full(br)],
        out_specs=[tok(D_MODEL), pl.BlockSpec((tm * ROW_CHUNKS, LANES), lambda i: (i, 0)),
                   pl.BlockSpec((8, tm), lambda i: (0, i)),
                   tok(LANES),
                   pl.BlockSpec((N_EXPERTS, LANES), lambda i: (0, 0))],
        out_shape=[jax.ShapeDtypeStruct((n_tok, D_MODEL), F32),
                   jax.ShapeDtypeStruct((n_tok * ROW_CHUNKS, LANES), F32),
                   jax.ShapeDtypeStruct((8, n_tok), I32),
                   jax.ShapeDtypeStruct((n_tok, LANES), F32),
                   jax.ShapeDtypeStruct((N_EXPERTS, LANES), F32)],
        scratch_shapes=([pltpu.VMEM((B_DIM // LANES, tm, LANES), F32)] * 4
                        + [pltpu.VMEM((tm, B_DIM), BF16), pltpu.VMEM((N_EXPERTS, LANES), F32),
                           pltpu.VMEM((tm, tm), BF16),
                           pltpu.VMEM(wa.shape, BF16), pltpu.VMEM(wb.shape, BF16), pltpu.VMEM(wo.shape, BF16)]),
        compiler_params=_cparams("arbitrary"),
        name="merge_route",
    )(x2d, ya, o0, l0, o1, l1, o2, l2, gates, wa, wb, wo, g2, wr, br)


def _row_tile(ref, t):
    return ref.at[pl.ds(pl.multiple_of(t * ROW_CHUNKS, ROW_CHUNKS), ROW_CHUNKS)]


def _store_row_tiles(ref, val):
    rows = val.shape[0]
    for c in range(ROW_CHUNKS):
        ref[pl.ds(c, rows, stride=ROW_CHUNKS), :] = val[:, c * LANES:(c + 1) * LANES]


def _load_row_tiles_chunk(ref, c):
    return ref[pl.ds(c, ref.shape[0] // ROW_CHUNKS, stride=ROW_CHUNKS), :]


def _dispatch_kernel(slot_ref, fill_ref, h_ref, *rest, n_tok, reuse):
    if reuse:
        _, xs_ref, sem = rest
    else:
        xs_ref, zero_ref, sem, zsem = rest
    i = pl.program_id(0)
    rows = h_ref.shape[0] // ROW_CHUNKS

    if not reuse:
        @pl.when(i == 0)
        def _():
            tile_rows = zero_ref.shape[0]
            zero_ref[...] = jnp.zeros_like(zero_ref)

            def zero_copy(t):
                dst = xs_ref.at[pl.ds(pl.multiple_of(t * tile_rows, tile_rows), tile_rows)]
                return pltpu.make_async_copy(zero_ref, dst, zsem)

            def start(t, carry):
                @pl.when(fill_ref[t] != 0)
                def _():
                    zero_copy(t).start()
                return carry

            def wait(t, carry):
                @pl.when(fill_ref[t] != 0)
                def _():
                    zero_copy(t).wait()
                return carry

            n_tiles = xs_ref.shape[0] // tile_rows
            lax.fori_loop(0, n_tiles, start, 0)
            lax.fori_loop(0, n_tiles, wait, 0)

    def issue(j, carry):
        t = i * rows + j
        for k in range(2):
            pltpu.make_async_copy(_row_tile(h_ref, j), _row_tile(xs_ref, slot_ref[k * n_tok + t]), sem).start(priority=k)
        return carry

    lax.fori_loop(0, rows, issue, 0, unroll=DMA_UNROLL)
    for _ in range(2):
        pltpu.make_async_copy(h_ref, xs_ref.at[pl.ds(0, rows * ROW_CHUNKS)], sem).wait()


def _dispatch(slots, fill, h2, xs_prev, n_slots):
    n_tok = h2.shape[0] // ROW_CHUNKS
    rows = DISPATCH_TILE
    reuse = xs_prev is not None
    h_spec = pl.BlockSpec((rows * ROW_CHUNKS, LANES), lambda i, s, f: (i, 0))
    any_spec = pl.BlockSpec(memory_space=pl.ANY)
    scratch = [pltpu.SemaphoreType.DMA(())] if reuse else [
        pltpu.VMEM((EXP_TILE * ROW_CHUNKS, LANES), F32), pltpu.SemaphoreType.DMA(()), pltpu.SemaphoreType.DMA(())]
    return pl.pallas_call(
        functools.partial(_dispatch_kernel, n_tok=n_tok, reuse=reuse),
        grid_spec=pltpu.PrefetchScalarGridSpec(
            num_scalar_prefetch=2,
            grid=(n_tok // rows,),
            in_specs=[h_spec, any_spec] if reuse else [h_spec],
            out_specs=any_spec,
            scratch_shapes=scratch,
        ),
        out_shape=jax.ShapeDtypeStruct((n_slots * ROW_CHUNKS, LANES), F32),
        input_output_aliases={3: 0} if reuse else {},
        compiler_params=_cparams("arbitrary"),
        name="dispatch_reuse" if reuse else "dispatch",
    )(*((slots, fill, h2, xs_prev) if reuse else (slots, fill, h2)))


def _combine_kernel(slot_ref, x_ref, w_ref, g_ref, ys_ref, o_ref, buf_ref, y_ref, sem, *, n_tok, final):
    i = pl.program_id(0)
    n_steps = pl.num_programs(0)
    rows = x_ref.shape[0]

    def issue_tile(tile, slot):
        def issue(j, carry):
            t = tile * rows + j
            for k in range(2):
                pltpu.make_async_copy(_row_tile(ys_ref, slot_ref[k * n_tok + t]),
                                      _row_tile(buf_ref.at[slot, k], j), sem.at[slot]).start(priority=k)
            return carry

        lax.fori_loop(0, rows, issue, 0, unroll=DMA_UNROLL)

    @pl.when(i == 0)
    def _():
        issue_tile(0, 0)

    @pl.when(i + 1 < n_steps)
    def _():
        issue_tile(i + 1, (i + 1) % 2)

    cur = i % 2
    for k in range(2):
        pltpu.make_async_copy(ys_ref.at[pl.ds(0, rows * ROW_CHUNKS)], buf_ref.at[cur, k], sem.at[cur]).wait()
    w = w_ref[...]
    w1, w2 = w[:, 0:1], w[:, 1:2]
    for c in range(ROW_CHUNKS):
        cs = slice(c * LANES, (c + 1) * LANES)
        y1 = _load_row_tiles_chunk(buf_ref.at[cur, 0], c)
        y2 = _load_row_tiles_chunk(buf_ref.at[cur, 1], c)
        (y_ref if final else o_ref)[:, cs] = x_ref[:, cs] + w1 * y1 + w2 * y2
    if final:
        y = y_ref[...]
        o_ref[...] = y * lax.rsqrt(jnp.mean(y * y, axis=-1, keepdims=True) + NORM_EPS) * g_ref[...]


def _combine(slots, x1, wcol, g, ys, final):
    n_tok = x1.shape[0]
    rows = ROW_TILE
    return pl.pallas_call(
        functools.partial(_combine_kernel, n_tok=n_tok, final=final),
        grid_spec=pltpu.PrefetchScalarGridSpec(
            num_scalar_prefetch=1,
            grid=(n_tok // rows,),
            in_specs=[pl.BlockSpec((rows, D_MODEL), lambda i, s: (i, 0)),
                      pl.BlockSpec((rows, LANES), lambda i, s: (i, 0)),
                      pl.BlockSpec((1, D_MODEL), lambda i, s: (0, 0)),
                      pl.BlockSpec(memory_space=pl.ANY)],
            out_specs=pl.BlockSpec((rows, D_MODEL), lambda i, s: (i, 0)),
            scratch_shapes=[pltpu.VMEM((2, 2, rows * ROW_CHUNKS, LANES), F32),
                            pltpu.VMEM((rows, D_MODEL), F32),
                            pltpu.SemaphoreType.DMA((2,))],
        ),
        out_shape=jax.ShapeDtypeStruct((n_tok, D_MODEL), F32),
        compiler_params=_cparams("arbitrary"),
        name="combine",
    )(slots, x1, wcol, g, ys)


def _experts_kernel(tile_e_ref, n_used_ref, xs_ref, wg_ref, wu_ref, wd_ref, ys_ref, xbuf_ref, x_ref, sem):
    del tile_e_ref
    i = pl.program_id(0)
    n_used = n_used_ref[0]
    tile_rows = xbuf_ref.shape[1]

    def fetch(tile):
        rows = pl.ds(pl.multiple_of(tile * tile_rows, tile_rows), tile_rows)
        slot = tile % XS_SLOTS
        return pltpu.make_async_copy(xs_ref.at[rows], xbuf_ref.at[slot], sem.at[slot])

    @pl.when(i == 0)
    def _():
        fetch(0).start()

        @pl.when(n_used > 1)
        def _():
            fetch(1).start()

    @pl.when(i < n_used)
    def _():
        @pl.when(i + XS_SLOTS - 1 < n_used)
        def _():
            fetch(i + XS_SLOTS - 1).start()

        fetch(i).wait()
        xb = xbuf_ref.at[i % XS_SLOTS]
        for c in range(ROW_CHUNKS):
            x_ref[:, c * LANES:(c + 1) * LANES] = _load_row_tiles_chunk(xb, c).astype(BF16)
        x = x_ref[...]
        a = jnp.dot(x, wg_ref[...].astype(BF16), preferred_element_type=F32)
        u = jnp.dot(x, wu_ref[...].astype(BF16), preferred_element_type=F32)
        z = (a * jax.nn.sigmoid(a)) * u
        _store_row_tiles(ys_ref, jnp.dot(z.astype(BF16), wd_ref[...].astype(BF16), preferred_element_type=F32))

    @pl.when(i >= n_used)
    def _():
        ys_ref[...] = jnp.zeros_like(ys_ref)


def _experts(tile_e, n_used, xs, wg, wu, wd, layer):
    n_slots = xs.shape[0] // ROW_CHUNKS
    te = EXP_TILE

    def out_map(i, tile_e, n_used):
        return (i, 0)

    def w_map(i, tile_e, n_used):
        return (layer, tile_e[i], 0, 0)

    return pl.pallas_call(
        _experts_kernel,
        grid_spec=pltpu.PrefetchScalarGridSpec(
            num_scalar_prefetch=2,
            grid=(n_slots // te,),
            in_specs=[pl.BlockSpec(memory_space=pl.ANY),
                      pl.BlockSpec((None, None, D_MODEL, D_EXPERT), w_map),
                      pl.BlockSpec((None, None, D_MODEL, D_EXPERT), w_map),
                      pl.BlockSpec((None, None, D_EXPERT, D_MODEL), w_map)],
            out_specs=pl.BlockSpec((te * ROW_CHUNKS, LANES), out_map),
            scratch_shapes=[pltpu.VMEM((XS_SLOTS, te * ROW_CHUNKS, LANES), F32),
                            pltpu.VMEM((te, D_MODEL), BF16),
                            pltpu.SemaphoreType.DMA((XS_SLOTS,))],
        ),
        out_shape=jax.ShapeDtypeStruct((n_slots * ROW_CHUNKS, LANES), F32),
        compiler_params=_cparams("arbitrary"),
        name="experts",
    )(tile_e, n_used, xs, wg, wu, wd)


def kernel(x, attn_norm_g, w_in, a_sink, w_branch_a, w_branch_b, w_out, ffn_norm_g,
           w_router_group, b_router_group, w_router_expert, b_router_expert,
           w_exp_gate, w_exp_up, w_exp_down, final_norm_g):
    batch, seq, d_model = x.shape
    depth = w_in.shape[0]
    n_tok = batch * seq
    assert d_model == D_MODEL and w_in.shape[2] == D_IN
    assert seq % (16 * B_SUB) == 0 and seq % IN_TILE == 0 and n_tok % DISPATCH_TILE == 0
    assert n_tok < (1 << RANK_BITS)

    cos_t, sin_t = _rope_tables(seq)
    tables = (cos_t, sin_t,
              _residue_order(cos_t, 4, IN_TILE), _residue_order(sin_t, 4, IN_TILE),
              _residue_order(cos_t, 16, IN_TILE), _residue_order(sin_t, 16, IN_TILE))

    n_slots = 2 * n_tok + N_EXPERTS * EXP_TILE
    n_tiles = n_slots // EXP_TILE
    x2d = x.reshape(n_tok, D_MODEL)

    for l in range(depth):
        nat, gates, grp1, grp2 = _in_proj(x2d, attn_norm_g[l][None, :], w_in, l, tables, batch, seq)
        nat3d = nat.reshape(batch, seq, NAT_COLS)
        ya = _attn_a(nat3d, a_sink[l]).reshape(n_tok, A_Q_DIM)
        o0, l0 = _attn_b(nat3d, NAT_B // B_DIM, "attn_b1")
        o1, l1 = _attn_b(grp1.reshape(batch * 4, seq // 4, GRP_COLS), 0, "attn_b4")
        o2, l2 = _attn_b(grp2.reshape(batch * 16, seq // 16, GRP_COLS), 0, "attn_b16")

        wr = jnp.zeros((D_MODEL, LANES), F32)
        wr = wr.at[:, 0:N_EXPERTS].set(w_router_expert[l]).at[:, N_EXPERTS:N_EXPERTS + MOE_GROUPS].set(w_router_group[l])
        br = jnp.zeros((1, LANES), F32)
        br = br.at[0, 0:N_EXPERTS].set(b_router_expert[l]).at[0, N_EXPERTS:N_EXPERTS + MOE_GROUPS].set(b_router_group[l])
        wr_hi = wr.astype(BF16)
        wr_lo = (wr - wr_hi.astype(F32)).astype(BF16)
        wr_stack = jnp.concatenate([wr_hi, wr_lo], axis=1)

        x1, h2, code, wcol, cnt = _merge(
            x2d, ya, o0.reshape(n_tok, B_DIM), l0.reshape(n_tok, B_DIM),
            o1.reshape(batch, 4, seq // 4, B_DIM), l1.reshape(batch, 4, seq // 4, B_DIM),
            o2.reshape(batch, 16, seq // 16, B_DIM), l2.reshape(batch, 16, seq // 16, B_DIM),
            gates, w_branch_a[l], w_branch_b[l], w_out[l],
            ffn_norm_g[l][None, :], wr_stack, br, batch, seq)

        counts = cnt[:, 0].astype(I32)
        padded = ((counts + EXP_TILE - 1) // EXP_TILE) * EXP_TILE
        ends = jnp.cumsum(padded)
        offs = ends - padded
        n_used = (ends[-1:] // EXP_TILE).astype(I32)
        tile_start = jnp.arange(n_tiles, dtype=I32) * EXP_TILE
        tile_e = jnp.minimum(jnp.sum((ends[None, :] <= tile_start[:, None]).astype(I32), axis=1), N_EXPERTS - 1)
        eid = code[0:2] >> RANK_BITS
        rank = code[0:2] & ((1 << RANK_BITS) - 1)
        expert_ids = jnp.arange(N_EXPERTS, dtype=I32)[:, None, None]
        slots = (rank + jnp.sum(jnp.where(eid[None] == expert_ids, offs[:, None, None], 0), axis=0)).reshape(-1)

        fill = (jnp.any(tile_start[:, None] + EXP_TILE == ends[None, :], axis=1) | (tile_start >= ends[-1])).astype(I32)
        xs = _dispatch(slots, fill, h2, None if l == 0 else xs, n_slots)
        ys = _experts(tile_e, n_used, xs, w_exp_gate, w_exp_up, w_exp_down, l)
        x2d = _combine(slots, x1, wcol, final_norm_g[None, :], ys, final=(l == depth - 1))

    return x2d.reshape(batch, seq, D_MODEL)
```

```python
import functools

import jax
import jax.numpy as jnp
import numpy as np
from jax import lax
from jax.experimental import pallas as pl
from jax.experimental.pallas import tpu as pltpu

F32 = jnp.float32
BF16 = jnp.bfloat16
I32 = jnp.int32

D_MODEL = 1024
HEAD_DIM = 64
HALF_HEAD = HEAD_DIM // 2
ROPE_THETA = 10000.0
NORM_EPS = 1e-6
NEG_INF = -1e30
LANES = 128

A_Q_HEADS = 8
A_KV_HEADS = 2
A_GROUP = A_Q_HEADS // A_KV_HEADS
A_HALF_WINDOW = 128
A_Q_DIM = A_Q_HEADS * HEAD_DIM
A_KV_DIM = A_KV_HEADS * HEAD_DIM

B_GROUPS = ((128, 1), (512, 4), (2048, 16))
B_HEADS = 4
B_DIM = B_HEADS * HEAD_DIM
B_HALF_WINDOW = 64

MOE_GROUPS = 4
EXPERTS_PER_GROUP = 8
N_EXPERTS = MOE_GROUPS * EXPERTS_PER_GROUP
D_EXPERT = 256

NAT_IN_COLS = A_Q_DIM + 2 * A_KV_DIM + 3 * B_DIM
NAT_B = A_Q_DIM
NAT_AK = NAT_B + 3 * B_DIM
NAT_AV = NAT_AK + 2 * A_KV_DIM
NAT_COLS = NAT_AV + A_KV_DIM
GRP_COLS = 3 * B_DIM
COL_G1 = NAT_IN_COLS
COL_G2 = COL_G1 + GRP_COLS
COL_GATE = COL_G2 + GRP_COLS
GATE_COLS = 2 * D_MODEL
D_IN = COL_GATE + GATE_COLS

TOK_TILE = 512
RING_SLOTS = 3
IN_TILE = 512
W_CHUNK = 512
A_Q_TILE = 2048
B_SUB = 128
T_ROWS = 512
B_Q_TILE = 2048
EXP_TILE = 512
XS_SLOTS = 3
ROW_CHUNKS = D_MODEL // LANES
ROW_TILE = 512
DISPATCH_TILE = 2048
DMA_UNROLL = 8
RANK_BITS = 16
VMEM_LIMIT = 56 * 1024 * 1024


def _cparams(*sem):
    return pltpu.CompilerParams(dimension_semantics=sem, vmem_limit_bytes=VMEM_LIMIT)


def _rope_tables(seq_len):
    inv = (1.0 / (np.float32(ROPE_THETA) ** (np.arange(0, HEAD_DIM, 2, dtype=np.float32) / np.float32(HEAD_DIM))))
    ang = np.arange(seq_len, dtype=np.float32)[:, None] * inv.astype(np.float32)[None, :]
    cos, sin = np.cos(ang).astype(np.float32), np.sin(ang).astype(np.float32)
    cos_t = np.concatenate([cos, cos, cos, cos], axis=-1)
    sin_t = np.concatenate([-sin, sin, -sin, sin], axis=-1)
    return cos_t, sin_t


def _residue_order(table, dilation, tile):
    s, c = table.shape
    return table.reshape(s // tile, tile // dilation, dilation, c).transpose(0, 2, 1, 3).reshape(s, c)


def _rope(t, cos, sin_signed, first_half):
    partner = jnp.where(first_half, pltpu.roll(t, LANES - HALF_HEAD, 1), pltpu.roll(t, HALF_HEAD, 1))
    return t * cos + partner * sin_signed


Q_KIND, K_KIND, V_KIND = 0, 1, 2
_NAT_KINDS = ([Q_KIND] * 4 + [K_KIND] + [V_KIND] + [Q_KIND] * 2 + [K_KIND] * 2 + [V_KIND] * 2)
_GRP_KINDS = [Q_KIND] * 2 + [K_KIND] * 2 + [V_KIND] * 2


def _in_proj_kernel(x_ref, g_ref, w_hbm_ref, c1_ref, s1_ref, c4_ref, s4_ref, c16_ref, s16_ref,
                    nat_ref, gate_ref, g1_ref, g2_ref, hf_ref, h4_ref, hb_ref, hd2_ref, w_ref, stage_ref, sem,
                    *, layer):
    tm = x_ref.shape[0]

    @pl.when(pl.program_id(0) == 0)
    def _():
        def chunk(j):
            cols = pl.ds(j * W_CHUNK, W_CHUNK)
            return pltpu.make_async_copy(w_hbm_ref.at[layer, :, cols], stage_ref.at[j % 2], sem.at[j % 2])

        n = D_IN // W_CHUNK
        chunk(0).start()
        for j in range(n):
            if j + 1 < n:
                chunk(j + 1).start()
            chunk(j).wait()
            w_ref[:, j * W_CHUNK:(j + 1) * W_CHUNK] = stage_ref[j % 2].astype(BF16)

    x = x_ref[...]
    h = x * lax.rsqrt(jnp.mean(x * x, axis=-1, keepdims=True) + NORM_EPS) * g_ref[...]
    n_chunks = D_MODEL // LANES
    for c in range(n_chunks):
        hf_ref[c] = h[:, c * LANES:(c + 1) * LANES]
    hb_ref[...] = h.astype(BF16)
    lane = lax.broadcasted_iota(I32, (1, LANES), 1)
    first_half = (lane % HEAD_DIM) < HALF_HEAD

    def project(h_b, col0, kinds, cos_ref, sin_ref, store):
        width = W_CHUNK
        for c0 in range(0, len(kinds) * LANES, width):
            w = min(width, len(kinds) * LANES - c0)
            res = jnp.dot(h_b, w_ref[:, col0 + c0:col0 + c0 + w], preferred_element_type=F32)
            for j in range(w // LANES):
                kind = kinds[(c0 // LANES) + j]
                t = res[:, j * LANES:(j + 1) * LANES]
                if kind != V_KIND:
                    t = _rope(t, cos_ref[...], sin_ref[...], first_half)
                if kind == Q_KIND:
                    t = t * (HEAD_DIM ** -0.5)
                store(c0 + j * LANES, t)

    low_head = lane < HEAD_DIM

    def store_nat(c, t):
        if A_Q_DIM <= c < A_Q_DIM + A_KV_DIM:
            swapped = pltpu.roll(t, HEAD_DIM, 1)
            nat_ref[:, NAT_AK:NAT_AK + LANES] = jnp.where(low_head, t, swapped).astype(BF16)
            nat_ref[:, NAT_AK + LANES:NAT_AK + 2 * LANES] = jnp.where(low_head, swapped, t).astype(BF16)
        elif c < A_Q_DIM + 2 * A_KV_DIM:
            out = c if c < A_Q_DIM else NAT_AV
            nat_ref[:, out:out + LANES] = t.astype(BF16)
        else:
            out = c - 2 * A_KV_DIM
            nat_ref[:, out:out + LANES] = t.astype(BF16)

    project(hb_ref[...], 0, _NAT_KINDS, c1_ref, s1_ref, store_nat)

    for c0 in range(0, GATE_COLS, W_CHUNK):
        res = jnp.dot(hb_ref[...], w_ref[:, COL_GATE + c0:COL_GATE + c0 + W_CHUNK], preferred_element_type=F32)
        gate_ref[:, c0:c0 + W_CHUNK] = jax.nn.sigmoid(res).astype(BF16)

    for slot, (dil, col0, cos_ref, sin_ref, out_ref) in enumerate(((4, COL_G1, c4_ref, s4_ref, g1_ref),
                                                                   (16, COL_G2, c16_ref, s16_ref, g2_ref))):
        n = tm // dil
        hd_ref = hd2_ref.at[slot]
        n4 = tm // 4
        for r in range(dil):
            for c in range(n_chunks):
                cs = slice(c * LANES, (c + 1) * LANES)
                if dil == 4:
                    rows4 = hf_ref[c, pl.ds(r, n, stride=4), :]
                    h4_ref[c, r * n:(r + 1) * n, :] = rows4
                    hd_ref[r * n:(r + 1) * n, cs] = rows4.astype(BF16)
                else:
                    r4, q = r % 4, r // 4
                    hd_ref[r * n:(r + 1) * n, cs] = h4_ref[c, pl.ds(r4 * n4 + q, n, stride=4), :].astype(BF16)

        def store_grp(c, t, out_ref=out_ref, dil=dil, n=n):
            v = t.astype(BF16)
            for r in range(dil):
                out_ref[r, :, c:c + LANES] = v[r * n:(r + 1) * n]

        project(hd_ref[...], col0, _GRP_KINDS, cos_ref, sin_ref, store_grp)


def _in_proj(x2d, g, w_in, layer, tables, batch, seq):
    tm = IN_TILE
    tiles_per_seq = seq // tm
    n_tok = batch * seq
    c1, s1, c4, s4, c16, s16 = tables
    tab_spec = pl.BlockSpec((tm, LANES), lambda i: (i % tiles_per_seq, 0))
    return pl.pallas_call(
        functools.partial(_in_proj_kernel, layer=layer),
        grid=(n_tok // tm,),
        in_specs=[
            pl.BlockSpec((tm, D_MODEL), lambda i: (i, 0)),
            pl.BlockSpec((1, D_MODEL), lambda i: (0, 0)),
            pl.BlockSpec(memory_space=pl.ANY),
            tab_spec, tab_spec, tab_spec, tab_spec, tab_spec, tab_spec,
        ],
        out_specs=[
            pl.BlockSpec((tm, NAT_COLS), lambda i: (i, 0)),
            pl.BlockSpec((tm, GATE_COLS), lambda i: (i, 0)),
            pl.BlockSpec((None, 4, tm // 4, GRP_COLS), lambda i: (i // tiles_per_seq, 0, i % tiles_per_seq, 0)),
            pl.BlockSpec((None, 16, tm // 16, GRP_COLS), lambda i: (i // tiles_per_seq, 0, i % tiles_per_seq, 0)),
        ],
        out_shape=[
            jax.ShapeDtypeStruct((n_tok, NAT_COLS), BF16),
            jax.ShapeDtypeStruct((n_tok, GATE_COLS), BF16),
            jax.ShapeDtypeStruct((batch, 4, seq // 4, GRP_COLS), BF16),
            jax.ShapeDtypeStruct((batch, 16, seq // 16, GRP_COLS), BF16),
        ],
        scratch_shapes=[
            pltpu.VMEM((D_MODEL // LANES, tm, LANES), F32),
            pltpu.VMEM((D_MODEL // LANES, tm, LANES), F32),
            pltpu.VMEM((tm, D_MODEL), BF16),
            pltpu.VMEM((2, tm, D_MODEL), BF16),
            pltpu.VMEM((D_MODEL, D_IN), BF16),
            pltpu.VMEM((2, D_MODEL, W_CHUNK), F32),
            pltpu.SemaphoreType.DMA((2,)),
        ],
        compiler_params=_cparams("arbitrary"),
        name="in_proj",
    )(x2d, g, w_in, c1, s1, c4, s4, c16, s16)


def _store_transposed(vt_ref, blocks):
    for r0, src in blocks:
        n = src.shape[0]
        for j in range(0, n, T_ROWS):
            m = min(T_ROWS, n - j)
            vt_ref[:, r0 + j:r0 + j + m] = src[j:j + m, :].astype(F32).T.astype(BF16)


def _masked_heads(q_pair, low_head, high_head):
    zero = jnp.zeros_like(q_pair)
    return [jnp.where(low_head, q_pair, zero), jnp.where(high_head, q_pair, zero)]


def _attn_a_kernel(sink_ref, q_ref, kp_ref, km_ref, kn_ref, vp_ref, vm_ref, vn_ref, o_ref, k_ref, vt_ref, *, seq):
    tq = q_ref.shape[0]
    hw = A_HALF_WINDOW
    n_sub = tq // hw
    i = pl.program_id(1)
    last_blk = seq // hw - 1
    k_ref[0:hw, :] = kp_ref[...]
    k_ref[hw:hw + tq, :] = km_ref[...]
    k_ref[hw + tq:tq + 2 * hw, :] = kn_ref[...]
    _store_transposed(vt_ref, ((0, vp_ref), (hw, vm_ref), (hw + tq, vn_ref)))
    grp_cols = A_GROUP * hw
    key = lax.broadcasted_iota(I32, (hw, grp_cols), 0)
    qry = lax.broadcasted_iota(I32, (hw, grp_cols), 1) % hw
    low_head = lax.broadcasted_iota(I32, (1, LANES), 1) < HEAD_DIM
    high_head = jnp.logical_not(low_head)
    for sb in range(n_sub):
        r0 = sb * hw
        blk = i * n_sub + sb
        mask_p = (key >= qry) if sb > 0 else (key >= qry + jnp.where(blk > 0, 0, hw))
        mask_n = (key <= qry) if sb < n_sub - 1 else (key <= qry - jnp.where(blk < last_blk, 0, hw))
        out_t = []
        for g in range(A_KV_HEADS):
            heads = range(g * A_GROUP, (g + 1) * A_GROUP)
            q_parts = []
            for c in range(g * A_GROUP // 2, (g + 1) * A_GROUP // 2):
                q_parts += _masked_heads(q_ref[r0:r0 + hw, c * LANES:(c + 1) * LANES], low_head, high_head)
            q = jnp.concatenate(q_parts, axis=0)
            sink = jnp.concatenate([jnp.full((1, hw), sink_ref[h], F32) for h in heads], axis=1)
            k = k_ref[r0:r0 + 3 * hw, g * LANES:(g + 1) * LANES]
            s = lax.dot_general(k, q, (((1,), (1,)), ((), ())), preferred_element_type=F32)
            sp = jnp.where(mask_p, s[0:hw], NEG_INF)
            so = s[hw:2 * hw]
            sn = jnp.where(mask_n, s[2 * hw:3 * hw], NEG_INF)
            m = jnp.max(jnp.maximum(jnp.maximum(sp, so), sn), axis=0, keepdims=True)
            m = jnp.maximum(m, sink)
            pp, po, pn = jnp.exp(sp - m), jnp.exp(so - m), jnp.exp(sn - m)
            denom = jnp.sum(pp + po + pn, axis=0, keepdims=True) + jnp.exp(sink - m)
            p = jnp.concatenate([pp, po, pn], axis=0).astype(BF16)
            vt = vt_ref[g * HEAD_DIM:(g + 1) * HEAD_DIM, r0:r0 + 3 * hw]
            o = jnp.dot(vt, p, preferred_element_type=F32) * (1.0 / denom)
            out_t += [o[:, j * hw:(j + 1) * hw] for j in range(A_GROUP)]
        o_ref[r0:r0 + hw, :] = jnp.concatenate(out_t, axis=0).T.astype(BF16)


def _attn_a(nat3d, sink):
    batch, seq, _ = nat3d.shape
    tq = A_Q_TILE
    hw = A_HALF_WINDOW
    per = tq // hw
    n_hw = seq // hw
    k_cols, v_cols = 2 * A_KV_DIM, A_KV_DIM
    k_blk, v_blk = NAT_AK // k_cols, NAT_AV // v_cols

    def prev_spec(cols, blk):
        return pl.BlockSpec((None, hw, cols), lambda b, i: (b, jnp.maximum(i * per - 1, 0), blk))

    def main_spec(cols, blk):
        return pl.BlockSpec((None, tq, cols), lambda b, i: (b, i, blk))

    def next_spec(cols, blk):
        return pl.BlockSpec((None, hw, cols), lambda b, i: (b, jnp.minimum((i + 1) * per, n_hw - 1), blk))

    return pl.pallas_call(
        functools.partial(_attn_a_kernel, seq=seq),
        grid=(batch, seq // tq),
        in_specs=[
            pl.BlockSpec(memory_space=pltpu.SMEM),
            main_spec(A_Q_DIM, 0),
            prev_spec(k_cols, k_blk), main_spec(k_cols, k_blk), next_spec(k_cols, k_blk),
            prev_spec(v_cols, v_blk), main_spec(v_cols, v_blk), next_spec(v_cols, v_blk),
        ],
        out_specs=pl.BlockSpec((None, tq, A_Q_DIM), lambda b, i: (b, i, 0)),
        out_shape=jax.ShapeDtypeStruct((batch, seq, A_Q_DIM), BF16),
        scratch_shapes=[pltpu.VMEM((tq + 2 * hw, k_cols), BF16), pltpu.VMEM((v_cols, tq + 2 * hw), BF16)],
        compiler_params=_cparams("parallel", "parallel"),
        name="attn_a",
    )(sink, nat3d, nat3d, nat3d, nat3d, nat3d, nat3d, nat3d)


def _attn_b_kernel(q_ref, kp_ref, km_ref, kn_ref, vp_ref, vm_ref, vn_ref, o_ref, lse_ref,
                   k_ref, vt_ref, *, sub_len):
    for s in range(q_ref.shape[0]):
        _attn_b_one(q_ref.at[s], kp_ref.at[s], km_ref.at[s], kn_ref.at[s], vp_ref.at[s], vm_ref.at[s], vn_ref.at[s],
                    o_ref.at[s], lse_ref.at[s], k_ref.at[s], vt_ref.at[s], sub_len)


def _attn_b_one(q_ref, kp_ref, km_ref, kn_ref, vp_ref, vm_ref, vn_ref, o_ref, lse_ref, k_ref, vt_ref, sub_len):
    tq = q_ref.shape[0]
    hw = B_HALF_WINDOW
    t0 = pl.program_id(1) * tq
    k_ref[0:hw, :] = kp_ref[...]
    k_ref[hw:hw + tq, :] = km_ref[...]
    k_ref[hw + tq:tq + 2 * hw, :] = kn_ref[...]
    _store_transposed(vt_ref, ((0, vp_ref), (hw, vm_ref), (hw + tq, vn_ref)))
    kw = B_SUB + 2 * hw
    n_sub = tq // B_SUB
    all_cols = B_HEADS * B_SUB
    key = lax.broadcasted_iota(I32, (kw, all_cols), 0)
    qry = lax.broadcasted_iota(I32, (kw, all_cols), 1) % B_SUB
    in_band = jnp.abs(key - hw - qry) <= hw
    low_head = lax.broadcasted_iota(I32, (1, LANES), 1) < HEAD_DIM
    high_head = jnp.logical_not(low_head)
    for sb in range(n_sub):
        r0 = sb * B_SUB
        valid = in_band
        if sb == 0:
            valid = valid & (t0 - hw + key >= 0)
        if sb == n_sub - 1:
            valid = valid & (t0 + r0 - hw + key < sub_len)
        s_parts = []
        for c in range(B_HEADS // 2):
            cs = slice(c * LANES, (c + 1) * LANES)
            q = jnp.concatenate(_masked_heads(q_ref[r0:r0 + B_SUB, cs], low_head, high_head), axis=0)
            s_parts.append(lax.dot_general(k_ref[r0:r0 + kw, cs], q, (((1,), (1,)), ((), ())),
                                           preferred_element_type=F32))
        s = jnp.where(valid, jnp.concatenate(s_parts, axis=1), NEG_INF)
        m = jnp.max(s, axis=0, keepdims=True)
        p = jnp.exp(s - m)
        denom = jnp.sum(p, axis=0, keepdims=True)
        p = p.astype(BF16)
        inv = 1.0 / denom
        lse = m + jnp.log(denom)
        out_t, lse_t = [], []
        for h in range(B_HEADS):
            qs = slice(h * B_SUB, (h + 1) * B_SUB)
            vt = vt_ref[h * HEAD_DIM:(h + 1) * HEAD_DIM, r0:r0 + kw]
            out_t.append(jnp.dot(vt, p[:, qs], preferred_element_type=F32) * inv[:, qs])
            lse_t.append(jnp.broadcast_to(lse[:, qs], (HEAD_DIM, B_SUB)))
        o_ref[r0:r0 + B_SUB, :] = jnp.concatenate(out_t, axis=0).T
        lse_ref[r0:r0 + B_SUB, :] = jnp.concatenate(lse_t, axis=0).T


def _attn_b(arr3d, q_blk, name):
    n_sub, sub_len, _ = arr3d.shape
    tq = min(B_Q_TILE, sub_len)
    n_per = B_Q_TILE // tq
    hw = B_HALF_WINDOW
    per = tq // hw
    n_hw = sub_len // hw

    def main_spec(c):
        return pl.BlockSpec((n_per, tq, B_DIM), lambda g, i: (g, i, c))

    def prev_spec(c):
        return pl.BlockSpec((n_per, hw, B_DIM), lambda g, i: (g, jnp.maximum(i * per - 1, 0), c))

    def next_spec(c):
        return pl.BlockSpec((n_per, hw, B_DIM), lambda g, i: (g, jnp.minimum((i + 1) * per, n_hw - 1), c))

    out_spec = pl.BlockSpec((n_per, tq, B_DIM), lambda g, i: (g, i, 0))
    return pl.pallas_call(
        functools.partial(_attn_b_kernel, sub_len=sub_len),
        grid=(n_sub // n_per, sub_len // tq),
        in_specs=[main_spec(q_blk),
                  prev_spec(q_blk + 1), main_spec(q_blk + 1), next_spec(q_blk + 1),
                  prev_spec(q_blk + 2), main_spec(q_blk + 2), next_spec(q_blk + 2)],
        out_specs=[out_spec, out_spec],
        out_shape=[jax.ShapeDtypeStruct((n_sub, sub_len, B_DIM), F32)] * 2,
        scratch_shapes=[pltpu.VMEM((n_per, tq + 2 * hw, B_DIM), BF16),
                        pltpu.VMEM((n_per, B_DIM, tq + 2 * hw), BF16)],
        compiler_params=_cparams("parallel", "parallel"),
        name=name,
    )(arr3d, arr3d, arr3d, arr3d, arr3d, arr3d, arr3d)


def _merge_kernel(x_hbm_ref, ya_ref, o0_ref, l0_ref, o1_ref, l1_ref, o2_ref, l2_ref, gate_hbm_ref,
                  wa_ref, wb_ref, wo_ref, g2_ref, wr_ref, br_ref,
                  x1_ref, h2_ref, code_ref, wcol_ref, cnt_ref,
                  so1_ref, sl1_ref, so2_ref, sl2_ref, yb_ref, run_ref, earlier_ref,
                  wab_ref, wbb_ref, wob_ref, xring_ref, gring_ref, ring_sem):
    tm = x1_ref.shape[0]
    step = pl.program_id(0)
    n_steps = pl.num_programs(0)

    def ring_copies(t):
        rows = pl.ds(pl.multiple_of(t * tm, tm), tm)
        slot = t % RING_SLOTS
        return (pltpu.make_async_copy(x_hbm_ref.at[rows], xring_ref.at[slot], ring_sem.at[0, slot]),
                pltpu.make_async_copy(gate_hbm_ref.at[rows], gring_ref.at[slot], ring_sem.at[1, slot]))

    @pl.when(step == 0)
    def _():
        for t in range(RING_SLOTS - 1):
            for c in ring_copies(t):
                c.start()

    @pl.when(step + RING_SLOTS - 1 < n_steps)
    def _():
        for c in ring_copies(step + RING_SLOTS - 1):
            c.start()

    for c in ring_copies(step):
        c.wait()
    x_ref = xring_ref.at[step % RING_SLOTS]
    gate_ref = gring_ref.at[step % RING_SLOTS]

    @pl.when(pl.program_id(0) == 0)
    def _():
        run_ref[...] = jnp.zeros_like(run_ref)
        wab_ref[...] = wa_ref[...].astype(BF16)
        wbb_ref[...] = wb_ref[...].astype(BF16)
        wob_ref[...] = wo_ref[...].astype(BF16)
        row = lax.broadcasted_iota(I32, (tm, tm), 0)
        col = lax.broadcasted_iota(I32, (tm, tm), 1)
        earlier_ref[...] = (row < col).astype(BF16)

    for dil, src_o, src_l, dst_o, dst_l in ((4, o1_ref, l1_ref, so1_ref, sl1_ref),
                                            (16, o2_ref, l2_ref, so2_ref, sl2_ref)):
        n = tm // dil
        for r in range(dil):
            for c in range(B_DIM // LANES):
                cs = slice(c * LANES, (c + 1) * LANES)
                dst_o[c, pl.ds(r, n, stride=dil), :] = src_o[r, :, cs]
                dst_l[c, pl.ds(r, n, stride=dil), :] = src_l[r, :, cs]

    for c in range(B_DIM // LANES):
        cs = slice(c * LANES, (c + 1) * LANES)
        l0, l1, l2 = l0_ref[:, cs], sl1_ref[c], sl2_ref[c]
        m = jnp.maximum(jnp.maximum(l0, l1), l2)
        e0, e1, e2 = jnp.exp(l0 - m), jnp.exp(l1 - m), jnp.exp(l2 - m)
        yb = (e0 * o0_ref[:, cs] + e1 * so1_ref[c] + e2 * so2_ref[c]) / (e0 + e1 + e2)
        yb_ref[:, cs] = yb.astype(BF16)

    ya_p = jnp.dot(ya_ref[...], wab_ref[...], preferred_element_type=F32)
    yb_p = jnp.dot(yb_ref[...], wbb_ref[...], preferred_element_type=F32)
    merged = gate_ref[:, 0:D_MODEL].astype(F32) * ya_p + gate_ref[:, D_MODEL:GATE_COLS].astype(F32) * yb_p
    x1 = x_ref[...] + jnp.dot(merged.astype(BF16), wob_ref[...], preferred_element_type=F32)
    x1_ref[...] = x1

    h2 = x1 * lax.rsqrt(jnp.mean(x1 * x1, axis=-1, keepdims=True) + NORM_EPS) * g2_ref[...]
    _store_row_tiles(h2_ref, h2)

    h_hi = h2.astype(BF16)
    h_lo = (h2 - h_hi.astype(F32)).astype(BF16)
    hi_terms = jnp.dot(h_hi, wr_ref[...], preferred_element_type=F32)
    logits = (hi_terms[:, 0:LANES] + hi_terms[:, LANES:2 * LANES]
              + jnp.dot(h_lo, wr_ref[:, 0:LANES], preferred_element_type=F32)) + br_ref[...]

    logits_t = logits.T
    sub = lax.broadcasted_iota(I32, (EXPERTS_PER_GROUP, tm), 0).astype(F32)
    none = float(EXPERTS_PER_GROUP)
    gl = jnp.where(sub < MOE_GROUPS, logits_t[N_EXPERTS:N_EXPERTS + EXPERTS_PER_GROUP], -jnp.inf)
    gmax = jnp.max(gl, axis=0, keepdims=True)
    gidx = jnp.min(jnp.where(gl == gmax, sub, none), axis=0, keepdims=True)
    gw = 1.0 / jnp.sum(jnp.exp(gl - gmax), axis=0, keepdims=True)
    el = logits_t[0:EXPERTS_PER_GROUP]
    for g in range(1, MOE_GROUPS):
        el = jnp.where(gidx == g, logits_t[g * EXPERTS_PER_GROUP:(g + 1) * EXPERTS_PER_GROUP], el)
    v1 = jnp.max(el, axis=0, keepdims=True)
    i1 = jnp.min(jnp.where(el == v1, sub, none), axis=0, keepdims=True)
    el2 = jnp.where(sub == i1, -jnp.inf, el)
    v2 = jnp.max(el2, axis=0, keepdims=True)
    i2 = jnp.min(jnp.where(el2 == v2, sub, none), axis=0, keepdims=True)
    t = jnp.exp(v2 - v1)
    w1 = gw / (1.0 + t)
    w2 = gw * t / (1.0 + t)
    e1 = gidx * EXPERTS_PER_GROUP + i1
    e2 = gidx * EXPERTS_PER_GROUP + i2

    expert = lax.broadcasted_iota(I32, (N_EXPERTS, tm), 0).astype(F32)
    oh1 = (expert == e1).astype(F32)
    oh2 = (expert == e2).astype(F32)
    oh = oh1 + oh2
    run = run_ref[...]
    before = (jnp.dot(oh.astype(BF16), earlier_ref[...], preferred_element_type=F32)
              + jnp.concatenate([run] * (tm // LANES), axis=1))
    rank1 = jnp.sum(before * oh1, axis=0, keepdims=True)
    rank2 = jnp.sum(before * oh2, axis=0, keepdims=True)
    run_ref[...] = run + jnp.sum(oh, axis=1, keepdims=True)
    cnt_ref[...] = run_ref[...]

    scale = float(1 << RANK_BITS)
    zeros = jnp.zeros((6, tm), F32)
    code_ref[...] = jnp.concatenate([e1 * scale + rank1, e2 * scale + rank2, zeros], axis=0).astype(I32)
    w_rows = jnp.concatenate([w1, w2, jnp.zeros((LANES - 2, tm), F32)], axis=0)
    wcol_ref[...] = w_rows.T


def _merge(x2d, ya, o0, l0, o1, l1, o2, l2, gates, wa, wb, wo, g2, wr, br, batch, seq):
    tm = TOK_TILE
    tps = seq // tm
    n_tok = batch * seq

    def tok(c):
        return pl.BlockSpec((tm, c), lambda i: (i, 0))

    def full(a):
        return pl.BlockSpec(a.shape, lambda i: (0,) * a.ndim)

    def res_spec(d):
        return pl.BlockSpec((None, d, tm // d, B_DIM), lambda i: (i // tps, 0, i % tps, 0))

    return pl.pallas_call(
        _merge_kernel,
        grid=(n_tok // tm,),
        in_specs=[pl.BlockSpec(memory_space=pl.ANY), tok(A_Q_DIM), tok(B_DIM), tok(B_DIM),
                  res_spec(4), res_spec(4), res_spec(16), res_spec(16), pl.BlockSpec(memory_space=pl.ANY),
                  full(wa), full(wb), full(wo), full(g2), full(wr), full(br)],
        out_specs=[tok(D_MODEL), pl.BlockSpec((tm * ROW_CHUNKS, LANES), lambda i: (i, 0)),
                   pl.BlockSpec((8, tm), lambda i: (0, i)),
                   tok(LANES),
                   pl.BlockSpec((N_EXPERTS, LANES), lambda i: (0, 0))],
        out_shape=[jax.ShapeDtypeStruct((n_tok, D_MODEL), F32),
                   jax.ShapeDtypeStruct((n_tok * ROW_CHUNKS, LANES), F32),
                   jax.ShapeDtypeStruct((8, n_tok), I32),
                   jax.ShapeDtypeStruct((n_tok, LANES), F32),
                   jax.ShapeDtypeStruct((N_EXPERTS, LANES), F32)],
        scratch_shapes=([pltpu.VMEM((B_DIM // LANES, tm, LANES), F32)] * 4
                        + [pltpu.VMEM((tm, B_DIM), BF16), pltpu.VMEM((N_EXPERTS, LANES), F32),
                           pltpu.VMEM((tm, tm), BF16),
                           pltpu.VMEM(wa.shape, BF16), pltpu.VMEM(wb.shape, BF16), pltpu.VMEM(wo.shape, BF16),
                           pltpu.VMEM((RING_SLOTS, tm, D_MODEL), F32), pltpu.VMEM((RING_SLOTS, tm, GATE_COLS), BF16),
                           pltpu.SemaphoreType.DMA((2, RING_SLOTS))]),
        compiler_params=_cparams("arbitrary"),
        name="merge_route",
    )(x2d, ya, o0, l0, o1, l1, o2, l2, gates, wa, wb, wo, g2, wr, br)


def _row_tile(ref, t):
    return ref.at[pl.ds(pl.multiple_of(t * ROW_CHUNKS, ROW_CHUNKS), ROW_CHUNKS)]


def _store_row_tiles(ref, val):
    rows = val.shape[0]
    for c in range(ROW_CHUNKS):
        ref[pl.ds(c, rows, stride=ROW_CHUNKS), :] = val[:, c * LANES:(c + 1) * LANES]


def _load_row_tiles_chunk(ref, c):
    return ref[pl.ds(c, ref.shape[0] // ROW_CHUNKS, stride=ROW_CHUNKS), :]


def _dispatch_kernel(slot_ref, fill_ref, h_ref, *rest, n_tok, reuse):
    if reuse:
        _, xs_ref, sem = rest
    else:
        xs_ref, zero_ref, sem, zsem = rest
    i = pl.program_id(0)
    rows = h_ref.shape[0] // ROW_CHUNKS

    if not reuse:
        @pl.when(i == 0)
        def _():
            tile_rows = zero_ref.shape[0]
            zero_ref[...] = jnp.zeros_like(zero_ref)

            def zero_copy(t):
                dst = xs_ref.at[pl.ds(pl.multiple_of(t * tile_rows, tile_rows), tile_rows)]
                return pltpu.make_async_copy(zero_ref, dst, zsem)

            def start(t, carry):
                @pl.when(fill_ref[t] != 0)
                def _():
                    zero_copy(t).start()
                return carry

            def wait(t, carry):
                @pl.when(fill_ref[t] != 0)
                def _():
                    zero_copy(t).wait()
                return carry

            n_tiles = xs_ref.shape[0] // tile_rows
            lax.fori_loop(0, n_tiles, start, 0)
            lax.fori_loop(0, n_tiles, wait, 0)

    def issue(j, carry):
        t = i * rows + j
        for k in range(2):
            pltpu.make_async_copy(_row_tile(h_ref, j), _row_tile(xs_ref, slot_ref[k * n_tok + t]), sem).start(priority=k)
        return carry

    lax.fori_loop(0, rows, issue, 0, unroll=DMA_UNROLL)
    for _ in range(2):
        pltpu.make_async_copy(h_ref, xs_ref.at[pl.ds(0, rows * ROW_CHUNKS)], sem).wait()


def _dispatch(slots, fill, h2, xs_prev, n_slots):
    n_tok = h2.shape[0] // ROW_CHUNKS
    rows = DISPATCH_TILE
    reuse = xs_prev is not None
    h_spec = pl.BlockSpec((rows * ROW_CHUNKS, LANES), lambda i, s, f: (i, 0))
    any_spec = pl.BlockSpec(memory_space=pl.ANY)
    scratch = [pltpu.SemaphoreType.DMA(())] if reuse else [
        pltpu.VMEM((EXP_TILE * ROW_CHUNKS, LANES), F32), pltpu.SemaphoreType.DMA(()), pltpu.SemaphoreType.DMA(())]
    return pl.pallas_call(
        functools.partial(_dispatch_kernel, n_tok=n_tok, reuse=reuse),
        grid_spec=pltpu.PrefetchScalarGridSpec(
            num_scalar_prefetch=2,
            grid=(n_tok // rows,),
            in_specs=[h_spec, any_spec] if reuse else [h_spec],
            out_specs=any_spec,
            scratch_shapes=scratch,
        ),
        out_shape=jax.ShapeDtypeStruct((n_slots * ROW_CHUNKS, LANES), F32),
        input_output_aliases={3: 0} if reuse else {},
        compiler_params=_cparams("arbitrary"),
        name="dispatch_reuse" if reuse else "dispatch",
    )(*((slots, fill, h2, xs_prev) if reuse else (slots, fill, h2)))


def _combine_kernel(slot_ref, x_ref, w_ref, g_ref, ys_ref, o_ref, buf_ref, y_ref, sem, *, n_tok, final):
    i = pl.program_id(0)
    n_steps = pl.num_programs(0)
    rows = x_ref.shape[0]

    def issue_tile(tile, slot):
        def issue(j, carry):
            t = tile * rows + j
            for k in range(2):
                pltpu.make_async_copy(_row_tile(ys_ref, slot_ref[k * n_tok + t]),
                                      _row_tile(buf_ref.at[slot, k], j), sem.at[slot]).start(priority=k)
            return carry

        lax.fori_loop(0, rows, issue, 0, unroll=DMA_UNROLL)

    @pl.when(i == 0)
    def _():
        issue_tile(0, 0)

    @pl.when(i + 1 < n_steps)
    def _():
        issue_tile(i + 1, (i + 1) % 2)

    cur = i % 2
    for k in range(2):
        pltpu.make_async_copy(ys_ref.at[pl.ds(0, rows * ROW_CHUNKS)], buf_ref.at[cur, k], sem.at[cur]).wait()
    w = w_ref[...]
    w1, w2 = w[:, 0:1], w[:, 1:2]
    for c in range(ROW_CHUNKS):
        cs = slice(c * LANES, (c + 1) * LANES)
        y1 = _load_row_tiles_chunk(buf_ref.at[cur, 0], c)
        y2 = _load_row_tiles_chunk(buf_ref.at[cur, 1], c)
        (y_ref if final else o_ref)[:, cs] = x_ref[:, cs] + w1 * y1 + w2 * y2
    if final:
        y = y_ref[...]
        o_ref[...] = y * lax.rsqrt(jnp.mean(y * y, axis=-1, keepdims=True) + NORM_EPS) * g_ref[...]


def _combine(slots, x1, wcol, g, ys, final):
    n_tok = x1.shape[0]
    rows = ROW_TILE
    return pl.pallas_call(
        functools.partial(_combine_kernel, n_tok=n_tok, final=final),
        grid_spec=pltpu.PrefetchScalarGridSpec(
            num_scalar_prefetch=1,
            grid=(n_tok // rows,),
            in_specs=[pl.BlockSpec((rows, D_MODEL), lambda i, s: (i, 0)),
                      pl.BlockSpec((rows, LANES), lambda i, s: (i, 0)),
                      pl.BlockSpec((1, D_MODEL), lambda i, s: (0, 0)),
                      pl.BlockSpec(memory_space=pl.ANY)],
            out_specs=pl.BlockSpec((rows, D_MODEL), lambda i, s: (i, 0)),
            scratch_shapes=[pltpu.VMEM((2, 2, rows * ROW_CHUNKS, LANES), F32),
                            pltpu.VMEM((rows, D_MODEL), F32),
                            pltpu.SemaphoreType.DMA((2,))],
        ),
        out_shape=jax.ShapeDtypeStruct((n_tok, D_MODEL), F32),
        compiler_params=_cparams("arbitrary"),
        name="combine",
    )(slots, x1, wcol, g, ys)


def _experts_kernel(tile_e_ref, n_used_ref, xs_ref, wg_ref, wu_ref, wd_ref, ys_ref, xbuf_ref, x_ref, sem):
    del tile_e_ref
    i = pl.program_id(0)
    n_used = n_used_ref[0]
    tile_rows = xbuf_ref.shape[1]

    def fetch(tile):
        rows = pl.ds(pl.multiple_of(tile * tile_rows, tile_rows), tile_rows)
        slot = tile % XS_SLOTS
        return pltpu.make_async_copy(xs_ref.at[rows], xbuf_ref.at[slot], sem.at[slot])

    @pl.when(i == 0)
    def _():
        fetch(0).start()

        @pl.when(n_used > 1)
        def _():
            fetch(1).start()

    @pl.when(i < n_used)
    def _():
        @pl.when(i + XS_SLOTS - 1 < n_used)
        def _():
            fetch(i + XS_SLOTS - 1).start()

        fetch(i).wait()
        xb = xbuf_ref.at[i % XS_SLOTS]
        for c in range(ROW_CHUNKS):
            x_ref[:, c * LANES:(c + 1) * LANES] = _load_row_tiles_chunk(xb, c).astype(BF16)
        x = x_ref[...]
        a = jnp.dot(x, wg_ref[...].astype(BF16), preferred_element_type=F32)
        u = jnp.dot(x, wu_ref[...].astype(BF16), preferred_element_type=F32)
        z = (a * jax.nn.sigmoid(a)) * u
        _store_row_tiles(ys_ref, jnp.dot(z.astype(BF16), wd_ref[...].astype(BF16), preferred_element_type=F32))

    @pl.when(i >= n_used)
    def _():
        ys_ref[...] = jnp.zeros_like(ys_ref)


def _experts(tile_e, n_used, xs, wg, wu, wd, layer):
    n_slots = xs.shape[0] // ROW_CHUNKS
    te = EXP_TILE

    def out_map(i, tile_e, n_used):
        return (i, 0)

    def w_map(i, tile_e, n_used):
        return (layer, tile_e[i], 0, 0)

    return pl.pallas_call(
        _experts_kernel,
        grid_spec=pltpu.PrefetchScalarGridSpec(
            num_scalar_prefetch=2,
            grid=(n_slots // te,),
            in_specs=[pl.BlockSpec(memory_space=pl.ANY),
                      pl.BlockSpec((None, None, D_MODEL, D_EXPERT), w_map),
                      pl.BlockSpec((None, None, D_MODEL, D_EXPERT), w_map),
                      pl.BlockSpec((None, None, D_EXPERT, D_MODEL), w_map)],
            out_specs=pl.BlockSpec((te * ROW_CHUNKS, LANES), out_map),
            scratch_shapes=[pltpu.VMEM((XS_SLOTS, te * ROW_CHUNKS, LANES), F32),
                            pltpu.VMEM((te, D_MODEL), BF16),
                            pltpu.SemaphoreType.DMA((XS_SLOTS,))],
        ),
        out_shape=jax.ShapeDtypeStruct((n_slots * ROW_CHUNKS, LANES), F32),
        compiler_params=_cparams("arbitrary"),
        name="experts",
    )(tile_e, n_used, xs, wg, wu, wd)


def kernel(x, attn_norm_g, w_in, a_sink, w_branch_a, w_branch_b, w_out, ffn_norm_g,
           w_router_group, b_router_group, w_router_expert, b_router_expert,
           w_exp_gate, w_exp_up, w_exp_down, final_norm_g):
    batch, seq, d_model = x.shape
    depth = w_in.shape[0]
    n_tok = batch * seq
    assert d_model == D_MODEL and w_in.shape[2] == D_IN
    assert seq % (16 * B_SUB) == 0 and seq % IN_TILE == 0 and n_tok % DISPATCH_TILE == 0
    assert n_tok < (1 << RANK_BITS)

    cos_t, sin_t = _rope_tables(seq)
    tables = (cos_t, sin_t,
              _residue_order(cos_t, 4, IN_TILE), _residue_order(sin_t, 4, IN_TILE),
              _residue_order(cos_t, 16, IN_TILE), _residue_order(sin_t, 16, IN_TILE))

    n_slots = 2 * n_tok + N_EXPERTS * EXP_TILE
    n_tiles = n_slots // EXP_TILE
    x2d = x.reshape(n_tok, D_MODEL)

    for l in range(depth):
        nat, gates, grp1, grp2 = _in_proj(x2d, attn_norm_g[l][None, :], w_in, l, tables, batch, seq)
        nat3d = nat.reshape(batch, seq, NAT_COLS)
        ya = _attn_a(nat3d, a_sink[l]).reshape(n_tok, A_Q_DIM)
        o0, l0 = _attn_b(nat3d, NAT_B // B_DIM, "attn_b1")
        o1, l1 = _attn_b(grp1.reshape(batch * 4, seq // 4, GRP_COLS), 0, "attn_b4")
        o2, l2 = _attn_b(grp2.reshape(batch * 16, seq // 16, GRP_COLS), 0, "attn_b16")

        wr = jnp.zeros((D_MODEL, LANES), F32)
        wr = wr.at[:, 0:N_EXPERTS].set(w_router_expert[l]).at[:, N_EXPERTS:N_EXPERTS + MOE_GROUPS].set(w_router_group[l])
        br = jnp.zeros((1, LANES), F32)
        br = br.at[0, 0:N_EXPERTS].set(b_router_expert[l]).at[0, N_EXPERTS:N_EXPERTS + MOE_GROUPS].set(b_router_group[l])
        wr_hi = wr.astype(BF16)
        wr_lo = (wr - wr_hi.astype(F32)).astype(BF16)
        wr_stack = jnp.concatenate([wr_hi, wr_lo], axis=1)

        x1, h2, code, wcol, cnt = _merge(
            x2d, ya, o0.reshape(n_tok, B_DIM), l0.reshape(n_tok, B_DIM),
            o1.reshape(batch, 4, seq // 4, B_DIM), l1.reshape(batch, 4, seq // 4, B_DIM),
            o2.reshape(batch, 16, seq // 16, B_DIM), l2.reshape(batch, 16, seq // 16, B_DIM),
            gates, w_branch_a[l], w_branch_b[l], w_out[l],
            ffn_norm_g[l][None, :], wr_stack, br, batch, seq)

        counts = cnt[:, 0].astype(I32)
        padded = ((counts + EXP_TILE - 1) // EXP_TILE) * EXP_TILE
        ends = jnp.cumsum(padded)
        offs = ends - padded
        n_used = (ends[-1:] // EXP_TILE).astype(I32)
        tile_start = jnp.arange(n_tiles, dtype=I32) * EXP_TILE
        tile_e = jnp.minimum(jnp.sum((ends[None, :] <= tile_start[:, None]).astype(I32), axis=1), N_EXPERTS - 1)
        eid = code[0:2] >> RANK_BITS
        rank = code[0:2] & ((1 << RANK_BITS) - 1)
        expert_ids = jnp.arange(N_EXPERTS, dtype=I32)[:, None, None]
        slots = (rank + jnp.sum(jnp.where(eid[None] == expert_ids, offs[:, None, None], 0), axis=0)).reshape(-1)

        fill = (jnp.any(tile_start[:, None] + EXP_TILE == ends[None, :], axis=1) | (tile_start >= ends[-1])).astype(I32)
        xs = _dispatch(slots, fill, h2, None if l == 0 else xs, n_slots)
        ys = _experts(tile_e, n_used, xs, w_exp_gate, w_exp_up, w_exp_down, l)
        x2d = _combine(slots, x1, wcol, final_norm_g[None, :], ys, final=(l == depth - 1))

    return x2d.reshape(batch, seq, D_MODEL)
```

```python
import functools

import jax
import jax.numpy as jnp
import numpy as np
from jax import lax
from jax.experimental import pallas as pl
from jax.experimental.pallas import tpu as pltpu

F32 = jnp.float32
BF16 = jnp.bfloat16
I32 = jnp.int32

D_MODEL = 1024
HEAD_DIM = 64
HALF_HEAD = HEAD_DIM // 2
ROPE_THETA = 10000.0
NORM_EPS = 1e-6
NEG_INF = -1e30
LANES = 128

A_Q_HEADS = 8
A_KV_HEADS = 2
A_GROUP = A_Q_HEADS // A_KV_HEADS
A_HALF_WINDOW = 128
A_Q_DIM = A_Q_HEADS * HEAD_DIM
A_KV_DIM = A_KV_HEADS * HEAD_DIM

B_GROUPS = ((128, 1), (512, 4), (2048, 16))
B_HEADS = 4
B_DIM = B_HEADS * HEAD_DIM
B_HALF_WINDOW = 64

MOE_GROUPS = 4
EXPERTS_PER_GROUP = 8
N_EXPERTS = MOE_GROUPS * EXPERTS_PER_GROUP
D_EXPERT = 256

NAT_IN_COLS = A_Q_DIM + 2 * A_KV_DIM + 3 * B_DIM
NAT_B = A_Q_DIM
NAT_AK = NAT_B + 3 * B_DIM
NAT_AV = NAT_AK + 2 * A_KV_DIM
NAT_COLS = NAT_AV + A_KV_DIM
GRP_COLS = 3 * B_DIM
COL_G1 = NAT_IN_COLS
COL_G2 = COL_G1 + GRP_COLS
COL_GATE = COL_G2 + GRP_COLS
GATE_COLS = 2 * D_MODEL
D_IN = COL_GATE + GATE_COLS

TOK_TILE = 512
IN_TILE = 512
W_CHUNK = 512
A_Q_TILE = 2048
B_SUB = 128
T_ROWS = 512
B_Q_TILE = 2048
EXP_TILE = 512
XS_SLOTS = 3
ROW_CHUNKS = D_MODEL // LANES
ROW_TILE = 512
DISPATCH_TILE = 2048
DMA_UNROLL = 8
RANK_BITS = 16
VMEM_LIMIT = 56 * 1024 * 1024


def _cparams(*sem):
    return pltpu.CompilerParams(dimension_semantics=sem, vmem_limit_bytes=VMEM_LIMIT)


def _rope_tables(seq_len):
    inv = (1.0 / (np.float32(ROPE_THETA) ** (np.arange(0, HEAD_DIM, 2, dtype=np.float32) / np.float32(HEAD_DIM))))
    ang = np.arange(seq_len, dtype=np.float32)[:, None] * inv.astype(np.float32)[None, :]
    cos, sin = np.cos(ang).astype(np.float32), np.sin(ang).astype(np.float32)
    cos_t = np.concatenate([cos, cos, cos, cos], axis=-1)
    sin_t = np.concatenate([-sin, sin, -sin, sin], axis=-1)
    return cos_t, sin_t


def _residue_order(table, dilation, tile):
    s, c = table.shape
    return table.reshape(s // tile, tile // dilation, dilation, c).transpose(0, 2, 1, 3).reshape(s, c)


def _rope(t, cos, sin_signed, first_half):
    partner = jnp.where(first_half, pltpu.roll(t, LANES - HALF_HEAD, 1), pltpu.roll(t, HALF_HEAD, 1))
    return t * cos + partner * sin_signed


Q_KIND, K_KIND, V_KIND = 0, 1, 2
_NAT_KINDS = ([Q_KIND] * 4 + [K_KIND] + [V_KIND] + [Q_KIND] * 2 + [K_KIND] * 2 + [V_KIND] * 2)
_GRP_KINDS = [Q_KIND] * 2 + [K_KIND] * 2 + [V_KIND] * 2


def _in_proj_kernel(x_ref, g_ref, w_hbm_ref, c1_ref, s1_ref, c4_ref, s4_ref, c16_ref, s16_ref,
                    nat_ref, gate_ref, g1_ref, g2_ref, hf_ref, h4_ref, hb_ref, hd2_ref, w_ref, stage_ref, sem,
                    *, layer):
    tm = x_ref.shape[0]

    @pl.when(pl.program_id(0) == 0)
    def _():
        def chunk(j):
            cols = pl.ds(j * W_CHUNK, W_CHUNK)
            return pltpu.make_async_copy(w_hbm_ref.at[layer, :, cols], stage_ref.at[j % 2], sem.at[j % 2])

        n = D_IN // W_CHUNK
        chunk(0).start()
        for j in range(n):
            if j + 1 < n:
                chunk(j + 1).start()
            chunk(j).wait()
            w_ref[:, j * W_CHUNK:(j + 1) * W_CHUNK] = stage_ref[j % 2].astype(BF16)

    x = x_ref[...]
    h = x * lax.rsqrt(jnp.mean(x * x, axis=-1, keepdims=True) + NORM_EPS) * g_ref[...]
    n_chunks = D_MODEL // LANES
    for c in range(n_chunks):
        hf_ref[c] = h[:, c * LANES:(c + 1) * LANES]
    hb_ref[...] = h.astype(BF16)
    lane = lax.broadcasted_iota(I32, (1, LANES), 1)
    first_half = (lane % HEAD_DIM) < HALF_HEAD

    def project(h_b, col0, kinds, cos_ref, sin_ref, store):
        width = W_CHUNK
        for c0 in range(0, len(kinds) * LANES, width):
            w = min(width, len(kinds) * LANES - c0)
            res = jnp.dot(h_b, w_ref[:, col0 + c0:col0 + c0 + w], preferred_element_type=F32)
            for j in range(w // LANES):
                kind = kinds[(c0 // LANES) + j]
                t = res[:, j * LANES:(j + 1) * LANES]
                if kind != V_KIND:
                    t = _rope(t, cos_ref[...], sin_ref[...], first_half)
                if kind == Q_KIND:
                    t = t * (HEAD_DIM ** -0.5)
                store(c0 + j * LANES, t)

    low_head = lane < HEAD_DIM

    def store_nat(c, t):
        if A_Q_DIM <= c < A_Q_DIM + A_KV_DIM:
            swapped = pltpu.roll(t, HEAD_DIM, 1)
            nat_ref[:, NAT_AK:NAT_AK + LANES] = jnp.where(low_head, t, swapped).astype(BF16)
            nat_ref[:, NAT_AK + LANES:NAT_AK + 2 * LANES] = jnp.where(low_head, swapped, t).astype(BF16)
        elif c < A_Q_DIM + 2 * A_KV_DIM:
            out = c if c < A_Q_DIM else NAT_AV
            nat_ref[:, out:out + LANES] = t.astype(BF16)
        else:
            out = c - 2 * A_KV_DIM
            nat_ref[:, out:out + LANES] = t.astype(BF16)

    project(hb_ref[...], 0, _NAT_KINDS, c1_ref, s1_ref, store_nat)

    for c0 in range(0, GATE_COLS, W_CHUNK):
        res = jnp.dot(hb_ref[...], w_ref[:, COL_GATE + c0:COL_GATE + c0 + W_CHUNK], preferred_element_type=F32)
        gate_ref[:, c0:c0 + W_CHUNK] = jax.nn.sigmoid(res).astype(BF16)

    for slot, (dil, col0, cos_ref, sin_ref, out_ref) in enumerate(((4, COL_G1, c4_ref, s4_ref, g1_ref),
                                                                   (16, COL_G2, c16_ref, s16_ref, g2_ref))):
        n = tm // dil
        hd_ref = hd2_ref.at[slot]
        n4 = tm // 4
        for r in range(dil):
            for c in range(n_chunks):
                cs = slice(c * LANES, (c + 1) * LANES)
                if dil == 4:
                    rows4 = hf_ref[c, pl.ds(r, n, stride=4), :]
                    h4_ref[c, r * n:(r + 1) * n, :] = rows4
                    hd_ref[r * n:(r + 1) * n, cs] = rows4.astype(BF16)
                else:
                    r4, q = r % 4, r // 4
                    hd_ref[r * n:(r + 1) * n, cs] = h4_ref[c, pl.ds(r4 * n4 + q, n, stride=4), :].astype(BF16)

        def store_grp(c, t, out_ref=out_ref, dil=dil, n=n):
            v = t.astype(BF16)
            for r in range(dil):
                out_ref[r, :, c:c + LANES] = v[r * n:(r + 1) * n]

        project(hd_ref[...], col0, _GRP_KINDS, cos_ref, sin_ref, store_grp)


def _in_proj(x2d, g, w_in, layer, tables, batch, seq):
    tm = IN_TILE
    tiles_per_seq = seq // tm
    n_tok = batch * seq
    c1, s1, c4, s4, c16, s16 = tables
    tab_spec = pl.BlockSpec((tm, LANES), lambda i: (i % tiles_per_seq, 0))
    return pl.pallas_call(
        functools.partial(_in_proj_kernel, layer=layer),
        grid=(n_tok // tm,),
        in_specs=[
            pl.BlockSpec((tm, D_MODEL), lambda i: (i, 0)),
            pl.BlockSpec((1, D_MODEL), lambda i: (0, 0)),
            pl.BlockSpec(memory_space=pl.ANY),
            tab_spec, tab_spec, tab_spec, tab_spec, tab_spec, tab_spec,
        ],
        out_specs=[
            pl.BlockSpec((tm, NAT_COLS), lambda i: (i, 0)),
            pl.BlockSpec((tm, GATE_COLS), lambda i: (i, 0)),
            pl.BlockSpec((None, 4, tm // 4, GRP_COLS), lambda i: (i // tiles_per_seq, 0, i % tiles_per_seq, 0)),
            pl.BlockSpec((None, 16, tm // 16, GRP_COLS), lambda i: (i // tiles_per_seq, 0, i % tiles_per_seq, 0)),
        ],
        out_shape=[
            jax.ShapeDtypeStruct((n_tok, NAT_COLS), BF16),
            jax.ShapeDtypeStruct((n_tok, GATE_COLS), BF16),
            jax.ShapeDtypeStruct((batch, 4, seq // 4, GRP_COLS), BF16),
            jax.ShapeDtypeStruct((batch, 16, seq // 16, GRP_COLS), BF16),
        ],
        scratch_shapes=[
            pltpu.VMEM((D_MODEL // LANES, tm, LANES), F32),
            pltpu.VMEM((D_MODEL // LANES, tm, LANES), F32),
            pltpu.VMEM((tm, D_MODEL), BF16),
            pltpu.VMEM((2, tm, D_MODEL), BF16),
            pltpu.VMEM((D_MODEL, D_IN), BF16),
            pltpu.VMEM((2, D_MODEL, W_CHUNK), F32),
            pltpu.SemaphoreType.DMA((2,)),
        ],
        compiler_params=_cparams("arbitrary"),
        name="in_proj",
    )(x2d, g, w_in, c1, s1, c4, s4, c16, s16)


def _store_transposed(vt_ref, blocks):
    for r0, src in blocks:
        n = src.shape[0]
        for j in range(0, n, T_ROWS):
            m = min(T_ROWS, n - j)
            vt_ref[:, r0 + j:r0 + j + m] = src[j:j + m, :].astype(F32).T.astype(BF16)


def _masked_heads(q_pair, low_head, high_head):
    zero = jnp.zeros_like(q_pair)
    return [jnp.where(low_head, q_pair, zero), jnp.where(high_head, q_pair, zero)]


def _attn_a_kernel(sink_ref, q_ref, kp_ref, km_ref, kn_ref, vp_ref, vm_ref, vn_ref, o_ref, k_ref, vt_ref, *, seq):
    tq = q_ref.shape[0]
    hw = A_HALF_WINDOW
    n_sub = tq // hw
    i = pl.program_id(1)
    last_blk = seq // hw - 1
    k_ref[0:hw, :] = kp_ref[...]
    k_ref[hw:hw + tq, :] = km_ref[...]
    k_ref[hw + tq:tq + 2 * hw, :] = kn_ref[...]
    _store_transposed(vt_ref, ((0, vp_ref), (hw, vm_ref), (hw + tq, vn_ref)))
    grp_cols = A_GROUP * hw
    key = lax.broadcasted_iota(I32, (hw, grp_cols), 0)
    qry = lax.broadcasted_iota(I32, (hw, grp_cols), 1) % hw
    low_head = lax.broadcasted_iota(I32, (1, LANES), 1) < HEAD_DIM
    high_head = jnp.logical_not(low_head)
    for sb in range(n_sub):
        r0 = sb * hw
        blk = i * n_sub + sb
        mask_p = (key >= qry) if sb > 0 else (key >= qry + jnp.where(blk > 0, 0, hw))
        mask_n = (key <= qry) if sb < n_sub - 1 else (key <= qry - jnp.where(blk < last_blk, 0, hw))
        s_parts = []
        for g in range(A_KV_HEADS):
            q_parts = []
            for c in range(g * A_GROUP // 2, (g + 1) * A_GROUP // 2):
                q_parts += _masked_heads(q_ref[r0:r0 + hw, c * LANES:(c + 1) * LANES], low_head, high_head)
            k = k_ref[r0:r0 + 3 * hw, g * LANES:(g + 1) * LANES]
            s_parts.append(lax.dot_general(k, jnp.concatenate(q_parts, axis=0), (((1,), (1,)), ((), ())),
                                           preferred_element_type=F32))
        s = jnp.concatenate(s_parts, axis=1)
        sink = jnp.concatenate([jnp.full((1, hw), sink_ref[h], F32) for h in range(A_Q_HEADS)], axis=1)
        sp = jnp.where(jnp.concatenate([mask_p] * A_KV_HEADS, axis=1), s[0:hw], NEG_INF)
        so = s[hw:2 * hw]
        sn = jnp.where(jnp.concatenate([mask_n] * A_KV_HEADS, axis=1), s[2 * hw:3 * hw], NEG_INF)
        m = jnp.max(jnp.maximum(jnp.maximum(sp, so), sn), axis=0, keepdims=True)
        m = jnp.maximum(m, sink)
        pp, po, pn = jnp.exp(sp - m), jnp.exp(so - m), jnp.exp(sn - m)
        denom = jnp.sum(pp + po + pn, axis=0, keepdims=True) + jnp.exp(sink - m)
        p = jnp.concatenate([pp, po, pn], axis=0).astype(BF16)
        inv = 1.0 / denom
        out_t = []
        for g in range(A_KV_HEADS):
            cols = slice(g * A_GROUP * hw, (g + 1) * A_GROUP * hw)
            vt = vt_ref[g * HEAD_DIM:(g + 1) * HEAD_DIM, r0:r0 + 3 * hw]
            o = jnp.dot(vt, p[:, cols], preferred_element_type=F32) * inv[:, cols]
            out_t += [o[:, j * hw:(j + 1) * hw] for j in range(A_GROUP)]
        o_ref[r0:r0 + hw, :] = jnp.concatenate(out_t, axis=0).T.astype(BF16)


def _attn_a(nat3d, sink):
    batch, seq, _ = nat3d.shape
    tq = A_Q_TILE
    hw = A_HALF_WINDOW
    per = tq // hw
    n_hw = seq // hw
    k_cols, v_cols = 2 * A_KV_DIM, A_KV_DIM
    k_blk, v_blk = NAT_AK // k_cols, NAT_AV // v_cols

    def prev_spec(cols, blk):
        return pl.BlockSpec((None, hw, cols), lambda b, i: (b, jnp.maximum(i * per - 1, 0), blk))

    def main_spec(cols, blk):
        return pl.BlockSpec((None, tq, cols), lambda b, i: (b, i, blk))

    def next_spec(cols, blk):
        return pl.BlockSpec((None, hw, cols), lambda b, i: (b, jnp.minimum((i + 1) * per, n_hw - 1), blk))

    return pl.pallas_call(
        functools.partial(_attn_a_kernel, seq=seq),
        grid=(batch, seq // tq),
        in_specs=[
            pl.BlockSpec(memory_space=pltpu.SMEM),
            main_spec(A_Q_DIM, 0),
            prev_spec(k_cols, k_blk), main_spec(k_cols, k_blk), next_spec(k_cols, k_blk),
            prev_spec(v_cols, v_blk), main_spec(v_cols, v_blk), next_spec(v_cols, v_blk),
        ],
        out_specs=pl.BlockSpec((None, tq, A_Q_DIM), lambda b, i: (b, i, 0)),
        out_shape=jax.ShapeDtypeStruct((batch, seq, A_Q_DIM), BF16),
        scratch_shapes=[pltpu.VMEM((tq + 2 * hw, k_cols), BF16), pltpu.VMEM((v_cols, tq + 2 * hw), BF16)],
        compiler_params=_cparams("parallel", "parallel"),
        name="attn_a",
    )(sink, nat3d, nat3d, nat3d, nat3d, nat3d, nat3d, nat3d)


def _attn_b_kernel(q_ref, kp_ref, km_ref, kn_ref, vp_ref, vm_ref, vn_ref, o_ref, lse_ref,
                   k_ref, vt_ref, *, sub_len):
    for s in range(q_ref.shape[0]):
        _attn_b_one(q_ref.at[s], kp_ref.at[s], km_ref.at[s], kn_ref.at[s], vp_ref.at[s], vm_ref.at[s], vn_ref.at[s],
                    o_ref.at[s], lse_ref.at[s], k_ref.at[s], vt_ref.at[s], sub_len)


def _attn_b_one(q_ref, kp_ref, km_ref, kn_ref, vp_ref, vm_ref, vn_ref, o_ref, lse_ref, k_ref, vt_ref, sub_len):
    tq = q_ref.shape[0]
    hw = B_HALF_WINDOW
    t0 = pl.program_id(1) * tq
    k_ref[0:hw, :] = kp_ref[...]
    k_ref[hw:hw + tq, :] = km_ref[...]
    k_ref[hw + tq:tq + 2 * hw, :] = kn_ref[...]
    _store_transposed(vt_ref, ((0, vp_ref), (hw, vm_ref), (hw + tq, vn_ref)))
    kw = B_SUB + 2 * hw
    n_sub = tq // B_SUB
    all_cols = B_HEADS * B_SUB
    key = lax.broadcasted_iota(I32, (kw, all_cols), 0)
    qry = lax.broadcasted_iota(I32, (kw, all_cols), 1) % B_SUB
    in_band = jnp.abs(key - hw - qry) <= hw
    low_head = lax.broadcasted_iota(I32, (1, LANES), 1) < HEAD_DIM
    high_head = jnp.logical_not(low_head)
    def edge_valid(sb):
        valid = in_band
        if sb == 0:
            valid = valid & (t0 - hw + key >= 0)
        if sb == n_sub - 1:
            valid = valid & (t0 + sb * B_SUB - hw + key < sub_len)
        return valid

    assert n_sub % 2 == 0
    for sb0 in range(0, n_sub, 2):
        s_parts, valids = [], []
        for sb in (sb0, sb0 + 1):
            r0 = sb * B_SUB
            for c in range(B_HEADS // 2):
                cs = slice(c * LANES, (c + 1) * LANES)
                q = jnp.concatenate(_masked_heads(q_ref[r0:r0 + B_SUB, cs], low_head, high_head), axis=0)
                s_parts.append(lax.dot_general(k_ref[r0:r0 + kw, cs], q, (((1,), (1,)), ((), ())),
                                               preferred_element_type=F32))
            valids.append(edge_valid(sb))
        s = jnp.where(jnp.concatenate(valids, axis=1), jnp.concatenate(s_parts, axis=1), NEG_INF)
        m = jnp.max(s, axis=0, keepdims=True)
        p = jnp.exp(s - m)
        denom = jnp.sum(p, axis=0, keepdims=True)
        p = p.astype(BF16)
        inv = 1.0 / denom
        lse = m + jnp.log(denom)
        for i_sb, sb in enumerate((sb0, sb0 + 1)):
            r0 = sb * B_SUB
            out_t, lse_t = [], []
            for h in range(B_HEADS):
                qs = slice((i_sb * B_HEADS + h) * B_SUB, (i_sb * B_HEADS + h + 1) * B_SUB)
                vt = vt_ref[h * HEAD_DIM:(h + 1) * HEAD_DIM, r0:r0 + kw]
                out_t.append(jnp.dot(vt, p[:, qs], preferred_element_type=F32) * inv[:, qs])
                lse_t.append(jnp.broadcast_to(lse[:, qs], (HEAD_DIM, B_SUB)))
            o_ref[r0:r0 + B_SUB, :] = jnp.concatenate(out_t, axis=0).T
            lse_ref[r0:r0 + B_SUB, :] = jnp.concatenate(lse_t, axis=0).T


def _attn_b(arr3d, q_blk, name):
    n_sub, sub_len, _ = arr3d.shape
    tq = min(B_Q_TILE, sub_len)
    n_per = B_Q_TILE // tq
    hw = B_HALF_WINDOW
    per = tq // hw
    n_hw = sub_len // hw

    def main_spec(c):
        return pl.BlockSpec((n_per, tq, B_DIM), lambda g, i: (g, i, c))

    def prev_spec(c):
        return pl.BlockSpec((n_per, hw, B_DIM), lambda g, i: (g, jnp.maximum(i * per - 1, 0), c))

    def next_spec(c):
        return pl.BlockSpec((n_per, hw, B_DIM), lambda g, i: (g, jnp.minimum((i + 1) * per, n_hw - 1), c))

    out_spec = pl.BlockSpec((n_per, tq, B_DIM), lambda g, i: (g, i, 0))
    return pl.pallas_call(
        functools.partial(_attn_b_kernel, sub_len=sub_len),
        grid=(n_sub // n_per, sub_len // tq),
        in_specs=[main_spec(q_blk),
                  prev_spec(q_blk + 1), main_spec(q_blk + 1), next_spec(q_blk + 1),
                  prev_spec(q_blk + 2), main_spec(q_blk + 2), next_spec(q_blk + 2)],
        out_specs=[out_spec, out_spec],
        out_shape=[jax.ShapeDtypeStruct((n_sub, sub_len, B_DIM), F32)] * 2,
        scratch_shapes=[pltpu.VMEM((n_per, tq + 2 * hw, B_DIM), BF16),
                        pltpu.VMEM((n_per, B_DIM, tq + 2 * hw), BF16)],
        compiler_params=_cparams("parallel", "parallel"),
        name=name,
    )(arr3d, arr3d, arr3d, arr3d, arr3d, arr3d, arr3d)


def _merge_kernel(x_ref, ya_ref, o0_ref, l0_ref, o1_ref, l1_ref, o2_ref, l2_ref, gate_ref,
                  wa_ref, wb_ref, wo_ref, g2_ref, wr_ref, br_ref,
                  x1_ref, h2_ref, code_ref, wcol_ref, cnt_ref,
                  so1_ref, sl1_ref, so2_ref, sl2_ref, yb_ref, run_ref, earlier_ref,
                  wab_ref, wbb_ref, wob_ref):
    tm = x_ref.shape[0]

    @pl.when(pl.program_id(0) == 0)
    def _():
        run_ref[...] = jnp.zeros_like(run_ref)
        wab_ref[...] = wa_ref[...].astype(BF16)
        wbb_ref[...] = wb_ref[...].astype(BF16)
        wob_ref[...] = wo_ref[...].astype(BF16)
        row = lax.broadcasted_iota(I32, (tm, tm), 0)
        col = lax.broadcasted_iota(I32, (tm, tm), 1)
        earlier_ref[...] = (row < col).astype(BF16)

    for dil, src_o, src_l, dst_o, dst_l in ((4, o1_ref, l1_ref, so1_ref, sl1_ref),
                                            (16, o2_ref, l2_ref, so2_ref, sl2_ref)):
        n = tm // dil
        for r in range(dil):
            for c in range(B_DIM // LANES):
                cs = slice(c * LANES, (c + 1) * LANES)
                dst_o[c, pl.ds(r, n, stride=dil), :] = src_o[r, :, cs]
                dst_l[c, pl.ds(r, n, stride=dil), :] = src_l[r, :, cs]

    for c in range(B_DIM // LANES):
        cs = slice(c * LANES, (c + 1) * LANES)
        l0, l1, l2 = l0_ref[:, cs], sl1_ref[c], sl2_ref[c]
        m = jnp.maximum(jnp.maximum(l0, l1), l2)
        e0, e1, e2 = jnp.exp(l0 - m), jnp.exp(l1 - m), jnp.exp(l2 - m)
        yb = (e0 * o0_ref[:, cs] + e1 * so1_ref[c] + e2 * so2_ref[c]) / (e0 + e1 + e2)
        yb_ref[:, cs] = yb.astype(BF16)

    ya_p = jnp.dot(ya_ref[...], wab_ref[...], preferred_element_type=F32)
    yb_p = jnp.dot(yb_ref[...], wbb_ref[...], preferred_element_type=F32)
    merged = gate_ref[:, 0:D_MODEL].astype(F32) * ya_p + gate_ref[:, D_MODEL:GATE_COLS].astype(F32) * yb_p
    x1 = x_ref[...] + jnp.dot(merged.astype(BF16), wob_ref[...], preferred_element_type=F32)
    x1_ref[...] = x1

    h2 = x1 * lax.rsqrt(jnp.mean(x1 * x1, axis=-1, keepdims=True) + NORM_EPS) * g2_ref[...]
    _store_row_tiles(h2_ref, h2)

    h_hi = h2.astype(BF16)
    h_lo = (h2 - h_hi.astype(F32)).astype(BF16)
    hi_terms = jnp.dot(h_hi, wr_ref[...], preferred_element_type=F32)
    logits = (hi_terms[:, 0:LANES] + hi_terms[:, LANES:2 * LANES]
              + jnp.dot(h_lo, wr_ref[:, 0:LANES], preferred_element_type=F32)) + br_ref[...]

    logits_t = logits.T
    sub = lax.broadcasted_iota(I32, (EXPERTS_PER_GROUP, tm), 0).astype(F32)
    none = float(EXPERTS_PER_GROUP)
    gl = jnp.where(sub < MOE_GROUPS, logits_t[N_EXPERTS:N_EXPERTS + EXPERTS_PER_GROUP], -jnp.inf)
    gmax = jnp.max(gl, axis=0, keepdims=True)
    gidx = jnp.min(jnp.where(gl == gmax, sub, none), axis=0, keepdims=True)
    gw = 1.0 / jnp.sum(jnp.exp(gl - gmax), axis=0, keepdims=True)
    el = logits_t[0:EXPERTS_PER_GROUP]
    for g in range(1, MOE_GROUPS):
        el = jnp.where(gidx == g, logits_t[g * EXPERTS_PER_GROUP:(g + 1) * EXPERTS_PER_GROUP], el)
    v1 = jnp.max(el, axis=0, keepdims=True)
    i1 = jnp.min(jnp.where(el == v1, sub, none), axis=0, keepdims=True)
    el2 = jnp.where(sub == i1, -jnp.inf, el)
    v2 = jnp.max(el2, axis=0, keepdims=True)
    i2 = jnp.min(jnp.where(el2 == v2, sub, none), axis=0, keepdims=True)
    t = jnp.exp(v2 - v1)
    w1 = gw / (1.0 + t)
    w2 = gw * t / (1.0 + t)
    e1 = gidx * EXPERTS_PER_GROUP + i1
    e2 = gidx * EXPERTS_PER_GROUP + i2

    expert = lax.broadcasted_iota(I32, (N_EXPERTS, tm), 0).astype(F32)
    oh1 = (expert == e1).astype(F32)
    oh2 = (expert == e2).astype(F32)
    oh = oh1 + oh2
    run = run_ref[...]
    before = (jnp.dot(oh.astype(BF16), earlier_ref[...], preferred_element_type=F32)
              + jnp.concatenate([run] * (tm // LANES), axis=1))
    rank1 = jnp.sum(before * oh1, axis=0, keepdims=True)
    rank2 = jnp.sum(before * oh2, axis=0, keepdims=True)
    run_ref[...] = run + jnp.sum(oh, axis=1, keepdims=True)
    cnt_ref[...] = run_ref[...]

    scale = float(1 << RANK_BITS)
    zeros = jnp.zeros((6, tm), F32)
    code_ref[...] = jnp.concatenate([e1 * scale + rank1, e2 * scale + rank2, zeros], axis=0).astype(I32)
    w_rows = jnp.concatenate([w1, w2, jnp.zeros((LANES - 2, tm), F32)], axis=0)
    wcol_ref[...] = w_rows.T


def _merge(x2d, ya, o0, l0, o1, l1, o2, l2, gates, wa, wb, wo, g2, wr, br, batch, seq):
    tm = TOK_TILE
    tps = seq // tm
    n_tok = batch * seq

    def tok(c):
        return pl.BlockSpec((tm, c), lambda i: (i, 0))

    def full(a):
        return pl.BlockSpec(a.shape, lambda i: (0,) * a.ndim)

    def res_spec(d):
        return pl.BlockSpec((None, d, tm // d, B_DIM), lambda i: (i // tps, 0, i % tps, 0))

    return pl.pallas_call(
        _merge_kernel,
        grid=(n_tok // tm,),
        in_specs=[tok(D_MODEL), tok(A_Q_DIM), tok(B_DIM), tok(B_DIM),
                  res_spec(4), res_spec(4), res_spec(16), res_spec(16), tok(GATE_COLS),
                  full(wa), full(wb), full(wo), full(g2), full(wr), full(br)],
        out_specs=[tok(D_MODEL), pl.BlockSpec((tm * ROW_CHUNKS, LANES), lambda i: (i, 0)),
                   pl.BlockSpec((8, tm), lambda i: (0, i)),
                   tok(LANES),
                   pl.BlockSpec((N_EXPERTS, LANES), lambda i: (0, 0))],
        out_shape=[jax.ShapeDtypeStruct((n_tok, D_MODEL), F32),
                   jax.ShapeDtypeStruct((n_tok * ROW_CHUNKS, LANES), F32),
                   jax.ShapeDtypeStruct((8, n_tok), I32),
                   jax.ShapeDtypeStruct((n_tok, LANES), F32),
                   jax.ShapeDtypeStruct((N_EXPERTS, LANES), F32)],
        scratch_shapes=([pltpu.VMEM((B_DIM // LANES, tm, LANES), F32)] * 4
                        + [pltpu.VMEM((tm, B_DIM), BF16), pltpu.VMEM((N_EXPERTS, LANES), F32),
                           pltpu.VMEM((tm, tm), BF16),
                           pltpu.VMEM(wa.shape, BF16), pltpu.VMEM(wb.shape, BF16), pltpu.VMEM(wo.shape, BF16)]),
        compiler_params=_cparams("arbitrary"),
        name="merge_route",
    )(x2d, ya, o0, l0, o1, l1, o2, l2, gates, wa, wb, wo, g2, wr, br)


def _row_tile(ref, t):
    return ref.at[pl.ds(pl.multiple_of(t * ROW_CHUNKS, ROW_CHUNKS), ROW_CHUNKS)]


def _store_row_tiles(ref, val):
    rows = val.shape[0]
    for c in range(ROW_CHUNKS):
        ref[pl.ds(c, rows, stride=ROW_CHUNKS), :] = val[:, c * LANES:(c + 1) * LANES]


def _load_row_tiles_chunk(ref, c):
    return ref[pl.ds(c, ref.shape[0] // ROW_CHUNKS, stride=ROW_CHUNKS), :]


def _dispatch_kernel(slot_ref, fill_ref, h_ref, *rest, n_tok, reuse):
    if reuse:
        _, xs_ref, sem = rest
    else:
        xs_ref, zero_ref, sem, zsem = rest
    i = pl.program_id(0)
    rows = h_ref.shape[0] // ROW_CHUNKS

    if not reuse:
        @pl.when(i == 0)
        def _():
            tile_rows = zero_ref.shape[0]
            zero_ref[...] = jnp.zeros_like(zero_ref)

            def zero_copy(t):
                dst = xs_ref.at[pl.ds(pl.multiple_of(t * tile_rows, tile_rows), tile_rows)]
                return pltpu.make_async_copy(zero_ref, dst, zsem)

            def start(t, carry):
                @pl.when(fill_ref[t] != 0)
                def _():
                    zero_copy(t).start()
                return carry

            def wait(t, carry):
                @pl.when(fill_ref[t] != 0)
                def _():
                    zero_copy(t).wait()
                return carry

            n_tiles = xs_ref.shape[0] // tile_rows
            lax.fori_loop(0, n_tiles, start, 0)
            lax.fori_loop(0, n_tiles, wait, 0)

    def issue(j, carry):
        t = i * rows + j
        for k in range(2):
            pltpu.make_async_copy(_row_tile(h_ref, j), _row_tile(xs_ref, slot_ref[k * n_tok + t]), sem).start(priority=k)
        return carry

    lax.fori_loop(0, rows, issue, 0, unroll=DMA_UNROLL)
    for _ in range(2):
        pltpu.make_async_copy(h_ref, xs_ref.at[pl.ds(0, rows * ROW_CHUNKS)], sem).wait()


def _dispatch(slots, fill, h2, xs_prev, n_slots):
    n_tok = h2.shape[0] // ROW_CHUNKS
    rows = DISPATCH_TILE
    reuse = xs_prev is not None
    h_spec = pl.BlockSpec((rows * ROW_CHUNKS, LANES), lambda i, s, f: (i, 0))
    any_spec = pl.BlockSpec(memory_space=pl.ANY)
    scratch = [pltpu.SemaphoreType.DMA(())] if reuse else [
        pltpu.VMEM((EXP_TILE * ROW_CHUNKS, LANES), F32), pltpu.SemaphoreType.DMA(()), pltpu.SemaphoreType.DMA(())]
    return pl.pallas_call(
        functools.partial(_dispatch_kernel, n_tok=n_tok, reuse=reuse),
        grid_spec=pltpu.PrefetchScalarGridSpec(
            num_scalar_prefetch=2,
            grid=(n_tok // rows,),
            in_specs=[h_spec, any_spec] if reuse else [h_spec],
            out_specs=any_spec,
            scratch_shapes=scratch,
        ),
        out_shape=jax.ShapeDtypeStruct((n_slots * ROW_CHUNKS, LANES), F32),
        input_output_aliases={3: 0} if reuse else {},
        compiler_params=_cparams("arbitrary"),
        name="dispatch_reuse" if reuse else "dispatch",
    )(*((slots, fill, h2, xs_prev) if reuse else (slots, fill, h2)))


def _combine_kernel(slot_ref, x_ref, w_ref, g_ref, ys_ref, o_ref, buf_ref, y_ref, sem, *, n_tok, final):
    i = pl.program_id(0)
    n_steps = pl.num_programs(0)
    rows = x_ref.shape[0]

    def issue_tile(tile, slot):
        def issue(j, carry):
            t = tile * rows + j
            for k in range(2):
                pltpu.make_async_copy(_row_tile(ys_ref, slot_ref[k * n_tok + t]),
                                      _row_tile(buf_ref.at[slot, k], j), sem.at[slot]).start(priority=k)
            return carry

        lax.fori_loop(0, rows, issue, 0, unroll=DMA_UNROLL)

    @pl.when(i == 0)
    def _():
        issue_tile(0, 0)

    @pl.when(i + 1 < n_steps)
    def _():
        issue_tile(i + 1, (i + 1) % 2)

    cur = i % 2
    for k in range(2):
        pltpu.make_async_copy(ys_ref.at[pl.ds(0, rows * ROW_CHUNKS)], buf_ref.at[cur, k], sem.at[cur]).wait()
    w = w_ref[...]
    w1, w2 = w[:, 0:1], w[:, 1:2]
    for c in range(ROW_CHUNKS):
        cs = slice(c * LANES, (c + 1) * LANES)
        y1 = _load_row_tiles_chunk(buf_ref.at[cur, 0], c)
        y2 = _load_row_tiles_chunk(buf_ref.at[cur, 1], c)
        (y_ref if final else o_ref)[:, cs] = x_ref[:, cs] + w1 * y1 + w2 * y2
    if final:
        y = y_ref[...]
        o_ref[...] = y * lax.rsqrt(jnp.mean(y * y, axis=-1, keepdims=True) + NORM_EPS) * g_ref[...]


def _combine(slots, x1, wcol, g, ys, final):
    n_tok = x1.shape[0]
    rows = ROW_TILE
    return pl.pallas_call(
        functools.partial(_combine_kernel, n_tok=n_tok, final=final),
        grid_spec=pltpu.PrefetchScalarGridSpec(
            num_scalar_prefetch=1,
            grid=(n_tok // rows,),
            in_specs=[pl.BlockSpec((rows, D_MODEL), lambda i, s: (i, 0)),
                      pl.BlockSpec((rows, LANES), lambda i, s: (i, 0)),
                      pl.BlockSpec((1, D_MODEL), lambda i, s: (0, 0)),
                      pl.BlockSpec(memory_space=pl.ANY)],
            out_specs=pl.BlockSpec((rows, D_MODEL), lambda i, s: (i, 0)),
            scratch_shapes=[pltpu.VMEM((2, 2, rows * ROW_CHUNKS, LANES), F32),
                            pltpu.VMEM((rows, D_MODEL), F32),
                            pltpu.SemaphoreType.DMA((2,))],
        ),
        out_shape=jax.ShapeDtypeStruct((n_tok, D_MODEL), F32),
        compiler_params=_cparams("arbitrary"),
        name="combine",
    )(slots, x1, wcol, g, ys)


def _experts_kernel(tile_e_ref, n_used_ref, xs_ref, wg_ref, wu_ref, wd_ref, ys_ref, xbuf_ref, x_ref, sem):
    del tile_e_ref
    i = pl.program_id(0)
    n_used = n_used_ref[0]
    tile_rows = xbuf_ref.shape[1]

    def fetch(tile):
        rows = pl.ds(pl.multiple_of(tile * tile_rows, tile_rows), tile_rows)
        slot = tile % XS_SLOTS
        return pltpu.make_async_copy(xs_ref.at[rows], xbuf_ref.at[slot], sem.at[slot])

    @pl.when(i == 0)
    def _():
        fetch(0).start()

        @pl.when(n_used > 1)
        def _():
            fetch(1).start()

    @pl.when(i < n_used)
    def _():
        @pl.when(i + XS_SLOTS - 1 < n_used)
        def _():
            fetch(i + XS_SLOTS - 1).start()

        fetch(i).wait()
        xb = xbuf_ref.at[i % XS_SLOTS]
        for c in range(ROW_CHUNKS):
            x_ref[:, c * LANES:(c + 1) * LANES] = _load_row_tiles_chunk(xb, c).astype(BF16)
        x = x_ref[...]
        a = jnp.dot(x, wg_ref[...].astype(BF16), preferred_element_type=F32)
        u = jnp.dot(x, wu_ref[...].astype(BF16), preferred_element_type=F32)
        z = (a * jax.nn.sigmoid(a)) * u
        _store_row_tiles(ys_ref, jnp.dot(z.astype(BF16), wd_ref[...].astype(BF16), preferred_element_type=F32))

    @pl.when(i >= n_used)
    def _():
        ys_ref[...] = jnp.zeros_like(ys_ref)


def _experts(tile_e, n_used, xs, wg, wu, wd, layer):
    n_slots = xs.shape[0] // ROW_CHUNKS
    te = EXP_TILE

    def out_map(i, tile_e, n_used):
        return (i, 0)

    def w_map(i, tile_e, n_used):
        return (layer, tile_e[i], 0, 0)

    return pl.pallas_call(
        _experts_kernel,
        grid_spec=pltpu.PrefetchScalarGridSpec(
            num_scalar_prefetch=2,
            grid=(n_slots // te,),
            in_specs=[pl.BlockSpec(memory_space=pl.ANY),
                      pl.BlockSpec((None, None, D_MODEL, D_EXPERT), w_map),
                      pl.BlockSpec((None, None, D_MODEL, D_EXPERT), w_map),
                      pl.BlockSpec((None, None, D_EXPERT, D_MODEL), w_map)],
            out_specs=pl.BlockSpec((te * ROW_CHUNKS, LANES), out_map),
            scratch_shapes=[pltpu.VMEM((XS_SLOTS, te * ROW_CHUNKS, LANES), F32),
                            pltpu.VMEM((te, D_MODEL), BF16),
                            pltpu.SemaphoreType.DMA((XS_SLOTS,))],
        ),
        out_shape=jax.ShapeDtypeStruct((n_slots * ROW_CHUNKS, LANES), F32),
        compiler_params=_cparams("arbitrary"),
        name="experts",
    )(tile_e, n_used, xs, wg, wu, wd)


def kernel(x, attn_norm_g, w_in, a_sink, w_branch_a, w_branch_b, w_out, ffn_norm_g,
           w_router_group, b_router_group, w_router_expert, b_router_expert,
           w_exp_gate, w_exp_up, w_exp_down, final_norm_g):
    batch, seq, d_model = x.shape
    depth = w_in.shape[0]
    n_tok = batch * seq
    assert d_model == D_MODEL and w_in.shape[2] == D_IN
    assert seq % (16 * B_SUB) == 0 and seq % IN_TILE == 0 and n_tok % DISPATCH_TILE == 0
    assert n_tok < (1 << RANK_BITS)

    cos_t, sin_t = _rope_tables(seq)
    tables = (cos_t, sin_t,
              _residue_order(cos_t, 4, IN_TILE), _residue_order(sin_t, 4, IN_TILE),
              _residue_order(cos_t, 16, IN_TILE), _residue_order(sin_t, 16, IN_TILE))

    n_slots = 2 * n_tok + N_EXPERTS * EXP_TILE
    n_tiles = n_slots // EXP_TILE
    x2d = x.reshape(n_tok, D_MODEL)

    for l in range(depth):
        nat, gates, grp1, grp2 = _in_proj(x2d, attn_norm_g[l][None, :], w_in, l, tables, batch, seq)
        nat3d = nat.reshape(batch, seq, NAT_COLS)
        ya = _attn_a(nat3d, a_sink[l]).reshape(n_tok, A_Q_DIM)
        o0, l0 = _attn_b(nat3d, NAT_B // B_DIM, "attn_b1")
        o1, l1 = _attn_b(grp1.reshape(batch * 4, seq // 4, GRP_COLS), 0, "attn_b4")
        o2, l2 = _attn_b(grp2.reshape(batch * 16, seq // 16, GRP_COLS), 0, "attn_b16")

        wr = jnp.zeros((D_MODEL, LANES), F32)
        wr = wr.at[:, 0:N_EXPERTS].set(w_router_expert[l]).at[:, N_EXPERTS:N_EXPERTS + MOE_GROUPS].set(w_router_group[l])
        br = jnp.zeros((1, LANES), F32)
        br = br.at[0, 0:N_EXPERTS].set(b_router_expert[l]).at[0, N_EXPERTS:N_EXPERTS + MOE_GROUPS].set(b_router_group[l])
        wr_hi = wr.astype(BF16)
        wr_lo = (wr - wr_hi.astype(F32)).astype(BF16)
        wr_stack = jnp.concatenate([wr_hi, wr_lo], axis=1)

        x1, h2, code, wcol, cnt = _merge(
            x2d, ya, o0.reshape(n_tok, B_DIM), l0.reshape(n_tok, B_DIM),
            o1.reshape(batch, 4, seq // 4, B_DIM), l1.reshape(batch, 4, seq // 4, B_DIM),
            o2.reshape(batch, 16, seq // 16, B_DIM), l2.reshape(batch, 16, seq // 16, B_DIM),
            gates, w_branch_a[l], w_branch_b[l], w_out[l],
            ffn_norm_g[l][None, :], wr_stack, br, batch, seq)

        counts = cnt[:, 0].astype(I32)
        padded = ((counts + EXP_TILE - 1) // EXP_TILE) * EXP_TILE
        ends = jnp.cumsum(padded)
        offs = ends - padded
        n_used = (ends[-1:] // EXP_TILE).astype(I32)
        tile_start = jnp.arange(n_tiles, dtype=I32) * EXP_TILE
        tile_e = jnp.minimum(jnp.sum((ends[None, :] <= tile_start[:, None]).astype(I32), axis=1), N_EXPERTS - 1)
        eid = code[0:2] >> RANK_BITS
        rank = code[0:2] & ((1 << RANK_BITS) - 1)
        expert_ids = jnp.arange(N_EXPERTS, dtype=I32)[:, None, None]
        slots = (rank + jnp.sum(jnp.where(eid[None] == expert_ids, offs[:, None, None], 0), axis=0)).reshape(-1)

        fill = (jnp.any(tile_start[:, None] + EXP_TILE == ends[None, :], axis=1) | (tile_start >= ends[-1])).astype(I32)
        xs = _dispatch(slots, fill, h2, None if l == 0 else xs, n_slots)
        ys = _experts(tile_e, n_used, xs, w_exp_gate, w_exp_up, w_exp_down, l)
        x2d = _combine(slots, x1, wcol, final_norm_g[None, :], ys, final=(l == depth - 1))

    return x2d.reshape(batch, seq, D_MODEL)
```
